```python
import math
import jax
import jax.numpy as jnp
from jax import lax
import numpy as np

D_MODEL = 1024
BATCH = 4
SEQ = 4096
DEPTH = 1

HEAD_DIM = 64
A_GROUPS = ((128, 1), (512, 4), (2048, 16))
A_HEADS = 4
A_QKV_COLS = len(A_GROUPS) * 3 * A_HEADS * HEAD_DIM
A_OUT = A_HEADS * HEAD_DIM
B_Q_HEADS = 8
B_KV_HEADS = 2
B_GROUP = B_Q_HEADS // B_KV_HEADS
B_WINDOW = 128
B_Q_COLS = B_Q_HEADS * HEAD_DIM
B_KV_COLS = B_KV_HEADS * HEAD_DIM
M_HEADS = 4
M_HEAD_DIM = 128
M_Q_COLS = M_HEADS * M_HEAD_DIM
MEM_LEN = 256
IN_SPLITS = (A_QKV_COLS,
             A_QKV_COLS + B_Q_COLS,
             A_QKV_COLS + B_Q_COLS + B_KV_COLS,
             A_QKV_COLS + B_Q_COLS + 2 * B_KV_COLS)
IN_COLS = IN_SPLITS[-1] + M_Q_COLS
N_BRANCH = 3
D_FF = 2816
CONV_WIDTH = 3
ROPE_THETA = 500000.0
ROPE_DIM_FRAC = 4
BLOCK = 128
EPS = 1e-6

kernel_name = 'hybrid_dilated_swa_memory_convffn'


def rmsnorm(t, gain):
    tf = t.astype(jnp.float32)
    y = tf * lax.rsqrt(jnp.mean(tf * tf, axis=-1, keepdims=True) + EPS)
    return (y * gain.astype(jnp.float32)).astype(t.dtype)


def rope_partial(t, positions):
    dh = t.shape[-1]
    rot = dh // ROPE_DIM_FRAC
    half = rot // 2
    freqs = jnp.exp(jnp.arange(half, dtype=jnp.float32) * (-2.0 * math.log(ROPE_THETA) / rot))
    ang = positions.astype(jnp.float32)[:, :, None, None] * freqs
    cos, sin = jnp.cos(ang), jnp.sin(ang)
    t1 = t[..., :half].astype(jnp.float32)
    t2 = t[..., half:rot].astype(jnp.float32)
    rotated = jnp.concatenate([t1 * cos - t2 * sin, t2 * cos + t1 * sin], axis=-1).astype(t.dtype)
    return jnp.concatenate([rotated, t[..., rot:]], axis=-1)


def banded_attn(q, k, v, max_dist, sink=None):
    assert max_dist <= BLOCK
    n, hk, g, L, dh = q.shape
    nb = -(-L // BLOCK)
    pad = nb * BLOCK - L
    q = jnp.pad(q, ((0, 0), (0, 0), (0, 0), (0, pad), (0, 0))).reshape(n, hk, g, nb, BLOCK, dh)
    k = jnp.pad(k, ((0, 0), (0, 0), (0, pad), (0, 0))).reshape(n, hk, nb, BLOCK, dh)
    v = jnp.pad(v, ((0, 0), (0, 0), (0, pad), (0, 0))).reshape(n, hk, nb, BLOCK, dh)

    def with_prev(t):
        prev = jnp.pad(t, ((0, 0), (0, 0), (1, 0), (0, 0), (0, 0)))[:, :, :-1]
        return jnp.concatenate([prev, t], axis=3)

    kw, vw = with_prev(k), with_prev(v)
    s = jnp.einsum('nhgbqd,nhbkd->nhgbqk', q, kw).astype(jnp.float32) * (dh ** -0.5)
    qi = jnp.arange(BLOCK)[:, None]
    kj = jnp.arange(2 * BLOCK)[None, :]
    dist = qi + BLOCK - kj
    band = (dist >= 0) & (dist <= max_dist)
    blk = jnp.arange(nb)[:, None, None]
    mask = band[None] & ((blk > 0) | (kj[None] >= BLOCK))
    s = jnp.where(mask, s, -jnp.inf)
    lse = jax.nn.logsumexp(s, axis=-1)
    if sink is not None:
        lse = jnp.logaddexp(lse, sink.astype(jnp.float32)[None, :, :, None, None])
    p = jnp.exp(s - lse[..., None])
    o = jnp.einsum('nhgbqk,nhbkd->nhgbqd', p.astype(v.dtype), vw)
    o = o.reshape(n, hk, g, nb * BLOCK, dh)[:, :, :, :L]
    lse = lse.reshape(n, hk, g, nb * BLOCK)[:, :, :, :L]
    return o, lse


def to_residue(t, d):
    b, s, h, dh = t.shape
    return t.reshape(b, s // d, d, h, dh).transpose(0, 2, 3, 1, 4).reshape(b * d, h, s // d, dh)


def from_residue(t, d):
    nd, h, L = t.shape[:3]
    rest = t.shape[3:]
    t = t.reshape(nd // d, d, h, L, *rest)
    t = jnp.moveaxis(t, 3, 1)
    return t.reshape(nd // d, L * d, h, *rest)


def dilated_mixture_attn(qkv_a, positions, q_gain, k_gain):
    b, s = qkv_a.shape[:2]
    outs, lses = [], []
    for gi, (window, dil) in enumerate(A_GROUPS):
        q = rope_partial(rmsnorm(qkv_a[:, :, gi, 0], q_gain[gi]), positions)
        k = rope_partial(rmsnorm(qkv_a[:, :, gi, 1], k_gain[gi]), positions)
        v = qkv_a[:, :, gi, 2]
        o, lse = banded_attn(to_residue(q, dil)[:, :, None], to_residue(k, dil),
                             to_residue(v, dil), window // dil)
        outs.append(from_residue(o[:, :, 0], dil))
        lses.append(from_residue(lse[:, :, 0], dil))
    w = jax.nn.softmax(jnp.stack(lses), axis=0)
    o = jnp.einsum('gbshd,gbsh->bshd', jnp.stack(outs), w.astype(qkv_a.dtype))
    return o.reshape(b, s, A_OUT)


def sink_swa_gqa(q, k, v, positions, q_gain, k_gain, sinks):
    b, s = q.shape[:2]
    q = rope_partial(rmsnorm(q, q_gain), positions)
    k = rope_partial(rmsnorm(k, k_gain), positions)
    qg = q.reshape(b, s, B_KV_HEADS, B_GROUP, HEAD_DIM).transpose(0, 2, 3, 1, 4)
    o, _ = banded_attn(qg, k.transpose(0, 2, 1, 3), v.transpose(0, 2, 1, 3),
                       B_WINDOW - 1, sink=sinks.reshape(B_KV_HEADS, B_GROUP))
    return o.transpose(0, 3, 1, 2, 4).reshape(b, s, B_Q_COLS)


def memory_attn(mq, mem, mem_gain, w_kv, q_gain, k_gain):
    b, s = mq.shape[:2]
    m = mem.shape[1]
    kv = (rmsnorm(mem, mem_gain) @ w_kv).reshape(b, m, 2, M_HEADS, M_HEAD_DIM)
    mk = rmsnorm(kv[:, :, 0], k_gain)
    mv = kv[:, :, 1]
    q = rmsnorm(mq.reshape(b, s, M_HEADS, M_HEAD_DIM), q_gain)
    sc = jnp.einsum('bshd,bmhd->bhsm', q, mk).astype(jnp.float32) * (M_HEAD_DIM ** -0.5)
    p = jax.nn.softmax(sc, axis=-1)
    o = jnp.einsum('bhsm,bmhd->bshd', p.astype(mv.dtype), mv)
    return o.reshape(b, s, M_Q_COLS)


def conv_ffn(h, w_up, conv_w, conv_b, w_down):
    s = h.shape[1]
    u = h @ w_up
    up = jnp.pad(u, ((0, 0), (CONV_WIDTH - 1, 0), (0, 0)))
    c = conv_b
    for j in range(CONV_WIDTH):
        c = c + conv_w[j] * up[:, j:j + s]
    a, g = jnp.split(c, 2, axis=-1)
    return (jax.nn.silu(a) * g) @ w_down


def setup_inputs(seed: int = 0) -> dict:
    key = jax.random.key(seed)
    ks = jax.random.split(key, 32)
    f32 = jnp.float32
    L = DEPTH

    def w(k, shape, fan_in):
        return jax.random.normal(k, shape, f32) * (fan_in ** -0.5)

    def gain(k, shape):
        return 1.0 + 0.02 * jax.random.normal(k, shape, f32)

    positions = (jax.random.randint(ks[2], (BATCH, 1), 0, 1024, dtype=jnp.int32)
                 + jnp.arange(SEQ, dtype=jnp.int32)[None, :])
    return {
        'x': jax.random.normal(ks[0], (BATCH, SEQ, D_MODEL), f32),
        'mem': jax.random.normal(ks[1], (BATCH, MEM_LEN, D_MODEL), f32),
        'positions': positions,
        'attn_norm': gain(ks[3], (L, D_MODEL)),
        'w_in': w(ks[4], (L, D_MODEL, IN_COLS), D_MODEL),
        'a_q_norm': gain(ks[5], (L, len(A_GROUPS), HEAD_DIM)),
        'a_k_norm': gain(ks[6], (L, len(A_GROUPS), HEAD_DIM)),
        'b_q_norm': gain(ks[7], (L, HEAD_DIM)),
        'b_k_norm': gain(ks[8], (L, HEAD_DIM)),
        'b_sinks': jax.random.normal(ks[9], (L, B_Q_HEADS), f32),
        'mem_norm': gain(ks[10], (L, D_MODEL)),
        'w_mem_kv': w(ks[11], (L, D_MODEL, 2 * M_Q_COLS), D_MODEL),
        'm_q_norm': gain(ks[12], (L, M_HEAD_DIM)),
        'm_k_norm': gain(ks[13], (L, M_HEAD_DIM)),
        'w_o_a': w(ks[14], (L, A_OUT, D_MODEL), A_OUT),
        'w_o_b': w(ks[15], (L, B_Q_COLS, D_MODEL), B_Q_COLS),
        'w_o_m': w(ks[16], (L, M_Q_COLS, D_MODEL), M_Q_COLS),
        'w_gate': w(ks[17], (L, D_MODEL, N_BRANCH * D_MODEL), D_MODEL),
        'b_gate': 0.01 * jax.random.normal(ks[18], (L, N_BRANCH * D_MODEL), f32),
        'w_out': w(ks[19], (L, D_MODEL, D_MODEL), D_MODEL),
        'ffn_norm': gain(ks[20], (L, D_MODEL)),
        'w_up': w(ks[21], (L, D_MODEL, 2 * D_FF), D_MODEL),
        'conv_w': w(ks[22], (L, CONV_WIDTH, 2 * D_FF), CONV_WIDTH),
        'conv_b': 0.01 * jax.random.normal(ks[23], (L, 2 * D_FF), f32),
        'w_down': w(ks[24], (L, D_FF, D_MODEL), D_FF),
    }


def reference(x, mem, positions, attn_norm, w_in, a_q_norm, a_k_norm, b_q_norm, b_k_norm,
              b_sinks, mem_norm, w_mem_kv, m_q_norm, m_k_norm, w_o_a, w_o_b, w_o_m,
              w_gate, b_gate, w_out, ffn_norm, w_up, conv_w, conv_b, w_down):
    b, s, _ = x.shape
    for l in range(DEPTH):
        h = rmsnorm(x, attn_norm[l])
        proj = h @ w_in[l]
        qkv_a, b_q, b_k, b_v, m_q = jnp.split(proj, IN_SPLITS, axis=-1)
        qkv_a = qkv_a.reshape(b, s, len(A_GROUPS), 3, A_HEADS, HEAD_DIM)
        o_a = dilated_mixture_attn(qkv_a, positions, a_q_norm[l], a_k_norm[l])
        o_b = sink_swa_gqa(b_q.reshape(b, s, B_Q_HEADS, HEAD_DIM),
                           b_k.reshape(b, s, B_KV_HEADS, HEAD_DIM),
                           b_v.reshape(b, s, B_KV_HEADS, HEAD_DIM),
                           positions, b_q_norm[l], b_k_norm[l], b_sinks[l])
        o_m = memory_attn(m_q, mem, mem_norm[l], w_mem_kv[l], m_q_norm[l], m_k_norm[l])
        gates = jax.nn.sigmoid((h @ w_gate[l] + b_gate[l]).astype(jnp.float32))
        gates = gates.astype(x.dtype).reshape(b, s, N_BRANCH, D_MODEL)
        merged = (gates[:, :, 0] * (o_a @ w_o_a[l])
                  + gates[:, :, 1] * (o_b @ w_o_b[l])
                  + gates[:, :, 2] * (o_m @ w_o_m[l]))
        x = x + merged @ w_out[l]
        x = x + conv_ffn(rmsnorm(x, ffn_norm[l]), w_up[l], conv_w[l], conv_b[l], w_down[l])
    return x
```

```python
import functools
import math

import jax
import jax.numpy as jnp
from jax import lax
from jax.experimental import pallas as pl
from jax.experimental.pallas import tpu as pltpu

D_MODEL = 1024
HEAD_DIM = 64
A_GROUPS = ((128, 1), (512, 4), (2048, 16))
A_HEADS = 4
A_SLAB = A_HEADS * HEAD_DIM
A_QKV_COLS = len(A_GROUPS) * 3 * A_SLAB
B_Q_HEADS = 8
B_KV_HEADS = 2
B_GROUP = B_Q_HEADS // B_KV_HEADS
B_WINDOW = 128
B_Q_COLS = B_Q_HEADS * HEAD_DIM
B_KV_COLS = B_KV_HEADS * HEAD_DIM
M_HEADS = 4
M_HEAD_DIM = 128
M_Q_COLS = M_HEADS * M_HEAD_DIM
B_Q_OFF = A_QKV_COLS
B_K_OFF = B_Q_OFF + B_Q_COLS
B_V_OFF = B_K_OFF + B_KV_COLS
M_Q_OFF = B_V_OFF + B_KV_COLS
IN_COLS = M_Q_OFF + M_Q_COLS
N_BRANCH = 3
D_FF = 2816
CONV_WIDTH = 3
ROPE_THETA = 500000.0
ROPE_DIM = HEAD_DIM // 4
ROPE_HALF = ROPE_DIM // 2
BLOCK = 128
EPS = 1e-6
NEG = -1e30

LANES = 128
TOKEN_BLOCK = 512
FF_CHUNK = 256
VMEM_LIMIT = 56 * 1024 * 1024

F32 = jnp.float32
BF16 = jnp.bfloat16


def _const_spec(shape):
    return pl.BlockSpec(shape, lambda *_: (0,) * len(shape))


def _rms_rows(x, gain):
    ms = jnp.mean(x * x, axis=-1, keepdims=True)
    return x * lax.rsqrt(ms + EPS) * gain


def _mem_kv_kernel(mem_ref, gain_ref, w_ref, kgain_ref, mk_ref, mv_ref):
    hm = _rms_rows(mem_ref[...], gain_ref[...]).astype(BF16)
    kv = jnp.dot(hm, w_ref[...], preferred_element_type=F32)
    ks = []
    for h in range(M_HEADS):
        kh = kv[:, h * M_HEAD_DIM:(h + 1) * M_HEAD_DIM]
        ks.append(_rms_rows(kh, kgain_ref[...]))
    mk_ref[...] = jnp.concatenate(ks, axis=1).astype(BF16)
    mv_ref[...] = kv[:, M_Q_COLS:].astype(BF16)


def _mem_kv(mem, mem_gain, w_kv, k_gain):
    b, m, _ = mem.shape
    return pl.pallas_call(
        _mem_kv_kernel,
        grid=(b,),
        in_specs=[pl.BlockSpec((None, m, D_MODEL), lambda i: (i, 0, 0)),
                  _const_spec((1, D_MODEL)),
                  _const_spec((D_MODEL, 2 * M_Q_COLS)),
                  _const_spec((1, M_HEAD_DIM))],
        out_specs=[pl.BlockSpec((None, m, M_Q_COLS), lambda i: (i, 0, 0)),
                   pl.BlockSpec((None, m, M_Q_COLS), lambda i: (i, 0, 0))],
        out_shape=[jax.ShapeDtypeStruct((b, m, M_Q_COLS), BF16)] * 2,
        compiler_params=pltpu.CompilerParams(dimension_semantics=("arbitrary",),
                                             vmem_limit_bytes=VMEM_LIMIT),
        name="mem_kv",
    )(mem, mem_gain, w_kv, k_gain)


def _in_proj_kernel(x_ref, pos_ref, gain_ref, w_ref, colgain_ref, freq_ref, bd64_ref, bd128_ref,
                    expand_ref, *out_refs):
    (qa0, ka0, va0, qa1, ka1, va1, qa2, ka2, va2, qb_ref, kb_ref, vb_ref, mq_ref) = out_refs
    h = _rms_rows(x_ref[...], gain_ref[...]).astype(BF16)

    ang = pos_ref[...] * freq_ref[...]
    cos = jnp.cos(ang)
    sin = jnp.sin(ang)
    first_half = (lax.broadcasted_iota(jnp.int32, (1, LANES), 1) % HEAD_DIM) < ROPE_HALF
    sin_lo = jnp.where(first_half, -sin, 0.0)
    sin_hi = jnp.where(first_half, 0.0, sin)

    def proj(c0, width):
        return jnp.dot(h, w_ref[:, c0:c0 + width], preferred_element_type=F32)

    def head_norm(t, bd, dim, c0):
        ss = jnp.dot((t * t).astype(BF16), bd, preferred_element_type=F32)
        return t * lax.rsqrt(ss * (1.0 / dim) + EPS) * colgain_ref[:, c0:c0 + t.shape[1]]

    def rope(t):
        parts = []
        for c in range(t.shape[1] // LANES):
            tc = t[:, c * LANES:(c + 1) * LANES]
            parts.append(tc * cos
                         + pltpu.roll(tc, LANES - ROPE_HALF, 1) * sin_lo
                         + pltpu.roll(tc, ROPE_HALF, 1) * sin_hi)
        return parts[0] if len(parts) == 1 else jnp.concatenate(parts, axis=1)

    bd64 = bd64_ref[...]
    for g, (q_ref, k_ref, v_ref) in enumerate(((qa0, ka0, va0), (qa1, ka1, va1), (qa2, ka2, va2))):
        c0 = g * 3 * A_SLAB
        q_ref[...] = rope(head_norm(proj(c0, A_SLAB), bd64, HEAD_DIM, c0)).astype(BF16)
        k_ref[...] = rope(head_norm(proj(c0 + A_SLAB, A_SLAB), bd64, HEAD_DIM, c0 + A_SLAB)).astype(BF16)
        v_ref[...] = proj(c0 + 2 * A_SLAB, A_SLAB).astype(BF16)

    for s in range(B_Q_COLS // A_SLAB):
        c0 = B_Q_OFF + s * A_SLAB
        qb_ref[:, s * A_SLAB:(s + 1) * A_SLAB] = rope(
            head_norm(proj(c0, A_SLAB), bd64, HEAD_DIM, c0)).astype(BF16)

    kb = rope(head_norm(proj(B_K_OFF, B_KV_COLS), bd64_ref[0:B_KV_COLS, 0:B_KV_COLS], HEAD_DIM, B_K_OFF))
    kb_ref[...] = jnp.dot(kb.astype(BF16), expand_ref[...], preferred_element_type=F32).astype(BF16)
    vb = proj(B_V_OFF, B_KV_COLS)
    vb_ref[...] = jnp.dot(vb.astype(BF16), expand_ref[...], preferred_element_type=F32).astype(BF16)

    bd128 = bd128_ref[...]
    for s in range(M_Q_COLS // A_SLAB):
        c0 = M_Q_OFF + s * A_SLAB
        mq_ref[:, s * A_SLAB:(s + 1) * A_SLAB] = head_norm(
            proj(c0, A_SLAB), bd128, M_HEAD_DIM, c0).astype(BF16)


def _in_proj(x2, pos_b, attn_gain, w_in, colgain, freq_row, bd64, bd128, expand):
    t = x2.shape[0]
    tb = TOKEN_BLOCK
    widths = [A_SLAB] * 9 + [B_Q_COLS, B_Q_COLS, B_Q_COLS, M_Q_COLS]
    return pl.pallas_call(
        _in_proj_kernel,
        grid=(t // tb,),
        in_specs=[pl.BlockSpec((tb, D_MODEL), lambda i: (i, 0)),
                  pl.BlockSpec((tb, LANES), lambda i: (i, 0)),
                  _const_spec((1, D_MODEL)),
                  _const_spec((D_MODEL, IN_COLS)),
                  _const_spec((1, IN_COLS)),
                  _const_spec((1, LANES)),
                  _const_spec((A_SLAB, A_SLAB)),
                  _const_spec((A_SLAB, A_SLAB)),
                  _const_spec((B_KV_COLS, B_Q_COLS))],
        out_specs=[pl.BlockSpec((tb, w), lambda i: (i, 0)) for w in widths],
        out_shape=[jax.ShapeDtypeStruct((t, w), BF16) for w in widths],
        compiler_params=pltpu.CompilerParams(dimension_semantics=("arbitrary",),
                                             vmem_limit_bytes=VMEM_LIMIT),
        name="in_proj",
    )(x2, pos_b, attn_gain, w_in, colgain, freq_row, bd64, bd128, expand)


def _band_attn_kernel(*refs, rows, slabs, prev_off, has_sink, with_lse):
    q_ref, kc_ref, kp_ref, vc_ref, vp_ref = refs[:5]
    pos = 5
    sink_ref = None
    if has_sink:
        sink_ref = refs[pos]
        pos += 1
    o_ref = refs[pos]
    lse_ref = refs[pos + 1] if with_lse else None

    first_step = pl.program_id(2) == 0
    row = lax.broadcasted_iota(jnp.int32, (BLOCK, 2 * BLOCK), 0)
    col = lax.broadcasted_iota(jnp.int32, (BLOCK, 2 * BLOCK), 1)
    bias = jnp.where(col < BLOCK,
                     jnp.where(col >= row + prev_off, 0.0, NEG),
                     jnp.where(col - BLOCK <= row, 0.0, NEG)).astype(F32)
    bias_first = jnp.where(jnp.logical_and(first_step, col < BLOCK), NEG, bias)
    head_of_lane = lax.broadcasted_iota(jnp.int32, (1, A_SLAB), 1) // HEAD_DIM
    nblk = rows // BLOCK

    for w in range(slabs):
        cs = slice(w * A_SLAB, (w + 1) * A_SLAB)
        if has_sink:
            sink = jnp.concatenate(
                [jnp.full((BLOCK, 1), sink_ref[w * A_HEADS + hh], F32) for hh in range(A_HEADS)], axis=0)
        for jb in range(nblk):
            q = q_ref[jb * BLOCK:(jb + 1) * BLOCK, cs]
            if jb == 0:
                k2 = jnp.concatenate([kp_ref[:, cs], kc_ref[0:BLOCK, cs]], axis=0)
                v2 = jnp.concatenate([vp_ref[:, cs], vc_ref[0:BLOCK, cs]], axis=0)
                b = bias_first
            else:
                k2 = kc_ref[(jb - 1) * BLOCK:(jb + 1) * BLOCK, cs]
                v2 = vc_ref[(jb - 1) * BLOCK:(jb + 1) * BLOCK, cs]
                b = bias
            qs = jnp.concatenate(
                [jnp.where(head_of_lane == hh, q, jnp.zeros_like(q)) for hh in range(A_HEADS)], axis=0)
            s = lax.dot_general(qs, k2, (((1,), (1,)), ((), ())), preferred_element_type=F32)
            s = (s.reshape(A_HEADS, BLOCK, 2 * BLOCK) + b[None]).reshape(A_HEADS * BLOCK, 2 * BLOCK)
            m = jnp.max(s, axis=-1, keepdims=True)
            if has_sink:
                m = jnp.maximum(m, sink)
            p = jnp.exp(s - m)
            l = jnp.sum(p, axis=-1, keepdims=True)
            if has_sink:
                l = l + jnp.exp(sink - m)
            ost = jnp.dot(p.astype(BF16), v2, preferred_element_type=F32)
            inv_l = 1.0 / l
            o = jnp.zeros((BLOCK, A_SLAB), F32)
            lse = jnp.zeros((BLOCK, A_SLAB), F32)
            for hh in range(A_HEADS):
                rs = slice(hh * BLOCK, (hh + 1) * BLOCK)
                o = jnp.where(head_of_lane == hh, ost[rs] * inv_l[rs], o)
                if with_lse:
                    lse = jnp.where(head_of_lane == hh, m[rs] + jnp.log(l[rs]), lse)
            o_ref[jb * BLOCK:(jb + 1) * BLOCK, cs] = o.astype(o_ref.dtype)
            if with_lse:
                lse_ref[jb * BLOCK:(jb + 1) * BLOCK, cs] = lse


def _band_attn(q, k, v, *, dil, rows, slabs, prev_off, sinks=None, with_lse=True):
    b, s, c = q.shape
    length = s // dil
    width = c * dil
    qv, kv, vv = (t.reshape(b, length, width) for t in (q, k, v))
    n_col = width // (slabs * A_SLAB)
    n_row = length // rows
    blk_per_step = rows // BLOCK
    cur = pl.BlockSpec((None, rows, slabs * A_SLAB), lambda i, r, j: (i, j, r))
    prev = pl.BlockSpec((None, BLOCK, slabs * A_SLAB),
                        lambda i, r, j: (i, jnp.maximum(j * blk_per_step - 1, 0), r))
    in_specs = [cur, cur, prev, cur, prev]
    args = [qv, kv, kv, vv, vv]
    if sinks is not None:
        in_specs.append(pl.BlockSpec(memory_space=pltpu.SMEM))
        args.append(sinks)
    out_specs = [cur]
    out_shape = [jax.ShapeDtypeStruct((b, length, width), BF16)]
    if with_lse:
        out_specs.append(cur)
        out_shape.append(jax.ShapeDtypeStruct((b, length, width), F32))
    outs = pl.pallas_call(
        functools.partial(_band_attn_kernel, rows=rows, slabs=slabs, prev_off=prev_off,
                          has_sink=sinks is not None, with_lse=with_lse),
        grid=(b, n_col, n_row),
        in_specs=in_specs,
        out_specs=out_specs,
        out_shape=out_shape,
        compiler_params=pltpu.CompilerParams(dimension_semantics=("arbitrary",) * 3,
                                             vmem_limit_bytes=VMEM_LIMIT),
        name=f"band_attn_d{dil}_w{slabs}",
    )(*args)
    return [t.reshape(b, s, c) for t in outs]


def _mix_kernel(x_ref, gain_ref, oa0, la0, oa1, la1, oa2, la2, ob_ref, mq_ref, mk_ref, mv_ref,
                wg_ref, bg_ref, woa_ref, wob_ref, wom_ref, wout_ref, out_ref):
    x = x_ref[...]
    h = _rms_rows(x, gain_ref[...]).astype(BF16)

    l0, l1, l2 = la0[...], la1[...], la2[...]
    mx = jnp.maximum(jnp.maximum(l0, l1), l2)
    e0, e1, e2 = jnp.exp(l0 - mx), jnp.exp(l1 - mx), jnp.exp(l2 - mx)
    o_a = (e0 * oa0[...].astype(F32) + e1 * oa1[...].astype(F32) + e2 * oa2[...].astype(F32)) / (e0 + e1 + e2)

    o_m = []
    for hh in range(M_HEADS):
        cs = slice(hh * M_HEAD_DIM, (hh + 1) * M_HEAD_DIM)
        s = lax.dot_general(mq_ref[:, cs], mk_ref[:, cs], (((1,), (1,)), ((), ())),
                            preferred_element_type=F32)
        p = jnp.exp(s - jnp.max(s, axis=-1, keepdims=True))
        l = jnp.sum(p, axis=-1, keepdims=True)
        o_m.append(jnp.dot(p.astype(BF16), mv_ref[:, cs], preferred_element_type=F32) / l)
    o_m = jnp.concatenate(o_m, axis=1)

    merged = jnp.zeros(x.shape, F32)
    for i, (o, w_ref) in enumerate(((o_a, woa_ref), (ob_ref[...], wob_ref), (o_m, wom_ref))):
        cs = slice(i * D_MODEL, (i + 1) * D_MODEL)
        z = jnp.dot(h, wg_ref[:, cs], preferred_element_type=F32) + bg_ref[:, cs]
        gate = 1.0 / (1.0 + jnp.exp(-z))
        merged = merged + gate * jnp.dot(o.astype(BF16), w_ref[...], preferred_element_type=F32)
    out_ref[...] = x + jnp.dot(merged.astype(BF16), wout_ref[...], preferred_element_type=F32)


def _mix(x2, attn_gain, oa, la, ob, mq, mk, mv, w_gate, b_gate, w_o_a, w_o_b, w_o_m, w_out, seq):
    t = x2.shape[0]
    tb = TOKEN_BLOCK
    mem_len = mk.shape[1]
    per_seq = seq // tb
    tok = lambda w: pl.BlockSpec((tb, w), lambda i: (i, 0))
    mem = pl.BlockSpec((None, mem_len, M_Q_COLS), lambda i: (i // per_seq, 0, 0))
    return pl.pallas_call(
        _mix_kernel,
        grid=(t // tb,),
        in_specs=[tok(D_MODEL), _const_spec((1, D_MODEL)),
                  tok(A_SLAB), tok(A_SLAB), tok(A_SLAB), tok(A_SLAB), tok(A_SLAB), tok(A_SLAB),
                  tok(B_Q_COLS), tok(M_Q_COLS), mem, mem,
                  _const_spec((D_MODEL, N_BRANCH * D_MODEL)), _const_spec((1, N_BRANCH * D_MODEL)),
                  _const_spec((A_SLAB, D_MODEL)), _const_spec((B_Q_COLS, D_MODEL)),
                  _const_spec((M_Q_COLS, D_MODEL)), _const_spec((D_MODEL, D_MODEL))],
        out_specs=tok(D_MODEL),
        out_shape=jax.ShapeDtypeStruct((t, D_MODEL), F32),
        compiler_params=pltpu.CompilerParams(dimension_semantics=("arbitrary",),
                                             vmem_limit_bytes=VMEM_LIMIT),
        name="mix",
    )(x2, attn_gain, oa[0], la[0], oa[1], la[1], oa[2], la[2], ob, mq, mk, mv,
      w_gate, b_gate, w_o_a, w_o_b, w_o_m, w_out)


def _conv_ffn_kernel(x_ref, gain_ref, wup_ref, cw_ref, cb_ref, wdown_ref, out_ref,
                     carry_ref, ext_ref, act_ref, *, per_seq):
    tb = x_ref.shape[0]
    halo = CONV_WIDTH - 1
    pad = 8

    @pl.when(pl.program_id(0) % per_seq == 0)
    def _():
        carry_ref[...] = jnp.zeros(carry_ref.shape, F32)

    x = x_ref[...]
    h = _rms_rows(x, gain_ref[...]).astype(BF16)

    def conv(c0):
        cs = slice(c0, c0 + FF_CHUNK)
        u = jnp.dot(h, wup_ref[:, cs], preferred_element_type=F32)
        ext_ref[0:pad, :] = carry_ref[:, cs]
        ext_ref[pad:pad + tb, :] = u
        carry_ref[:, cs] = u[tb - pad:tb, :]
        c = cb_ref[:, cs] + cw_ref[CONV_WIDTH - 1:CONV_WIDTH, cs] * u
        for j in range(halo):
            c = c + cw_ref[j:j + 1, cs] * ext_ref[pad - halo + j:pad - halo + j + tb, :]
        return c

    for ch in range(D_FF // FF_CHUNK):
        a = conv(ch * FF_CHUNK)
        g = conv(D_FF + ch * FF_CHUNK)
        act = a * (1.0 / (1.0 + jnp.exp(-a))) * g
        act_ref[:, ch * FF_CHUNK:(ch + 1) * FF_CHUNK] = act.astype(BF16)
    out_ref[...] = x + jnp.dot(act_ref[...], wdown_ref[...], preferred_element_type=F32)


def _conv_ffn(x2, gain, w_up, conv_w, conv_b, w_down, seq):
    t = x2.shape[0]
    tb = TOKEN_BLOCK
    return pl.pallas_call(
        functools.partial(_conv_ffn_kernel, per_seq=seq // tb),
        grid=(t // tb,),
        in_specs=[pl.BlockSpec((tb, D_MODEL), lambda i: (i, 0)),
                  _const_spec((1, D_MODEL)),
                  _const_spec((D_MODEL, 2 * D_FF)),
                  _const_spec((CONV_WIDTH, 2 * D_FF)),
                  _const_spec((1, 2 * D_FF)),
                  _const_spec((D_FF, D_MODEL))],
        out_specs=pl.BlockSpec((tb, D_MODEL), lambda i: (i, 0)),
        out_shape=jax.ShapeDtypeStruct((t, D_MODEL), F32),
        scratch_shapes=[pltpu.VMEM((8, 2 * D_FF), F32),
                        pltpu.VMEM((tb + 8, FF_CHUNK), F32),
                        pltpu.VMEM((tb, D_FF), BF16)],
        compiler_params=pltpu.CompilerParams(dimension_semantics=("arbitrary",),
                                             vmem_limit_bytes=VMEM_LIMIT),
        name="conv_ffn",
    )(x2, gain, w_up, conv_w, conv_b, w_down)


def _tables():
    lane = jnp.arange(A_SLAB)
    bd64 = (lane[:, None] // HEAD_DIM == lane[None, :] // HEAD_DIM).astype(BF16)
    bd128 = (lane[:, None] // M_HEAD_DIM == lane[None, :] // M_HEAD_DIM).astype(BF16)
    src = jnp.arange(B_KV_COLS)
    dst = jnp.arange(B_Q_COLS)
    expand = (src[:, None] == (dst[None, :] // A_SLAB) * HEAD_DIM + dst[None, :] % HEAD_DIM).astype(BF16)
    freqs = jnp.exp(jnp.arange(ROPE_HALF, dtype=F32) * (-2.0 * math.log(ROPE_THETA) / ROPE_DIM))
    in_head = jnp.arange(LANES) % HEAD_DIM
    freq_row = jnp.where(in_head < ROPE_DIM, freqs[in_head % ROPE_HALF], 0.0).reshape(1, LANES)
    return bd64, bd128, expand, freq_row


def _layer(x2, mem, pos_b, batch, seq, p):
    bd64, bd128, expand, freq_row = _tables()
    qk_scale = HEAD_DIM ** -0.5
    ones = jnp.ones((A_SLAB,), F32)
    colgain = jnp.concatenate(
        [jnp.concatenate([jnp.tile(p["a_q_norm"][g], A_HEADS) * qk_scale,
                          jnp.tile(p["a_k_norm"][g], A_HEADS), ones]) for g in range(len(A_GROUPS))]
        + [jnp.tile(p["b_q_norm"], B_Q_HEADS) * qk_scale,
           jnp.tile(p["b_k_norm"], B_KV_HEADS), jnp.ones((B_KV_COLS,), F32),
           jnp.tile(p["m_q_norm"], M_HEADS) * (M_HEAD_DIM ** -0.5)]).reshape(1, IN_COLS)

    mk, mv = _mem_kv(mem, p["mem_norm"].reshape(1, D_MODEL), p["w_mem_kv"].astype(BF16),
                     p["m_k_norm"].reshape(1, M_HEAD_DIM))
    attn_gain = p["attn_norm"].reshape(1, D_MODEL)
    (qa0, ka0, va0, qa1, ka1, va1, qa2, ka2, va2, qb, kb, vb, mq) = _in_proj(
        x2, pos_b, attn_gain, p["w_in"].astype(BF16), colgain, freq_row, bd64, bd128, expand)

    def seqd(t):
        return t.reshape(batch, seq, t.shape[-1])

    oa, la = [], []
    cfg = {1: dict(rows=512, slabs=1), 4: dict(rows=seq // 4, slabs=1), 16: dict(rows=seq // 16, slabs=4)}
    for (window, dil), (q, k, v) in zip(A_GROUPS, ((qa0, ka0, va0), (qa1, ka1, va1), (qa2, ka2, va2))):
        assert window // dil == BLOCK
        o, lse = _band_attn(seqd(q), seqd(k), seqd(v), dil=dil, prev_off=0, **cfg[dil])
        oa.append(o.reshape(batch * seq, A_SLAB))
        la.append(lse.reshape(batch * seq, A_SLAB))
    (ob,) = _band_attn(seqd(qb), seqd(kb), seqd(vb), dil=1, rows=512, slabs=B_KV_HEADS,
                       prev_off=BLOCK - (B_WINDOW - 1), sinks=p["b_sinks"], with_lse=False)
    ob = ob.reshape(batch * seq, B_Q_COLS)

    x2 = _mix(x2, attn_gain, oa, la, ob, mq, mk, mv, p["w_gate"].astype(BF16),
              p["b_gate"].reshape(1, N_BRANCH * D_MODEL), p["w_o_a"].astype(BF16),
              p["w_o_b"].astype(BF16), p["w_o_m"].astype(BF16), p["w_out"].astype(BF16), seq)
    return _conv_ffn(x2, p["ffn_norm"].reshape(1, D_MODEL), p["w_up"].astype(BF16), p["conv_w"],
                     p["conv_b"].reshape(1, 2 * D_FF), p["w_down"].astype(BF16), seq)


def kernel(x, mem, positions, attn_norm, w_in, a_q_norm, a_k_norm, b_q_norm, b_k_norm, b_sinks,
           mem_norm, w_mem_kv, m_q_norm, m_k_norm, w_o_a, w_o_b, w_o_m, w_gate, b_gate, w_out,
           ffn_norm, w_up, conv_w, conv_b, w_down):
    batch, seq, _ = x.shape
    params = dict(attn_norm=attn_norm, w_in=w_in, a_q_norm=a_q_norm, a_k_norm=a_k_norm,
                  b_q_norm=b_q_norm, b_k_norm=b_k_norm, b_sinks=b_sinks, mem_norm=mem_norm,
                  w_mem_kv=w_mem_kv, m_q_norm=m_q_norm, m_k_norm=m_k_norm, w_o_a=w_o_a, w_o_b=w_o_b,
                  w_o_m=w_o_m, w_gate=w_gate, b_gate=b_gate, w_out=w_out, ffn_norm=ffn_norm,
                  w_up=w_up, conv_w=conv_w, conv_b=conv_b, w_down=w_down)
    pos_b = jnp.broadcast_to(positions.astype(F32).reshape(batch * seq, 1), (batch * seq, LANES))
    x2 = x.reshape(batch * seq, D_MODEL)
    for layer in range(attn_norm.shape[0]):
        x2 = _layer(x2, mem, pos_b, batch, seq, {k: v[layer] for k, v in params.items()})
    return x2.reshape(batch, seq, D_MODEL)
```

```python
import functools
import math

import numpy as np

import jax
import jax.numpy as jnp
from jax import lax
from jax.experimental import pallas as pl
from jax.experimental.pallas import tpu as pltpu

D_MODEL = 1024
HEAD_DIM = 64
A_GROUPS = ((128, 1), (512, 4), (2048, 16))
A_HEADS = 4
A_SLAB = A_HEADS * HEAD_DIM
A_QKV_COLS = len(A_GROUPS) * 3 * A_SLAB
B_Q_HEADS = 8
B_KV_HEADS = 2
B_GROUP = B_Q_HEADS // B_KV_HEADS
B_WINDOW = 128
B_Q_COLS = B_Q_HEADS * HEAD_DIM
B_KV_COLS = B_KV_HEADS * HEAD_DIM
M_HEADS = 4
M_HEAD_DIM = 128
M_Q_COLS = M_HEADS * M_HEAD_DIM
B_Q_OFF = A_QKV_COLS
B_K_OFF = B_Q_OFF + B_Q_COLS
B_V_OFF = B_K_OFF + B_KV_COLS
M_Q_OFF = B_V_OFF + B_KV_COLS
IN_COLS = M_Q_OFF + M_Q_COLS
N_BRANCH = 3
D_FF = 2816
CONV_WIDTH = 3
ROPE_THETA = 500000.0
ROPE_DIM = HEAD_DIM // 4
ROPE_HALF = ROPE_DIM // 2
BLOCK = 128
EPS = 1e-6
NEG = -1e30

LANES = 128
TOKEN_BLOCK = 512
FF_CHUNK = 256
VMEM_LIMIT = 56 * 1024 * 1024

F32 = jnp.float32
BF16 = jnp.bfloat16


def _const_spec(shape):
    return pl.BlockSpec(shape, lambda *_: (0,) * len(shape))


def _rms_rows(x, gain):
    ms = jnp.mean(x * x, axis=-1, keepdims=True)
    return x * lax.rsqrt(ms + EPS) * gain


def _mem_kv_kernel(mem_ref, gain_ref, w_ref, kgain_ref, mk_ref, mv_ref):
    hm = _rms_rows(mem_ref[...], gain_ref[...]).astype(BF16)
    kv = jnp.dot(hm, w_ref[...], preferred_element_type=F32)
    ks = []
    for h in range(M_HEADS):
        kh = kv[:, h * M_HEAD_DIM:(h + 1) * M_HEAD_DIM]
        ks.append(_rms_rows(kh, kgain_ref[...]))
    mk_ref[...] = jnp.concatenate(ks, axis=1).astype(BF16)
    mv_ref[...] = kv[:, M_Q_COLS:].astype(BF16)


def _mem_kv(mem, mem_gain, w_kv, k_gain):
    b, m, _ = mem.shape
    return pl.pallas_call(
        _mem_kv_kernel,
        grid=(b,),
        in_specs=[pl.BlockSpec((None, m, D_MODEL), lambda i: (i, 0, 0)),
                  _const_spec((1, D_MODEL)),
                  _const_spec((D_MODEL, 2 * M_Q_COLS)),
                  _const_spec((1, M_HEAD_DIM))],
        out_specs=[pl.BlockSpec((None, m, M_Q_COLS), lambda i: (i, 0, 0)),
                   pl.BlockSpec((None, m, M_Q_COLS), lambda i: (i, 0, 0))],
        out_shape=[jax.ShapeDtypeStruct((b, m, M_Q_COLS), BF16)] * 2,
        compiler_params=pltpu.CompilerParams(dimension_semantics=("arbitrary",),
                                             vmem_limit_bytes=VMEM_LIMIT),
        name="mem_kv",
    )(mem, mem_gain, w_kv, k_gain)


def _in_proj_kernel(x_ref, pos_ref, gain_ref, w_ref, colgain_ref, freq_ref, bd64_ref, bd128_ref,
                    expand_ref, *refs):
    (qa0, ka0, va0, qa1, ka1, va1, qa2, ka2, va2, qb_ref, kb_ref, vb_ref, mq_ref) = refs[:13]
    stage_refs = list(refs[13:])
    h = _rms_rows(x_ref[...], gain_ref[...]).astype(BF16)

    def store_tile(out_ref, t, dil):
        if dil == 1:
            out_ref[...] = t.astype(BF16)
            return
        stage_ref = stage_refs.pop()
        rows = t.shape[0] // dil
        for s in range(A_SLAB // LANES):
            stage_ref[s] = t[:, s * LANES:(s + 1) * LANES]
        for r in range(dil):
            for s in range(A_SLAB // LANES):
                c0 = r * A_SLAB + s * LANES
                out_ref[:, c0:c0 + LANES] = stage_ref[s, pl.ds(r, rows, stride=dil), :].astype(BF16)

    ang = pos_ref[...] * freq_ref[...]
    cos = jnp.cos(ang)
    sin = jnp.sin(ang)
    first_half = (lax.broadcasted_iota(jnp.int32, (1, LANES), 1) % HEAD_DIM) < ROPE_HALF
    sin_lo = jnp.where(first_half, -sin, 0.0)
    sin_hi = jnp.where(first_half, 0.0, sin)

    def proj(c0, width):
        return jnp.dot(h, w_ref[:, c0:c0 + width], preferred_element_type=F32)

    def head_norm(t, bd, dim, c0):
        ss = jnp.dot((t * t).astype(BF16), bd, preferred_element_type=F32)
        return t * lax.rsqrt(ss * (1.0 / dim) + EPS) * colgain_ref[:, c0:c0 + t.shape[1]]

    def rope(t):
        parts = []
        for c in range(t.shape[1] // LANES):
            tc = t[:, c * LANES:(c + 1) * LANES]
            parts.append(tc * cos
                         + pltpu.roll(tc, LANES - ROPE_HALF, 1) * sin_lo
                         + pltpu.roll(tc, ROPE_HALF, 1) * sin_hi)
        return parts[0] if len(parts) == 1 else jnp.concatenate(parts, axis=1)

    bd64 = bd64_ref[...]
    for g, (q_ref, k_ref, v_ref) in enumerate(((qa0, ka0, va0), (qa1, ka1, va1), (qa2, ka2, va2))):
        c0 = g * 3 * A_SLAB
        dil = A_GROUPS[g][1]
        store_tile(q_ref, rope(head_norm(proj(c0, A_SLAB), bd64, HEAD_DIM, c0)), dil)
        store_tile(k_ref, rope(head_norm(proj(c0 + A_SLAB, A_SLAB), bd64, HEAD_DIM, c0 + A_SLAB)), dil)
        store_tile(v_ref, proj(c0 + 2 * A_SLAB, A_SLAB), dil)

    for s in range(B_Q_COLS // A_SLAB):
        c0 = B_Q_OFF + s * A_SLAB
        qb_ref[:, s * A_SLAB:(s + 1) * A_SLAB] = rope(
            head_norm(proj(c0, A_SLAB), bd64, HEAD_DIM, c0)).astype(BF16)

    kb = rope(head_norm(proj(B_K_OFF, B_KV_COLS), bd64_ref[0:B_KV_COLS, 0:B_KV_COLS], HEAD_DIM, B_K_OFF))
    kb_ref[...] = jnp.dot(kb.astype(BF16), expand_ref[...], preferred_element_type=F32).astype(BF16)
    vb = proj(B_V_OFF, B_KV_COLS)
    vb_ref[...] = jnp.dot(vb.astype(BF16), expand_ref[...], preferred_element_type=F32).astype(BF16)

    bd128 = bd128_ref[...]
    for s in range(M_Q_COLS // A_SLAB):
        c0 = M_Q_OFF + s * A_SLAB
        mq_ref[:, s * A_SLAB:(s + 1) * A_SLAB] = head_norm(
            proj(c0, A_SLAB), bd128, M_HEAD_DIM, c0).astype(BF16)


def _in_proj(x2, pos_b, attn_gain, w_in, colgain, freq_row, bd64, bd128, expand):
    t = x2.shape[0]
    tb = TOKEN_BLOCK
    dils = [dil for _, dil in A_GROUPS for _ in range(3)] + [1, 1, 1, 1]
    widths = [A_SLAB] * 9 + [B_Q_COLS, B_Q_COLS, B_Q_COLS, M_Q_COLS]
    n_staged = sum(d > 1 for d in dils)
    return pl.pallas_call(
        _in_proj_kernel,
        grid=(t // tb,),
        in_specs=[pl.BlockSpec((tb, D_MODEL), lambda i: (i, 0)),
                  pl.BlockSpec((tb, LANES), lambda i: (i, 0)),
                  _const_spec((1, D_MODEL)),
                  _const_spec((D_MODEL, IN_COLS)),
                  _const_spec((1, IN_COLS)),
                  _const_spec((1, LANES)),
                  _const_spec((A_SLAB, A_SLAB)),
                  _const_spec((A_SLAB, A_SLAB)),
                  _const_spec((B_KV_COLS, B_Q_COLS))],
        out_specs=[pl.BlockSpec((tb // d, w * d), lambda i: (i, 0)) for w, d in zip(widths, dils)],
        out_shape=[jax.ShapeDtypeStruct((t // d, w * d), BF16) for w, d in zip(widths, dils)],
        scratch_shapes=[pltpu.VMEM((A_SLAB // LANES, tb, LANES), F32)] * n_staged,
        compiler_params=pltpu.CompilerParams(dimension_semantics=("arbitrary",),
                                             vmem_limit_bytes=VMEM_LIMIT),
        name="in_proj",
    )(x2, pos_b, attn_gain, w_in, colgain, freq_row, bd64, bd128, expand)


def _band_attn_kernel(*refs, rows, slabs, prev_off, has_sink, with_lse):
    q_ref, kc_ref, kp_ref, vc_ref, vp_ref = refs[:5]
    pos = 5
    sink_ref = None
    if has_sink:
        sink_ref = refs[pos]
        pos += 1
    o_ref = refs[pos]
    lse_ref = refs[pos + 1] if with_lse else None

    first_step = pl.program_id(2) == 0
    row = lax.broadcasted_iota(jnp.int32, (BLOCK, 2 * BLOCK), 0)
    col = lax.broadcasted_iota(jnp.int32, (BLOCK, 2 * BLOCK), 1)
    bias = jnp.where(col < BLOCK,
                     jnp.where(col >= row + prev_off, 0.0, NEG),
                     jnp.where(col - BLOCK <= row, 0.0, NEG)).astype(F32)
    bias_first = jnp.where(jnp.logical_and(first_step, col < BLOCK), NEG, bias)
    head_of_lane = lax.broadcasted_iota(jnp.int32, (1, A_SLAB), 1) // HEAD_DIM
    key_row = lax.broadcasted_iota(jnp.int32, (2 * BLOCK, A_SLAB), 0)
    nblk = rows // BLOCK

    for w in range(slabs):
        cs = slice(w * A_SLAB, (w + 1) * A_SLAB)
        for jb in range(nblk):
            q = q_ref[jb * BLOCK:(jb + 1) * BLOCK, cs]
            if jb == 0:
                k2 = jnp.concatenate([kp_ref[:, cs], kc_ref[0:BLOCK, cs]], axis=0)
                v2 = jnp.concatenate([vp_ref[:, cs], vc_ref[0:BLOCK, cs]], axis=0)
                b = bias_first
            else:
                k2 = kc_ref[(jb - 1) * BLOCK:(jb + 1) * BLOCK, cs]
                v2 = vc_ref[(jb - 1) * BLOCK:(jb + 1) * BLOCK, cs]
                b = bias
            if has_sink:
                k2 = jnp.where(key_row == 0, jnp.zeros_like(k2), k2)
                v2 = jnp.where(key_row == 0, jnp.zeros_like(v2), v2)
            qs = jnp.concatenate(
                [jnp.where(head_of_lane == hh, q, jnp.zeros_like(q)) for hh in range(A_HEADS)], axis=0)
            s = lax.dot_general(qs, k2, (((1,), (1,)), ((), ())), preferred_element_type=F32)
            ps, ms, ls = [], [], []
            for hh in range(A_HEADS):
                bh = jnp.where(col == 0, sink_ref[w * A_HEADS + hh], b) if has_sink else b
                sh = s[hh * BLOCK:(hh + 1) * BLOCK] + bh
                m = jnp.max(sh, axis=-1, keepdims=True)
                p = jnp.exp(sh - m)
                ms.append(m)
                ls.append(jnp.sum(p, axis=-1, keepdims=True))
                ps.append(p.astype(BF16))
            ost = jnp.dot(jnp.concatenate(ps, axis=0), v2, preferred_element_type=F32)
            o = jnp.zeros((BLOCK, A_SLAB), F32)
            lse = jnp.zeros((BLOCK, A_SLAB), F32)
            for hh in range(A_HEADS):
                o = jnp.where(head_of_lane == hh, ost[hh * BLOCK:(hh + 1) * BLOCK] * (1.0 / ls[hh]), o)
                if with_lse:
                    lse = jnp.where(head_of_lane == hh, ms[hh] + jnp.log(ls[hh]), lse)
            o_ref[jb * BLOCK:(jb + 1) * BLOCK, cs] = o.astype(o_ref.dtype)
            if with_lse:
                lse_ref[jb * BLOCK:(jb + 1) * BLOCK, cs] = lse


def _band_attn(q, k, v, *, name, rows, slabs, prev_off, sinks=None, with_lse=True):
    b, length, width = q.shape
    assert sinks is None or prev_off >= 1
    n_col = width // (slabs * A_SLAB)
    n_row = length // rows
    blk_per_step = rows // BLOCK
    cur = pl.BlockSpec((None, rows, slabs * A_SLAB), lambda i, r, j: (i, j, r))
    prev = pl.BlockSpec((None, BLOCK, slabs * A_SLAB),
                        lambda i, r, j: (i, jnp.maximum(j * blk_per_step - 1, 0), r))
    in_specs = [cur, cur, prev, cur, prev]
    args = [q, k, k, v, v]
    if sinks is not None:
        in_specs.append(pl.BlockSpec(memory_space=pltpu.SMEM))
        args.append(sinks)
    out_specs = [cur]
    out_shape = [jax.ShapeDtypeStruct((b, length, width), BF16)]
    if with_lse:
        out_specs.append(cur)
        out_shape.append(jax.ShapeDtypeStruct((b, length, width), F32))
    return pl.pallas_call(
        functools.partial(_band_attn_kernel, rows=rows, slabs=slabs, prev_off=prev_off,
                          has_sink=sinks is not None, with_lse=with_lse),
        grid=(b, n_col, n_row),
        in_specs=in_specs,
        out_specs=out_specs,
        out_shape=out_shape,
        compiler_params=pltpu.CompilerParams(dimension_semantics=("arbitrary",) * 3,
                                             vmem_limit_bytes=VMEM_LIMIT),
        name=name,
    )(*args)


def _mix_kernel(x_ref, gain_ref, oa0, la0, oa1, la1, oa2, la2, ob_ref, mq_ref, mk_ref, mv_ref,
                wg_ref, bg_ref, woa_ref, wob_ref, wom_ref, wout_ref, out_ref, *stage_refs):
    x = x_ref[...]
    h = _rms_rows(x, gain_ref[...]).astype(BF16)
    stage_refs = list(stage_refs)

    def token_major(blk_ref, dil):
        if dil == 1:
            return blk_ref[...].astype(F32)
        stage_ref = stage_refs.pop()
        rows = blk_ref.shape[0]
        for r in range(dil):
            for s in range(A_SLAB // LANES):
                c0 = r * A_SLAB + s * LANES
                stage_ref[s, pl.ds(r, rows, stride=dil), :] = blk_ref[:, c0:c0 + LANES].astype(F32)
        return jnp.concatenate([stage_ref[s] for s in range(A_SLAB // LANES)], axis=1)

    dils = [dil for _, dil in A_GROUPS]
    l0, l1, l2 = (token_major(r, d) for r, d in zip((la0, la1, la2), dils))
    o0, o1, o2 = (token_major(r, d) for r, d in zip((oa0, oa1, oa2), dils))
    mx = jnp.maximum(jnp.maximum(l0, l1), l2)
    e0, e1, e2 = jnp.exp(l0 - mx), jnp.exp(l1 - mx), jnp.exp(l2 - mx)
    o_a = (e0 * o0 + e1 * o1 + e2 * o2) / (e0 + e1 + e2)

    o_m = []
    for hh in range(M_HEADS):
        cs = slice(hh * M_HEAD_DIM, (hh + 1) * M_HEAD_DIM)
        s = lax.dot_general(mq_ref[:, cs], mk_ref[:, cs], (((1,), (1,)), ((), ())),
                            preferred_element_type=F32)
        p = jnp.exp(s - jnp.max(s, axis=-1, keepdims=True))
        l = jnp.sum(p, axis=-1, keepdims=True)
        o_m.append(jnp.dot(p.astype(BF16), mv_ref[:, cs], preferred_element_type=F32) / l)
    o_m = jnp.concatenate(o_m, axis=1)

    merged = jnp.zeros(x.shape, F32)
    for i, (o, w_ref) in enumerate(((o_a, woa_ref), (ob_ref[...], wob_ref), (o_m, wom_ref))):
        cs = slice(i * D_MODEL, (i + 1) * D_MODEL)
        z = jnp.dot(h, wg_ref[:, cs], preferred_element_type=F32) + bg_ref[:, cs]
        gate = 1.0 / (1.0 + jnp.exp(-z))
        merged = merged + gate * jnp.dot(o.astype(BF16), w_ref[...], preferred_element_type=F32)
    out_ref[...] = x + jnp.dot(merged.astype(BF16), wout_ref[...], preferred_element_type=F32)


def _mix(x2, attn_gain, oa, la, ob, mq, mk, mv, w_gate, b_gate, w_o_a, w_o_b, w_o_m, w_out, seq):
    t = x2.shape[0]
    tb = TOKEN_BLOCK
    mem_len = mk.shape[1]
    per_seq = seq // tb
    tok = lambda w, d=1: pl.BlockSpec((tb // d, w * d), lambda i: (i, 0))
    mem = pl.BlockSpec((None, mem_len, M_Q_COLS), lambda i: (i // per_seq, 0, 0))
    dils = [dil for _, dil in A_GROUPS]
    return pl.pallas_call(
        _mix_kernel,
        grid=(t // tb,),
        in_specs=[tok(D_MODEL), _const_spec((1, D_MODEL))]
                 + [tok(A_SLAB, d) for d in dils for _ in range(2)]
                 + [tok(B_Q_COLS), tok(M_Q_COLS), mem, mem,
                  _const_spec((D_MODEL, N_BRANCH * D_MODEL)), _const_spec((1, N_BRANCH * D_MODEL)),
                  _const_spec((A_SLAB, D_MODEL)), _const_spec((B_Q_COLS, D_MODEL)),
                  _const_spec((M_Q_COLS, D_MODEL)), _const_spec((D_MODEL, D_MODEL))],
        out_specs=tok(D_MODEL),
        out_shape=jax.ShapeDtypeStruct((t, D_MODEL), F32),
        scratch_shapes=[pltpu.VMEM((A_SLAB // LANES, tb, LANES), F32)] * (2 * sum(d > 1 for d in dils)),
        compiler_params=pltpu.CompilerParams(dimension_semantics=("arbitrary",),
                                             vmem_limit_bytes=VMEM_LIMIT),
        name="mix",
    )(x2, attn_gain, oa[0], la[0], oa[1], la[1], oa[2], la[2], ob, mq, mk, mv,
      w_gate, b_gate, w_o_a, w_o_b, w_o_m, w_out)


def _conv_ffn_kernel(x_ref, gain_ref, wup_ref, cw_ref, cb_ref, wdown_ref, out_ref,
                     carry_ref, ext_ref, act_ref, *, per_seq):
    tb = x_ref.shape[0]
    halo = CONV_WIDTH - 1
    pad = 8

    @pl.when(pl.program_id(0) % per_seq == 0)
    def _():
        carry_ref[...] = jnp.zeros(carry_ref.shape, F32)

    x = x_ref[...]
    h = _rms_rows(x, gain_ref[...]).astype(BF16)

    def conv(c0):
        cs = slice(c0, c0 + FF_CHUNK)
        u = jnp.dot(h, wup_ref[:, cs], preferred_element_type=F32)
        ext_ref[0:pad, :] = carry_ref[:, cs]
        ext_ref[pad:pad + tb, :] = u
        carry_ref[:, cs] = u[tb - pad:tb, :]
        c = cb_ref[:, cs] + cw_ref[CONV_WIDTH - 1:CONV_WIDTH, cs] * u
        for j in range(halo):
            c = c + cw_ref[j:j + 1, cs] * ext_ref[pad - halo + j:pad - halo + j + tb, :]
        return c

    for ch in range(D_FF // FF_CHUNK):
        a = conv(ch * FF_CHUNK)
        g = conv(D_FF + ch * FF_CHUNK)
        act = a * (1.0 / (1.0 + jnp.exp(-a))) * g
        act_ref[:, ch * FF_CHUNK:(ch + 1) * FF_CHUNK] = act.astype(BF16)
    out_ref[...] = x + jnp.dot(act_ref[...], wdown_ref[...], preferred_element_type=F32)


def _conv_ffn(x2, gain, w_up, conv_w, conv_b, w_down, seq):
    t = x2.shape[0]
    tb = TOKEN_BLOCK
    return pl.pallas_call(
        functools.partial(_conv_ffn_kernel, per_seq=seq // tb),
        grid=(t // tb,),
        in_specs=[pl.BlockSpec((tb, D_MODEL), lambda i: (i, 0)),
                  _const_spec((1, D_MODEL)),
                  _const_spec((D_MODEL, 2 * D_FF)),
                  _const_spec((CONV_WIDTH, 2 * D_FF)),
                  _const_spec((1, 2 * D_FF)),
                  _const_spec((D_FF, D_MODEL))],
        out_specs=pl.BlockSpec((tb, D_MODEL), lambda i: (i, 0)),
        out_shape=jax.ShapeDtypeStruct((t, D_MODEL), F32),
        scratch_shapes=[pltpu.VMEM((8, 2 * D_FF), F32),
                        pltpu.VMEM((tb + 8, FF_CHUNK), F32),
                        pltpu.VMEM((tb, D_FF), BF16)],
        compiler_params=pltpu.CompilerParams(dimension_semantics=("arbitrary",),
                                             vmem_limit_bytes=VMEM_LIMIT),
        name="conv_ffn",
    )(x2, gain, w_up, conv_w, conv_b, w_down)


def _tables():
    lane = np.arange(A_SLAB)
    bd64 = jnp.asarray(lane[:, None] // HEAD_DIM == lane[None, :] // HEAD_DIM, BF16)
    bd128 = jnp.asarray(lane[:, None] // M_HEAD_DIM == lane[None, :] // M_HEAD_DIM, BF16)
    src = np.arange(B_KV_COLS)
    dst = np.arange(B_Q_COLS)
    expand = jnp.asarray(src[:, None] == (dst[None, :] // A_SLAB) * HEAD_DIM + dst[None, :] % HEAD_DIM, BF16)
    freqs = jnp.exp(jnp.arange(ROPE_HALF, dtype=F32) * (-2.0 * math.log(ROPE_THETA) / ROPE_DIM))
    in_head = jnp.arange(LANES) % HEAD_DIM
    freq_row = jnp.where(in_head < ROPE_DIM, freqs[in_head % ROPE_HALF], 0.0).reshape(1, LANES)
    return bd64, bd128, expand, freq_row


def _layer(x2, mem, pos_b, batch, seq, p):
    bd64, bd128, expand, freq_row = _tables()
    qk_scale = HEAD_DIM ** -0.5
    ones = jnp.ones((A_SLAB,), F32)
    colgain = jnp.concatenate(
        [jnp.concatenate([jnp.tile(p["a_q_norm"][g], A_HEADS) * qk_scale,
                          jnp.tile(p["a_k_norm"][g], A_HEADS), ones]) for g in range(len(A_GROUPS))]
        + [jnp.tile(p["b_q_norm"], B_Q_HEADS) * qk_scale,
           jnp.tile(p["b_k_norm"], B_KV_HEADS), jnp.ones((B_KV_COLS,), F32),
           jnp.tile(p["m_q_norm"], M_HEADS) * (M_HEAD_DIM ** -0.5)]).reshape(1, IN_COLS)

    mk, mv = _mem_kv(mem, p["mem_norm"].reshape(1, D_MODEL), p["w_mem_kv"].astype(BF16),
                     p["m_k_norm"].reshape(1, M_HEAD_DIM))
    attn_gain = p["attn_norm"].reshape(1, D_MODEL)
    (qa0, ka0, va0, qa1, ka1, va1, qa2, ka2, va2, qb, kb, vb, mq) = _in_proj(
        x2, pos_b, attn_gain, p["w_in"].astype(BF16), colgain, freq_row, bd64, bd128, expand)

    def per_batch(t):
        return t.reshape(batch, t.shape[0] // batch, t.shape[-1])

    def flat(t):
        return t.reshape(t.shape[0] * t.shape[1], t.shape[2])

    oa, la = [], []
    cfg = {1: dict(rows=512, slabs=1), 4: dict(rows=seq // 4, slabs=1), 16: dict(rows=seq // 16, slabs=4)}
    for (window, dil), (q, k, v) in zip(A_GROUPS, ((qa0, ka0, va0), (qa1, ka1, va1), (qa2, ka2, va2))):
        assert window // dil == BLOCK
        o, lse = _band_attn(per_batch(q), per_batch(k), per_batch(v), name=f"band_attn_d{dil}",
                            prev_off=0, **cfg[dil])
        oa.append(flat(o))
        la.append(flat(lse))
    (ob,) = _band_attn(per_batch(qb), per_batch(kb), per_batch(vb), name="band_attn_swa", rows=512,
                       slabs=B_KV_HEADS, prev_off=BLOCK - (B_WINDOW - 1), sinks=p["b_sinks"],
                       with_lse=False)
    ob = flat(ob)

    x2 = _mix(x2, attn_gain, oa, la, ob, mq, mk, mv, p["w_gate"].astype(BF16),
              p["b_gate"].reshape(1, N_BRANCH * D_MODEL), p["w_o_a"].astype(BF16),
              p["w_o_b"].astype(BF16), p["w_o_m"].astype(BF16), p["w_out"].astype(BF16), seq)
    return _conv_ffn(x2, p["ffn_norm"].reshape(1, D_MODEL), p["w_up"].astype(BF16), p["conv_w"],
                     p["conv_b"].reshape(1, 2 * D_FF), p["w_down"].astype(BF16), seq)


def kernel(x, mem, positions, attn_norm, w_in, a_q_norm, a_k_norm, b_q_norm, b_k_norm, b_sinks,
           mem_norm, w_mem_kv, m_q_norm, m_k_norm, w_o_a, w_o_b, w_o_m, w_gate, b_gate, w_out,
           ffn_norm, w_up, conv_w, conv_b, w_down):
    batch, seq, _ = x.shape
    params = dict(attn_norm=attn_norm, w_in=w_in, a_q_norm=a_q_norm, a_k_norm=a_k_norm,
                  b_q_norm=b_q_norm, b_k_norm=b_k_norm, b_sinks=b_sinks, mem_norm=mem_norm,
                  w_mem_kv=w_mem_kv, m_q_norm=m_q_norm, m_k_norm=m_k_norm, w_o_a=w_o_a, w_o_b=w_o_b,
                  w_o_m=w_o_m, w_gate=w_gate, b_gate=b_gate, w_out=w_out, ffn_norm=ffn_norm,
                  w_up=w_up, conv_w=conv_w, conv_b=conv_b, w_down=w_down)
    pos_b = jnp.broadcast_to(positions.astype(F32).reshape(batch * seq, 1), (batch * seq, LANES))
    x2 = x.reshape(batch * seq, D_MODEL)
    for layer in range(attn_norm.shape[0]):
        x2 = _layer(x2, mem, pos_b, batch, seq, {k: v[layer] for k, v in params.items()})
    return x2.reshape(batch, seq, D_MODEL)
```

```python
import functools
import math

import numpy as np

import jax
import jax.numpy as jnp
from jax import lax
from jax.experimental import pallas as pl
from jax.experimental.pallas import tpu as pltpu

D_MODEL = 1024
HEAD_DIM = 64
A_GROUPS = ((128, 1), (512, 4), (2048, 16))
A_HEADS = 4
A_SLAB = A_HEADS * HEAD_DIM
A_QKV_COLS = len(A_GROUPS) * 3 * A_SLAB
B_Q_HEADS = 8
B_KV_HEADS = 2
B_GROUP = B_Q_HEADS // B_KV_HEADS
B_WINDOW = 128
B_Q_COLS = B_Q_HEADS * HEAD_DIM
B_KV_COLS = B_KV_HEADS * HEAD_DIM
M_HEADS = 4
M_HEAD_DIM = 128
M_Q_COLS = M_HEADS * M_HEAD_DIM
B_Q_OFF = A_QKV_COLS
B_K_OFF = B_Q_OFF + B_Q_COLS
B_V_OFF = B_K_OFF + B_KV_COLS
M_Q_OFF = B_V_OFF + B_KV_COLS
IN_COLS = M_Q_OFF + M_Q_COLS
N_BRANCH = 3
D_FF = 2816
CONV_WIDTH = 3
ROPE_THETA = 500000.0
ROPE_DIM = HEAD_DIM // 4
ROPE_HALF = ROPE_DIM // 2
BLOCK = 128
EPS = 1e-6
NEG = -1e30

LANES = 128
TOKEN_BLOCK = 512
ROW_CHUNK = 64
FF_CHUNK = 256
VMEM_LIMIT = 56 * 1024 * 1024

F32 = jnp.float32
BF16 = jnp.bfloat16


def _const_spec(shape):
    return pl.BlockSpec(shape, lambda *_: (0,) * len(shape))


def _rms_rows(x, gain):
    ms = jnp.mean(x * x, axis=-1, keepdims=True)
    return x * lax.rsqrt(ms + EPS) * gain


def _mem_kv_kernel(mem_ref, gain_ref, w_ref, kgain_ref, mk_ref, mv_ref):
    hm = _rms_rows(mem_ref[...], gain_ref[...]).astype(BF16)
    kv = jnp.dot(hm, w_ref[...], preferred_element_type=F32)
    ks = []
    for h in range(M_HEADS):
        kh = kv[:, h * M_HEAD_DIM:(h + 1) * M_HEAD_DIM]
        ks.append(_rms_rows(kh, kgain_ref[...]))
    mk_ref[...] = jnp.concatenate(ks, axis=1).astype(BF16)
    mv_ref[...] = kv[:, M_Q_COLS:].astype(BF16)


def _mem_kv(mem, mem_gain, w_kv, k_gain):
    b, m, _ = mem.shape
    return pl.pallas_call(
        _mem_kv_kernel,
        grid=(b,),
        in_specs=[pl.BlockSpec((None, m, D_MODEL), lambda i: (i, 0, 0)),
                  _const_spec((1, D_MODEL)),
                  _const_spec((D_MODEL, 2 * M_Q_COLS)),
                  _const_spec((1, M_HEAD_DIM))],
        out_specs=[pl.BlockSpec((None, m, M_Q_COLS), lambda i: (i, 0, 0)),
                   pl.BlockSpec((None, m, M_Q_COLS), lambda i: (i, 0, 0))],
        out_shape=[jax.ShapeDtypeStruct((b, m, M_Q_COLS), BF16)] * 2,
        compiler_params=pltpu.CompilerParams(dimension_semantics=("arbitrary",),
                                             vmem_limit_bytes=VMEM_LIMIT),
        name="mem_kv",
    )(mem, mem_gain, w_kv, k_gain)


def _in_proj_kernel(x_ref, pos_ref, gain_ref, w_ref, colgain_ref, freq_ref, spread_ref, one_ref,
                    bd64_ref, bd128_ref, expand_ref, *refs):
    (qa0, ka0, va0, qa1, ka1, va1, qa2, ka2, va2, qb_ref, kb_ref, vb_ref, mq_ref) = refs[:13]
    proj_ref, h_ref, cos_ref, sin_lo_ref, sin_hi_ref, kvb_ref = refs[13:19]
    stage_refs = list(refs[19:])
    tb = x_ref.shape[0]
    chunks = [slice(r, r + ROW_CHUNK) for r in range(0, tb, ROW_CHUNK)]

    @pl.when(pl.program_id(0) == 0)
    def _():
        proj_ref[...] = jnp.zeros(proj_ref.shape, F32)

    for rows in chunks:
        h_ref[rows, :] = _rms_rows(x_ref[rows, :], gain_ref[...]).astype(BF16)

    ang = freq_ref[...] * pos_ref[...]
    trig = jnp.concatenate([jnp.cos(ang), jnp.sin(ang),
                            jnp.zeros((LANES - ROPE_DIM, tb), F32)], axis=0).T
    tables = jnp.zeros((tb, 3 * LANES), F32)
    for _ in range(3):
        part = trig.astype(BF16)
        tables = tables + jnp.dot(part, spread_ref[...], preferred_element_type=F32)
        trig = trig - part.astype(F32)
    cos_ref[...] = tables[:, 0:LANES] + one_ref[...]
    sin_lo_ref[...] = tables[:, LANES:2 * LANES]
    sin_hi_ref[...] = tables[:, 2 * LANES:3 * LANES]

    def proj(c0, width):
        tile, off = divmod(c0, A_SLAB)
        assert off + width <= A_SLAB and off % LANES == 0
        t = proj_ref[tile, :, off:off + width]
        proj_ref[tile, :, off:off + width] = jnp.dot(h_ref[...], w_ref[:, c0:c0 + width],
                                                     preferred_element_type=F32)
        return t

    def rope(y, rows):
        parts = []
        for c in range(y.shape[1] // LANES):
            yc = y[:, c * LANES:(c + 1) * LANES]
            parts.append(yc * cos_ref[rows, :]
                         + pltpu.roll(yc, LANES - ROPE_HALF, 1) * sin_lo_ref[rows, :]
                         + pltpu.roll(yc, ROPE_HALF, 1) * sin_hi_ref[rows, :])
        return parts[0] if len(parts) == 1 else jnp.concatenate(parts, axis=1)

    def finish(t, emit, c0, *, bd=None, dim=None, rotary=False):
        if bd is not None:
            ss = jnp.dot((t * t).astype(BF16), bd, preferred_element_type=F32)
            gain = colgain_ref[:, c0:c0 + t.shape[1]]
        for rows in chunks:
            y = t[rows]
            if bd is not None:
                y = y * lax.rsqrt(ss[rows] * (1.0 / dim) + EPS) * gain
            if rotary:
                y = rope(y, rows)
            emit(rows, y)

    def to_ref(out_ref, col0=0):
        def emit(rows, y):
            out_ref[rows, col0:col0 + y.shape[1]] = y.astype(BF16)
        return emit

    def store_tile(out_ref, t, dil, c0, **kw):
        if dil == 1:
            finish(t, to_ref(out_ref), c0, **kw)
            return
        stage_ref = stage_refs.pop()

        def emit(rows, y):
            for s in range(A_SLAB // LANES):
                stage_ref[s, rows, :] = y[:, s * LANES:(s + 1) * LANES]
        finish(t, emit, c0, **kw)
        for r in range(dil):
            for s in range(A_SLAB // LANES):
                col = r * A_SLAB + s * LANES
                out_ref[:, col:col + LANES] = stage_ref[s, pl.ds(r, tb // dil, stride=dil), :].astype(BF16)

    bd64 = bd64_ref[...]
    qk = dict(bd=bd64, dim=HEAD_DIM, rotary=True)
    for g, (q_ref, k_ref, v_ref) in enumerate(((qa0, ka0, va0), (qa1, ka1, va1), (qa2, ka2, va2))):
        c0 = g * 3 * A_SLAB
        dil = A_GROUPS[g][1]
        store_tile(q_ref, proj(c0, A_SLAB), dil, c0, **qk)
        store_tile(k_ref, proj(c0 + A_SLAB, A_SLAB), dil, c0 + A_SLAB, **qk)
        store_tile(v_ref, proj(c0 + 2 * A_SLAB, A_SLAB), dil, c0 + 2 * A_SLAB)

    for s in range(B_Q_COLS // A_SLAB):
        c0 = B_Q_OFF + s * A_SLAB
        finish(proj(c0, A_SLAB), to_ref(qb_ref, s * A_SLAB), c0, **qk)

    finish(proj(B_K_OFF, B_KV_COLS), to_ref(kvb_ref), B_K_OFF,
           bd=bd64_ref[0:B_KV_COLS, 0:B_KV_COLS], dim=HEAD_DIM, rotary=True)
    kb_ref[...] = jnp.dot(kvb_ref[...], expand_ref[...], preferred_element_type=F32).astype(BF16)
    vb = proj(B_V_OFF, B_KV_COLS).astype(BF16)
    vb_ref[...] = jnp.dot(vb, expand_ref[...], preferred_element_type=F32).astype(BF16)

    bd128 = bd128_ref[...]
    for s in range(M_Q_COLS // A_SLAB):
        c0 = M_Q_OFF + s * A_SLAB
        finish(proj(c0, A_SLAB), to_ref(mq_ref, s * A_SLAB), c0, bd=bd128, dim=M_HEAD_DIM)


def _in_proj(x2, pos_rows, attn_gain, w_in, colgain, freq_col, spread, one_row, bd64, bd128, expand):
    t = x2.shape[0]
    tb = TOKEN_BLOCK
    dils = [dil for _, dil in A_GROUPS for _ in range(3)] + [1, 1, 1, 1]
    widths = [A_SLAB] * 9 + [B_Q_COLS, B_Q_COLS, B_Q_COLS, M_Q_COLS]
    n_staged = sum(d > 1 for d in dils)
    n = t // tb
    done = lambda i: (jnp.maximum(i - 1, 0), 0)
    return pl.pallas_call(
        _in_proj_kernel,
        grid=(n + 1,),
        in_specs=[pl.BlockSpec((tb, D_MODEL), lambda i: (jnp.minimum(i, n - 1), 0)),
                  pl.BlockSpec((None, 1, tb), lambda i: (jnp.maximum(i - 1, 0), 0, 0)),
                  _const_spec((1, D_MODEL)),
                  _const_spec((D_MODEL, IN_COLS)),
                  _const_spec((1, IN_COLS)),
                  _const_spec((ROPE_HALF, 1)),
                  _const_spec((LANES, 3 * LANES)),
                  _const_spec((1, LANES)),
                  _const_spec((A_SLAB, A_SLAB)),
                  _const_spec((A_SLAB, A_SLAB)),
                  _const_spec((B_KV_COLS, B_Q_COLS))],
        out_specs=[pl.BlockSpec((tb // d, w * d), done) for w, d in zip(widths, dils)],
        out_shape=[jax.ShapeDtypeStruct((t // d, w * d), BF16) for w, d in zip(widths, dils)],
        scratch_shapes=[pltpu.VMEM((IN_COLS // A_SLAB, tb, A_SLAB), F32)]
                       + [pltpu.VMEM((tb, D_MODEL), BF16)] + [pltpu.VMEM((tb, LANES), F32)] * 3
                       + [pltpu.VMEM((tb, B_KV_COLS), BF16)]
                       + [pltpu.VMEM((A_SLAB // LANES, tb, LANES), F32)] * n_staged,
        compiler_params=pltpu.CompilerParams(dimension_semantics=("arbitrary",),
                                             vmem_limit_bytes=VMEM_LIMIT),
        name="in_proj",
    )(x2, pos_rows, attn_gain, w_in, colgain, freq_col, spread, one_row, bd64, bd128, expand)


def _band_attn_kernel(*refs, rows, slabs, prev_off, has_sink, with_lse):
    q_ref, kc_ref, kp_ref, vc_ref, vp_ref = refs[:5]
    pos = 5
    sink_ref = None
    if has_sink:
        sink_ref = refs[pos]
        pos += 1
    o_ref = refs[pos]
    lse_ref = refs[pos + 1] if with_lse else None

    first_step = pl.program_id(2) == 0
    row = lax.broadcasted_iota(jnp.int32, (BLOCK, 2 * BLOCK), 0)
    col = lax.broadcasted_iota(jnp.int32, (BLOCK, 2 * BLOCK), 1)
    bias = jnp.where(col < BLOCK,
                     jnp.where(col >= row + prev_off, 0.0, NEG),
                     jnp.where(col - BLOCK <= row, 0.0, NEG)).astype(F32)
    bias_first = jnp.where(jnp.logical_and(first_step, col < BLOCK), NEG, bias)
    head_of_lane = lax.broadcasted_iota(jnp.int32, (1, A_SLAB), 1) // HEAD_DIM
    key_row = lax.broadcasted_iota(jnp.int32, (2 * BLOCK, A_SLAB), 0)
    nblk = rows // BLOCK

    for w in range(slabs):
        cs = slice(w * A_SLAB, (w + 1) * A_SLAB)
        for jb in range(nblk):
            q = q_ref[jb * BLOCK:(jb + 1) * BLOCK, cs]
            if jb == 0:
                k2 = jnp.concatenate([kp_ref[:, cs], kc_ref[0:BLOCK, cs]], axis=0)
                v2 = jnp.concatenate([vp_ref[:, cs], vc_ref[0:BLOCK, cs]], axis=0)
                b = bias_first
            else:
                k2 = kc_ref[(jb - 1) * BLOCK:(jb + 1) * BLOCK, cs]
                v2 = vc_ref[(jb - 1) * BLOCK:(jb + 1) * BLOCK, cs]
                b = bias
            if has_sink:
                k2 = jnp.where(key_row == 0, jnp.zeros_like(k2), k2)
                v2 = jnp.where(key_row == 0, jnp.zeros_like(v2), v2)
            qs = jnp.concatenate(
                [jnp.where(head_of_lane == hh, q, jnp.zeros_like(q)) for hh in range(A_HEADS)], axis=0)
            s = lax.dot_general(qs, k2, (((1,), (1,)), ((), ())), preferred_element_type=F32)
            ps, ms, ls = [], [], []
            for hh in range(A_HEADS):
                bh = jnp.where(col == 0, sink_ref[w * A_HEADS + hh], b) if has_sink else b
                sh = s[hh * BLOCK:(hh + 1) * BLOCK] + bh
                m = jnp.max(sh, axis=-1, keepdims=True)
                p = jnp.exp(sh - m)
                ms.append(m)
                ls.append(jnp.sum(p, axis=-1, keepdims=True))
                ps.append(p.astype(BF16))
            ost = jnp.dot(jnp.concatenate(ps, axis=0), v2, preferred_element_type=F32)
            o = jnp.zeros((BLOCK, A_SLAB), F32)
            lse = jnp.zeros((BLOCK, A_SLAB), F32)
            for hh in range(A_HEADS):
                o = jnp.where(head_of_lane == hh, ost[hh * BLOCK:(hh + 1) * BLOCK] * (1.0 / ls[hh]), o)
                if with_lse:
                    lse = jnp.where(head_of_lane == hh, ms[hh] + jnp.log(ls[hh]), lse)
            o_ref[jb * BLOCK:(jb + 1) * BLOCK, cs] = o.astype(o_ref.dtype)
            if with_lse:
                lse_ref[jb * BLOCK:(jb + 1) * BLOCK, cs] = lse


def _band_attn(q, k, v, *, name, rows, slabs, prev_off, sinks=None, with_lse=True):
    b, length, width = q.shape
    assert sinks is None or prev_off >= 1
    n_col = width // (slabs * A_SLAB)
    n_row = length // rows
    blk_per_step = rows // BLOCK
    cur = pl.BlockSpec((None, rows, slabs * A_SLAB), lambda i, r, j: (i, j, r))
    prev = pl.BlockSpec((None, BLOCK, slabs * A_SLAB),
                        lambda i, r, j: (i, jnp.maximum(j * blk_per_step - 1, 0), r))
    in_specs = [cur, cur, prev, cur, prev]
    args = [q, k, k, v, v]
    if sinks is not None:
        in_specs.append(pl.BlockSpec(memory_space=pltpu.SMEM))
        args.append(sinks)
    out_specs = [cur]
    out_shape = [jax.ShapeDtypeStruct((b, length, width), BF16)]
    if with_lse:
        out_specs.append(cur)
        out_shape.append(jax.ShapeDtypeStruct((b, length, width), F32))
    return pl.pallas_call(
        functools.partial(_band_attn_kernel, rows=rows, slabs=slabs, prev_off=prev_off,
                          has_sink=sinks is not None, with_lse=with_lse),
        grid=(b, n_col, n_row),
        in_specs=in_specs,
        out_specs=out_specs,
        out_shape=out_shape,
        compiler_params=pltpu.CompilerParams(dimension_semantics=("arbitrary",) * 3,
                                             vmem_limit_bytes=VMEM_LIMIT),
        name=name,
    )(*args)


def _mix_kernel(x_ref, gain_ref, oa0, la0, oa1, la1, oa2, la2, ob_ref, mq_ref, mk_ref, mv_ref,
                wg_ref, bg_ref, woa_ref, wob_ref, wom_ref, wout_ref, out_ref, *stage_refs):
    x = x_ref[...]
    h = _rms_rows(x, gain_ref[...]).astype(BF16)
    stage_refs = list(stage_refs)

    def token_major(blk_ref, dil):
        if dil == 1:
            return blk_ref[...].astype(F32)
        stage_ref = stage_refs.pop()
        rows = blk_ref.shape[0]
        for r in range(dil):
            for s in range(A_SLAB // LANES):
                c0 = r * A_SLAB + s * LANES
                stage_ref[s, pl.ds(r, rows, stride=dil), :] = blk_ref[:, c0:c0 + LANES].astype(F32)
        return jnp.concatenate([stage_ref[s] for s in range(A_SLAB // LANES)], axis=1)

    dils = [dil for _, dil in A_GROUPS]
    l0, l1, l2 = (token_major(r, d) for r, d in zip((la0, la1, la2), dils))
    o0, o1, o2 = (token_major(r, d) for r, d in zip((oa0, oa1, oa2), dils))
    mx = jnp.maximum(jnp.maximum(l0, l1), l2)
    e0, e1, e2 = jnp.exp(l0 - mx), jnp.exp(l1 - mx), jnp.exp(l2 - mx)
    o_a = (e0 * o0 + e1 * o1 + e2 * o2) / (e0 + e1 + e2)

    o_m = []
    for hh in range(M_HEADS):
        cs = slice(hh * M_HEAD_DIM, (hh + 1) * M_HEAD_DIM)
        s = lax.dot_general(mq_ref[:, cs], mk_ref[:, cs], (((1,), (1,)), ((), ())),
                            preferred_element_type=F32)
        p = jnp.exp(s - jnp.max(s, axis=-1, keepdims=True))
        l = jnp.sum(p, axis=-1, keepdims=True)
        o_m.append(jnp.dot(p.astype(BF16), mv_ref[:, cs], preferred_element_type=F32) / l)
    o_m = jnp.concatenate(o_m, axis=1)

    merged = jnp.zeros(x.shape, F32)
    for i, (o, w_ref) in enumerate(((o_a, woa_ref), (ob_ref[...], wob_ref), (o_m, wom_ref))):
        cs = slice(i * D_MODEL, (i + 1) * D_MODEL)
        z = jnp.dot(h, wg_ref[:, cs], preferred_element_type=F32) + bg_ref[:, cs]
        gate = 1.0 / (1.0 + jnp.exp(-z))
        merged = merged + gate * jnp.dot(o.astype(BF16), w_ref[...], preferred_element_type=F32)
    out_ref[...] = x + jnp.dot(merged.astype(BF16), wout_ref[...], preferred_element_type=F32)


def _mix(x2, attn_gain, oa, la, ob, mq, mk, mv, w_gate, b_gate, w_o_a, w_o_b, w_o_m, w_out, seq):
    t = x2.shape[0]
    tb = TOKEN_BLOCK
    mem_len = mk.shape[1]
    per_seq = seq // tb
    tok = lambda w, d=1: pl.BlockSpec((tb // d, w * d), lambda i: (i, 0))
    mem = pl.BlockSpec((None, mem_len, M_Q_COLS), lambda i: (i // per_seq, 0, 0))
    dils = [dil for _, dil in A_GROUPS]
    return pl.pallas_call(
        _mix_kernel,
        grid=(t // tb,),
        in_specs=[tok(D_MODEL), _const_spec((1, D_MODEL))]
                 + [tok(A_SLAB, d) for d in dils for _ in range(2)]
                 + [tok(B_Q_COLS), tok(M_Q_COLS), mem, mem,
                  _const_spec((D_MODEL, N_BRANCH * D_MODEL)), _const_spec((1, N_BRANCH * D_MODEL)),
                  _const_spec((A_SLAB, D_MODEL)), _const_spec((B_Q_COLS, D_MODEL)),
                  _const_spec((M_Q_COLS, D_MODEL)), _const_spec((D_MODEL, D_MODEL))],
        out_specs=tok(D_MODEL),
        out_shape=jax.ShapeDtypeStruct((t, D_MODEL), F32),
        scratch_shapes=[pltpu.VMEM((A_SLAB // LANES, tb, LANES), F32)] * (2 * sum(d > 1 for d in dils)),
        compiler_params=pltpu.CompilerParams(dimension_semantics=("arbitrary",),
                                             vmem_limit_bytes=VMEM_LIMIT),
        name="mix",
    )(x2, attn_gain, oa[0], la[0], oa[1], la[1], oa[2], la[2], ob, mq, mk, mv,
      w_gate, b_gate, w_o_a, w_o_b, w_o_m, w_out)


def _conv_ffn_kernel(x_ref, gain_ref, wup_ref, cw_ref, cb_ref, wdown_ref, out_ref,
                     carry_ref, ext_ref, act_ref, *, per_seq):
    tb = x_ref.shape[0]
    halo = CONV_WIDTH - 1
    pad = 8

    @pl.when(pl.program_id(0) % per_seq == 0)
    def _():
        carry_ref[...] = jnp.zeros(carry_ref.shape, F32)

    x = x_ref[...]
    h = _rms_rows(x, gain_ref[...]).astype(BF16)

    def conv(c0):
        cs = slice(c0, c0 + FF_CHUNK)
        u = jnp.dot(h, wup_ref[:, cs], preferred_element_type=F32)
        ext_ref[0:pad, :] = carry_ref[:, cs]
        ext_ref[pad:pad + tb, :] = u
        carry_ref[:, cs] = u[tb - pad:tb, :]
        c = cb_ref[:, cs] + cw_ref[CONV_WIDTH - 1:CONV_WIDTH, cs] * u
        for j in range(halo):
            c = c + cw_ref[j:j + 1, cs] * ext_ref[pad - halo + j:pad - halo + j + tb, :]
        return c

    for ch in range(D_FF // FF_CHUNK):
        a = conv(ch * FF_CHUNK)
        g = conv(D_FF + ch * FF_CHUNK)
        act = a * (1.0 / (1.0 + jnp.exp(-a))) * g
        act_ref[:, ch * FF_CHUNK:(ch + 1) * FF_CHUNK] = act.astype(BF16)
    out_ref[...] = x + jnp.dot(act_ref[...], wdown_ref[...], preferred_element_type=F32)


def _conv_ffn(x2, gain, w_up, conv_w, conv_b, w_down, seq):
    t = x2.shape[0]
    tb = TOKEN_BLOCK
    return pl.pallas_call(
        functools.partial(_conv_ffn_kernel, per_seq=seq // tb),
        grid=(t // tb,),
        in_specs=[pl.BlockSpec((tb, D_MODEL), lambda i: (i, 0)),
                  _const_spec((1, D_MODEL)),
                  _const_spec((D_MODEL, 2 * D_FF)),
                  _const_spec((CONV_WIDTH, 2 * D_FF)),
                  _const_spec((1, 2 * D_FF)),
                  _const_spec((D_FF, D_MODEL))],
        out_specs=pl.BlockSpec((tb, D_MODEL), lambda i: (i, 0)),
        out_shape=jax.ShapeDtypeStruct((t, D_MODEL), F32),
        scratch_shapes=[pltpu.VMEM((8, 2 * D_FF), F32),
                        pltpu.VMEM((tb + 8, FF_CHUNK), F32),
                        pltpu.VMEM((tb, D_FF), BF16)],
        compiler_params=pltpu.CompilerParams(dimension_semantics=("arbitrary",),
                                             vmem_limit_bytes=VMEM_LIMIT),
        name="conv_ffn",
    )(x2, gain, w_up, conv_w, conv_b, w_down)


def _tables():
    lane = np.arange(A_SLAB)
    bd64 = jnp.asarray(lane[:, None] // HEAD_DIM == lane[None, :] // HEAD_DIM, BF16)
    bd128 = jnp.asarray(lane[:, None] // M_HEAD_DIM == lane[None, :] // M_HEAD_DIM, BF16)
    src = np.arange(B_KV_COLS)
    dst = np.arange(B_Q_COLS)
    expand = jnp.asarray(src[:, None] == (dst[None, :] // A_SLAB) * HEAD_DIM + dst[None, :] % HEAD_DIM, BF16)
    freq_col = jnp.exp(jnp.arange(ROPE_HALF, dtype=F32) * (-2.0 * math.log(ROPE_THETA) / ROPE_DIM))
    freq_col = freq_col.reshape(ROPE_HALF, 1)
    in_head = np.arange(LANES) % HEAD_DIM
    j = np.arange(LANES)[:, None]
    rot = in_head[None, :] < ROPE_DIM
    cos_sel = rot & (j == in_head[None, :] % ROPE_HALF)
    sin_sel = rot & (j == ROPE_HALF + in_head[None, :] % ROPE_HALF)
    lo = in_head[None, :] < ROPE_HALF
    spread = np.concatenate([cos_sel.astype(np.float32), -(sin_sel & lo).astype(np.float32),
                             (sin_sel & ~lo).astype(np.float32)], axis=1)
    one_row = (in_head >= ROPE_DIM).astype(np.float32).reshape(1, LANES)
    return bd64, bd128, expand, freq_col, jnp.asarray(spread, BF16), jnp.asarray(one_row)


def _layer(x2, mem, pos_rows, batch, seq, p):
    bd64, bd128, expand, freq_col, spread, one_row = _tables()
    qk_scale = HEAD_DIM ** -0.5
    ones = jnp.ones((A_SLAB,), F32)
    colgain = jnp.concatenate(
        [jnp.concatenate([jnp.tile(p["a_q_norm"][g], A_HEADS) * qk_scale,
                          jnp.tile(p["a_k_norm"][g], A_HEADS), ones]) for g in range(len(A_GROUPS))]
        + [jnp.tile(p["b_q_norm"], B_Q_HEADS) * qk_scale,
           jnp.tile(p["b_k_norm"], B_KV_HEADS), jnp.ones((B_KV_COLS,), F32),
           jnp.tile(p["m_q_norm"], M_HEADS) * (M_HEAD_DIM ** -0.5)]).reshape(1, IN_COLS)

    mk, mv = _mem_kv(mem, p["mem_norm"].reshape(1, D_MODEL), p["w_mem_kv"].astype(BF16),
                     p["m_k_norm"].reshape(1, M_HEAD_DIM))
    attn_gain = p["attn_norm"].reshape(1, D_MODEL)
    (qa0, ka0, va0, qa1, ka1, va1, qa2, ka2, va2, qb, kb, vb, mq) = _in_proj(
        x2, pos_rows, attn_gain, p["w_in"].astype(BF16), colgain, freq_col, spread, one_row,
        bd64, bd128, expand)

    def per_batch(t):
        return t.reshape(batch, t.shape[0] // batch, t.shape[-1])

    def flat(t):
        return t.reshape(t.shape[0] * t.shape[1], t.shape[2])

    oa, la = [], []
    cfg = {1: dict(rows=512, slabs=1), 4: dict(rows=seq // 4, slabs=1), 16: dict(rows=seq // 16, slabs=4)}
    for (window, dil), (q, k, v) in zip(A_GROUPS, ((qa0, ka0, va0), (qa1, ka1, va1), (qa2, ka2, va2))):
        assert window // dil == BLOCK
        o, lse = _band_attn(per_batch(q), per_batch(k), per_batch(v), name=f"band_attn_d{dil}",
                            prev_off=0, **cfg[dil])
        oa.append(flat(o))
        la.append(flat(lse))
    (ob,) = _band_attn(per_batch(qb), per_batch(kb), per_batch(vb), name="band_attn_swa", rows=512,
                       slabs=B_KV_HEADS, prev_off=BLOCK - (B_WINDOW - 1), sinks=p["b_sinks"],
                       with_lse=False)
    ob = flat(ob)

    x2 = _mix(x2, attn_gain, oa, la, ob, mq, mk, mv, p["w_gate"].astype(BF16),
              p["b_gate"].reshape(1, N_BRANCH * D_MODEL), p["w_o_a"].astype(BF16),
              p["w_o_b"].astype(BF16), p["w_o_m"].astype(BF16), p["w_out"].astype(BF16), seq)
    return _conv_ffn(x2, p["ffn_norm"].reshape(1, D_MODEL), p["w_up"].astype(BF16), p["conv_w"],
                     p["conv_b"].reshape(1, 2 * D_FF), p["w_down"].astype(BF16), seq)


def kernel(x, mem, positions, attn_norm, w_in, a_q_norm, a_k_norm, b_q_norm, b_k_norm, b_sinks,
           mem_norm, w_mem_kv, m_q_norm, m_k_norm, w_o_a, w_o_b, w_o_m, w_gate, b_gate, w_out,
           ffn_norm, w_up, conv_w, conv_b, w_down):
    batch, seq, _ = x.shape
    params = dict(attn_norm=attn_norm, w_in=w_in, a_q_norm=a_q_norm, a_k_norm=a_k_norm,
                  b_q_norm=b_q_norm, b_k_norm=b_k_norm, b_sinks=b_sinks, mem_norm=mem_norm,
                  w_mem_kv=w_mem_kv, m_q_norm=m_q_norm, m_k_norm=m_k_norm, w_o_a=w_o_a, w_o_b=w_o_b,
                  w_o_m=w_o_m, w_gate=w_gate, b_gate=b_gate, w_out=w_out, ffn_norm=ffn_norm,
                  w_up=w_up, conv_w=conv_w, conv_b=conv_b, w_down=w_down)
    pos_rows = positions.astype(F32).reshape(batch * seq // TOKEN_BLOCK, 1, TOKEN_BLOCK)
    x2 = x.reshape(batch * seq, D_MODEL)
    for layer in range(attn_norm.shape[0]):
        x2 = _layer(x2, mem, pos_rows, batch, seq, {k: v[layer] for k, v in params.items()})
    return x2.reshape(batch, seq, D_MODEL)
```

```python
import functools
import math

import numpy as np

import jax
import jax.numpy as jnp
from jax import lax
from jax.experimental import pallas as pl
from jax.experimental.pallas import tpu as pltpu

D_MODEL = 1024
HEAD_DIM = 64
A_GROUPS = ((128, 1), (512, 4), (2048, 16))
A_HEADS = 4
A_SLAB = A_HEADS * HEAD_DIM
A_QKV_COLS = len(A_GROUPS) * 3 * A_SLAB
B_Q_HEADS = 8
B_KV_HEADS = 2
B_GROUP = B_Q_HEADS // B_KV_HEADS
B_WINDOW = 128
B_Q_COLS = B_Q_HEADS * HEAD_DIM
B_KV_COLS = B_KV_HEADS * HEAD_DIM
M_HEADS = 4
M_HEAD_DIM = 128
M_Q_COLS = M_HEADS * M_HEAD_DIM
B_Q_OFF = A_QKV_COLS
B_K_OFF = B_Q_OFF + B_Q_COLS
B_V_OFF = B_K_OFF + B_KV_COLS
M_Q_OFF = B_V_OFF + B_KV_COLS
IN_COLS = M_Q_OFF + M_Q_COLS
N_BRANCH = 3
D_FF = 2816
CONV_WIDTH = 3
ROPE_THETA = 500000.0
ROPE_DIM = HEAD_DIM // 4
ROPE_HALF = ROPE_DIM // 2
BLOCK = 128
EPS = 1e-6
NEG = -1e30
LOG2E = math.log2(math.e)

LANES = 128
BF16_ROWS = 16
TOKEN_BLOCK = 512
ROW_CHUNK = 64
FF_CHUNK = 256
VMEM_LIMIT = 56 * 1024 * 1024

F32 = jnp.float32
BF16 = jnp.bfloat16


def _const_spec(shape):
    return pl.BlockSpec(shape, lambda *_: (0,) * len(shape), pipeline_mode=pl.Buffered(1))


def _rms_rows(x, gain):
    ms = jnp.mean(x * x, axis=-1, keepdims=True)
    return x * lax.rsqrt(ms + EPS) * gain


def _mem_kv_kernel(mem_ref, gain_ref, w_ref, kgain_ref, mk_ref, mv_ref):
    hm = _rms_rows(mem_ref[...], gain_ref[...]).astype(BF16)
    kv = jnp.dot(hm, w_ref[...], preferred_element_type=F32)
    ks = []
    for h in range(M_HEADS):
        kh = kv[:, h * M_HEAD_DIM:(h + 1) * M_HEAD_DIM]
        ks.append(_rms_rows(kh, kgain_ref[...]))
    mk_ref[...] = jnp.concatenate(ks, axis=1).astype(BF16)
    mv_ref[...] = kv[:, M_Q_COLS:].astype(BF16)


def _mem_kv(mem, mem_gain, w_kv, k_gain):
    b, m, _ = mem.shape
    return pl.pallas_call(
        _mem_kv_kernel,
        grid=(b,),
        in_specs=[pl.BlockSpec((None, m, D_MODEL), lambda i: (i, 0, 0)),
                  _const_spec((1, D_MODEL)),
                  _const_spec((D_MODEL, 2 * M_Q_COLS)),
                  _const_spec((1, M_HEAD_DIM))],
        out_specs=[pl.BlockSpec((None, m, M_Q_COLS), lambda i: (i, 0, 0)),
                   pl.BlockSpec((None, m, M_Q_COLS), lambda i: (i, 0, 0))],
        out_shape=[jax.ShapeDtypeStruct((b, m, M_Q_COLS), BF16)] * 2,
        compiler_params=pltpu.CompilerParams(dimension_semantics=("arbitrary",),
                                             vmem_limit_bytes=VMEM_LIMIT),
        name="mem_kv",
    )(mem, mem_gain, w_kv, k_gain)


def _in_proj_kernel(x_ref, pos_ref, gain_ref, w_ref, colgain_ref, freq_ref, spread_ref, one_ref,
                    bd64_ref, bd128_ref, expand_ref, *refs):
    (qa0, ka0, va0, qa1, ka1, va1, qa2, ka2, va2, qb_ref, kb_ref, vb_ref, mq_ref) = refs[:13]
    proj_ref, h_ref, cos_ref, sin_lo_ref, sin_hi_ref, kvb_ref = refs[13:19]
    stage_refs = list(refs[19:])
    tb = x_ref.shape[0]
    chunks = [slice(r, r + ROW_CHUNK) for r in range(0, tb, ROW_CHUNK)]

    @pl.when(pl.program_id(0) == 0)
    def _():
        proj_ref[...] = jnp.zeros(proj_ref.shape, F32)

    for rows in chunks:
        h_ref[rows, :] = _rms_rows(x_ref[rows, :], gain_ref[...]).astype(BF16)

    ang = freq_ref[...] * pos_ref[...]
    trig = jnp.concatenate([jnp.cos(ang), jnp.sin(ang),
                            jnp.zeros((LANES - ROPE_DIM, tb), F32)], axis=0).T
    tables = jnp.zeros((tb, 3 * LANES), F32)
    for _ in range(3):
        part = trig.astype(BF16)
        tables = tables + jnp.dot(part, spread_ref[...], preferred_element_type=F32)
        trig = trig - part.astype(F32)
    cos_ref[...] = tables[:, 0:LANES] + one_ref[...]
    sin_lo_ref[...] = tables[:, LANES:2 * LANES]
    sin_hi_ref[...] = tables[:, 2 * LANES:3 * LANES]

    def proj(c0, width):
        tile, off = divmod(c0, A_SLAB)
        assert off + width <= A_SLAB and off % LANES == 0
        t = proj_ref[tile, :, off:off + width]
        proj_ref[tile, :, off:off + width] = jnp.dot(h_ref[...], w_ref[:, c0:c0 + width],
                                                     preferred_element_type=F32)
        return t

    def rope(y, rows):
        parts = []
        for c in range(y.shape[1] // LANES):
            yc = y[:, c * LANES:(c + 1) * LANES]
            parts.append(yc * cos_ref[rows, :]
                         + pltpu.roll(yc, LANES - ROPE_HALF, 1) * sin_lo_ref[rows, :]
                         + pltpu.roll(yc, ROPE_HALF, 1) * sin_hi_ref[rows, :])
        return parts[0] if len(parts) == 1 else jnp.concatenate(parts, axis=1)

    def finish(t, emit, c0, *, bd=None, dim=None, rotary=False):
        if bd is not None:
            ss = jnp.dot((t * t).astype(BF16), bd, preferred_element_type=F32)
            gain = colgain_ref[:, c0:c0 + t.shape[1]]
        for rows in chunks:
            y = t[rows]
            if bd is not None:
                y = y * lax.rsqrt(ss[rows] * (1.0 / dim) + EPS) * gain
            if rotary:
                y = rope(y, rows)
            emit(rows, y)

    def to_ref(out_ref, col0=0):
        def emit(rows, y):
            out_ref[rows, col0:col0 + y.shape[1]] = y.astype(BF16)
        return emit

    def store_tile(out_ref, t, dil, c0, **kw):
        if dil == 1:
            finish(t, to_ref(out_ref), c0, **kw)
            return
        stage_ref = stage_refs.pop()

        def emit(rows, y):
            for s in range(A_SLAB // LANES):
                stage_ref[s, rows, :] = y[:, s * LANES:(s + 1) * LANES]
        finish(t, emit, c0, **kw)
        for r in range(dil):
            for s in range(A_SLAB // LANES):
                col = r * A_SLAB + s * LANES
                out_ref[:, col:col + LANES] = stage_ref[s, pl.ds(r, tb // dil, stride=dil), :].astype(BF16)

    bd64 = bd64_ref[...]
    qk = dict(bd=bd64, dim=HEAD_DIM, rotary=True)
    for g, (q_ref, k_ref, v_ref) in enumerate(((qa0, ka0, va0), (qa1, ka1, va1), (qa2, ka2, va2))):
        c0 = g * 3 * A_SLAB
        dil = A_GROUPS[g][1]
        store_tile(q_ref, proj(c0, A_SLAB), dil, c0, **qk)
        store_tile(k_ref, proj(c0 + A_SLAB, A_SLAB), dil, c0 + A_SLAB, **qk)
        store_tile(v_ref, proj(c0 + 2 * A_SLAB, A_SLAB), dil, c0 + 2 * A_SLAB)

    for s in range(B_Q_COLS // A_SLAB):
        c0 = B_Q_OFF + s * A_SLAB
        finish(proj(c0, A_SLAB), to_ref(qb_ref, s * A_SLAB), c0, **qk)

    finish(proj(B_K_OFF, B_KV_COLS), to_ref(kvb_ref), B_K_OFF,
           bd=bd64_ref[0:B_KV_COLS, 0:B_KV_COLS], dim=HEAD_DIM, rotary=True)
    kb_ref[...] = jnp.dot(kvb_ref[...], expand_ref[...], preferred_element_type=F32).astype(BF16)
    vb = proj(B_V_OFF, B_KV_COLS).astype(BF16)
    vb_ref[...] = jnp.dot(vb, expand_ref[...], preferred_element_type=F32).astype(BF16)

    bd128 = bd128_ref[...]
    for s in range(M_Q_COLS // A_SLAB):
        c0 = M_Q_OFF + s * A_SLAB
        finish(proj(c0, A_SLAB), to_ref(mq_ref, s * A_SLAB), c0, bd=bd128, dim=M_HEAD_DIM)


def _in_proj(x2, pos_rows, attn_gain, w_in, colgain, freq_col, spread, one_row, bd64, bd128, expand):
    t = x2.shape[0]
    tb = TOKEN_BLOCK
    dils = [dil for _, dil in A_GROUPS for _ in range(3)] + [1, 1, 1, 1]
    widths = [A_SLAB] * 9 + [B_Q_COLS, B_Q_COLS, B_Q_COLS, M_Q_COLS]
    n_staged = sum(d > 1 for d in dils)
    n = t // tb
    done = lambda i: (jnp.maximum(i - 1, 0), 0)
    return pl.pallas_call(
        _in_proj_kernel,
        grid=(n + 1,),
        in_specs=[pl.BlockSpec((tb, D_MODEL), lambda i: (jnp.minimum(i, n - 1), 0)),
                  pl.BlockSpec((None, 1, tb), lambda i: (jnp.maximum(i - 1, 0), 0, 0)),
                  _const_spec((1, D_MODEL)),
                  _const_spec((D_MODEL, IN_COLS)),
                  _const_spec((1, IN_COLS)),
                  _const_spec((ROPE_HALF, 1)),
                  _const_spec((LANES, 3 * LANES)),
                  _const_spec((1, LANES)),
                  _const_spec((A_SLAB, A_SLAB)),
                  _const_spec((A_SLAB, A_SLAB)),
                  _const_spec((B_KV_COLS, B_Q_COLS))],
        out_specs=[pl.BlockSpec((tb // d, w * d), done) for w, d in zip(widths, dils)],
        out_shape=[jax.ShapeDtypeStruct((t // d, w * d), BF16) for w, d in zip(widths, dils)],
        scratch_shapes=[pltpu.VMEM((IN_COLS // A_SLAB, tb, A_SLAB), F32)]
                       + [pltpu.VMEM((tb, D_MODEL), BF16)] + [pltpu.VMEM((tb, LANES), F32)] * 3
                       + [pltpu.VMEM((tb, B_KV_COLS), BF16)]
                       + [pltpu.VMEM((A_SLAB // LANES, tb, LANES), F32)] * n_staged,
        compiler_params=pltpu.CompilerParams(dimension_semantics=("arbitrary",),
                                             vmem_limit_bytes=VMEM_LIMIT),
        name="in_proj",
    )(x2, pos_rows, attn_gain, w_in, colgain, freq_col, spread, one_row, bd64, bd128, expand)


def _band_attn_kernel(*refs, rows, slabs, prev_off, has_sink, with_lse):
    q_ref, kc_ref, kp_ref, vc_ref, vp_ref = refs[:5]
    pos = 5
    sink_ref = None
    if has_sink:
        sink_ref = refs[pos]
        pos += 1
    o_ref = refs[pos]
    lse_ref = refs[pos + 1] if with_lse else None

    first_step = pl.program_id(2) == 0
    row = lax.broadcasted_iota(jnp.int32, (BLOCK, 2 * BLOCK), 0)
    col = lax.broadcasted_iota(jnp.int32, (BLOCK, 2 * BLOCK), 1)
    bias = jnp.where(col < BLOCK,
                     jnp.where(col >= row + prev_off, 0.0, NEG),
                     jnp.where(col - BLOCK <= row, 0.0, NEG)).astype(F32)
    bias_first = jnp.where(jnp.logical_and(first_step, col < BLOCK), NEG, bias)
    head_of_lane = lax.broadcasted_iota(jnp.int32, (1, A_SLAB), 1) // HEAD_DIM
    head_mask = [(head_of_lane == hh).astype(BF16) for hh in range(A_HEADS)]
    pack_row = lax.broadcasted_iota(jnp.int32, (BF16_ROWS, A_SLAB), 0)
    nblk = rows // BLOCK

    def drop_first_row(t):
        top = jnp.where(pack_row == 0, jnp.zeros((BF16_ROWS, A_SLAB), BF16), t[0:BF16_ROWS])
        return jnp.concatenate([top, t[BF16_ROWS:]], axis=0)

    for w in range(slabs):
        cs = slice(w * A_SLAB, (w + 1) * A_SLAB)
        biases = {False: [bias] * A_HEADS, True: [bias_first] * A_HEADS}
        if has_sink:
            biases = {first: [jnp.where(col == 0, sink_ref[w * A_HEADS + hh] * LOG2E, b[hh])
                              for hh in range(A_HEADS)] for first, b in biases.items()}
        for jb in range(nblk):
            q = q_ref[jb * BLOCK:(jb + 1) * BLOCK, cs]
            if jb == 0:
                k2 = jnp.concatenate([kp_ref[:, cs], kc_ref[0:BLOCK, cs]], axis=0)
                v2 = jnp.concatenate([vp_ref[:, cs], vc_ref[0:BLOCK, cs]], axis=0)
            else:
                k2 = kc_ref[(jb - 1) * BLOCK:(jb + 1) * BLOCK, cs]
                v2 = vc_ref[(jb - 1) * BLOCK:(jb + 1) * BLOCK, cs]
            if has_sink:
                k2 = drop_first_row(k2)
                v2 = drop_first_row(v2)
            qs = jnp.concatenate([q * head_mask[hh] for hh in range(A_HEADS)], axis=0)
            s = lax.dot_general(qs, k2, (((1,), (1,)), ((), ())), preferred_element_type=F32)
            ps, ms, ls = [], [], []
            for hh in range(A_HEADS):
                sh = s[hh * BLOCK:(hh + 1) * BLOCK] + biases[jb == 0][hh]
                m = jnp.max(sh, axis=-1, keepdims=True)
                p = jnp.exp2(sh - m)
                ms.append(m)
                ls.append(jnp.sum(p, axis=-1, keepdims=True))
                ps.append(p.astype(BF16))
            ost = jnp.dot(jnp.concatenate(ps, axis=0), v2, preferred_element_type=F32)
            o = jnp.zeros((BLOCK, A_SLAB), F32)
            lse = jnp.zeros((BLOCK, A_SLAB), F32)
            for hh in range(A_HEADS):
                o = jnp.where(head_of_lane == hh, ost[hh * BLOCK:(hh + 1) * BLOCK] * (1.0 / ls[hh]), o)
                if with_lse:
                    lse = jnp.where(head_of_lane == hh, ms[hh] + jnp.log2(ls[hh]), lse)
            o_ref[jb * BLOCK:(jb + 1) * BLOCK, cs] = o.astype(o_ref.dtype)
            if with_lse:
                lse_ref[jb * BLOCK:(jb + 1) * BLOCK, cs] = lse


def _band_attn(q, k, v, *, name, rows, slabs, prev_off, sinks=None, with_lse=True):
    b, length, width = q.shape
    assert sinks is None or prev_off >= 1
    n_col = width // (slabs * A_SLAB)
    n_row = length // rows
    blk_per_step = rows // BLOCK
    cur = pl.BlockSpec((None, rows, slabs * A_SLAB), lambda i, r, j: (i, j, r))
    prev = pl.BlockSpec((None, BLOCK, slabs * A_SLAB),
                        lambda i, r, j: (i, jnp.maximum(j * blk_per_step - 1, 0), r))
    in_specs = [cur, cur, prev, cur, prev]
    args = [q, k, k, v, v]
    if sinks is not None:
        in_specs.append(pl.BlockSpec(memory_space=pltpu.SMEM))
        args.append(sinks)
    out_specs = [cur]
    out_shape = [jax.ShapeDtypeStruct((b, length, width), BF16)]
    if with_lse:
        out_specs.append(cur)
        out_shape.append(jax.ShapeDtypeStruct((b, length, width), F32))
    return pl.pallas_call(
        functools.partial(_band_attn_kernel, rows=rows, slabs=slabs, prev_off=prev_off,
                          has_sink=sinks is not None, with_lse=with_lse),
        grid=(b, n_col, n_row),
        in_specs=in_specs,
        out_specs=out_specs,
        out_shape=out_shape,
        compiler_params=pltpu.CompilerParams(dimension_semantics=("arbitrary",) * 3,
                                             vmem_limit_bytes=VMEM_LIMIT),
        name=name,
    )(*args)


def _mix_kernel(x_ref, gain_ref, oa0, la0, oa1, la1, oa2, la2, ob_ref, mq_ref, mk_ref, mv_ref,
                wg_ref, bg_ref, woa_ref, wob_ref, wom_ref, wout_ref, out_ref, *stage_refs):
    x = x_ref[...]
    h = _rms_rows(x, gain_ref[...]).astype(BF16)
    stage_refs = list(stage_refs)

    def token_major(blk_ref, dil):
        if dil == 1:
            return blk_ref[...].astype(F32)
        stage_ref = stage_refs.pop()
        rows = blk_ref.shape[0]
        for r in range(dil):
            for s in range(A_SLAB // LANES):
                c0 = r * A_SLAB + s * LANES
                stage_ref[s, pl.ds(r, rows, stride=dil), :] = blk_ref[:, c0:c0 + LANES].astype(F32)
        return jnp.concatenate([stage_ref[s] for s in range(A_SLAB // LANES)], axis=1)

    head_cols = [slice(hh * M_HEAD_DIM, (hh + 1) * M_HEAD_DIM) for hh in range(M_HEADS)]
    scores = [lax.dot_general(mq_ref[:, cs], mk_ref[:, cs], (((1,), (1,)), ((), ())),
                              preferred_element_type=F32) for cs in head_cols]
    gate_cols = [slice(i * D_MODEL, (i + 1) * D_MODEL) for i in range(N_BRANCH)]
    gates = []
    for cs in gate_cols:
        z = jnp.dot(h, wg_ref[:, cs], preferred_element_type=F32) + bg_ref[:, cs]
        gates.append(0.5 + 0.5 * jnp.tanh(0.5 * z))

    dils = [dil for _, dil in A_GROUPS]
    l0, l1, l2 = (token_major(r, d) for r, d in zip((la0, la1, la2), dils))
    o0, o1, o2 = (token_major(r, d) for r, d in zip((oa0, oa1, oa2), dils))
    mx = jnp.maximum(jnp.maximum(l0, l1), l2)
    e0, e1, e2 = jnp.exp2(l0 - mx), jnp.exp2(l1 - mx), jnp.exp2(l2 - mx)
    o_a = (e0 * o0 + e1 * o1 + e2 * o2) / (e0 + e1 + e2)

    o_m = []
    for s, cs in zip(scores, head_cols):
        p = jnp.exp2(s - jnp.max(s, axis=-1, keepdims=True))
        l = jnp.sum(p, axis=-1, keepdims=True)
        o_m.append(jnp.dot(p.astype(BF16), mv_ref[:, cs], preferred_element_type=F32) / l)
    o_m = jnp.concatenate(o_m, axis=1)

    merged = jnp.zeros(x.shape, F32)
    for gate, o, w_ref in zip(gates, (o_a, ob_ref[...], o_m), (woa_ref, wob_ref, wom_ref)):
        merged = merged + gate * jnp.dot(o.astype(BF16), w_ref[...], preferred_element_type=F32)
    out_ref[...] = x + jnp.dot(merged.astype(BF16), wout_ref[...], preferred_element_type=F32)


def _mix(x2, attn_gain, oa, la, ob, mq, mk, mv, w_gate, b_gate, w_o_a, w_o_b, w_o_m, w_out, seq):
    t = x2.shape[0]
    tb = TOKEN_BLOCK
    mem_len = mk.shape[1]
    per_seq = seq // tb
    tok = lambda w, d=1: pl.BlockSpec((tb // d, w * d), lambda i: (i, 0))
    mem = pl.BlockSpec((None, mem_len, M_Q_COLS), lambda i: (i // per_seq, 0, 0))
    dils = [dil for _, dil in A_GROUPS]
    return pl.pallas_call(
        _mix_kernel,
        grid=(t // tb,),
        in_specs=[tok(D_MODEL), _const_spec((1, D_MODEL))]
                 + [tok(A_SLAB, d) for d in dils for _ in range(2)]
                 + [tok(B_Q_COLS), tok(M_Q_COLS), mem, mem,
                  _const_spec((D_MODEL, N_BRANCH * D_MODEL)), _const_spec((1, N_BRANCH * D_MODEL)),
                  _const_spec((A_SLAB, D_MODEL)), _const_spec((B_Q_COLS, D_MODEL)),
                  _const_spec((M_Q_COLS, D_MODEL)), _const_spec((D_MODEL, D_MODEL))],
        out_specs=tok(D_MODEL),
        out_shape=jax.ShapeDtypeStruct((t, D_MODEL), F32),
        scratch_shapes=[pltpu.VMEM((A_SLAB // LANES, tb, LANES), F32)] * (2 * sum(d > 1 for d in dils)),
        compiler_params=pltpu.CompilerParams(dimension_semantics=("arbitrary",),
                                             vmem_limit_bytes=VMEM_LIMIT),
        name="mix",
    )(x2, attn_gain, oa[0], la[0], oa[1], la[1], oa[2], la[2], ob, mq, mk, mv,
      w_gate, b_gate, w_o_a, w_o_b, w_o_m, w_out)


def _conv_ffn_kernel(x_ref, gain_ref, wup_ref, cw_ref, cb_ref, wdown_ref, out_ref,
                     carry_ref, ext_ref, act_ref, *, per_seq):
    tb = x_ref.shape[0]
    halo = CONV_WIDTH - 1
    pad = 8

    @pl.when(pl.program_id(0) % per_seq == 0)
    def _():
        carry_ref[...] = jnp.zeros(carry_ref.shape, F32)

    x = x_ref[...]
    h = _rms_rows(x, gain_ref[...]).astype(BF16)

    def conv(c0):
        cs = slice(c0, c0 + FF_CHUNK)
        u = jnp.dot(h, wup_ref[:, cs], preferred_element_type=F32)
        ext_ref[0:pad, :] = carry_ref[:, cs]
        ext_ref[pad:pad + tb, :] = u
        carry_ref[:, cs] = u[tb - pad:tb, :]
        c = cb_ref[:, cs] + cw_ref[CONV_WIDTH - 1:CONV_WIDTH, cs] * u
        for j in range(halo):
            c = c + cw_ref[j:j + 1, cs] * ext_ref[pad - halo + j:pad - halo + j + tb, :]
        return c

    for ch in range(D_FF // FF_CHUNK):
        a = conv(ch * FF_CHUNK)
        g = conv(D_FF + ch * FF_CHUNK)
        half = 0.5 * a
        act = (half + half * jnp.tanh(half)) * g
        act_ref[:, ch * FF_CHUNK:(ch + 1) * FF_CHUNK] = act.astype(BF16)
    out_ref[...] = x + jnp.dot(act_ref[...], wdown_ref[...], preferred_element_type=F32)


def _conv_ffn(x2, gain, w_up, conv_w, conv_b, w_down, seq):
    t = x2.shape[0]
    tb = TOKEN_BLOCK
    return pl.pallas_call(
        functools.partial(_conv_ffn_kernel, per_seq=seq // tb),
        grid=(t // tb,),
        in_specs=[pl.BlockSpec((tb, D_MODEL), lambda i: (i, 0)),
                  _const_spec((1, D_MODEL)),
                  _const_spec((D_MODEL, 2 * D_FF)),
                  _const_spec((CONV_WIDTH, 2 * D_FF)),
                  _const_spec((1, 2 * D_FF)),
                  _const_spec((D_FF, D_MODEL))],
        out_specs=pl.BlockSpec((tb, D_MODEL), lambda i: (i, 0)),
        out_shape=jax.ShapeDtypeStruct((t, D_MODEL), F32),
        scratch_shapes=[pltpu.VMEM((8, 2 * D_FF), F32),
                        pltpu.VMEM((tb + 8, FF_CHUNK), F32),
                        pltpu.VMEM((tb, D_FF), BF16)],
        compiler_params=pltpu.CompilerParams(dimension_semantics=("arbitrary",),
                                             vmem_limit_bytes=VMEM_LIMIT),
        name="conv_ffn",
    )(x2, gain, w_up, conv_w, conv_b, w_down)


def _tables():
    lane = np.arange(A_SLAB)
    bd64 = jnp.asarray(lane[:, None] // HEAD_DIM == lane[None, :] // HEAD_DIM, BF16)
    bd128 = jnp.asarray(lane[:, None] // M_HEAD_DIM == lane[None, :] // M_HEAD_DIM, BF16)
    src = np.arange(B_KV_COLS)
    dst = np.arange(B_Q_COLS)
    expand = jnp.asarray(src[:, None] == (dst[None, :] // A_SLAB) * HEAD_DIM + dst[None, :] % HEAD_DIM, BF16)
    freq_col = jnp.exp(jnp.arange(ROPE_HALF, dtype=F32) * (-2.0 * math.log(ROPE_THETA) / ROPE_DIM))
    freq_col = freq_col.reshape(ROPE_HALF, 1)
    in_head = np.arange(LANES) % HEAD_DIM
    j = np.arange(LANES)[:, None]
    rot = in_head[None, :] < ROPE_DIM
    cos_sel = rot & (j == in_head[None, :] % ROPE_HALF)
    sin_sel = rot & (j == ROPE_HALF + in_head[None, :] % ROPE_HALF)
    lo = in_head[None, :] < ROPE_HALF
    spread = np.concatenate([cos_sel.astype(np.float32), -(sin_sel & lo).astype(np.float32),
                             (sin_sel & ~lo).astype(np.float32)], axis=1)
    one_row = (in_head >= ROPE_DIM).astype(np.float32).reshape(1, LANES)
    return bd64, bd128, expand, freq_col, jnp.asarray(spread, BF16), jnp.asarray(one_row)


def _layer(x2, mem, pos_rows, batch, seq, p):
    bd64, bd128, expand, freq_col, spread, one_row = _tables()
    qk_scale = HEAD_DIM ** -0.5 * LOG2E
    ones = jnp.ones((A_SLAB,), F32)
    colgain = jnp.concatenate(
        [jnp.concatenate([jnp.tile(p["a_q_norm"][g], A_HEADS) * qk_scale,
                          jnp.tile(p["a_k_norm"][g], A_HEADS), ones]) for g in range(len(A_GROUPS))]
        + [jnp.tile(p["b_q_norm"], B_Q_HEADS) * qk_scale,
           jnp.tile(p["b_k_norm"], B_KV_HEADS), jnp.ones((B_KV_COLS,), F32),
           jnp.tile(p["m_q_norm"], M_HEADS) * (M_HEAD_DIM ** -0.5 * LOG2E)]).reshape(1, IN_COLS)

    mk, mv = _mem_kv(mem, p["mem_norm"].reshape(1, D_MODEL), p["w_mem_kv"].astype(BF16),
                     p["m_k_norm"].reshape(1, M_HEAD_DIM))
    attn_gain = p["attn_norm"].reshape(1, D_MODEL)
    (qa0, ka0, va0, qa1, ka1, va1, qa2, ka2, va2, qb, kb, vb, mq) = _in_proj(
        x2, pos_rows, attn_gain, p["w_in"].astype(BF16), colgain, freq_col, spread, one_row,
        bd64, bd128, expand)

    def per_batch(t):
        return t.reshape(batch, t.shape[0] // batch, t.shape[-1])

    def flat(t):
        return t.reshape(t.shape[0] * t.shape[1], t.shape[2])

    oa, la = [], []
    cfg = {1: dict(rows=2048, slabs=1), 4: dict(rows=seq // 4, slabs=2), 16: dict(rows=seq // 16, slabs=8)}
    for (window, dil), (q, k, v) in zip(A_GROUPS, ((qa0, ka0, va0), (qa1, ka1, va1), (qa2, ka2, va2))):
        assert window // dil == BLOCK
        o, lse = _band_attn(per_batch(q), per_batch(k), per_batch(v), name=f"band_attn_d{dil}",
                            prev_off=0, **cfg[dil])
        oa.append(flat(o))
        la.append(flat(lse))
    (ob,) = _band_attn(per_batch(qb), per_batch(kb), per_batch(vb), name="band_attn_swa", rows=1024,
                       slabs=B_KV_HEADS, prev_off=BLOCK - (B_WINDOW - 1), sinks=p["b_sinks"],
                       with_lse=False)
    ob = flat(ob)

    x2 = _mix(x2, attn_gain, oa, la, ob, mq, mk, mv, p["w_gate"].astype(BF16),
              p["b_gate"].reshape(1, N_BRANCH * D_MODEL), p["w_o_a"].astype(BF16),
              p["w_o_b"].astype(BF16), p["w_o_m"].astype(BF16), p["w_out"].astype(BF16), seq)
    return _conv_ffn(x2, p["ffn_norm"].reshape(1, D_MODEL), p["w_up"].astype(BF16), p["conv_w"],
                     p["conv_b"].reshape(1, 2 * D_FF), p["w_down"].astype(BF16), seq)


def kernel(x, mem, positions, attn_norm, w_in, a_q_norm, a_k_norm, b_q_norm, b_k_norm, b_sinks,
           mem_norm, w_mem_kv, m_q_norm, m_k_norm, w_o_a, w_o_b, w_o_m, w_gate, b_gate, w_out,
           ffn_norm, w_up, conv_w, conv_b, w_down):
    batch, seq, _ = x.shape
    params = dict(attn_norm=attn_norm, w_in=w_in, a_q_norm=a_q_norm, a_k_norm=a_k_norm,
                  b_q_norm=b_q_norm, b_k_norm=b_k_norm, b_sinks=b_sinks, mem_norm=mem_norm,
                  w_mem_kv=w_mem_kv, m_q_norm=m_q_norm, m_k_norm=m_k_norm, w_o_a=w_o_a, w_o_b=w_o_b,
                  w_o_m=w_o_m, w_gate=w_gate, b_gate=b_gate, w_out=w_out, ffn_norm=ffn_norm,
                  w_up=w_up, conv_w=conv_w, conv_b=conv_b, w_down=w_down)
    pos_rows = positions.astype(F32).reshape(batch * seq // TOKEN_BLOCK, 1, TOKEN_BLOCK)
    x2 = x.reshape(batch * seq, D_MODEL)
    for layer in range(attn_norm.shape[0]):
        x2 = _layer(x2, mem, pos_rows, batch, seq, {k: v[layer] for k, v in params.items()})
    return x2.reshape(batch, seq, D_MODEL)
```

```python
import functools
import math

import numpy as np

import jax
import jax.numpy as jnp
from jax import lax
from jax.experimental import pallas as pl
from jax.experimental.pallas import tpu as pltpu

D_MODEL = 1024
HEAD_DIM = 64
A_GROUPS = ((128, 1), (512, 4), (2048, 16))
A_HEADS = 4
A_SLAB = A_HEADS * HEAD_DIM
A_QKV_COLS = len(A_GROUPS) * 3 * A_SLAB
B_Q_HEADS = 8
B_KV_HEADS = 2
B_GROUP = B_Q_HEADS // B_KV_HEADS
B_WINDOW = 128
B_Q_COLS = B_Q_HEADS * HEAD_DIM
B_KV_COLS = B_KV_HEADS * HEAD_DIM
M_HEADS = 4
M_HEAD_DIM = 128
M_Q_COLS = M_HEADS * M_HEAD_DIM
B_Q_OFF = A_QKV_COLS
B_K_OFF = B_Q_OFF + B_Q_COLS
B_V_OFF = B_K_OFF + B_KV_COLS
M_Q_OFF = B_V_OFF + B_KV_COLS
IN_COLS = M_Q_OFF + M_Q_COLS
N_BRANCH = 3
D_FF = 2816
CONV_WIDTH = 3
ROPE_THETA = 500000.0
ROPE_DIM = HEAD_DIM // 4
ROPE_HALF = ROPE_DIM // 2
BLOCK = 128
EPS = 1e-6
NEG = -1e30
LOG2E = math.log2(math.e)

LANES = 128
BF16_ROWS = 16
TOKEN_BLOCK = 512
ROW_CHUNK = 64
FF_CHUNK = 256
VMEM_LIMIT = 56 * 1024 * 1024

F32 = jnp.float32
BF16 = jnp.bfloat16


def _const_spec(shape):
    return pl.BlockSpec(shape, lambda *_: (0,) * len(shape), pipeline_mode=pl.Buffered(1))


def _rms_rows(x, gain):
    ms = jnp.mean(x * x, axis=-1, keepdims=True)
    return x * lax.rsqrt(ms + EPS) * gain


def _mem_kv_kernel(mem_ref, gain_ref, w_ref, kgain_ref, mk_ref, mv_ref):
    hm = _rms_rows(mem_ref[...], gain_ref[...]).astype(BF16)
    kv = jnp.dot(hm, w_ref[...], preferred_element_type=F32)
    ks = []
    for h in range(M_HEADS):
        kh = kv[:, h * M_HEAD_DIM:(h + 1) * M_HEAD_DIM]
        ks.append(_rms_rows(kh, kgain_ref[...]))
    mk_ref[...] = jnp.concatenate(ks, axis=1).astype(BF16)
    mv_ref[...] = kv[:, M_Q_COLS:].astype(BF16)


def _mem_kv(mem, mem_gain, w_kv, k_gain):
    b, m, _ = mem.shape
    return pl.pallas_call(
        _mem_kv_kernel,
        grid=(b,),
        in_specs=[pl.BlockSpec((None, m, D_MODEL), lambda i: (i, 0, 0)),
                  _const_spec((1, D_MODEL)),
                  _const_spec((D_MODEL, 2 * M_Q_COLS)),
                  _const_spec((1, M_HEAD_DIM))],
        out_specs=[pl.BlockSpec((None, m, M_Q_COLS), lambda i: (i, 0, 0)),
                   pl.BlockSpec((None, m, M_Q_COLS), lambda i: (i, 0, 0))],
        out_shape=[jax.ShapeDtypeStruct((b, m, M_Q_COLS), BF16)] * 2,
        compiler_params=pltpu.CompilerParams(dimension_semantics=("arbitrary",),
                                             vmem_limit_bytes=VMEM_LIMIT),
        name="mem_kv",
    )(mem, mem_gain, w_kv, k_gain)


def _in_proj_kernel(x_ref, pos_ref, gain_ref, w_ref, colgain_ref, freq_ref, spread_ref, one_ref,
                    bd64_ref, bd128_ref, expand_ref, *refs, n_cast):
    cast_in, refs = refs[:n_cast], refs[n_cast:]
    (qa0, ka0, va0, qa1, ka1, va1, qa2, ka2, va2, qb_ref, kb_ref, vb_ref, mq_ref) = refs[:13]
    cast_out, refs = refs[13:13 + n_cast], refs[13 + n_cast:]
    proj_ref, h_ref, cos_ref, sin_lo_ref, sin_hi_ref, kvb_ref = refs[:6]
    stage_refs = list(refs[6:])
    tb = x_ref.shape[0]

    for src, dst in zip(cast_in, cast_out):
        dst[...] = src[...].astype(BF16)
    chunks = [slice(r, r + ROW_CHUNK) for r in range(0, tb, ROW_CHUNK)]

    @pl.when(pl.program_id(0) == 0)
    def _():
        proj_ref[...] = jnp.zeros(proj_ref.shape, F32)

    for rows in chunks:
        h_ref[rows, :] = _rms_rows(x_ref[rows, :], gain_ref[...]).astype(BF16)

    ang = freq_ref[...] * pos_ref[...]
    trig = jnp.concatenate([jnp.cos(ang), jnp.sin(ang),
                            jnp.zeros((LANES - ROPE_DIM, tb), F32)], axis=0).T
    tables = jnp.zeros((tb, 3 * LANES), F32)
    for _ in range(3):
        part = trig.astype(BF16)
        tables = tables + jnp.dot(part, spread_ref[...], preferred_element_type=F32)
        trig = trig - part.astype(F32)
    cos_ref[...] = tables[:, 0:LANES] + one_ref[...]
    sin_lo_ref[...] = tables[:, LANES:2 * LANES]
    sin_hi_ref[...] = tables[:, 2 * LANES:3 * LANES]

    def proj(c0, width):
        tile, off = divmod(c0, A_SLAB)
        assert off + width <= A_SLAB and off % LANES == 0
        t = proj_ref[tile, :, off:off + width]
        proj_ref[tile, :, off:off + width] = jnp.dot(h_ref[...], w_ref[:, c0:c0 + width],
                                                     preferred_element_type=F32)
        return t

    def rope(y, rows):
        parts = []
        for c in range(y.shape[1] // LANES):
            yc = y[:, c * LANES:(c + 1) * LANES]
            parts.append(yc * cos_ref[rows, :]
                         + pltpu.roll(yc, LANES - ROPE_HALF, 1) * sin_lo_ref[rows, :]
                         + pltpu.roll(yc, ROPE_HALF, 1) * sin_hi_ref[rows, :])
        return parts[0] if len(parts) == 1 else jnp.concatenate(parts, axis=1)

    def finish(t, emit, c0, *, bd=None, dim=None, rotary=False):
        if bd is not None:
            ss = jnp.dot((t * t).astype(BF16), bd, preferred_element_type=F32)
            gain = colgain_ref[:, c0:c0 + t.shape[1]]
        for rows in chunks:
            y = t[rows]
            if bd is not None:
                y = y * lax.rsqrt(ss[rows] * (1.0 / dim) + EPS) * gain
            if rotary:
                y = rope(y, rows)
            emit(rows, y)

    def to_ref(out_ref, col0=0):
        def emit(rows, y):
            out_ref[rows, col0:col0 + y.shape[1]] = y.astype(BF16)
        return emit

    def store_tile(out_ref, t, dil, c0, **kw):
        if dil == 1:
            finish(t, to_ref(out_ref), c0, **kw)
            return
        stage_ref = stage_refs.pop()

        def emit(rows, y):
            for s in range(A_SLAB // LANES):
                stage_ref[s, rows, :] = y[:, s * LANES:(s + 1) * LANES]
        finish(t, emit, c0, **kw)
        for r in range(dil):
            for s in range(A_SLAB // LANES):
                col = r * A_SLAB + s * LANES
                out_ref[:, col:col + LANES] = stage_ref[s, pl.ds(r, tb // dil, stride=dil), :].astype(BF16)

    bd64 = bd64_ref[...]
    qk = dict(bd=bd64, dim=HEAD_DIM, rotary=True)
    for g, (q_ref, k_ref, v_ref) in enumerate(((qa0, ka0, va0), (qa1, ka1, va1), (qa2, ka2, va2))):
        c0 = g * 3 * A_SLAB
        dil = A_GROUPS[g][1]
        store_tile(q_ref, proj(c0, A_SLAB), dil, c0, **qk)
        store_tile(k_ref, proj(c0 + A_SLAB, A_SLAB), dil, c0 + A_SLAB, **qk)
        store_tile(v_ref, proj(c0 + 2 * A_SLAB, A_SLAB), dil, c0 + 2 * A_SLAB)

    for s in range(B_Q_COLS // A_SLAB):
        c0 = B_Q_OFF + s * A_SLAB
        finish(proj(c0, A_SLAB), to_ref(qb_ref, s * A_SLAB), c0, **qk)

    finish(proj(B_K_OFF, B_KV_COLS), to_ref(kvb_ref), B_K_OFF,
           bd=bd64_ref[0:B_KV_COLS, 0:B_KV_COLS], dim=HEAD_DIM, rotary=True)
    kb_ref[...] = jnp.dot(kvb_ref[...], expand_ref[...], preferred_element_type=F32).astype(BF16)
    vb = proj(B_V_OFF, B_KV_COLS).astype(BF16)
    vb_ref[...] = jnp.dot(vb, expand_ref[...], preferred_element_type=F32).astype(BF16)

    bd128 = bd128_ref[...]
    for s in range(M_Q_COLS // A_SLAB):
        c0 = M_Q_OFF + s * A_SLAB
        finish(proj(c0, A_SLAB), to_ref(mq_ref, s * A_SLAB), c0, bd=bd128, dim=M_HEAD_DIM)


def _cast_rows(rows, n_steps):
    for per_step in range(BF16_ROWS, rows + 1, BF16_ROWS):
        if rows % per_step == 0 and rows // per_step <= n_steps:
            return per_step
    raise ValueError(f"no bf16-aligned split of {rows} rows over {n_steps} steps")


def _in_proj(x2, pos_rows, attn_gain, w_in, colgain, freq_col, spread, one_row, bd64, bd128, expand,
             cast_weights):
    t = x2.shape[0]
    tb = TOKEN_BLOCK
    dils = [dil for _, dil in A_GROUPS for _ in range(3)] + [1, 1, 1, 1]
    widths = [A_SLAB] * 9 + [B_Q_COLS, B_Q_COLS, B_Q_COLS, M_Q_COLS]
    n_staged = sum(d > 1 for d in dils)
    n = t // tb
    done = lambda i: (jnp.maximum(i - 1, 0), 0)

    def cast_spec(w):
        per_step = _cast_rows(w.shape[0], n + 1)
        last = w.shape[0] // per_step - 1
        return pl.BlockSpec((per_step, w.shape[1]), lambda i: (jnp.minimum(i, last), 0))

    cast_specs = [cast_spec(w) for w in cast_weights]
    return pl.pallas_call(
        functools.partial(_in_proj_kernel, n_cast=len(cast_weights)),
        grid=(n + 1,),
        in_specs=[pl.BlockSpec((tb, D_MODEL), lambda i: (jnp.minimum(i, n - 1), 0)),
                  pl.BlockSpec((None, 1, tb), lambda i: (jnp.maximum(i - 1, 0), 0, 0)),
                  _const_spec((1, D_MODEL)),
                  _const_spec((D_MODEL, IN_COLS)),
                  _const_spec((1, IN_COLS)),
                  _const_spec((ROPE_HALF, 1)),
                  _const_spec((LANES, 3 * LANES)),
                  _const_spec((1, LANES)),
                  _const_spec((A_SLAB, A_SLAB)),
                  _const_spec((A_SLAB, A_SLAB)),
                  _const_spec((B_KV_COLS, B_Q_COLS))] + cast_specs,
        out_specs=[pl.BlockSpec((tb // d, w * d), done) for w, d in zip(widths, dils)] + cast_specs,
        out_shape=[jax.ShapeDtypeStruct((t // d, w * d), BF16) for w, d in zip(widths, dils)]
                  + [jax.ShapeDtypeStruct(w.shape, BF16) for w in cast_weights],
        scratch_shapes=[pltpu.VMEM((IN_COLS // A_SLAB, tb, A_SLAB), F32)]
                       + [pltpu.VMEM((tb, D_MODEL), BF16)] + [pltpu.VMEM((tb, LANES), F32)] * 3
                       + [pltpu.VMEM((tb, B_KV_COLS), BF16)]
                       + [pltpu.VMEM((A_SLAB // LANES, tb, LANES), F32)] * n_staged,
        compiler_params=pltpu.CompilerParams(dimension_semantics=("arbitrary",),
                                             vmem_limit_bytes=VMEM_LIMIT),
        name="in_proj",
    )(x2, pos_rows, attn_gain, w_in, colgain, freq_col, spread, one_row, bd64, bd128, expand,
      *cast_weights)


def _band_attn_kernel(*refs, rows, slabs, prev_off, has_sink, with_lse):
    q_ref, kc_ref, kp_ref, vc_ref, vp_ref = refs[:5]
    pos = 5
    sink_ref = None
    if has_sink:
        sink_ref = refs[pos]
        pos += 1
    o_ref = refs[pos]
    lse_ref = refs[pos + 1] if with_lse else None

    first_step = pl.program_id(2) == 0
    row = lax.broadcasted_iota(jnp.int32, (BLOCK, 2 * BLOCK), 0)
    col = lax.broadcasted_iota(jnp.int32, (BLOCK, 2 * BLOCK), 1)
    bias = jnp.where(col < BLOCK,
                     jnp.where(col >= row + prev_off, 0.0, NEG),
                     jnp.where(col - BLOCK <= row, 0.0, NEG)).astype(F32)
    bias_first = jnp.where(jnp.logical_and(first_step, col < BLOCK), NEG, bias)
    head_of_lane = lax.broadcasted_iota(jnp.int32, (1, A_SLAB), 1) // HEAD_DIM
    head_mask = [(head_of_lane == hh).astype(BF16) for hh in range(A_HEADS)]
    pack_row = lax.broadcasted_iota(jnp.int32, (BF16_ROWS, A_SLAB), 0)
    nblk = rows // BLOCK

    def drop_first_row(t):
        top = jnp.where(pack_row == 0, jnp.zeros((BF16_ROWS, A_SLAB), BF16), t[0:BF16_ROWS])
        return jnp.concatenate([top, t[BF16_ROWS:]], axis=0)

    for w in range(slabs):
        cs = slice(w * A_SLAB, (w + 1) * A_SLAB)
        biases = {False: [bias] * A_HEADS, True: [bias_first] * A_HEADS}
        if has_sink:
            biases = {first: [jnp.where(col == 0, sink_ref[w * A_HEADS + hh] * LOG2E, b[hh])
                              for hh in range(A_HEADS)] for first, b in biases.items()}
        for jb in range(nblk):
            q = q_ref[jb * BLOCK:(jb + 1) * BLOCK, cs]
            if jb == 0:
                k2 = jnp.concatenate([kp_ref[:, cs], kc_ref[0:BLOCK, cs]], axis=0)
                v2 = jnp.concatenate([vp_ref[:, cs], vc_ref[0:BLOCK, cs]], axis=0)
            else:
                k2 = kc_ref[(jb - 1) * BLOCK:(jb + 1) * BLOCK, cs]
                v2 = vc_ref[(jb - 1) * BLOCK:(jb + 1) * BLOCK, cs]
            if has_sink:
                k2 = drop_first_row(k2)
                v2 = drop_first_row(v2)
            qs = jnp.concatenate([q * head_mask[hh] for hh in range(A_HEADS)], axis=0)
            s = lax.dot_general(qs, k2, (((1,), (1,)), ((), ())), preferred_element_type=F32)
            ps, ms, ls = [], [], []
            for hh in range(A_HEADS):
                sh = s[hh * BLOCK:(hh + 1) * BLOCK] + biases[jb == 0][hh]
                m = jnp.max(sh, axis=-1, keepdims=True)
                p = jnp.exp2(sh - m)
                ms.append(m)
                ls.append(jnp.sum(p, axis=-1, keepdims=True))
                ps.append(p.astype(BF16))
            ost = jnp.dot(jnp.concatenate(ps, axis=0), v2, preferred_element_type=F32)
            o = jnp.zeros((BLOCK, A_SLAB), F32)
            lse = jnp.zeros((BLOCK, A_SLAB), F32)
            for hh in range(A_HEADS):
                o = jnp.where(head_of_lane == hh, ost[hh * BLOCK:(hh + 1) * BLOCK] * (1.0 / ls[hh]), o)
                if with_lse:
                    lse = jnp.where(head_of_lane == hh, ms[hh] + jnp.log2(ls[hh]), lse)
            o_ref[jb * BLOCK:(jb + 1) * BLOCK, cs] = o.astype(o_ref.dtype)
            if with_lse:
                lse_ref[jb * BLOCK:(jb + 1) * BLOCK, cs] = lse


def _band_attn(q, k, v, *, name, rows, slabs, prev_off, sinks=None, with_lse=True):
    b, length, width = q.shape
    assert sinks is None or prev_off >= 1
    n_col = width // (slabs * A_SLAB)
    n_row = length // rows
    blk_per_step = rows // BLOCK
    cur = pl.BlockSpec((None, rows, slabs * A_SLAB), lambda i, r, j: (i, j, r))
    prev = pl.BlockSpec((None, BLOCK, slabs * A_SLAB),
                        lambda i, r, j: (i, jnp.maximum(j * blk_per_step - 1, 0), r))
    in_specs = [cur, cur, prev, cur, prev]
    args = [q, k, k, v, v]
    if sinks is not None:
        in_specs.append(pl.BlockSpec(memory_space=pltpu.SMEM))
        args.append(sinks)
    out_specs = [cur]
    out_shape = [jax.ShapeDtypeStruct((b, length, width), BF16)]
    if with_lse:
        out_specs.append(cur)
        out_shape.append(jax.ShapeDtypeStruct((b, length, width), F32))
    return pl.pallas_call(
        functools.partial(_band_attn_kernel, rows=rows, slabs=slabs, prev_off=prev_off,
                          has_sink=sinks is not None, with_lse=with_lse),
        grid=(b, n_col, n_row),
        in_specs=in_specs,
        out_specs=out_specs,
        out_shape=out_shape,
        compiler_params=pltpu.CompilerParams(dimension_semantics=("arbitrary",) * 3,
                                             vmem_limit_bytes=VMEM_LIMIT),
        name=name,
    )(*args)


def _mix_kernel(x_ref, gain_ref, oa0, la0, oa1, la1, oa2, la2, ob_ref, mq_ref, mk_ref, mv_ref,
                wg_ref, bg_ref, woa_ref, wob_ref, wom_ref, wout_ref, out_ref, *stage_refs):
    x = x_ref[...]
    h = _rms_rows(x, gain_ref[...]).astype(BF16)
    stage_refs = list(stage_refs)

    def token_major(blk_ref, dil):
        if dil == 1:
            return blk_ref[...].astype(F32)
        stage_ref = stage_refs.pop()
        rows = blk_ref.shape[0]
        for r in range(dil):
            for s in range(A_SLAB // LANES):
                c0 = r * A_SLAB + s * LANES
                stage_ref[s, pl.ds(r, rows, stride=dil), :] = blk_ref[:, c0:c0 + LANES].astype(F32)
        return jnp.concatenate([stage_ref[s] for s in range(A_SLAB // LANES)], axis=1)

    head_cols = [slice(hh * M_HEAD_DIM, (hh + 1) * M_HEAD_DIM) for hh in range(M_HEADS)]
    scores = [lax.dot_general(mq_ref[:, cs], mk_ref[:, cs], (((1,), (1,)), ((), ())),
                              preferred_element_type=F32) for cs in head_cols]
    gate_cols = [slice(i * D_MODEL, (i + 1) * D_MODEL) for i in range(N_BRANCH)]
    gates = []
    for cs in gate_cols:
        z = jnp.dot(h, wg_ref[:, cs], preferred_element_type=F32) + bg_ref[:, cs]
        gates.append(0.5 + 0.5 * jnp.tanh(0.5 * z))

    dils = [dil for _, dil in A_GROUPS]
    l0, l1, l2 = (token_major(r, d) for r, d in zip((la0, la1, la2), dils))
    o0, o1, o2 = (token_major(r, d) for r, d in zip((oa0, oa1, oa2), dils))
    mx = jnp.maximum(jnp.maximum(l0, l1), l2)
    e0, e1, e2 = jnp.exp2(l0 - mx), jnp.exp2(l1 - mx), jnp.exp2(l2 - mx)
    o_a = (e0 * o0 + e1 * o1 + e2 * o2) / (e0 + e1 + e2)

    o_m = []
    for s, cs in zip(scores, head_cols):
        p = jnp.exp2(s - jnp.max(s, axis=-1, keepdims=True))
        l = jnp.sum(p, axis=-1, keepdims=True)
        o_m.append(jnp.dot(p.astype(BF16), mv_ref[:, cs], preferred_element_type=F32) / l)
    o_m = jnp.concatenate(o_m, axis=1)

    merged = jnp.zeros(x.shape, F32)
    for gate, o, w_ref in zip(gates, (o_a, ob_ref[...], o_m), (woa_ref, wob_ref, wom_ref)):
        merged = merged + gate * jnp.dot(o.astype(BF16), w_ref[...], preferred_element_type=F32)
    out_ref[...] = x + jnp.dot(merged.astype(BF16), wout_ref[...], preferred_element_type=F32)


def _mix(x2, attn_gain, oa, la, ob, mq, mk, mv, w_gate, b_gate, w_o_a, w_o_b, w_o_m, w_out, seq):
    t = x2.shape[0]
    tb = TOKEN_BLOCK
    mem_len = mk.shape[1]
    per_seq = seq // tb
    tok = lambda w, d=1: pl.BlockSpec((tb // d, w * d), lambda i: (i, 0))
    mem = pl.BlockSpec((None, mem_len, M_Q_COLS), lambda i: (i // per_seq, 0, 0))
    dils = [dil for _, dil in A_GROUPS]
    return pl.pallas_call(
        _mix_kernel,
        grid=(t // tb,),
        in_specs=[tok(D_MODEL), _const_spec((1, D_MODEL))]
                 + [tok(A_SLAB, d) for d in dils for _ in range(2)]
                 + [tok(B_Q_COLS), tok(M_Q_COLS), mem, mem,
                  _const_spec((D_MODEL, N_BRANCH * D_MODEL)), _const_spec((1, N_BRANCH * D_MODEL)),
                  _const_spec((A_SLAB, D_MODEL)), _const_spec((B_Q_COLS, D_MODEL)),
                  _const_spec((M_Q_COLS, D_MODEL)), _const_spec((D_MODEL, D_MODEL))],
        out_specs=tok(D_MODEL),
        out_shape=jax.ShapeDtypeStruct((t, D_MODEL), F32),
        scratch_shapes=[pltpu.VMEM((A_SLAB // LANES, tb, LANES), F32)] * (2 * sum(d > 1 for d in dils)),
        compiler_params=pltpu.CompilerParams(dimension_semantics=("arbitrary",),
                                             vmem_limit_bytes=VMEM_LIMIT),
        name="mix",
    )(x2, attn_gain, oa[0], la[0], oa[1], la[1], oa[2], la[2], ob, mq, mk, mv,
      w_gate, b_gate, w_o_a, w_o_b, w_o_m, w_out)


def _conv_ffn_kernel(x_ref, gain_ref, wup_ref, cw_ref, cb_ref, wdown_ref, out_ref,
                     carry_ref, ext_ref, act_ref, *, per_seq):
    tb = x_ref.shape[0]
    halo = CONV_WIDTH - 1
    pad = 8

    @pl.when(pl.program_id(0) % per_seq == 0)
    def _():
        carry_ref[...] = jnp.zeros(carry_ref.shape, F32)

    x = x_ref[...]
    h = _rms_rows(x, gain_ref[...]).astype(BF16)

    def conv(c0):
        cs = slice(c0, c0 + FF_CHUNK)
        u = jnp.dot(h, wup_ref[:, cs], preferred_element_type=F32)
        ext_ref[0:pad, :] = carry_ref[:, cs]
        ext_ref[pad:pad + tb, :] = u
        carry_ref[:, cs] = u[tb - pad:tb, :]
        c = cb_ref[:, cs] + cw_ref[CONV_WIDTH - 1:CONV_WIDTH, cs] * u
        for j in range(halo):
            c = c + cw_ref[j:j + 1, cs] * ext_ref[pad - halo + j:pad - halo + j + tb, :]
        return c

    for ch in range(D_FF // FF_CHUNK):
        a = conv(ch * FF_CHUNK)
        g = conv(D_FF + ch * FF_CHUNK)
        half = 0.5 * a
        act = (half + half * jnp.tanh(half)) * g
        act_ref[:, ch * FF_CHUNK:(ch + 1) * FF_CHUNK] = act.astype(BF16)
    out_ref[...] = x + jnp.dot(act_ref[...], wdown_ref[...], preferred_element_type=F32)


def _conv_ffn(x2, gain, w_up, conv_w, conv_b, w_down, seq):
    t = x2.shape[0]
    tb = TOKEN_BLOCK
    return pl.pallas_call(
        functools.partial(_conv_ffn_kernel, per_seq=seq // tb),
        grid=(t // tb,),
        in_specs=[pl.BlockSpec((tb, D_MODEL), lambda i: (i, 0)),
                  _const_spec((1, D_MODEL)),
                  _const_spec((D_MODEL, 2 * D_FF)),
                  _const_spec((CONV_WIDTH, 2 * D_FF)),
                  _const_spec((1, 2 * D_FF)),
                  _const_spec((D_FF, D_MODEL))],
        out_specs=pl.BlockSpec((tb, D_MODEL), lambda i: (i, 0)),
        out_shape=jax.ShapeDtypeStruct((t, D_MODEL), F32),
        scratch_shapes=[pltpu.VMEM((8, 2 * D_FF), F32),
                        pltpu.VMEM((tb + 8, FF_CHUNK), F32),
                        pltpu.VMEM((tb, D_FF), BF16)],
        compiler_params=pltpu.CompilerParams(dimension_semantics=("arbitrary",),
                                             vmem_limit_bytes=VMEM_LIMIT),
        name="conv_ffn",
    )(x2, gain, w_up, conv_w, conv_b, w_down)


def _tables():
    lane = np.arange(A_SLAB)
    bd64 = jnp.asarray(lane[:, None] // HEAD_DIM == lane[None, :] // HEAD_DIM, BF16)
    bd128 = jnp.asarray(lane[:, None] // M_HEAD_DIM == lane[None, :] // M_HEAD_DIM, BF16)
    src = np.arange(B_KV_COLS)
    dst = np.arange(B_Q_COLS)
    expand = jnp.asarray(src[:, None] == (dst[None, :] // A_SLAB) * HEAD_DIM + dst[None, :] % HEAD_DIM, BF16)
    freq_col = jnp.exp(jnp.arange(ROPE_HALF, dtype=F32) * (-2.0 * math.log(ROPE_THETA) / ROPE_DIM))
    freq_col = freq_col.reshape(ROPE_HALF, 1)
    in_head = np.arange(LANES) % HEAD_DIM
    j = np.arange(LANES)[:, None]
    rot = in_head[None, :] < ROPE_DIM
    cos_sel = rot & (j == in_head[None, :] % ROPE_HALF)
    sin_sel = rot & (j == ROPE_HALF + in_head[None, :] % ROPE_HALF)
    lo = in_head[None, :] < ROPE_HALF
    spread = np.concatenate([cos_sel.astype(np.float32), -(sin_sel & lo).astype(np.float32),
                             (sin_sel & ~lo).astype(np.float32)], axis=1)
    one_row = (in_head >= ROPE_DIM).astype(np.float32).reshape(1, LANES)
    return bd64, bd128, expand, freq_col, jnp.asarray(spread, BF16), jnp.asarray(one_row)


def _layer(x2, mem, pos_rows, batch, seq, p):
    bd64, bd128, expand, freq_col, spread, one_row = _tables()
    qk_scale = HEAD_DIM ** -0.5 * LOG2E
    ones = jnp.ones((A_SLAB,), F32)
    colgain = jnp.concatenate(
        [jnp.concatenate([jnp.tile(p["a_q_norm"][g], A_HEADS) * qk_scale,
                          jnp.tile(p["a_k_norm"][g], A_HEADS), ones]) for g in range(len(A_GROUPS))]
        + [jnp.tile(p["b_q_norm"], B_Q_HEADS) * qk_scale,
           jnp.tile(p["b_k_norm"], B_KV_HEADS), jnp.ones((B_KV_COLS,), F32),
           jnp.tile(p["m_q_norm"], M_HEADS) * (M_HEAD_DIM ** -0.5 * LOG2E)]).reshape(1, IN_COLS)

    attn_gain = p["attn_norm"].reshape(1, D_MODEL)
    later_weights = ("w_mem_kv", "w_gate", "w_o_a", "w_o_b", "w_o_m", "w_out", "w_up", "w_down")
    outs = _in_proj(x2, pos_rows, attn_gain, p["w_in"].astype(BF16), colgain, freq_col, spread, one_row,
                    bd64, bd128, expand, [p[name] for name in later_weights])
    (qa0, ka0, va0, qa1, ka1, va1, qa2, ka2, va2, qb, kb, vb, mq) = outs[:13]
    wb = dict(zip(later_weights, outs[13:]))
    mk, mv = _mem_kv(mem, p["mem_norm"].reshape(1, D_MODEL), wb["w_mem_kv"],
                     p["m_k_norm"].reshape(1, M_HEAD_DIM))

    def per_batch(t):
        return t.reshape(batch, t.shape[0] // batch, t.shape[-1])

    def flat(t):
        return t.reshape(t.shape[0] * t.shape[1], t.shape[2])

    oa, la = [], []
    cfg = {1: dict(rows=2048, slabs=1), 4: dict(rows=seq // 4, slabs=2), 16: dict(rows=seq // 16, slabs=8)}
    for (window, dil), (q, k, v) in zip(A_GROUPS, ((qa0, ka0, va0), (qa1, ka1, va1), (qa2, ka2, va2))):
        assert window // dil == BLOCK
        o, lse = _band_attn(per_batch(q), per_batch(k), per_batch(v), name=f"band_attn_d{dil}",
                            prev_off=0, **cfg[dil])
        oa.append(flat(o))
        la.append(flat(lse))
    (ob,) = _band_attn(per_batch(qb), per_batch(kb), per_batch(vb), name="band_attn_swa", rows=1024,
                       slabs=B_KV_HEADS, prev_off=BLOCK - (B_WINDOW - 1), sinks=p["b_sinks"],
                       with_lse=False)
    ob = flat(ob)

    x2 = _mix(x2, attn_gain, oa, la, ob, mq, mk, mv, wb["w_gate"],
              p["b_gate"].reshape(1, N_BRANCH * D_MODEL), wb["w_o_a"], wb["w_o_b"], wb["w_o_m"],
              wb["w_out"], seq)
    return _conv_ffn(x2, p["ffn_norm"].reshape(1, D_MODEL), wb["w_up"], p["conv_w"],
                     p["conv_b"].reshape(1, 2 * D_FF), wb["w_down"], seq)


def kernel(x, mem, positions, attn_norm, w_in, a_q_norm, a_k_norm, b_q_norm, b_k_norm, b_sinks,
           mem_norm, w_mem_kv, m_q_norm, m_k_norm, w_o_a, w_o_b, w_o_m, w_gate, b_gate, w_out,
           ffn_norm, w_up, conv_w, conv_b, w_down):
    batch, seq, _ = x.shape
    params = dict(attn_norm=attn_norm, w_in=w_in, a_q_norm=a_q_norm, a_k_norm=a_k_norm,
                  b_q_norm=b_q_norm, b_k_norm=b_k_norm, b_sinks=b_sinks, mem_norm=mem_norm,
                  w_mem_kv=w_mem_kv, m_q_norm=m_q_norm, m_k_norm=m_k_norm, w_o_a=w_o_a, w_o_b=w_o_b,
                  w_o_m=w_o_m, w_gate=w_gate, b_gate=b_gate, w_out=w_out, ffn_norm=ffn_norm,
                  w_up=w_up, conv_w=conv_w, conv_b=conv_b, w_down=w_down)
    pos_rows = positions.astype(F32).reshape(batch * seq // TOKEN_BLOCK, 1, TOKEN_BLOCK)
    x2 = x.reshape(batch * seq, D_MODEL)
    for layer in range(attn_norm.shape[0]):
        x2 = _layer(x2, mem, pos_rows, batch, seq, {k: v[layer] for k, v in params.items()})
    return x2.reshape(batch, seq, D_MODEL)
```

```python
import functools
import math

import numpy as np

import jax
import jax.numpy as jnp
from jax import lax
from jax.experimental import pallas as pl
from jax.experimental.pallas import tpu as pltpu

D_MODEL = 1024
HEAD_DIM = 64
A_GROUPS = ((128, 1), (512, 4), (2048, 16))
A_HEADS = 4
A_SLAB = A_HEADS * HEAD_DIM
A_QKV_COLS = len(A_GROUPS) * 3 * A_SLAB
B_Q_HEADS = 8
B_KV_HEADS = 2
B_GROUP = B_Q_HEADS // B_KV_HEADS
B_WINDOW = 128
B_Q_COLS = B_Q_HEADS * HEAD_DIM
B_KV_COLS = B_KV_HEADS * HEAD_DIM
M_HEADS = 4
M_HEAD_DIM = 128
M_Q_COLS = M_HEADS * M_HEAD_DIM
B_Q_OFF = A_QKV_COLS
B_K_OFF = B_Q_OFF + B_Q_COLS
B_V_OFF = B_K_OFF + B_KV_COLS
M_Q_OFF = B_V_OFF + B_KV_COLS
IN_COLS = M_Q_OFF + M_Q_COLS
N_NORMED_TILES = 2 * len(A_GROUPS) + B_Q_COLS // A_SLAB + 1 + M_Q_COLS // A_SLAB
N_BRANCH = 3
D_FF = 2816
CONV_WIDTH = 3
ROPE_THETA = 500000.0
ROPE_DIM = HEAD_DIM // 4
ROPE_HALF = ROPE_DIM // 2
BLOCK = 128
EPS = 1e-6
NEG = -1e30
LOG2E = math.log2(math.e)

LANES = 128
BF16_ROWS = 16
TOKEN_BLOCK = 512
FFN_TOKEN_BLOCK = 1024
ROW_CHUNK = 64
FF_CHUNK = 256
VMEM_LIMIT = 56 * 1024 * 1024

F32 = jnp.float32
BF16 = jnp.bfloat16


def _const_spec(shape):
    return pl.BlockSpec(shape, lambda *_: (0,) * len(shape), pipeline_mode=pl.Buffered(1))


def _rms_rows(x, gain):
    ms = jnp.mean(x * x, axis=-1, keepdims=True)
    return x * lax.rsqrt(ms + EPS) * gain


def _mem_kv_kernel(mem_ref, gain_ref, w_ref, kgain_ref, mk_ref, mv_ref):
    hm = _rms_rows(mem_ref[...], gain_ref[...]).astype(BF16)
    kv = jnp.dot(hm, w_ref[...], preferred_element_type=F32)
    ks = []
    for h in range(M_HEADS):
        kh = kv[:, h * M_HEAD_DIM:(h + 1) * M_HEAD_DIM]
        ks.append(_rms_rows(kh, kgain_ref[...]))
    mk_ref[...] = jnp.concatenate(ks, axis=1).astype(BF16)
    mv_ref[...] = kv[:, M_Q_COLS:].astype(BF16)


def _mem_kv(mem, mem_gain, w_kv, k_gain):
    b, m, _ = mem.shape
    return pl.pallas_call(
        _mem_kv_kernel,
        grid=(b,),
        in_specs=[pl.BlockSpec((None, m, D_MODEL), lambda i: (i, 0, 0)),
                  _const_spec((1, D_MODEL)),
                  _const_spec((D_MODEL, 2 * M_Q_COLS)),
                  _const_spec((1, M_HEAD_DIM))],
        out_specs=[pl.BlockSpec((None, m, M_Q_COLS), lambda i: (i, 0, 0)),
                   pl.BlockSpec((None, m, M_Q_COLS), lambda i: (i, 0, 0))],
        out_shape=[jax.ShapeDtypeStruct((b, m, M_Q_COLS), BF16)] * 2,
        compiler_params=pltpu.CompilerParams(dimension_semantics=("arbitrary",),
                                             vmem_limit_bytes=VMEM_LIMIT),
        name="mem_kv",
    )(mem, mem_gain, w_kv, k_gain)


def _in_proj_kernel(x_ref, pos_ref, gain_ref, w_ref, colgain_ref, freq_ref, spread_ref, one_ref,
                    bd64_ref, bd128_ref, expand_ref, *refs, n_cast):
    cast_in, refs = refs[:n_cast], refs[n_cast:]
    (qa0, ka0, va0, qa1, ka1, va1, qa2, ka2, va2, qb_ref, kb_ref, vb_ref, mq_ref) = refs[:13]
    cast_out, refs = refs[13:13 + n_cast], refs[13 + n_cast:]
    proj_ref, ss_ref, h_ref, cos_ref, sin_lo_ref, sin_hi_ref, kvb_ref = refs[:7]
    stage_refs = list(refs[7:])
    tb = x_ref.shape[0]

    for src, dst in zip(cast_in, cast_out):
        dst[...] = src[...].astype(BF16)
    chunks = [slice(r, r + ROW_CHUNK) for r in range(0, tb, ROW_CHUNK)]

    @pl.when(pl.program_id(0) == 0)
    def _():
        proj_ref[...] = jnp.zeros(proj_ref.shape, F32)
        ss_ref[...] = jnp.zeros(ss_ref.shape, F32)

    for rows in chunks:
        h_ref[rows, :] = _rms_rows(x_ref[rows, :], gain_ref[...]).astype(BF16)

    ang = freq_ref[...] * pos_ref[...]
    trig = jnp.concatenate([jnp.cos(ang), jnp.sin(ang),
                            jnp.zeros((LANES - ROPE_DIM, tb), F32)], axis=0).T
    tables = jnp.zeros((tb, 3 * LANES), F32)
    for _ in range(3):
        part = trig.astype(BF16)
        tables = tables + jnp.dot(part, spread_ref[...], preferred_element_type=F32)
        trig = trig - part.astype(F32)
    cos_ref[...] = tables[:, 0:LANES] + one_ref[...]
    sin_lo_ref[...] = tables[:, LANES:2 * LANES]
    sin_hi_ref[...] = tables[:, 2 * LANES:3 * LANES]

    ss_slots = iter(range(ss_ref.shape[0]))
    pending = []

    def flush_pending():
        while pending:
            new, bd, slot = pending.pop()
            ss_ref[slot, :, 0:new.shape[1]] = jnp.dot((new * new).astype(BF16), bd,
                                                      preferred_element_type=F32)

    def proj(c0, width, bd=None):
        tile, off = divmod(c0, A_SLAB)
        assert off + width <= A_SLAB and off % LANES == 0
        t = proj_ref[tile, :, off:off + width]
        ss = None
        if bd is not None:
            slot = next(ss_slots)
            ss = ss_ref[slot, :, 0:width]
        new = jnp.dot(h_ref[...], w_ref[:, c0:c0 + width], preferred_element_type=F32)
        proj_ref[tile, :, off:off + width] = new
        flush_pending()
        if bd is not None:
            pending.append((new, bd, slot))
        return t, ss

    def rope(y, rows):
        parts = []
        for c in range(y.shape[1] // LANES):
            yc = y[:, c * LANES:(c + 1) * LANES]
            parts.append(yc * cos_ref[rows, :]
                         + pltpu.roll(yc, LANES - ROPE_HALF, 1) * sin_lo_ref[rows, :]
                         + pltpu.roll(yc, ROPE_HALF, 1) * sin_hi_ref[rows, :])
        return parts[0] if len(parts) == 1 else jnp.concatenate(parts, axis=1)

    def finish(t_ss, emit, c0, *, dim=None, rotary=False):
        t, ss = t_ss
        if ss is not None:
            gain = colgain_ref[:, c0:c0 + t.shape[1]]
        for rows in chunks:
            y = t[rows]
            if ss is not None:
                y = y * lax.rsqrt(ss[rows] * (1.0 / dim) + EPS) * gain
            if rotary:
                y = rope(y, rows)
            emit(rows, y)

    def to_ref(out_ref, col0=0):
        def emit(rows, y):
            out_ref[rows, col0:col0 + y.shape[1]] = y.astype(BF16)
        return emit

    def store_tile(out_ref, t, dil, c0, **kw):
        if dil == 1:
            finish(t, to_ref(out_ref), c0, **kw)
            return
        stage_ref = stage_refs.pop()

        def emit(rows, y):
            for s in range(A_SLAB // LANES):
                stage_ref[s, rows, :] = y[:, s * LANES:(s + 1) * LANES]
        finish(t, emit, c0, **kw)
        for r in range(dil):
            for s in range(A_SLAB // LANES):
                col = r * A_SLAB + s * LANES
                out_ref[:, col:col + LANES] = stage_ref[s, pl.ds(r, tb // dil, stride=dil), :].astype(BF16)

    bd64 = bd64_ref[...]
    qk = dict(dim=HEAD_DIM, rotary=True)
    for g, (q_ref, k_ref, v_ref) in enumerate(((qa0, ka0, va0), (qa1, ka1, va1), (qa2, ka2, va2))):
        c0 = g * 3 * A_SLAB
        dil = A_GROUPS[g][1]
        store_tile(q_ref, proj(c0, A_SLAB, bd64), dil, c0, **qk)
        store_tile(k_ref, proj(c0 + A_SLAB, A_SLAB, bd64), dil, c0 + A_SLAB, **qk)
        store_tile(v_ref, proj(c0 + 2 * A_SLAB, A_SLAB), dil, c0 + 2 * A_SLAB)

    for s in range(B_Q_COLS // A_SLAB):
        c0 = B_Q_OFF + s * A_SLAB
        finish(proj(c0, A_SLAB, bd64), to_ref(qb_ref, s * A_SLAB), c0, **qk)

    finish(proj(B_K_OFF, B_KV_COLS, bd64_ref[0:B_KV_COLS, 0:B_KV_COLS]), to_ref(kvb_ref), B_K_OFF, **qk)
    kb_ref[...] = jnp.dot(kvb_ref[...], expand_ref[...], preferred_element_type=F32).astype(BF16)
    vb = proj(B_V_OFF, B_KV_COLS)[0].astype(BF16)
    vb_ref[...] = jnp.dot(vb, expand_ref[...], preferred_element_type=F32).astype(BF16)

    bd128 = bd128_ref[...]
    for s in range(M_Q_COLS // A_SLAB):
        c0 = M_Q_OFF + s * A_SLAB
        finish(proj(c0, A_SLAB, bd128), to_ref(mq_ref, s * A_SLAB), c0, dim=M_HEAD_DIM)
    flush_pending()


def _cast_rows(rows, n_steps):
    for per_step in range(BF16_ROWS, rows + 1, BF16_ROWS):
        if rows % per_step == 0 and rows // per_step <= n_steps:
            return per_step
    raise ValueError(f"no bf16-aligned split of {rows} rows over {n_steps} steps")


def _in_proj(x2, pos_rows, attn_gain, w_in, colgain, freq_col, spread, one_row, bd64, bd128, expand,
             cast_weights):
    t = x2.shape[0]
    tb = TOKEN_BLOCK
    dils = [dil for _, dil in A_GROUPS for _ in range(3)] + [1, 1, 1, 1]
    widths = [A_SLAB] * 9 + [B_Q_COLS, B_Q_COLS, B_Q_COLS, M_Q_COLS]
    n_staged = sum(d > 1 for d in dils)
    n = t // tb
    done = lambda i: (jnp.maximum(i - 1, 0), 0)

    def cast_spec(w):
        per_step = _cast_rows(w.shape[0], n + 1)
        last = w.shape[0] // per_step - 1
        return pl.BlockSpec((per_step, w.shape[1]), lambda i: (jnp.minimum(i, last), 0))

    cast_specs = [cast_spec(w) for w in cast_weights]
    return pl.pallas_call(
        functools.partial(_in_proj_kernel, n_cast=len(cast_weights)),
        grid=(n + 1,),
        in_specs=[pl.BlockSpec((tb, D_MODEL), lambda i: (jnp.minimum(i, n - 1), 0)),
                  pl.BlockSpec((None, 1, tb), lambda i: (jnp.maximum(i - 1, 0), 0, 0)),
                  _const_spec((1, D_MODEL)),
                  _const_spec((D_MODEL, IN_COLS)),
                  _const_spec((1, IN_COLS)),
                  _const_spec((ROPE_HALF, 1)),
                  _const_spec((LANES, 3 * LANES)),
                  _const_spec((1, LANES)),
                  _const_spec((A_SLAB, A_SLAB)),
                  _const_spec((A_SLAB, A_SLAB)),
                  _const_spec((B_KV_COLS, B_Q_COLS))] + cast_specs,
        out_specs=[pl.BlockSpec((tb // d, w * d), done) for w, d in zip(widths, dils)] + cast_specs,
        out_shape=[jax.ShapeDtypeStruct((t // d, w * d), BF16) for w, d in zip(widths, dils)]
                  + [jax.ShapeDtypeStruct(w.shape, BF16) for w in cast_weights],
        scratch_shapes=[pltpu.VMEM((IN_COLS // A_SLAB, tb, A_SLAB), F32),
                        pltpu.VMEM((N_NORMED_TILES, tb, A_SLAB), F32)]
                       + [pltpu.VMEM((tb, D_MODEL), BF16)] + [pltpu.VMEM((tb, LANES), F32)] * 3
                       + [pltpu.VMEM((tb, B_KV_COLS), BF16)]
                       + [pltpu.VMEM((A_SLAB // LANES, tb, LANES), F32)] * n_staged,
        compiler_params=pltpu.CompilerParams(dimension_semantics=("arbitrary",),
                                             vmem_limit_bytes=VMEM_LIMIT),
        name="in_proj",
    )(x2, pos_rows, attn_gain, w_in, colgain, freq_col, spread, one_row, bd64, bd128, expand,
      *cast_weights)


def _band_attn_kernel(*refs, rows, slabs, prev_off, has_sink, with_lse):
    q_ref, kc_ref, kp_ref, vc_ref, vp_ref = refs[:5]
    pos = 5
    sink_ref = None
    if has_sink:
        sink_ref = refs[pos]
        pos += 1
    o_ref = refs[pos]
    lse_ref = refs[pos + 1] if with_lse else None

    first_step = pl.program_id(2) == 0
    row = lax.broadcasted_iota(jnp.int32, (BLOCK, 2 * BLOCK), 0)
    col = lax.broadcasted_iota(jnp.int32, (BLOCK, 2 * BLOCK), 1)
    bias = jnp.where(col < BLOCK,
                     jnp.where(col >= row + prev_off, 0.0, NEG),
                     jnp.where(col - BLOCK <= row, 0.0, NEG)).astype(F32)
    bias_first = jnp.where(jnp.logical_and(first_step, col < BLOCK), NEG, bias)
    head_of_lane = lax.broadcasted_iota(jnp.int32, (1, A_SLAB), 1) // HEAD_DIM
    head_mask = [(head_of_lane == hh).astype(BF16) for hh in range(A_HEADS)]
    pack_row = lax.broadcasted_iota(jnp.int32, (BF16_ROWS, A_SLAB), 0)
    nblk = rows // BLOCK

    def drop_first_row(t):
        top = jnp.where(pack_row == 0, jnp.zeros((BF16_ROWS, A_SLAB), BF16), t[0:BF16_ROWS])
        return jnp.concatenate([top, t[BF16_ROWS:]], axis=0)

    for w in range(slabs):
        cs = slice(w * A_SLAB, (w + 1) * A_SLAB)
        biases = {False: [bias] * A_HEADS, True: [bias_first] * A_HEADS}
        if has_sink:
            biases = {first: [jnp.where(col == 0, sink_ref[w * A_HEADS + hh] * LOG2E, b[hh])
                              for hh in range(A_HEADS)] for first, b in biases.items()}
        for jb in range(nblk):
            q = q_ref[jb * BLOCK:(jb + 1) * BLOCK, cs]
            if jb == 0:
                k2 = jnp.concatenate([kp_ref[:, cs], kc_ref[0:BLOCK, cs]], axis=0)
                v2 = jnp.concatenate([vp_ref[:, cs], vc_ref[0:BLOCK, cs]], axis=0)
            else:
                k2 = kc_ref[(jb - 1) * BLOCK:(jb + 1) * BLOCK, cs]
                v2 = vc_ref[(jb - 1) * BLOCK:(jb + 1) * BLOCK, cs]
            if has_sink:
                k2 = drop_first_row(k2)
                v2 = drop_first_row(v2)
            qs = jnp.concatenate([q * head_mask[hh] for hh in range(A_HEADS)], axis=0)
            s = lax.dot_general(qs, k2, (((1,), (1,)), ((), ())), preferred_element_type=F32)
            ps, ms, ls = [], [], []
            for hh in range(A_HEADS):
                sh = s[hh * BLOCK:(hh + 1) * BLOCK] + biases[jb == 0][hh]
                m = jnp.max(sh, axis=-1, keepdims=True)
                p = jnp.exp2(sh - m)
                ms.append(m)
                ls.append(jnp.sum(p, axis=-1, keepdims=True))
                ps.append(p.astype(BF16))
            ost = jnp.dot(jnp.concatenate(ps, axis=0), v2, preferred_element_type=F32)
            o = jnp.zeros((BLOCK, A_SLAB), F32)
            lse = jnp.zeros((BLOCK, A_SLAB), F32)
            for hh in range(A_HEADS):
                o = jnp.where(head_of_lane == hh, ost[hh * BLOCK:(hh + 1) * BLOCK] * (1.0 / ls[hh]), o)
                if with_lse:
                    lse = jnp.where(head_of_lane == hh, ms[hh] + jnp.log2(ls[hh]), lse)
            o_ref[jb * BLOCK:(jb + 1) * BLOCK, cs] = o.astype(o_ref.dtype)
            if with_lse:
                lse_ref[jb * BLOCK:(jb + 1) * BLOCK, cs] = lse


def _band_attn(q, k, v, *, name, rows, slabs, prev_off, sinks=None, with_lse=True):
    b, length, width = q.shape
    assert sinks is None or prev_off >= 1
    n_col = width // (slabs * A_SLAB)
    n_row = length // rows
    blk_per_step = rows // BLOCK
    cur = pl.BlockSpec((None, rows, slabs * A_SLAB), lambda i, r, j: (i, j, r))
    prev = pl.BlockSpec((None, BLOCK, slabs * A_SLAB),
                        lambda i, r, j: (i, jnp.maximum(j * blk_per_step - 1, 0), r))
    in_specs = [cur, cur, prev, cur, prev]
    args = [q, k, k, v, v]
    if sinks is not None:
        in_specs.append(pl.BlockSpec(memory_space=pltpu.SMEM))
        args.append(sinks)
    out_specs = [cur]
    out_shape = [jax.ShapeDtypeStruct((b, length, width), BF16)]
    if with_lse:
        out_specs.append(cur)
        out_shape.append(jax.ShapeDtypeStruct((b, length, width), F32))
    return pl.pallas_call(
        functools.partial(_band_attn_kernel, rows=rows, slabs=slabs, prev_off=prev_off,
                          has_sink=sinks is not None, with_lse=with_lse),
        grid=(b, n_col, n_row),
        in_specs=in_specs,
        out_specs=out_specs,
        out_shape=out_shape,
        compiler_params=pltpu.CompilerParams(dimension_semantics=("arbitrary",) * 3,
                                             vmem_limit_bytes=VMEM_LIMIT),
        name=name,
    )(*args)


def _mix_kernel(x_ref, gain_ref, oa0, la0, oa1, la1, oa2, la2, ob_ref, mq_ref, mk_ref, mv_ref,
                wg_ref, bg_ref, woa_ref, wob_ref, wom_ref, wout_ref, out_ref, *stage_refs):
    x = x_ref[...]
    h = _rms_rows(x, gain_ref[...]).astype(BF16)
    stage_refs = list(stage_refs)

    def token_major(blk_ref, dil):
        if dil == 1:
            return blk_ref[...].astype(F32)
        stage_ref = stage_refs.pop()
        rows = blk_ref.shape[0]
        for r in range(dil):
            for s in range(A_SLAB // LANES):
                c0 = r * A_SLAB + s * LANES
                stage_ref[s, pl.ds(r, rows, stride=dil), :] = blk_ref[:, c0:c0 + LANES].astype(F32)
        return jnp.concatenate([stage_ref[s] for s in range(A_SLAB // LANES)], axis=1)

    head_cols = [slice(hh * M_HEAD_DIM, (hh + 1) * M_HEAD_DIM) for hh in range(M_HEADS)]
    scores = [lax.dot_general(mq_ref[:, cs], mk_ref[:, cs], (((1,), (1,)), ((), ())),
                              preferred_element_type=F32) for cs in head_cols]
    gate_cols = [slice(i * D_MODEL, (i + 1) * D_MODEL) for i in range(N_BRANCH)]
    gates = []
    for cs in gate_cols:
        z = jnp.dot(h, wg_ref[:, cs], preferred_element_type=F32) + bg_ref[:, cs]
        gates.append(0.5 + 0.5 * jnp.tanh(0.5 * z))

    dils = [dil for _, dil in A_GROUPS]
    l0, l1, l2 = (token_major(r, d) for r, d in zip((la0, la1, la2), dils))
    o0, o1, o2 = (token_major(r, d) for r, d in zip((oa0, oa1, oa2), dils))
    mx = jnp.maximum(jnp.maximum(l0, l1), l2)
    e0, e1, e2 = jnp.exp2(l0 - mx), jnp.exp2(l1 - mx), jnp.exp2(l2 - mx)
    o_a = (e0 * o0 + e1 * o1 + e2 * o2) / (e0 + e1 + e2)

    o_m = []
    for s, cs in zip(scores, head_cols):
        p = jnp.exp2(s - jnp.max(s, axis=-1, keepdims=True))
        l = jnp.sum(p, axis=-1, keepdims=True)
        o_m.append(jnp.dot(p.astype(BF16), mv_ref[:, cs], preferred_element_type=F32) / l)
    o_m = jnp.concatenate(o_m, axis=1)

    merged = jnp.zeros(x.shape, F32)
    for gate, o, w_ref in zip(gates, (o_a, ob_ref[...], o_m), (woa_ref, wob_ref, wom_ref)):
        merged = merged + gate * jnp.dot(o.astype(BF16), w_ref[...], preferred_element_type=F32)
    out_ref[...] = x + jnp.dot(merged.astype(BF16), wout_ref[...], preferred_element_type=F32)


def _mix(x2, attn_gain, oa, la, ob, mq, mk, mv, w_gate, b_gate, w_o_a, w_o_b, w_o_m, w_out, seq):
    t = x2.shape[0]
    tb = TOKEN_BLOCK
    mem_len = mk.shape[1]
    per_seq = seq // tb
    tok = lambda w, d=1: pl.BlockSpec((tb // d, w * d), lambda i: (i, 0))
    mem = pl.BlockSpec((None, mem_len, M_Q_COLS), lambda i: (i // per_seq, 0, 0))
    dils = [dil for _, dil in A_GROUPS]
    return pl.pallas_call(
        _mix_kernel,
        grid=(t // tb,),
        in_specs=[tok(D_MODEL), _const_spec((1, D_MODEL))]
                 + [tok(A_SLAB, d) for d in dils for _ in range(2)]
                 + [tok(B_Q_COLS), tok(M_Q_COLS), mem, mem,
                  _const_spec((D_MODEL, N_BRANCH * D_MODEL)), _const_spec((1, N_BRANCH * D_MODEL)),
                  _const_spec((A_SLAB, D_MODEL)), _const_spec((B_Q_COLS, D_MODEL)),
                  _const_spec((M_Q_COLS, D_MODEL)), _const_spec((D_MODEL, D_MODEL))],
        out_specs=tok(D_MODEL),
        out_shape=jax.ShapeDtypeStruct((t, D_MODEL), F32),
        scratch_shapes=[pltpu.VMEM((A_SLAB // LANES, tb, LANES), F32)] * (2 * sum(d > 1 for d in dils)),
        compiler_params=pltpu.CompilerParams(dimension_semantics=("arbitrary",),
                                             vmem_limit_bytes=VMEM_LIMIT),
        name="mix",
    )(x2, attn_gain, oa[0], la[0], oa[1], la[1], oa[2], la[2], ob, mq, mk, mv,
      w_gate, b_gate, w_o_a, w_o_b, w_o_m, w_out)


def _conv_ffn_kernel(x_ref, gain_ref, wup_ref, cw_ref, cb_ref, wdown_ref, out_ref,
                     carry_ref, ext_ref, act_ref, *, per_seq):
    tb = x_ref.shape[0]
    halo = CONV_WIDTH - 1
    pad = 8

    @pl.when(pl.program_id(0) % per_seq == 0)
    def _():
        carry_ref[...] = jnp.zeros(carry_ref.shape, F32)

    x = x_ref[...]
    h = _rms_rows(x, gain_ref[...]).astype(BF16)

    def conv(c0):
        cs = slice(c0, c0 + FF_CHUNK)
        u = jnp.dot(h, wup_ref[:, cs], preferred_element_type=F32)
        ext_ref[0:pad, :] = carry_ref[:, cs]
        ext_ref[pad:pad + tb, :] = u
        carry_ref[:, cs] = u[tb - pad:tb, :]
        c = cb_ref[:, cs] + cw_ref[CONV_WIDTH - 1:CONV_WIDTH, cs] * u
        for j in range(halo):
            c = c + cw_ref[j:j + 1, cs] * ext_ref[pad - halo + j:pad - halo + j + tb, :]
        return c

    for ch in range(D_FF // FF_CHUNK):
        a = conv(ch * FF_CHUNK)
        g = conv(D_FF + ch * FF_CHUNK)
        half = 0.5 * a
        act = (half + half * jnp.tanh(half)) * g
        act_ref[:, ch * FF_CHUNK:(ch + 1) * FF_CHUNK] = act.astype(BF16)
    out_ref[...] = x + jnp.dot(act_ref[...], wdown_ref[...], preferred_element_type=F32)


def _conv_ffn(x2, gain, w_up, conv_w, conv_b, w_down, seq):
    t = x2.shape[0]
    tb = FFN_TOKEN_BLOCK
    return pl.pallas_call(
        functools.partial(_conv_ffn_kernel, per_seq=seq // tb),
        grid=(t // tb,),
        in_specs=[pl.BlockSpec((tb, D_MODEL), lambda i: (i, 0)),
                  _const_spec((1, D_MODEL)),
                  _const_spec((D_MODEL, 2 * D_FF)),
                  _const_spec((CONV_WIDTH, 2 * D_FF)),
                  _const_spec((1, 2 * D_FF)),
                  _const_spec((D_FF, D_MODEL))],
        out_specs=pl.BlockSpec((tb, D_MODEL), lambda i: (i, 0)),
        out_shape=jax.ShapeDtypeStruct((t, D_MODEL), F32),
        scratch_shapes=[pltpu.VMEM((8, 2 * D_FF), F32),
                        pltpu.VMEM((tb + 8, FF_CHUNK), F32),
                        pltpu.VMEM((tb, D_FF), BF16)],
        compiler_params=pltpu.CompilerParams(dimension_semantics=("arbitrary",),
                                             vmem_limit_bytes=VMEM_LIMIT),
        name="conv_ffn",
    )(x2, gain, w_up, conv_w, conv_b, w_down)


def _tables():
    lane = np.arange(A_SLAB)
    bd64 = jnp.asarray(lane[:, None] // HEAD_DIM == lane[None, :] // HEAD_DIM, BF16)
    bd128 = jnp.asarray(lane[:, None] // M_HEAD_DIM == lane[None, :] // M_HEAD_DIM, BF16)
    src = np.arange(B_KV_COLS)
    dst = np.arange(B_Q_COLS)
    expand = jnp.asarray(src[:, None] == (dst[None, :] // A_SLAB) * HEAD_DIM + dst[None, :] % HEAD_DIM, BF16)
    freq_col = jnp.exp(jnp.arange(ROPE_HALF, dtype=F32) * (-2.0 * math.log(ROPE_THETA) / ROPE_DIM))
    freq_col = freq_col.reshape(ROPE_HALF, 1)
    in_head = np.arange(LANES) % HEAD_DIM
    j = np.arange(LANES)[:, None]
    rot = in_head[None, :] < ROPE_DIM
    cos_sel = rot & (j == in_head[None, :] % ROPE_HALF)
    sin_sel = rot & (j == ROPE_HALF + in_head[None, :] % ROPE_HALF)
    lo = in_head[None, :] < ROPE_HALF
    spread = np.concatenate([cos_sel.astype(np.float32), -(sin_sel & lo).astype(np.float32),
                             (sin_sel & ~lo).astype(np.float32)], axis=1)
    one_row = (in_head >= ROPE_DIM).astype(np.float32).reshape(1, LANES)
    return bd64, bd128, expand, freq_col, jnp.asarray(spread, BF16), jnp.asarray(one_row)


def _layer(x2, mem, pos_rows, batch, seq, p):
    bd64, bd128, expand, freq_col, spread, one_row = _tables()
    gains = jnp.concatenate([p["a_q_norm"].reshape(-1), p["a_k_norm"].reshape(-1), p["b_q_norm"],
                             p["b_k_norm"], p["m_q_norm"], jnp.ones((1,), F32)])
    n_a = len(A_GROUPS) * HEAD_DIM
    one_at = 2 * n_a + 2 * HEAD_DIM + M_HEAD_DIM
    col = np.arange(IN_COLS)
    src = np.full(IN_COLS, one_at)
    scale = np.ones(IN_COLS, np.float32)
    for g in range(len(A_GROUPS)):
        q_cols = slice(g * 3 * A_SLAB, g * 3 * A_SLAB + A_SLAB)
        k_cols = slice(g * 3 * A_SLAB + A_SLAB, g * 3 * A_SLAB + 2 * A_SLAB)
        src[q_cols] = g * HEAD_DIM + col[q_cols] % HEAD_DIM
        src[k_cols] = n_a + g * HEAD_DIM + col[k_cols] % HEAD_DIM
        scale[q_cols] = HEAD_DIM ** -0.5 * LOG2E
    src[B_Q_OFF:B_K_OFF] = 2 * n_a + col[B_Q_OFF:B_K_OFF] % HEAD_DIM
    scale[B_Q_OFF:B_K_OFF] = HEAD_DIM ** -0.5 * LOG2E
    src[B_K_OFF:B_V_OFF] = 2 * n_a + HEAD_DIM + col[B_K_OFF:B_V_OFF] % HEAD_DIM
    src[M_Q_OFF:] = 2 * n_a + 2 * HEAD_DIM + (col[M_Q_OFF:] - M_Q_OFF) % M_HEAD_DIM
    scale[M_Q_OFF:] = M_HEAD_DIM ** -0.5 * LOG2E
    colgain = (gains[src] * scale).reshape(1, IN_COLS)

    attn_gain = p["attn_norm"].reshape(1, D_MODEL)
    later_weights = ("w_mem_kv", "w_gate", "w_o_a", "w_o_b", "w_o_m", "w_out", "w_up", "w_down")
    outs = _in_proj(x2, pos_rows, attn_gain, p["w_in"].astype(BF16), colgain, freq_col, spread, one_row,
                    bd64, bd128, expand, [p[name] for name in later_weights])
    (qa0, ka0, va0, qa1, ka1, va1, qa2, ka2, va2, qb, kb, vb, mq) = outs[:13]
    wb = dict(zip(later_weights, outs[13:]))
    mk, mv = _mem_kv(mem, p["mem_norm"].reshape(1, D_MODEL), wb["w_mem_kv"],
                     p["m_k_norm"].reshape(1, M_HEAD_DIM))

    def per_batch(t):
        return t.reshape(batch, t.shape[0] // batch, t.shape[-1])

    def flat(t):
        return t.reshape(t.shape[0] * t.shape[1], t.shape[2])

    oa, la = [], []
    cfg = {1: dict(rows=2048, slabs=1), 4: dict(rows=seq // 4, slabs=2), 16: dict(rows=seq // 16, slabs=8)}
    for (window, dil), (q, k, v) in zip(A_GROUPS, ((qa0, ka0, va0), (qa1, ka1, va1), (qa2, ka2, va2))):
        assert window // dil == BLOCK
        o, lse = _band_attn(per_batch(q), per_batch(k), per_batch(v), name=f"band_attn_d{dil}",
                            prev_off=0, **cfg[dil])
        oa.append(flat(o))
        la.append(flat(lse))
    (ob,) = _band_attn(per_batch(qb), per_batch(kb), per_batch(vb), name="band_attn_swa", rows=1024,
                       slabs=B_KV_HEADS, prev_off=BLOCK - (B_WINDOW - 1), sinks=p["b_sinks"],
                       with_lse=False)
    ob = flat(ob)

    x2 = _mix(x2, attn_gain, oa, la, ob, mq, mk, mv, wb["w_gate"],
              p["b_gate"].reshape(1, N_BRANCH * D_MODEL), wb["w_o_a"], wb["w_o_b"], wb["w_o_m"],
              wb["w_out"], seq)
    return _conv_ffn(x2, p["ffn_norm"].reshape(1, D_MODEL), wb["w_up"], p["conv_w"],
                     p["conv_b"].reshape(1, 2 * D_FF), wb["w_down"], seq)


def kernel(x, mem, positions, attn_norm, w_in, a_q_norm, a_k_norm, b_q_norm, b_k_norm, b_sinks,
           mem_norm, w_mem_kv, m_q_norm, m_k_norm, w_o_a, w_o_b, w_o_m, w_gate, b_gate, w_out,
           ffn_norm, w_up, conv_w, conv_b, w_down):
    batch, seq, _ = x.shape
    params = dict(attn_norm=attn_norm, w_in=w_in, a_q_norm=a_q_norm, a_k_norm=a_k_norm,
                  b_q_norm=b_q_norm, b_k_norm=b_k_norm, b_sinks=b_sinks, mem_norm=mem_norm,
                  w_mem_kv=w_mem_kv, m_q_norm=m_q_norm, m_k_norm=m_k_norm, w_o_a=w_o_a, w_o_b=w_o_b,
                  w_o_m=w_o_m, w_gate=w_gate, b_gate=b_gate, w_out=w_out, ffn_norm=ffn_norm,
                  w_up=w_up, conv_w=conv_w, conv_b=conv_b, w_down=w_down)
    pos_rows = positions.astype(F32).reshape(batch * seq // TOKEN_BLOCK, 1, TOKEN_BLOCK)
    x2 = x.reshape(batch * seq, D_MODEL)
    for layer in range(attn_norm.shape[0]):
        x2 = _layer(x2, mem, pos_rows, batch, seq, {k: v[layer] for k, v in params.items()})
    return x2.reshape(batch, seq, D_MODEL)
```

```python
import functools
import math

import numpy as np

import jax
import jax.numpy as jnp
from jax import lax
from jax.experimental import pallas as pl
from jax.experimental.pallas import tpu as pltpu

D_MODEL = 1024
HEAD_DIM = 64
A_GROUPS = ((128, 1), (512, 4), (2048, 16))
A_HEADS = 4
A_SLAB = A_HEADS * HEAD_DIM
A_QKV_COLS = len(A_GROUPS) * 3 * A_SLAB
B_Q_HEADS = 8
B_KV_HEADS = 2
B_GROUP = B_Q_HEADS // B_KV_HEADS
B_WINDOW = 128
B_Q_COLS = B_Q_HEADS * HEAD_DIM
B_KV_COLS = B_KV_HEADS * HEAD_DIM
M_HEADS = 4
M_HEAD_DIM = 128
M_Q_COLS = M_HEADS * M_HEAD_DIM
B_Q_OFF = A_QKV_COLS
B_K_OFF = B_Q_OFF + B_Q_COLS
B_V_OFF = B_K_OFF + B_KV_COLS
M_Q_OFF = B_V_OFF + B_KV_COLS
IN_COLS = M_Q_OFF + M_Q_COLS
N_NORMED_TILES = 2 * len(A_GROUPS) + B_Q_COLS // A_SLAB + 1 + M_Q_COLS // A_SLAB
N_BRANCH = 3
D_FF = 2816
CONV_WIDTH = 3
ROPE_THETA = 500000.0
ROPE_DIM = HEAD_DIM // 4
ROPE_HALF = ROPE_DIM // 2
BLOCK = 128
EPS = 1e-6
NEG = -1e30
LOG2E = math.log2(math.e)

LANES = 128
BF16_ROWS = 16
HEADS_PER_TILE = LANES // HEAD_DIM
TOKEN_BLOCK = 512
ROW_CHUNK = 64
FF_CHUNK = 256
VMEM_LIMIT = 56 * 1024 * 1024

F32 = jnp.float32
BF16 = jnp.bfloat16


def _const_spec(shape):
    return pl.BlockSpec(shape, lambda *_: (0,) * len(shape), pipeline_mode=pl.Buffered(1))


def _rms_rows(x, gain):
    ms = jnp.mean(x * x, axis=-1, keepdims=True)
    return x * lax.rsqrt(ms + EPS) * gain


def _mem_kv_kernel(mem_ref, gain_ref, w_ref, kgain_ref, mk_ref, mv_ref):
    hm = _rms_rows(mem_ref[...], gain_ref[...]).astype(BF16)
    kv = jnp.dot(hm, w_ref[...], preferred_element_type=F32)
    ks = []
    for h in range(M_HEADS):
        kh = kv[:, h * M_HEAD_DIM:(h + 1) * M_HEAD_DIM]
        ks.append(_rms_rows(kh, kgain_ref[...]))
    mk_ref[...] = jnp.concatenate(ks, axis=1).astype(BF16)
    mv_ref[...] = kv[:, M_Q_COLS:].astype(BF16)


def _mem_kv(mem, mem_gain, w_kv, k_gain):
    b, m, _ = mem.shape
    return pl.pallas_call(
        _mem_kv_kernel,
        grid=(b,),
        in_specs=[pl.BlockSpec((None, m, D_MODEL), lambda i: (i, 0, 0)),
                  _const_spec((1, D_MODEL)),
                  _const_spec((D_MODEL, 2 * M_Q_COLS)),
                  _const_spec((1, M_HEAD_DIM))],
        out_specs=[pl.BlockSpec((None, m, M_Q_COLS), lambda i: (i, 0, 0)),
                   pl.BlockSpec((None, m, M_Q_COLS), lambda i: (i, 0, 0))],
        out_shape=[jax.ShapeDtypeStruct((b, m, M_Q_COLS), BF16)] * 2,
        compiler_params=pltpu.CompilerParams(dimension_semantics=("arbitrary",),
                                             vmem_limit_bytes=VMEM_LIMIT),
        name="mem_kv",
    )(mem, mem_gain, w_kv, k_gain)


def _in_proj_kernel(x_ref, pos_ref, gain_ref, w_ref, colgain_ref, freq_ref, spread_ref, one_ref,
                    bd64_ref, bd128_ref, expand_ref, *refs, n_cast):
    cast_in, refs = refs[:n_cast], refs[n_cast:]
    (qa0, ka0, va0, qa1, ka1, va1, qa2, ka2, va2, qb_ref, kb_ref, vb_ref, mq_ref) = refs[:13]
    cast_out, refs = refs[13:13 + n_cast], refs[13 + n_cast:]
    proj_ref, ss_ref, h_ref, cos_ref, sin_lo_ref, sin_hi_ref, kvb_ref = refs[:7]
    stage_refs = list(refs[7:])
    tb = x_ref.shape[0]

    for src, dst in zip(cast_in, cast_out):
        dst[...] = src[...].astype(BF16)
    chunks = [slice(r, r + ROW_CHUNK) for r in range(0, tb, ROW_CHUNK)]

    @pl.when(pl.program_id(0) == 0)
    def _():
        proj_ref[...] = jnp.zeros(proj_ref.shape, F32)
        ss_ref[...] = jnp.zeros(ss_ref.shape, F32)

    for rows in chunks:
        h_ref[rows, :] = _rms_rows(x_ref[rows, :], gain_ref[...]).astype(BF16)

    ang = freq_ref[...] * pos_ref[...]
    trig = jnp.concatenate([jnp.cos(ang), jnp.sin(ang),
                            jnp.zeros((LANES - ROPE_DIM, tb), F32)], axis=0).T
    tables = jnp.zeros((tb, 3 * LANES), F32)
    for _ in range(3):
        part = trig.astype(BF16)
        tables = tables + jnp.dot(part, spread_ref[...], preferred_element_type=F32)
        trig = trig - part.astype(F32)
    cos_ref[...] = tables[:, 0:LANES] + one_ref[...]
    sin_lo_ref[...] = tables[:, LANES:2 * LANES]
    sin_hi_ref[...] = tables[:, 2 * LANES:3 * LANES]

    ss_slots = iter(range(ss_ref.shape[0]))
    pending = []

    def flush_pending():
        while pending:
            new, bd, slot = pending.pop()
            ss_ref[slot, :, 0:new.shape[1]] = jnp.dot((new * new).astype(BF16), bd,
                                                      preferred_element_type=F32)

    def proj(c0, width, bd=None):
        tile, off = divmod(c0, A_SLAB)
        assert off + width <= A_SLAB and off % LANES == 0
        t = proj_ref[tile, :, off:off + width]
        ss = None
        if bd is not None:
            slot = next(ss_slots)
            ss = ss_ref[slot, :, 0:width]
        new = jnp.dot(h_ref[...], w_ref[:, c0:c0 + width], preferred_element_type=F32)
        proj_ref[tile, :, off:off + width] = new
        flush_pending()
        if bd is not None:
            pending.append((new, bd, slot))
        return t, ss

    def rope(y, rows):
        parts = []
        for c in range(y.shape[1] // LANES):
            yc = y[:, c * LANES:(c + 1) * LANES]
            parts.append(yc * cos_ref[rows, :]
                         + pltpu.roll(yc, LANES - ROPE_HALF, 1) * sin_lo_ref[rows, :]
                         + pltpu.roll(yc, ROPE_HALF, 1) * sin_hi_ref[rows, :])
        return parts[0] if len(parts) == 1 else jnp.concatenate(parts, axis=1)

    def finish(t_ss, emit, c0, *, dim=None, rotary=False):
        t, ss = t_ss
        if ss is not None:
            gain = colgain_ref[:, c0:c0 + t.shape[1]]
        for rows in chunks:
            y = t[rows]
            if ss is not None:
                y = y * lax.rsqrt(ss[rows] * (1.0 / dim) + EPS) * gain
            if rotary:
                y = rope(y, rows)
            emit(rows, y)

    def to_ref(out_ref, col0=0):
        def emit(rows, y):
            out_ref[rows, col0:col0 + y.shape[1]] = y.astype(BF16)
        return emit

    def store_tile(out_ref, t, dil, c0, **kw):
        if dil == 1:
            finish(t, to_ref(out_ref), c0, **kw)
            return
        stage_ref = stage_refs.pop()

        def emit(rows, y):
            for s in range(A_SLAB // LANES):
                stage_ref[s, rows, :] = y[:, s * LANES:(s + 1) * LANES]
        finish(t, emit, c0, **kw)
        for r in range(dil):
            for s in range(A_SLAB // LANES):
                col = r * A_SLAB + s * LANES
                out_ref[:, col:col + LANES] = stage_ref[s, pl.ds(r, tb // dil, stride=dil), :].astype(BF16)

    bd64 = bd64_ref[...]
    qk = dict(dim=HEAD_DIM, rotary=True)
    for g, (q_ref, k_ref, v_ref) in enumerate(((qa0, ka0, va0), (qa1, ka1, va1), (qa2, ka2, va2))):
        c0 = g * 3 * A_SLAB
        dil = A_GROUPS[g][1]
        store_tile(q_ref, proj(c0, A_SLAB, bd64), dil, c0, **qk)
        store_tile(k_ref, proj(c0 + A_SLAB, A_SLAB, bd64), dil, c0 + A_SLAB, **qk)
        store_tile(v_ref, proj(c0 + 2 * A_SLAB, A_SLAB), dil, c0 + 2 * A_SLAB)

    for s in range(B_Q_COLS // A_SLAB):
        c0 = B_Q_OFF + s * A_SLAB
        finish(proj(c0, A_SLAB, bd64), to_ref(qb_ref, s * A_SLAB), c0, **qk)

    finish(proj(B_K_OFF, B_KV_COLS, bd64_ref[0:B_KV_COLS, 0:B_KV_COLS]), to_ref(kvb_ref), B_K_OFF, **qk)
    kb_ref[...] = jnp.dot(kvb_ref[...], expand_ref[...], preferred_element_type=F32).astype(BF16)
    vb = proj(B_V_OFF, B_KV_COLS)[0].astype(BF16)
    vb_ref[...] = jnp.dot(vb, expand_ref[...], preferred_element_type=F32).astype(BF16)

    bd128 = bd128_ref[...]
    for s in range(M_Q_COLS // A_SLAB):
        c0 = M_Q_OFF + s * A_SLAB
        finish(proj(c0, A_SLAB, bd128), to_ref(mq_ref, s * A_SLAB), c0, dim=M_HEAD_DIM)
    flush_pending()


def _cast_rows(rows, n_steps):
    for per_step in range(BF16_ROWS, rows + 1, BF16_ROWS):
        if rows % per_step == 0 and rows // per_step <= n_steps:
            return per_step
    raise ValueError(f"no bf16-aligned split of {rows} rows over {n_steps} steps")


def _in_proj(x2, pos_rows, attn_gain, w_in, colgain, freq_col, spread, one_row, bd64, bd128, expand,
             cast_weights):
    t = x2.shape[0]
    tb = TOKEN_BLOCK
    dils = [dil for _, dil in A_GROUPS for _ in range(3)] + [1, 1, 1, 1]
    widths = [A_SLAB] * 9 + [B_Q_COLS, B_Q_COLS, B_Q_COLS, M_Q_COLS]
    n_staged = sum(d > 1 for d in dils)
    n = t // tb
    done = lambda i: (jnp.maximum(i - 1, 0), 0)

    def cast_spec(w):
        per_step = _cast_rows(w.shape[0], n + 1)
        last = w.shape[0] // per_step - 1
        return pl.BlockSpec((per_step, w.shape[1]), lambda i: (jnp.minimum(i, last), 0))

    cast_specs = [cast_spec(w) for w in cast_weights]
    return pl.pallas_call(
        functools.partial(_in_proj_kernel, n_cast=len(cast_weights)),
        grid=(n + 1,),
        in_specs=[pl.BlockSpec((tb, D_MODEL), lambda i: (jnp.minimum(i, n - 1), 0)),
                  pl.BlockSpec((None, 1, tb), lambda i: (jnp.maximum(i - 1, 0), 0, 0)),
                  _const_spec((1, D_MODEL)),
                  _const_spec((D_MODEL, IN_COLS)),
                  _const_spec((1, IN_COLS)),
                  _const_spec((ROPE_HALF, 1)),
                  _const_spec((LANES, 3 * LANES)),
                  _const_spec((1, LANES)),
                  _const_spec((A_SLAB, A_SLAB)),
                  _const_spec((A_SLAB, A_SLAB)),
                  _const_spec((B_KV_COLS, B_Q_COLS))] + cast_specs,
        out_specs=[pl.BlockSpec((tb // d, w * d), done) for w, d in zip(widths, dils)] + cast_specs,
        out_shape=[jax.ShapeDtypeStruct((t // d, w * d), BF16) for w, d in zip(widths, dils)]
                  + [jax.ShapeDtypeStruct(w.shape, BF16) for w in cast_weights],
        scratch_shapes=[pltpu.VMEM((IN_COLS // A_SLAB, tb, A_SLAB), F32),
                        pltpu.VMEM((N_NORMED_TILES, tb, A_SLAB), F32)]
                       + [pltpu.VMEM((tb, D_MODEL), BF16)] + [pltpu.VMEM((tb, LANES), F32)] * 3
                       + [pltpu.VMEM((tb, B_KV_COLS), BF16)]
                       + [pltpu.VMEM((A_SLAB // LANES, tb, LANES), F32)] * n_staged,
        compiler_params=pltpu.CompilerParams(dimension_semantics=("arbitrary",),
                                             vmem_limit_bytes=VMEM_LIMIT),
        name="in_proj",
    )(x2, pos_rows, attn_gain, w_in, colgain, freq_col, spread, one_row, bd64, bd128, expand,
      *cast_weights)


def _band_attn_kernel(*refs, rows, slabs, prev_off, has_sink, with_lse):
    q_ref, kc_ref, kp_ref, vc_ref, vp_ref = refs[:5]
    pos = 5
    sink_ref = None
    if has_sink:
        sink_ref = refs[pos]
        pos += 1
    o_ref = refs[pos]
    lse_ref = refs[pos + 1] if with_lse else None

    first_step = pl.program_id(2) == 0
    row = lax.broadcasted_iota(jnp.int32, (BLOCK, 2 * BLOCK), 0)
    col = lax.broadcasted_iota(jnp.int32, (BLOCK, 2 * BLOCK), 1)
    bias = jnp.where(col < BLOCK,
                     jnp.where(col >= row + prev_off, 0.0, NEG),
                     jnp.where(col - BLOCK <= row, 0.0, NEG)).astype(F32)
    bias_first = jnp.where(jnp.logical_and(first_step, col < BLOCK), NEG, bias)
    head_of_lane = lax.broadcasted_iota(jnp.int32, (1, A_SLAB), 1) // HEAD_DIM
    head_mask = [(head_of_lane == hh).astype(BF16) for hh in range(A_HEADS)]
    first_head_lanes = lax.broadcasted_iota(jnp.int32, (1, LANES), 1) < HEAD_DIM
    pack_row = lax.broadcasted_iota(jnp.int32, (BF16_ROWS, A_SLAB), 0)
    nblk = rows // BLOCK

    def drop_first_row(t):
        top = jnp.where(pack_row == 0, jnp.zeros((BF16_ROWS, A_SLAB), BF16), t[0:BF16_ROWS])
        return jnp.concatenate([top, t[BF16_ROWS:]], axis=0)

    for w in range(slabs):
        cs = slice(w * A_SLAB, (w + 1) * A_SLAB)
        biases = {False: [bias] * A_HEADS, True: [bias_first] * A_HEADS}
        if has_sink:
            biases = {first: [jnp.where(col == 0, sink_ref[w * A_HEADS + hh] * LOG2E, b[hh])
                              for hh in range(A_HEADS)] for first, b in biases.items()}
        for jb in range(nblk):
            q = q_ref[jb * BLOCK:(jb + 1) * BLOCK, cs]
            if jb == 0:
                k2 = jnp.concatenate([kp_ref[:, cs], kc_ref[0:BLOCK, cs]], axis=0)
                v2 = jnp.concatenate([vp_ref[:, cs], vc_ref[0:BLOCK, cs]], axis=0)
            else:
                k2 = kc_ref[(jb - 1) * BLOCK:(jb + 1) * BLOCK, cs]
                v2 = vc_ref[(jb - 1) * BLOCK:(jb + 1) * BLOCK, cs]
            if has_sink:
                k2 = drop_first_row(k2)
                v2 = drop_first_row(v2)
            qs = jnp.concatenate([q * head_mask[hh] for hh in range(A_HEADS)], axis=0)
            s = lax.dot_general(qs, k2, (((1,), (1,)), ((), ())), preferred_element_type=F32)
            ps, ms, ls = [], [], []
            for hh in range(A_HEADS):
                sh = s[hh * BLOCK:(hh + 1) * BLOCK] + biases[jb == 0][hh]
                m = jnp.max(sh, axis=-1, keepdims=True)
                p = jnp.exp2(sh - m)
                ms.append(m)
                ls.append(jnp.sum(p, axis=-1, keepdims=True))
                ps.append(p.astype(BF16))
            ost = jnp.dot(jnp.concatenate(ps, axis=0), v2, preferred_element_type=F32)
            o_tiles, lse_tiles = [], []
            for t0 in range(0, A_HEADS, HEADS_PER_TILE):
                lanes = slice(t0 * HEAD_DIM, (t0 + HEADS_PER_TILE) * HEAD_DIM)
                per_head = [ost[hh * BLOCK:(hh + 1) * BLOCK, lanes] * (1.0 / ls[hh])
                            for hh in range(t0, t0 + HEADS_PER_TILE)]
                o_tiles.append(jnp.where(first_head_lanes, per_head[0], per_head[1]))
                if with_lse:
                    lse_tiles.append(jnp.where(first_head_lanes, ms[t0] + jnp.log2(ls[t0]),
                                               ms[t0 + 1] + jnp.log2(ls[t0 + 1])))
            o_ref[jb * BLOCK:(jb + 1) * BLOCK, cs] = jnp.concatenate(o_tiles, axis=1).astype(o_ref.dtype)
            if with_lse:
                lse_ref[jb * BLOCK:(jb + 1) * BLOCK, cs] = jnp.concatenate(lse_tiles, axis=1)


def _band_attn(q, k, v, *, name, rows, slabs, prev_off, sinks=None, with_lse=True):
    b, length, width = q.shape
    assert sinks is None or prev_off >= 1
    n_col = width // (slabs * A_SLAB)
    n_row = length // rows
    blk_per_step = rows // BLOCK
    cur = pl.BlockSpec((None, rows, slabs * A_SLAB), lambda i, r, j: (i, j, r))
    prev = pl.BlockSpec((None, BLOCK, slabs * A_SLAB),
                        lambda i, r, j: (i, jnp.maximum(j * blk_per_step - 1, 0), r))
    in_specs = [cur, cur, prev, cur, prev]
    args = [q, k, k, v, v]
    if sinks is not None:
        in_specs.append(pl.BlockSpec(memory_space=pltpu.SMEM))
        args.append(sinks)
    out_specs = [cur]
    out_shape = [jax.ShapeDtypeStruct((b, length, width), BF16)]
    if with_lse:
        out_specs.append(cur)
        out_shape.append(jax.ShapeDtypeStruct((b, length, width), F32))
    return pl.pallas_call(
        functools.partial(_band_attn_kernel, rows=rows, slabs=slabs, prev_off=prev_off,
                          has_sink=sinks is not None, with_lse=with_lse),
        grid=(b, n_col, n_row),
        in_specs=in_specs,
        out_specs=out_specs,
        out_shape=out_shape,
        compiler_params=pltpu.CompilerParams(dimension_semantics=("arbitrary",) * 3,
                                             vmem_limit_bytes=VMEM_LIMIT),
        name=name,
    )(*args)


def _mix_kernel(x_ref, gain_ref, oa0, la0, oa1, la1, oa2, la2, ob_ref, mq_ref, mk_ref, mv_ref,
                wg_ref, bg_ref, woa_ref, wob_ref, wom_ref, wout_ref, out_ref, *stage_refs):
    x = x_ref[...]
    h = _rms_rows(x, gain_ref[...]).astype(BF16)
    stage_refs = list(stage_refs)

    def token_major(blk_ref, dil):
        if dil == 1:
            return blk_ref[...].astype(F32)
        stage_ref = stage_refs.pop()
        rows = blk_ref.shape[0]
        for r in range(dil):
            for s in range(A_SLAB // LANES):
                c0 = r * A_SLAB + s * LANES
                stage_ref[s, pl.ds(r, rows, stride=dil), :] = blk_ref[:, c0:c0 + LANES].astype(F32)
        return jnp.concatenate([stage_ref[s] for s in range(A_SLAB // LANES)], axis=1)

    head_cols = [slice(hh * M_HEAD_DIM, (hh + 1) * M_HEAD_DIM) for hh in range(M_HEADS)]
    scores = [lax.dot_general(mq_ref[:, cs], mk_ref[:, cs], (((1,), (1,)), ((), ())),
                              preferred_element_type=F32) for cs in head_cols]
    gate_cols = [slice(i * D_MODEL, (i + 1) * D_MODEL) for i in range(N_BRANCH)]
    gates = []
    for cs in gate_cols:
        z = jnp.dot(h, wg_ref[:, cs], preferred_element_type=F32) + bg_ref[:, cs]
        gates.append(0.5 + 0.5 * jnp.tanh(0.5 * z))

    dils = [dil for _, dil in A_GROUPS]
    l0, l1, l2 = (token_major(r, d) for r, d in zip((la0, la1, la2), dils))
    o0, o1, o2 = (token_major(r, d) for r, d in zip((oa0, oa1, oa2), dils))
    mx = jnp.maximum(jnp.maximum(l0, l1), l2)
    e0, e1, e2 = jnp.exp2(l0 - mx), jnp.exp2(l1 - mx), jnp.exp2(l2 - mx)
    o_a = (e0 * o0 + e1 * o1 + e2 * o2) / (e0 + e1 + e2)

    o_m = []
    for s, cs in zip(scores, head_cols):
        p = jnp.exp2(s - jnp.max(s, axis=-1, keepdims=True))
        l = jnp.sum(p, axis=-1, keepdims=True)
        o_m.append(jnp.dot(p.astype(BF16), mv_ref[:, cs], preferred_element_type=F32) / l)
    o_m = jnp.concatenate(o_m, axis=1)

    merged = jnp.zeros(x.shape, F32)
    for gate, o, w_ref in zip(gates, (o_a, ob_ref[...], o_m), (woa_ref, wob_ref, wom_ref)):
        merged = merged + gate * jnp.dot(o.astype(BF16), w_ref[...], preferred_element_type=F32)
    out_ref[...] = x + jnp.dot(merged.astype(BF16), wout_ref[...], preferred_element_type=F32)


def _mix(x2, attn_gain, oa, la, ob, mq, mk, mv, w_gate, b_gate, w_o_a, w_o_b, w_o_m, w_out, seq):
    t = x2.shape[0]
    tb = TOKEN_BLOCK
    mem_len = mk.shape[1]
    per_seq = seq // tb
    tok = lambda w, d=1: pl.BlockSpec((tb // d, w * d), lambda i: (i, 0))
    mem = pl.BlockSpec((None, mem_len, M_Q_COLS), lambda i: (i // per_seq, 0, 0))
    dils = [dil for _, dil in A_GROUPS]
    return pl.pallas_call(
        _mix_kernel,
        grid=(t // tb,),
        in_specs=[tok(D_MODEL), _const_spec((1, D_MODEL))]
                 + [tok(A_SLAB, d) for d in dils for _ in range(2)]
                 + [tok(B_Q_COLS), tok(M_Q_COLS), mem, mem,
                  _const_spec((D_MODEL, N_BRANCH * D_MODEL)), _const_spec((1, N_BRANCH * D_MODEL)),
                  _const_spec((A_SLAB, D_MODEL)), _const_spec((B_Q_COLS, D_MODEL)),
                  _const_spec((M_Q_COLS, D_MODEL)), _const_spec((D_MODEL, D_MODEL))],
        out_specs=tok(D_MODEL),
        out_shape=jax.ShapeDtypeStruct((t, D_MODEL), F32),
        scratch_shapes=[pltpu.VMEM((A_SLAB // LANES, tb, LANES), F32)] * (2 * sum(d > 1 for d in dils)),
        compiler_params=pltpu.CompilerParams(dimension_semantics=("arbitrary",),
                                             vmem_limit_bytes=VMEM_LIMIT),
        name="mix",
    )(x2, attn_gain, oa[0], la[0], oa[1], la[1], oa[2], la[2], ob, mq, mk, mv,
      w_gate, b_gate, w_o_a, w_o_b, w_o_m, w_out)


def _conv_ffn_kernel(x_ref, gain_ref, wup_ref, cw_ref, cb_ref, wdown_ref, out_ref,
                     carry_ref, ext_ref, act_ref, *, per_seq):
    tb = x_ref.shape[0]
    halo = CONV_WIDTH - 1
    pad = 8

    @pl.when(pl.program_id(0) % per_seq == 0)
    def _():
        carry_ref[...] = jnp.zeros(carry_ref.shape, F32)

    x = x_ref[...]
    h = _rms_rows(x, gain_ref[...]).astype(BF16)

    def conv(c0):
        cs = slice(c0, c0 + FF_CHUNK)
        u = jnp.dot(h, wup_ref[:, cs], preferred_element_type=F32)
        ext_ref[0:pad, :] = carry_ref[:, cs]
        ext_ref[pad:pad + tb, :] = u
        carry_ref[:, cs] = u[tb - pad:tb, :]
        c = cb_ref[:, cs] + cw_ref[CONV_WIDTH - 1:CONV_WIDTH, cs] * u
        for j in range(halo):
            c = c + cw_ref[j:j + 1, cs] * ext_ref[pad - halo + j:pad - halo + j + tb, :]
        return c

    for ch in range(D_FF // FF_CHUNK):
        a = conv(ch * FF_CHUNK)
        g = conv(D_FF + ch * FF_CHUNK)
        half = 0.5 * a
        act = (half + half * jnp.tanh(half)) * g
        act_ref[:, ch * FF_CHUNK:(ch + 1) * FF_CHUNK] = act.astype(BF16)
    out_ref[...] = x + jnp.dot(act_ref[...], wdown_ref[...], preferred_element_type=F32)


def _conv_ffn(x2, gain, w_up, conv_w, conv_b, w_down, seq):
    t = x2.shape[0]
    tb = TOKEN_BLOCK
    return pl.pallas_call(
        functools.partial(_conv_ffn_kernel, per_seq=seq // tb),
        grid=(t // tb,),
        in_specs=[pl.BlockSpec((tb, D_MODEL), lambda i: (i, 0)),
                  _const_spec((1, D_MODEL)),
                  _const_spec((D_MODEL, 2 * D_FF)),
                  _const_spec((CONV_WIDTH, 2 * D_FF)),
                  _const_spec((1, 2 * D_FF)),
                  _const_spec((D_FF, D_MODEL))],
        out_specs=pl.BlockSpec((tb, D_MODEL), lambda i: (i, 0)),
        out_shape=jax.ShapeDtypeStruct((t, D_MODEL), F32),
        scratch_shapes=[pltpu.VMEM((8, 2 * D_FF), F32),
                        pltpu.VMEM((tb + 8, FF_CHUNK), F32),
                        pltpu.VMEM((tb, D_FF), BF16)],
        compiler_params=pltpu.CompilerParams(dimension_semantics=("arbitrary",),
                                             vmem_limit_bytes=VMEM_LIMIT),
        name="conv_ffn",
    )(x2, gain, w_up, conv_w, conv_b, w_down)


def _tables():
    lane = np.arange(A_SLAB)
    bd64 = jnp.asarray(lane[:, None] // HEAD_DIM == lane[None, :] // HEAD_DIM, BF16)
    bd128 = jnp.asarray(lane[:, None] // M_HEAD_DIM == lane[None, :] // M_HEAD_DIM, BF16)
    src = np.arange(B_KV_COLS)
    dst = np.arange(B_Q_COLS)
    expand = jnp.asarray(src[:, None] == (dst[None, :] // A_SLAB) * HEAD_DIM + dst[None, :] % HEAD_DIM, BF16)
    freq_col = jnp.exp(jnp.arange(ROPE_HALF, dtype=F32) * (-2.0 * math.log(ROPE_THETA) / ROPE_DIM))
    freq_col = freq_col.reshape(ROPE_HALF, 1)
    in_head = np.arange(LANES) % HEAD_DIM
    j = np.arange(LANES)[:, None]
    rot = in_head[None, :] < ROPE_DIM
    cos_sel = rot & (j == in_head[None, :] % ROPE_HALF)
    sin_sel = rot & (j == ROPE_HALF + in_head[None, :] % ROPE_HALF)
    lo = in_head[None, :] < ROPE_HALF
    spread = np.concatenate([cos_sel.astype(np.float32), -(sin_sel & lo).astype(np.float32),
                             (sin_sel & ~lo).astype(np.float32)], axis=1)
    one_row = (in_head >= ROPE_DIM).astype(np.float32).reshape(1, LANES)
    return bd64, bd128, expand, freq_col, jnp.asarray(spread, BF16), jnp.asarray(one_row)


def _layer(x2, mem, pos_rows, batch, seq, p):
    bd64, bd128, expand, freq_col, spread, one_row = _tables()
    qk_scale = HEAD_DIM ** -0.5 * LOG2E
    ones = jnp.ones((A_SLAB,), F32)
    colgain = jnp.concatenate(
        [jnp.concatenate([jnp.tile(p["a_q_norm"][g], A_HEADS) * qk_scale,
                          jnp.tile(p["a_k_norm"][g], A_HEADS), ones]) for g in range(len(A_GROUPS))]
        + [jnp.tile(p["b_q_norm"], B_Q_HEADS) * qk_scale,
           jnp.tile(p["b_k_norm"], B_KV_HEADS), jnp.ones((B_KV_COLS,), F32),
           jnp.tile(p["m_q_norm"], M_HEADS) * (M_HEAD_DIM ** -0.5 * LOG2E)]).reshape(1, IN_COLS)

    attn_gain = p["attn_norm"].reshape(1, D_MODEL)
    later_weights = ("w_mem_kv", "w_gate", "w_o_a", "w_o_b", "w_o_m", "w_out", "w_up", "w_down")
    outs = _in_proj(x2, pos_rows, attn_gain, p["w_in"].astype(BF16), colgain, freq_col, spread, one_row,
                    bd64, bd128, expand, [p[name] for name in later_weights])
    (qa0, ka0, va0, qa1, ka1, va1, qa2, ka2, va2, qb, kb, vb, mq) = outs[:13]
    wb = dict(zip(later_weights, outs[13:]))
    mk, mv = _mem_kv(mem, p["mem_norm"].reshape(1, D_MODEL), wb["w_mem_kv"],
                     p["m_k_norm"].reshape(1, M_HEAD_DIM))

    def per_batch(t):
        return t.reshape(batch, t.shape[0] // batch, t.shape[-1])

    def flat(t):
        return t.reshape(t.shape[0] * t.shape[1], t.shape[2])

    oa, la = [], []
    cfg = {1: dict(rows=2048, slabs=1), 4: dict(rows=seq // 4, slabs=2), 16: dict(rows=seq // 16, slabs=8)}
    for (window, dil), (q, k, v) in zip(A_GROUPS, ((qa0, ka0, va0), (qa1, ka1, va1), (qa2, ka2, va2))):
        assert window // dil == BLOCK
        o, lse = _band_attn(per_batch(q), per_batch(k), per_batch(v), name=f"band_attn_d{dil}",
                            prev_off=0, **cfg[dil])
        oa.append(flat(o))
        la.append(flat(lse))
    (ob,) = _band_attn(per_batch(qb), per_batch(kb), per_batch(vb), name="band_attn_swa", rows=1024,
                       slabs=B_KV_HEADS, prev_off=BLOCK - (B_WINDOW - 1), sinks=p["b_sinks"],
                       with_lse=False)
    ob = flat(ob)

    x2 = _mix(x2, attn_gain, oa, la, ob, mq, mk, mv, wb["w_gate"],
              p["b_gate"].reshape(1, N_BRANCH * D_MODEL), wb["w_o_a"], wb["w_o_b"], wb["w_o_m"],
              wb["w_out"], seq)
    return _conv_ffn(x2, p["ffn_norm"].reshape(1, D_MODEL), wb["w_up"], p["conv_w"],
                     p["conv_b"].reshape(1, 2 * D_FF), wb["w_down"], seq)


def kernel(x, mem, positions, attn_norm, w_in, a_q_norm, a_k_norm, b_q_norm, b_k_norm, b_sinks,
           mem_norm, w_mem_kv, m_q_norm, m_k_norm, w_o_a, w_o_b, w_o_m, w_gate, b_gate, w_out,
           ffn_norm, w_up, conv_w, conv_b, w_down):
    batch, seq, _ = x.shape
    params = dict(attn_norm=attn_norm, w_in=w_in, a_q_norm=a_q_norm, a_k_norm=a_k_norm,
                  b_q_norm=b_q_norm, b_k_norm=b_k_norm, b_sinks=b_sinks, mem_norm=mem_norm,
                  w_mem_kv=w_mem_kv, m_q_norm=m_q_norm, m_k_norm=m_k_norm, w_o_a=w_o_a, w_o_b=w_o_b,
                  w_o_m=w_o_m, w_gate=w_gate, b_gate=b_gate, w_out=w_out, ffn_norm=ffn_norm,
                  w_up=w_up, conv_w=conv_w, conv_b=conv_b, w_down=w_down)
    pos_rows = positions.astype(F32).reshape(batch * seq // TOKEN_BLOCK, 1, TOKEN_BLOCK)
    x2 = x.reshape(batch * seq, D_MODEL)
    for layer in range(attn_norm.shape[0]):
        x2 = _layer(x2, mem, pos_rows, batch, seq, {k: v[layer] for k, v in params.items()})
    return x2.reshape(batch, seq, D_MODEL)
```

```python
import functools
import math

import numpy as np

import jax
import jax.numpy as jnp
from jax import lax
from jax.experimental import pallas as pl
from jax.experimental.pallas import tpu as pltpu

D_MODEL = 1024
HEAD_DIM = 64
A_GROUPS = ((128, 1), (512, 4), (2048, 16))
A_HEADS = 4
A_SLAB = A_HEADS * HEAD_DIM
A_QKV_COLS = len(A_GROUPS) * 3 * A_SLAB
B_Q_HEADS = 8
B_KV_HEADS = 2
B_GROUP = B_Q_HEADS // B_KV_HEADS
B_WINDOW = 128
B_Q_COLS = B_Q_HEADS * HEAD_DIM
B_KV_COLS = B_KV_HEADS * HEAD_DIM
M_HEADS = 4
M_HEAD_DIM = 128
M_Q_COLS = M_HEADS * M_HEAD_DIM
B_Q_OFF = A_QKV_COLS
B_K_OFF = B_Q_OFF + B_Q_COLS
B_V_OFF = B_K_OFF + B_KV_COLS
M_Q_OFF = B_V_OFF + B_KV_COLS
IN_COLS = M_Q_OFF + M_Q_COLS
N_NORMED_TILES = 2 * len(A_GROUPS) + B_Q_COLS // A_SLAB + 1 + M_Q_COLS // A_SLAB
N_BRANCH = 3
D_FF = 2816
CONV_WIDTH = 3
ROPE_THETA = 500000.0
ROPE_DIM = HEAD_DIM // 4
ROPE_HALF = ROPE_DIM // 2
BLOCK = 128
EPS = 1e-6
NEG = -1e30
LOG2E = math.log2(math.e)

LANES = 128
BF16_ROWS = 16
HEADS_PER_TILE = LANES // HEAD_DIM
TOKEN_BLOCK = 512
ROW_CHUNK = 64
FF_CHUNK = 256
VMEM_LIMIT = 56 * 1024 * 1024

F32 = jnp.float32
BF16 = jnp.bfloat16


def _const_spec(shape):
    return pl.BlockSpec(shape, lambda *_: (0,) * len(shape), pipeline_mode=pl.Buffered(1))


def _rms_rows(x, gain):
    ms = jnp.mean(x * x, axis=-1, keepdims=True)
    return x * lax.rsqrt(ms + EPS) * gain


def _mem_kv_kernel(mem_ref, gain_ref, w_ref, kgain_ref, mk_ref, mv_ref):
    hm = _rms_rows(mem_ref[...], gain_ref[...]).astype(BF16)
    kv = jnp.dot(hm, w_ref[...], preferred_element_type=F32)
    ks = []
    for h in range(M_HEADS):
        kh = kv[:, h * M_HEAD_DIM:(h + 1) * M_HEAD_DIM]
        ks.append(_rms_rows(kh, kgain_ref[...]))
    mk_ref[...] = jnp.concatenate(ks, axis=1).astype(BF16)
    mv_ref[...] = kv[:, M_Q_COLS:].astype(BF16)


def _mem_kv(mem, mem_gain, w_kv, k_gain):
    b, m, _ = mem.shape
    return pl.pallas_call(
        _mem_kv_kernel,
        grid=(b,),
        in_specs=[pl.BlockSpec((None, m, D_MODEL), lambda i: (i, 0, 0)),
                  _const_spec((1, D_MODEL)),
                  _const_spec((D_MODEL, 2 * M_Q_COLS)),
                  _const_spec((1, M_HEAD_DIM))],
        out_specs=[pl.BlockSpec((None, m, M_Q_COLS), lambda i: (i, 0, 0)),
                   pl.BlockSpec((None, m, M_Q_COLS), lambda i: (i, 0, 0))],
        out_shape=[jax.ShapeDtypeStruct((b, m, M_Q_COLS), BF16)] * 2,
        compiler_params=pltpu.CompilerParams(dimension_semantics=("arbitrary",),
                                             vmem_limit_bytes=VMEM_LIMIT),
        name="mem_kv",
    )(mem, mem_gain, w_kv, k_gain)


def _in_proj_kernel(x_ref, pos_ref, gain_ref, w_ref, colgain_ref, freq_ref, spread_ref, one_ref,
                    bd64_ref, bd128_ref, *refs, n_cast):
    cast_in, refs = refs[:n_cast], refs[n_cast:]
    (qa0, ka0, va0, qa1, ka1, va1, qa2, ka2, va2, qb_ref, kb_ref, vb_ref, mq_ref) = refs[:13]
    cast_out, refs = refs[13:13 + n_cast], refs[13 + n_cast:]
    proj_ref, ss_ref, h_ref, cos_ref, sin_lo_ref, sin_hi_ref = refs[:6]
    stage_refs = list(refs[6:])
    tb = x_ref.shape[0]

    for src, dst in zip(cast_in, cast_out):
        dst[...] = src[...].astype(BF16)
    chunks = [slice(r, r + ROW_CHUNK) for r in range(0, tb, ROW_CHUNK)]

    @pl.when(pl.program_id(0) == 0)
    def _():
        proj_ref[...] = jnp.zeros(proj_ref.shape, F32)
        ss_ref[...] = jnp.zeros(ss_ref.shape, F32)

    for rows in chunks:
        h_ref[rows, :] = _rms_rows(x_ref[rows, :], gain_ref[...]).astype(BF16)

    ang = freq_ref[...] * pos_ref[...]
    trig = jnp.concatenate([jnp.cos(ang), jnp.sin(ang),
                            jnp.zeros((LANES - ROPE_DIM, tb), F32)], axis=0).T
    tables = jnp.zeros((tb, 3 * LANES), F32)
    for _ in range(3):
        part = trig.astype(BF16)
        tables = tables + jnp.dot(part, spread_ref[...], preferred_element_type=F32)
        trig = trig - part.astype(F32)
    cos_ref[...] = tables[:, 0:LANES] + one_ref[...]
    sin_lo_ref[...] = tables[:, LANES:2 * LANES]
    sin_hi_ref[...] = tables[:, 2 * LANES:3 * LANES]

    ss_slots = iter(range(ss_ref.shape[0]))
    pending = []

    def flush_pending():
        while pending:
            new, bd, slot = pending.pop()
            ss_ref[slot, :, 0:new.shape[1]] = jnp.dot((new * new).astype(BF16), bd,
                                                      preferred_element_type=F32)

    def proj(c0, width, bd=None):
        tile, off = divmod(c0, A_SLAB)
        assert off + width <= A_SLAB and off % LANES == 0
        t = proj_ref[tile, :, off:off + width]
        ss = None
        if bd is not None:
            slot = next(ss_slots)
            ss = ss_ref[slot, :, 0:width]
        new = jnp.dot(h_ref[...], w_ref[:, c0:c0 + width], preferred_element_type=F32)
        proj_ref[tile, :, off:off + width] = new
        flush_pending()
        if bd is not None:
            pending.append((new, bd, slot))
        return t, ss

    def rope(y, rows):
        parts = []
        for c in range(y.shape[1] // LANES):
            yc = y[:, c * LANES:(c + 1) * LANES]
            parts.append(yc * cos_ref[rows, :]
                         + pltpu.roll(yc, LANES - ROPE_HALF, 1) * sin_lo_ref[rows, :]
                         + pltpu.roll(yc, ROPE_HALF, 1) * sin_hi_ref[rows, :])
        return parts[0] if len(parts) == 1 else jnp.concatenate(parts, axis=1)

    def finish(t_ss, emit, c0, *, dim=None, rotary=False):
        t, ss = t_ss
        if ss is not None:
            gain = colgain_ref[:, c0:c0 + t.shape[1]]
        for rows in chunks:
            y = t[rows]
            if ss is not None:
                y = y * lax.rsqrt(ss[rows] * (1.0 / dim) + EPS) * gain
            if rotary:
                y = rope(y, rows)
            emit(rows, y)

    def to_ref(out_ref, col0=0):
        def emit(rows, y):
            out_ref[rows, col0:col0 + y.shape[1]] = y.astype(BF16)
        return emit

    def store_tile(out_ref, t, dil, c0, **kw):
        if dil == 1:
            finish(t, to_ref(out_ref), c0, **kw)
            return
        stage_ref = stage_refs.pop()

        def emit(rows, y):
            for s in range(A_SLAB // LANES):
                stage_ref[s, rows, :] = y[:, s * LANES:(s + 1) * LANES]
        finish(t, emit, c0, **kw)
        for r in range(dil):
            for s in range(A_SLAB // LANES):
                col = r * A_SLAB + s * LANES
                out_ref[:, col:col + LANES] = stage_ref[s, pl.ds(r, tb // dil, stride=dil), :].astype(BF16)

    bd64 = bd64_ref[...]
    qk = dict(dim=HEAD_DIM, rotary=True)
    for g, (q_ref, k_ref, v_ref) in enumerate(((qa0, ka0, va0), (qa1, ka1, va1), (qa2, ka2, va2))):
        c0 = g * 3 * A_SLAB
        dil = A_GROUPS[g][1]
        store_tile(q_ref, proj(c0, A_SLAB, bd64), dil, c0, **qk)
        store_tile(k_ref, proj(c0 + A_SLAB, A_SLAB, bd64), dil, c0 + A_SLAB, **qk)
        store_tile(v_ref, proj(c0 + 2 * A_SLAB, A_SLAB), dil, c0 + 2 * A_SLAB)

    for s in range(B_Q_COLS // A_SLAB):
        c0 = B_Q_OFF + s * A_SLAB
        finish(proj(c0, A_SLAB, bd64), to_ref(qb_ref, s * A_SLAB), c0, **qk)

    def under_query_heads(out_ref):
        def emit(rows, y):
            swapped = pltpu.roll(y, HEAD_DIM, 1)
            first = jnp.where(first_head_lanes, y, swapped).astype(BF16)
            second = jnp.where(first_head_lanes, swapped, y).astype(BF16)
            reps = B_GROUP // HEADS_PER_TILE
            out_ref[rows, :] = jnp.concatenate([first] * reps + [second] * reps, axis=1)
        return emit
    first_head_lanes = lax.broadcasted_iota(jnp.int32, (1, LANES), 1) < HEAD_DIM
    finish(proj(B_K_OFF, B_KV_COLS, bd64_ref[0:B_KV_COLS, 0:B_KV_COLS]), under_query_heads(kb_ref),
           B_K_OFF, **qk)
    finish(proj(B_V_OFF, B_KV_COLS), under_query_heads(vb_ref), B_V_OFF)

    bd128 = bd128_ref[...]
    for s in range(M_Q_COLS // A_SLAB):
        c0 = M_Q_OFF + s * A_SLAB
        finish(proj(c0, A_SLAB, bd128), to_ref(mq_ref, s * A_SLAB), c0, dim=M_HEAD_DIM)
    flush_pending()


def _cast_rows(rows, n_steps):
    for per_step in range(BF16_ROWS, rows + 1, BF16_ROWS):
        if rows % per_step == 0 and rows // per_step <= n_steps:
            return per_step
    raise ValueError(f"no bf16-aligned split of {rows} rows over {n_steps} steps")


def _in_proj(x2, pos_rows, attn_gain, w_in, colgain, freq_col, spread, one_row, bd64, bd128,
             cast_weights):
    t = x2.shape[0]
    tb = TOKEN_BLOCK
    dils = [dil for _, dil in A_GROUPS for _ in range(3)] + [1, 1, 1, 1]
    widths = [A_SLAB] * 9 + [B_Q_COLS, B_Q_COLS, B_Q_COLS, M_Q_COLS]
    n_staged = sum(d > 1 for d in dils)
    n = t // tb
    done = lambda i: (jnp.maximum(i - 1, 0), 0)

    def cast_spec(w):
        per_step = _cast_rows(w.shape[0], n + 1)
        last = w.shape[0] // per_step - 1
        return pl.BlockSpec((per_step, w.shape[1]), lambda i: (jnp.minimum(i, last), 0))

    cast_specs = [cast_spec(w) for w in cast_weights]
    return pl.pallas_call(
        functools.partial(_in_proj_kernel, n_cast=len(cast_weights)),
        grid=(n + 1,),
        in_specs=[pl.BlockSpec((tb, D_MODEL), lambda i: (jnp.minimum(i, n - 1), 0)),
                  pl.BlockSpec((None, 1, tb), lambda i: (jnp.maximum(i - 1, 0), 0, 0)),
                  _const_spec((1, D_MODEL)),
                  _const_spec((D_MODEL, IN_COLS)),
                  _const_spec((1, IN_COLS)),
                  _const_spec((ROPE_HALF, 1)),
                  _const_spec((LANES, 3 * LANES)),
                  _const_spec((1, LANES)),
                  _const_spec((A_SLAB, A_SLAB)),
                  _const_spec((A_SLAB, A_SLAB))] + cast_specs,
        out_specs=[pl.BlockSpec((tb // d, w * d), done) for w, d in zip(widths, dils)] + cast_specs,
        out_shape=[jax.ShapeDtypeStruct((t // d, w * d), BF16) for w, d in zip(widths, dils)]
                  + [jax.ShapeDtypeStruct(w.shape, BF16) for w in cast_weights],
        scratch_shapes=[pltpu.VMEM((IN_COLS // A_SLAB, tb, A_SLAB), F32),
                        pltpu.VMEM((N_NORMED_TILES, tb, A_SLAB), F32)]
                       + [pltpu.VMEM((tb, D_MODEL), BF16)] + [pltpu.VMEM((tb, LANES), F32)] * 3
                       + [pltpu.VMEM((A_SLAB // LANES, tb, LANES), F32)] * n_staged,
        compiler_params=pltpu.CompilerParams(dimension_semantics=("arbitrary",),
                                             vmem_limit_bytes=VMEM_LIMIT),
        name="in_proj",
    )(x2, pos_rows, attn_gain, w_in, colgain, freq_col, spread, one_row, bd64, bd128, *cast_weights)


def _band_attn_kernel(*refs, rows, slabs, prev_off, has_sink, with_lse):
    q_ref, kc_ref, kp_ref, vc_ref, vp_ref = refs[:5]
    pos = 5
    sink_ref = None
    if has_sink:
        sink_ref = refs[pos]
        pos += 1
    o_ref = refs[pos]
    lse_ref = refs[pos + 1] if with_lse else None

    first_step = pl.program_id(2) == 0
    row = lax.broadcasted_iota(jnp.int32, (BLOCK, 2 * BLOCK), 0)
    col = lax.broadcasted_iota(jnp.int32, (BLOCK, 2 * BLOCK), 1)
    bias = jnp.where(col < BLOCK,
                     jnp.where(col >= row + prev_off, 0.0, NEG),
                     jnp.where(col - BLOCK <= row, 0.0, NEG)).astype(F32)
    bias_first = jnp.where(jnp.logical_and(first_step, col < BLOCK), NEG, bias)
    head_of_lane = lax.broadcasted_iota(jnp.int32, (1, A_SLAB), 1) // HEAD_DIM
    head_mask = [(head_of_lane == hh).astype(BF16) for hh in range(A_HEADS)]
    first_head_lanes = lax.broadcasted_iota(jnp.int32, (1, LANES), 1) < HEAD_DIM
    pack_row = lax.broadcasted_iota(jnp.int32, (BF16_ROWS, A_SLAB), 0)
    nblk = rows // BLOCK

    def drop_first_row(t):
        top = jnp.where(pack_row == 0, jnp.zeros((BF16_ROWS, A_SLAB), BF16), t[0:BF16_ROWS])
        return jnp.concatenate([top, t[BF16_ROWS:]], axis=0)

    for w in range(slabs):
        cs = slice(w * A_SLAB, (w + 1) * A_SLAB)
        biases = {False: [bias] * A_HEADS, True: [bias_first] * A_HEADS}
        if has_sink:
            biases = {first: [jnp.where(col == 0, sink_ref[w * A_HEADS + hh] * LOG2E, b[hh])
                              for hh in range(A_HEADS)] for first, b in biases.items()}
        for jb in range(nblk):
            q = q_ref[jb * BLOCK:(jb + 1) * BLOCK, cs]
            if jb == 0:
                k2 = jnp.concatenate([kp_ref[:, cs], kc_ref[0:BLOCK, cs]], axis=0)
                v2 = jnp.concatenate([vp_ref[:, cs], vc_ref[0:BLOCK, cs]], axis=0)
            else:
                k2 = kc_ref[(jb - 1) * BLOCK:(jb + 1) * BLOCK, cs]
                v2 = vc_ref[(jb - 1) * BLOCK:(jb + 1) * BLOCK, cs]
            if has_sink:
                k2 = drop_first_row(k2)
                v2 = drop_first_row(v2)
            qs = jnp.concatenate([q * head_mask[hh] for hh in range(A_HEADS)], axis=0)
            s = lax.dot_general(qs, k2, (((1,), (1,)), ((), ())), preferred_element_type=F32)
            ps, ms, ls = [], [], []
            for hh in range(A_HEADS):
                sh = s[hh * BLOCK:(hh + 1) * BLOCK] + biases[jb == 0][hh]
                m = jnp.max(sh, axis=-1, keepdims=True)
                p = jnp.exp2(sh - m)
                ms.append(m)
                ls.append(jnp.sum(p, axis=-1, keepdims=True))
                ps.append(p.astype(BF16))
            ost = jnp.dot(jnp.concatenate(ps, axis=0), v2, preferred_element_type=F32)
            o_tiles, lse_tiles = [], []
            for t0 in range(0, A_HEADS, HEADS_PER_TILE):
                lanes = slice(t0 * HEAD_DIM, (t0 + HEADS_PER_TILE) * HEAD_DIM)
                per_head = [ost[hh * BLOCK:(hh + 1) * BLOCK, lanes] * (1.0 / ls[hh])
                            for hh in range(t0, t0 + HEADS_PER_TILE)]
                o_tiles.append(jnp.where(first_head_lanes, per_head[0], per_head[1]))
                if with_lse:
                    lse_tiles.append(jnp.where(first_head_lanes, ms[t0] + jnp.log2(ls[t0]),
                                               ms[t0 + 1] + jnp.log2(ls[t0 + 1])))
            o_ref[jb * BLOCK:(jb + 1) * BLOCK, cs] = jnp.concatenate(o_tiles, axis=1).astype(o_ref.dtype)
            if with_lse:
                lse_ref[jb * BLOCK:(jb + 1) * BLOCK, cs] = jnp.concatenate(lse_tiles, axis=1)


def _band_attn(q, k, v, *, name, rows, slabs, prev_off, sinks=None, with_lse=True):
    b, length, width = q.shape
    assert sinks is None or prev_off >= 1
    n_col = width // (slabs * A_SLAB)
    n_row = length // rows
    blk_per_step = rows // BLOCK
    cur = pl.BlockSpec((None, rows, slabs * A_SLAB), lambda i, r, j: (i, j, r))
    prev = pl.BlockSpec((None, BLOCK, slabs * A_SLAB),
                        lambda i, r, j: (i, jnp.maximum(j * blk_per_step - 1, 0), r))
    in_specs = [cur, cur, prev, cur, prev]
    args = [q, k, k, v, v]
    if sinks is not None:
        in_specs.append(pl.BlockSpec(memory_space=pltpu.SMEM))
        args.append(sinks)
    out_specs = [cur]
    out_shape = [jax.ShapeDtypeStruct((b, length, width), BF16)]
    if with_lse:
        out_specs.append(cur)
        out_shape.append(jax.ShapeDtypeStruct((b, length, width), F32))
    return pl.pallas_call(
        functools.partial(_band_attn_kernel, rows=rows, slabs=slabs, prev_off=prev_off,
                          has_sink=sinks is not None, with_lse=with_lse),
        grid=(b, n_col, n_row),
        in_specs=in_specs,
        out_specs=out_specs,
        out_shape=out_shape,
        compiler_params=pltpu.CompilerParams(dimension_semantics=("arbitrary",) * 3,
                                             vmem_limit_bytes=VMEM_LIMIT),
        name=name,
    )(*args)


def _mix_kernel(x_ref, gain_ref, oa0, la0, oa1, la1, oa2, la2, ob_ref, mq_ref, mk_ref, mv_ref,
                wg_ref, bg_ref, woa_ref, wob_ref, wom_ref, wout_ref, out_ref, *stage_refs):
    x = x_ref[...]
    h = _rms_rows(x, gain_ref[...]).astype(BF16)
    stage_refs = list(stage_refs)

    def token_major(blk_ref, dil):
        if dil == 1:
            return blk_ref[...].astype(F32)
        stage_ref = stage_refs.pop()
        rows = blk_ref.shape[0]
        for r in range(dil):
            for s in range(A_SLAB // LANES):
                c0 = r * A_SLAB + s * LANES
                stage_ref[s, pl.ds(r, rows, stride=dil), :] = blk_ref[:, c0:c0 + LANES].astype(F32)
        return jnp.concatenate([stage_ref[s] for s in range(A_SLAB // LANES)], axis=1)

    head_cols = [slice(hh * M_HEAD_DIM, (hh + 1) * M_HEAD_DIM) for hh in range(M_HEADS)]
    scores = [lax.dot_general(mq_ref[:, cs], mk_ref[:, cs], (((1,), (1,)), ((), ())),
                              preferred_element_type=F32) for cs in head_cols]
    gate_cols = [slice(i * D_MODEL, (i + 1) * D_MODEL) for i in range(N_BRANCH)]
    gates = []
    for cs in gate_cols:
        z = jnp.dot(h, wg_ref[:, cs], preferred_element_type=F32) + bg_ref[:, cs]
        gates.append(0.5 + 0.5 * jnp.tanh(0.5 * z))

    dils = [dil for _, dil in A_GROUPS]
    l0, l1, l2 = (token_major(r, d) for r, d in zip((la0, la1, la2), dils))
    o0, o1, o2 = (token_major(r, d) for r, d in zip((oa0, oa1, oa2), dils))
    mx = jnp.maximum(jnp.maximum(l0, l1), l2)
    e0, e1, e2 = jnp.exp2(l0 - mx), jnp.exp2(l1 - mx), jnp.exp2(l2 - mx)
    o_a = (e0 * o0 + e1 * o1 + e2 * o2) / (e0 + e1 + e2)

    o_m = []
    for s, cs in zip(scores, head_cols):
        p = jnp.exp2(s - jnp.max(s, axis=-1, keepdims=True))
        l = jnp.sum(p, axis=-1, keepdims=True)
        o_m.append(jnp.dot(p.astype(BF16), mv_ref[:, cs], preferred_element_type=F32) / l)
    o_m = jnp.concatenate(o_m, axis=1)

    merged = jnp.zeros(x.shape, F32)
    for gate, o, w_ref in zip(gates, (o_a, ob_ref[...], o_m), (woa_ref, wob_ref, wom_ref)):
        merged = merged + gate * jnp.dot(o.astype(BF16), w_ref[...], preferred_element_type=F32)
    out_ref[...] = x + jnp.dot(merged.astype(BF16), wout_ref[...], preferred_element_type=F32)


def _mix(x2, attn_gain, oa, la, ob, mq, mk, mv, w_gate, b_gate, w_o_a, w_o_b, w_o_m, w_out, seq):
    t = x2.shape[0]
    tb = TOKEN_BLOCK
    mem_len = mk.shape[1]
    per_seq = seq // tb
    tok = lambda w, d=1: pl.BlockSpec((tb // d, w * d), lambda i: (i, 0))
    mem = pl.BlockSpec((None, mem_len, M_Q_COLS), lambda i: (i // per_seq, 0, 0))
    dils = [dil for _, dil in A_GROUPS]
    return pl.pallas_call(
        _mix_kernel,
        grid=(t // tb,),
        in_specs=[tok(D_MODEL), _const_spec((1, D_MODEL))]
                 + [tok(A_SLAB, d) for d in dils for _ in range(2)]
                 + [tok(B_Q_COLS), tok(M_Q_COLS), mem, mem,
                  _const_spec((D_MODEL, N_BRANCH * D_MODEL)), _const_spec((1, N_BRANCH * D_MODEL)),
                  _const_spec((A_SLAB, D_MODEL)), _const_spec((B_Q_COLS, D_MODEL)),
                  _const_spec((M_Q_COLS, D_MODEL)), _const_spec((D_MODEL, D_MODEL))],
        out_specs=tok(D_MODEL),
        out_shape=jax.ShapeDtypeStruct((t, D_MODEL), F32),
        scratch_shapes=[pltpu.VMEM((A_SLAB // LANES, tb, LANES), F32)] * (2 * sum(d > 1 for d in dils)),
        compiler_params=pltpu.CompilerParams(dimension_semantics=("arbitrary",),
                                             vmem_limit_bytes=VMEM_LIMIT),
        name="mix",
    )(x2, attn_gain, oa[0], la[0], oa[1], la[1], oa[2], la[2], ob, mq, mk, mv,
      w_gate, b_gate, w_o_a, w_o_b, w_o_m, w_out)


def _conv_ffn_kernel(x_ref, gain_ref, wup_ref, cw_ref, cb_ref, wdown_ref, out_ref,
                     carry_ref, ext_ref, act_ref, *, per_seq):
    tb = x_ref.shape[0]
    halo = CONV_WIDTH - 1
    pad = 8

    @pl.when(pl.program_id(0) % per_seq == 0)
    def _():
        carry_ref[...] = jnp.zeros(carry_ref.shape, F32)

    x = x_ref[...]
    h = _rms_rows(x, gain_ref[...]).astype(BF16)

    def conv(c0):
        cs = slice(c0, c0 + FF_CHUNK)
        u = jnp.dot(h, wup_ref[:, cs], preferred_element_type=F32)
        ext_ref[0:pad, :] = carry_ref[:, cs]
        ext_ref[pad:pad + tb, :] = u
        carry_ref[:, cs] = u[tb - pad:tb, :]
        c = cb_ref[:, cs] + cw_ref[CONV_WIDTH - 1:CONV_WIDTH, cs] * u
        for j in range(halo):
            c = c + cw_ref[j:j + 1, cs] * ext_ref[pad - halo + j:pad - halo + j + tb, :]
        return c

    for ch in range(D_FF // FF_CHUNK):
        a = conv(ch * FF_CHUNK)
        g = conv(D_FF + ch * FF_CHUNK)
        half = 0.5 * a
        act = (half + half * jnp.tanh(half)) * g
        act_ref[:, ch * FF_CHUNK:(ch + 1) * FF_CHUNK] = act.astype(BF16)
    out_ref[...] = x + jnp.dot(act_ref[...], wdown_ref[...], preferred_element_type=F32)


def _conv_ffn(x2, gain, w_up, conv_w, conv_b, w_down, seq):
    t = x2.shape[0]
    tb = TOKEN_BLOCK
    return pl.pallas_call(
        functools.partial(_conv_ffn_kernel, per_seq=seq // tb),
        grid=(t // tb,),
        in_specs=[pl.BlockSpec((tb, D_MODEL), lambda i: (i, 0)),
                  _const_spec((1, D_MODEL)),
                  _const_spec((D_MODEL, 2 * D_FF)),
                  _const_spec((CONV_WIDTH, 2 * D_FF)),
                  _const_spec((1, 2 * D_FF)),
                  _const_spec((D_FF, D_MODEL))],
        out_specs=pl.BlockSpec((tb, D_MODEL), lambda i: (i, 0)),
        out_shape=jax.ShapeDtypeStruct((t, D_MODEL), F32),
        scratch_shapes=[pltpu.VMEM((8, 2 * D_FF), F32),
                        pltpu.VMEM((tb + 8, FF_CHUNK), F32),
                        pltpu.VMEM((tb, D_FF), BF16)],
        compiler_params=pltpu.CompilerParams(dimension_semantics=("arbitrary",),
                                             vmem_limit_bytes=VMEM_LIMIT),
        name="conv_ffn",
    )(x2, gain, w_up, conv_w, conv_b, w_down)


def _tables():
    lane = np.arange(A_SLAB)
    bd64 = jnp.asarray(lane[:, None] // HEAD_DIM == lane[None, :] // HEAD_DIM, BF16)
    bd128 = jnp.asarray(lane[:, None] // M_HEAD_DIM == lane[None, :] // M_HEAD_DIM, BF16)
    freq_col = jnp.exp(jnp.arange(ROPE_HALF, dtype=F32) * (-2.0 * math.log(ROPE_THETA) / ROPE_DIM))
    freq_col = freq_col.reshape(ROPE_HALF, 1)
    in_head = np.arange(LANES) % HEAD_DIM
    j = np.arange(LANES)[:, None]
    rot = in_head[None, :] < ROPE_DIM
    cos_sel = rot & (j == in_head[None, :] % ROPE_HALF)
    sin_sel = rot & (j == ROPE_HALF + in_head[None, :] % ROPE_HALF)
    lo = in_head[None, :] < ROPE_HALF
    spread = np.concatenate([cos_sel.astype(np.float32), -(sin_sel & lo).astype(np.float32),
                             (sin_sel & ~lo).astype(np.float32)], axis=1)
    one_row = (in_head >= ROPE_DIM).astype(np.float32).reshape(1, LANES)
    return bd64, bd128, freq_col, jnp.asarray(spread, BF16), jnp.asarray(one_row)


def _layer(x2, mem, pos_rows, batch, seq, p):
    bd64, bd128, freq_col, spread, one_row = _tables()
    qk_scale = HEAD_DIM ** -0.5 * LOG2E
    ones = jnp.ones((A_SLAB,), F32)
    colgain = jnp.concatenate(
        [jnp.concatenate([jnp.tile(p["a_q_norm"][g], A_HEADS) * qk_scale,
                          jnp.tile(p["a_k_norm"][g], A_HEADS), ones]) for g in range(len(A_GROUPS))]
        + [jnp.tile(p["b_q_norm"], B_Q_HEADS) * qk_scale,
           jnp.tile(p["b_k_norm"], B_KV_HEADS), jnp.ones((B_KV_COLS,), F32),
           jnp.tile(p["m_q_norm"], M_HEADS) * (M_HEAD_DIM ** -0.5 * LOG2E)]).reshape(1, IN_COLS)

    attn_gain = p["attn_norm"].reshape(1, D_MODEL)
    later_weights = ("w_mem_kv", "w_gate", "w_o_a", "w_o_b", "w_o_m", "w_out", "w_up", "w_down")
    outs = _in_proj(x2, pos_rows, attn_gain, p["w_in"].astype(BF16), colgain, freq_col, spread, one_row,
                    bd64, bd128, [p[name] for name in later_weights])
    (qa0, ka0, va0, qa1, ka1, va1, qa2, ka2, va2, qb, kb, vb, mq) = outs[:13]
    wb = dict(zip(later_weights, outs[13:]))
    mk, mv = _mem_kv(mem, p["mem_norm"].reshape(1, D_MODEL), wb["w_mem_kv"],
                     p["m_k_norm"].reshape(1, M_HEAD_DIM))

    def per_batch(t):
        return t.reshape(batch, t.shape[0] // batch, t.shape[-1])

    def flat(t):
        return t.reshape(t.shape[0] * t.shape[1], t.shape[2])

    oa, la = [], []
    cfg = {1: dict(rows=2048, slabs=1), 4: dict(rows=seq // 4, slabs=2), 16: dict(rows=seq // 16, slabs=8)}
    for (window, dil), (q, k, v) in zip(A_GROUPS, ((qa0, ka0, va0), (qa1, ka1, va1), (qa2, ka2, va2))):
        assert window // dil == BLOCK
        o, lse = _band_attn(per_batch(q), per_batch(k), per_batch(v), name=f"band_attn_d{dil}",
                            prev_off=0, **cfg[dil])
        oa.append(flat(o))
        la.append(flat(lse))
    (ob,) = _band_attn(per_batch(qb), per_batch(kb), per_batch(vb), name="band_attn_swa", rows=1024,
                       slabs=B_KV_HEADS, prev_off=BLOCK - (B_WINDOW - 1), sinks=p["b_sinks"],
                       with_lse=False)
    ob = flat(ob)

    x2 = _mix(x2, attn_gain, oa, la, ob, mq, mk, mv, wb["w_gate"],
              p["b_gate"].reshape(1, N_BRANCH * D_MODEL), wb["w_o_a"], wb["w_o_b"], wb["w_o_m"],
              wb["w_out"], seq)
    return _conv_ffn(x2, p["ffn_norm"].reshape(1, D_MODEL), wb["w_up"], p["conv_w"],
                     p["conv_b"].reshape(1, 2 * D_FF), wb["w_down"], seq)


def kernel(x, mem, positions, attn_norm, w_in, a_q_norm, a_k_norm, b_q_norm, b_k_norm, b_sinks,
           mem_norm, w_mem_kv, m_q_norm, m_k_norm, w_o_a, w_o_b, w_o_m, w_gate, b_gate, w_out,
           ffn_norm, w_up, conv_w, conv_b, w_down):
    batch, seq, _ = x.shape
    params = dict(attn_norm=attn_norm, w_in=w_in, a_q_norm=a_q_norm, a_k_norm=a_k_norm,
                  b_q_norm=b_q_norm, b_k_norm=b_k_norm, b_sinks=b_sinks, mem_norm=mem_norm,
                  w_mem_kv=w_mem_kv, m_q_norm=m_q_norm, m_k_norm=m_k_norm, w_o_a=w_o_a, w_o_b=w_o_b,
                  w_o_m=w_o_m, w_gate=w_gate, b_gate=b_gate, w_out=w_out, ffn_norm=ffn_norm,
                  w_up=w_up, conv_w=conv_w, conv_b=conv_b, w_down=w_down)
    pos_rows = positions.astype(F32).reshape(batch * seq // TOKEN_BLOCK, 1, TOKEN_BLOCK)
    x2 = x.reshape(batch * seq, D_MODEL)
    for layer in range(attn_norm.shape[0]):
        x2 = _layer(x2, mem, pos_rows, batch, seq, {k: v[layer] for k, v in params.items()})
    return x2.reshape(batch, seq, D_MODEL)
```

```python
import functools
import math

import numpy as np

import jax
import jax.numpy as jnp
from jax import lax
from jax.experimental import pallas as pl
from jax.experimental.pallas import tpu as pltpu

D_MODEL = 1024
HEAD_DIM = 64
A_GROUPS = ((128, 1), (512, 4), (2048, 16))
A_HEADS = 4
A_SLAB = A_HEADS * HEAD_DIM
A_QKV_COLS = len(A_GROUPS) * 3 * A_SLAB
B_Q_HEADS = 8
B_KV_HEADS = 2
B_GROUP = B_Q_HEADS // B_KV_HEADS
B_WINDOW = 128
B_Q_COLS = B_Q_HEADS * HEAD_DIM
B_KV_COLS = B_KV_HEADS * HEAD_DIM
M_HEADS = 4
M_HEAD_DIM = 128
M_Q_COLS = M_HEADS * M_HEAD_DIM
B_Q_OFF = A_QKV_COLS
B_K_OFF = B_Q_OFF + B_Q_COLS
B_V_OFF = B_K_OFF + B_KV_COLS
M_Q_OFF = B_V_OFF + B_KV_COLS
IN_COLS = M_Q_OFF + M_Q_COLS
N_NORMED_TILES = 2 * len(A_GROUPS) + B_Q_COLS // A_SLAB + 1 + M_Q_COLS // A_SLAB
N_BRANCH = 3
D_FF = 2816
CONV_WIDTH = 3
ROPE_THETA = 500000.0
ROPE_DIM = HEAD_DIM // 4
ROPE_HALF = ROPE_DIM // 2
BLOCK = 128
EPS = 1e-6
NEG = -1e30
LOG2E = math.log2(math.e)

LANES = 128
BF16_ROWS = 16
HEADS_PER_TILE = LANES // HEAD_DIM
TOKEN_BLOCK = 512
ROW_CHUNK = 64
FF_CHUNK = 256
VMEM_LIMIT = 56 * 1024 * 1024

F32 = jnp.float32
BF16 = jnp.bfloat16


def _const_spec(shape):
    return pl.BlockSpec(shape, lambda *_: (0,) * len(shape), pipeline_mode=pl.Buffered(1))


def _rms_rows(x, gain):
    ms = jnp.mean(x * x, axis=-1, keepdims=True)
    return x * lax.rsqrt(ms + EPS) * gain


def _mem_kv_kernel(mem_ref, gain_ref, w_ref, kgain_ref, mk_ref, mv_ref):
    hm = _rms_rows(mem_ref[...], gain_ref[...]).astype(BF16)
    kv = jnp.dot(hm, w_ref[...], preferred_element_type=F32)
    ks = []
    for h in range(M_HEADS):
        kh = kv[:, h * M_HEAD_DIM:(h + 1) * M_HEAD_DIM]
        ks.append(_rms_rows(kh, kgain_ref[...]))
    mk_ref[...] = jnp.concatenate(ks, axis=1).astype(BF16)
    mv_ref[...] = kv[:, M_Q_COLS:].astype(BF16)


def _mem_kv(mem, mem_gain, w_kv, k_gain):
    b, m, _ = mem.shape
    return pl.pallas_call(
        _mem_kv_kernel,
        grid=(b,),
        in_specs=[pl.BlockSpec((None, m, D_MODEL), lambda i: (i, 0, 0)),
                  _const_spec((1, D_MODEL)),
                  _const_spec((D_MODEL, 2 * M_Q_COLS)),
                  _const_spec((1, M_HEAD_DIM))],
        out_specs=[pl.BlockSpec((None, m, M_Q_COLS), lambda i: (i, 0, 0)),
                   pl.BlockSpec((None, m, M_Q_COLS), lambda i: (i, 0, 0))],
        out_shape=[jax.ShapeDtypeStruct((b, m, M_Q_COLS), BF16)] * 2,
        compiler_params=pltpu.CompilerParams(dimension_semantics=("arbitrary",),
                                             vmem_limit_bytes=VMEM_LIMIT),
        name="mem_kv",
    )(mem, mem_gain, w_kv, k_gain)


def _in_proj_kernel(x_ref, pos_ref, gain_ref, w_ref, colgain_ref, freq_ref, spread_ref, one_ref,
                    bd64_ref, bd128_ref, *refs, cast_scales):
    n_cast = len(cast_scales)
    cast_in, refs = refs[:n_cast], refs[n_cast:]
    (qa0, ka0, va0, qa1, ka1, va1, qa2, ka2, va2, qb_ref, kb_ref, vb_ref, mq_ref) = refs[:13]
    cast_out, refs = refs[13:13 + n_cast], refs[13 + n_cast:]
    proj_ref, ss_ref, h_ref, cos_ref, sin_lo_ref, sin_hi_ref = refs[:6]
    stage_refs = list(refs[6:])
    tb = x_ref.shape[0]

    for src, dst, scale in zip(cast_in, cast_out, cast_scales):
        w = src[...]
        dst[...] = (w if scale == 1.0 else w * scale).astype(BF16)
    chunks = [slice(r, r + ROW_CHUNK) for r in range(0, tb, ROW_CHUNK)]

    @pl.when(pl.program_id(0) == 0)
    def _():
        proj_ref[...] = jnp.zeros(proj_ref.shape, F32)
        ss_ref[...] = jnp.zeros(ss_ref.shape, F32)

    for rows in chunks:
        h_ref[rows, :] = _rms_rows(x_ref[rows, :], gain_ref[...]).astype(BF16)

    ang = freq_ref[...] * pos_ref[...]
    trig = jnp.concatenate([jnp.cos(ang), jnp.sin(ang),
                            jnp.zeros((LANES - ROPE_DIM, tb), F32)], axis=0).T
    tables = jnp.zeros((tb, 3 * LANES), F32)
    for _ in range(3):
        part = trig.astype(BF16)
        tables = tables + jnp.dot(part, spread_ref[...], preferred_element_type=F32)
        trig = trig - part.astype(F32)
    cos_ref[...] = tables[:, 0:LANES] + one_ref[...]
    sin_lo_ref[...] = tables[:, LANES:2 * LANES]
    sin_hi_ref[...] = tables[:, 2 * LANES:3 * LANES]

    ss_slots = iter(range(ss_ref.shape[0]))
    pending = []

    def flush_pending():
        while pending:
            new, bd, slot = pending.pop()
            ss_ref[slot, :, 0:new.shape[1]] = jnp.dot((new * new).astype(BF16), bd,
                                                      preferred_element_type=F32)

    def proj(c0, width, bd=None):
        tile, off = divmod(c0, A_SLAB)
        assert off + width <= A_SLAB and off % LANES == 0
        t = proj_ref[tile, :, off:off + width]
        ss = None
        if bd is not None:
            slot = next(ss_slots)
            ss = ss_ref[slot, :, 0:width]
        new = jnp.dot(h_ref[...], w_ref[:, c0:c0 + width], preferred_element_type=F32)
        proj_ref[tile, :, off:off + width] = new
        flush_pending()
        if bd is not None:
            pending.append((new, bd, slot))
        return t, ss

    def rope(y, rows):
        parts = []
        for c in range(y.shape[1] // LANES):
            yc = y[:, c * LANES:(c + 1) * LANES]
            parts.append(yc * cos_ref[rows, :]
                         + pltpu.roll(yc, LANES - ROPE_HALF, 1) * sin_lo_ref[rows, :]
                         + pltpu.roll(yc, ROPE_HALF, 1) * sin_hi_ref[rows, :])
        return parts[0] if len(parts) == 1 else jnp.concatenate(parts, axis=1)

    def finish(t_ss, emit, c0, *, dim=None, rotary=False):
        t, ss = t_ss
        if ss is not None:
            gain = colgain_ref[:, c0:c0 + t.shape[1]]
        for rows in chunks:
            y = t[rows]
            if ss is not None:
                y = y * lax.rsqrt(ss[rows] * (1.0 / dim) + EPS) * gain
            if rotary:
                y = rope(y, rows)
            emit(rows, y)

    def to_ref(out_ref, col0=0):
        def emit(rows, y):
            out_ref[rows, col0:col0 + y.shape[1]] = y.astype(BF16)
        return emit

    def store_tile(out_ref, t, dil, c0, **kw):
        if dil == 1:
            finish(t, to_ref(out_ref), c0, **kw)
            return
        stage_ref = stage_refs.pop()

        def emit(rows, y):
            for s in range(A_SLAB // LANES):
                stage_ref[s, rows, :] = y[:, s * LANES:(s + 1) * LANES]
        finish(t, emit, c0, **kw)
        for r in range(dil):
            for s in range(A_SLAB // LANES):
                col = r * A_SLAB + s * LANES
                out_ref[:, col:col + LANES] = stage_ref[s, pl.ds(r, tb // dil, stride=dil), :].astype(BF16)

    bd64 = bd64_ref[...]
    qk = dict(dim=HEAD_DIM, rotary=True)
    for g, (q_ref, k_ref, v_ref) in enumerate(((qa0, ka0, va0), (qa1, ka1, va1), (qa2, ka2, va2))):
        c0 = g * 3 * A_SLAB
        dil = A_GROUPS[g][1]
        store_tile(q_ref, proj(c0, A_SLAB, bd64), dil, c0, **qk)
        store_tile(k_ref, proj(c0 + A_SLAB, A_SLAB, bd64), dil, c0 + A_SLAB, **qk)
        store_tile(v_ref, proj(c0 + 2 * A_SLAB, A_SLAB), dil, c0 + 2 * A_SLAB)

    for s in range(B_Q_COLS // A_SLAB):
        c0 = B_Q_OFF + s * A_SLAB
        finish(proj(c0, A_SLAB, bd64), to_ref(qb_ref, s * A_SLAB), c0, **qk)

    def under_query_heads(out_ref):
        def emit(rows, y):
            swapped = pltpu.roll(y, HEAD_DIM, 1)
            first = jnp.where(first_head_lanes, y, swapped).astype(BF16)
            second = jnp.where(first_head_lanes, swapped, y).astype(BF16)
            reps = B_GROUP // HEADS_PER_TILE
            out_ref[rows, :] = jnp.concatenate([first] * reps + [second] * reps, axis=1)
        return emit
    first_head_lanes = lax.broadcasted_iota(jnp.int32, (1, LANES), 1) < HEAD_DIM
    finish(proj(B_K_OFF, B_KV_COLS, bd64_ref[0:B_KV_COLS, 0:B_KV_COLS]), under_query_heads(kb_ref),
           B_K_OFF, **qk)
    finish(proj(B_V_OFF, B_KV_COLS), under_query_heads(vb_ref), B_V_OFF)

    bd128 = bd128_ref[...]
    for s in range(M_Q_COLS // A_SLAB):
        c0 = M_Q_OFF + s * A_SLAB
        finish(proj(c0, A_SLAB, bd128), to_ref(mq_ref, s * A_SLAB), c0, dim=M_HEAD_DIM)
    flush_pending()


def _cast_rows(rows, n_steps):
    for per_step in range(BF16_ROWS, rows + 1, BF16_ROWS):
        if rows % per_step == 0 and rows // per_step <= n_steps:
            return per_step
    raise ValueError(f"no bf16-aligned split of {rows} rows over {n_steps} steps")


def _in_proj(x2, pos_rows, attn_gain, w_in, colgain, freq_col, spread, one_row, bd64, bd128,
             cast_weights, cast_scales):
    t = x2.shape[0]
    tb = TOKEN_BLOCK
    dils = [dil for _, dil in A_GROUPS for _ in range(3)] + [1, 1, 1, 1]
    widths = [A_SLAB] * 9 + [B_Q_COLS, B_Q_COLS, B_Q_COLS, M_Q_COLS]
    n_staged = sum(d > 1 for d in dils)
    n = t // tb
    done = lambda i: (jnp.maximum(i - 1, 0), 0)

    def cast_spec(w):
        per_step = _cast_rows(w.shape[0], n + 1)
        last = w.shape[0] // per_step - 1
        return pl.BlockSpec((per_step, w.shape[1]), lambda i: (jnp.minimum(i, last), 0))

    cast_specs = [cast_spec(w) for w in cast_weights]
    return pl.pallas_call(
        functools.partial(_in_proj_kernel, cast_scales=tuple(cast_scales)),
        grid=(n + 1,),
        in_specs=[pl.BlockSpec((tb, D_MODEL), lambda i: (jnp.minimum(i, n - 1), 0)),
                  pl.BlockSpec((None, 1, tb), lambda i: (jnp.maximum(i - 1, 0), 0, 0)),
                  _const_spec((1, D_MODEL)),
                  _const_spec((D_MODEL, IN_COLS)),
                  _const_spec((1, IN_COLS)),
                  _const_spec((ROPE_HALF, 1)),
                  _const_spec((LANES, 3 * LANES)),
                  _const_spec((1, LANES)),
                  _const_spec((A_SLAB, A_SLAB)),
                  _const_spec((A_SLAB, A_SLAB))] + cast_specs,
        out_specs=[pl.BlockSpec((tb // d, w * d), done) for w, d in zip(widths, dils)] + cast_specs,
        out_shape=[jax.ShapeDtypeStruct((t // d, w * d), BF16) for w, d in zip(widths, dils)]
                  + [jax.ShapeDtypeStruct(w.shape, BF16) for w in cast_weights],
        scratch_shapes=[pltpu.VMEM((IN_COLS // A_SLAB, tb, A_SLAB), F32),
                        pltpu.VMEM((N_NORMED_TILES, tb, A_SLAB), F32)]
                       + [pltpu.VMEM((tb, D_MODEL), BF16)] + [pltpu.VMEM((tb, LANES), F32)] * 3
                       + [pltpu.VMEM((A_SLAB // LANES, tb, LANES), F32)] * n_staged,
        compiler_params=pltpu.CompilerParams(dimension_semantics=("arbitrary",),
                                             vmem_limit_bytes=VMEM_LIMIT),
        name="in_proj",
    )(x2, pos_rows, attn_gain, w_in, colgain, freq_col, spread, one_row, bd64, bd128, *cast_weights)


def _band_attn_kernel(*refs, rows, slabs, prev_off, has_sink, with_lse):
    q_ref, kc_ref, kp_ref, vc_ref, vp_ref = refs[:5]
    pos = 5
    sink_ref = None
    if has_sink:
        sink_ref = refs[pos]
        pos += 1
    o_ref = refs[pos]
    lse_ref = refs[pos + 1] if with_lse else None

    first_step = pl.program_id(2) == 0
    row = lax.broadcasted_iota(jnp.int32, (BLOCK, 2 * BLOCK), 0)
    col = lax.broadcasted_iota(jnp.int32, (BLOCK, 2 * BLOCK), 1)
    bias = jnp.where(col < BLOCK,
                     jnp.where(col >= row + prev_off, 0.0, NEG),
                     jnp.where(col - BLOCK <= row, 0.0, NEG)).astype(F32)
    bias_first = jnp.where(jnp.logical_and(first_step, col < BLOCK), NEG, bias)
    head_of_lane = lax.broadcasted_iota(jnp.int32, (1, A_SLAB), 1) // HEAD_DIM
    head_mask = [(head_of_lane == hh).astype(BF16) for hh in range(A_HEADS)]
    first_head_lanes = lax.broadcasted_iota(jnp.int32, (1, LANES), 1) < HEAD_DIM
    pack_row = lax.broadcasted_iota(jnp.int32, (BF16_ROWS, A_SLAB), 0)
    nblk = rows // BLOCK

    def drop_first_row(t):
        top = jnp.where(pack_row == 0, jnp.zeros((BF16_ROWS, A_SLAB), BF16), t[0:BF16_ROWS])
        return jnp.concatenate([top, t[BF16_ROWS:]], axis=0)

    for w in range(slabs):
        cs = slice(w * A_SLAB, (w + 1) * A_SLAB)
        biases = {False: [bias] * A_HEADS, True: [bias_first] * A_HEADS}
        if has_sink:
            biases = {first: [jnp.where(col == 0, sink_ref[w * A_HEADS + hh] * LOG2E, b[hh])
                              for hh in range(A_HEADS)] for first, b in biases.items()}
        for jb in range(nblk):
            q = q_ref[jb * BLOCK:(jb + 1) * BLOCK, cs]
            if jb == 0:
                k2 = jnp.concatenate([kp_ref[:, cs], kc_ref[0:BLOCK, cs]], axis=0)
                v2 = jnp.concatenate([vp_ref[:, cs], vc_ref[0:BLOCK, cs]], axis=0)
            else:
                k2 = kc_ref[(jb - 1) * BLOCK:(jb + 1) * BLOCK, cs]
                v2 = vc_ref[(jb - 1) * BLOCK:(jb + 1) * BLOCK, cs]
            if has_sink:
                k2 = drop_first_row(k2)
                v2 = drop_first_row(v2)
            qs = jnp.concatenate([q * head_mask[hh] for hh in range(A_HEADS)], axis=0)
            s = lax.dot_general(qs, k2, (((1,), (1,)), ((), ())), preferred_element_type=F32)
            ps, ms, ls = [], [], []
            for hh in range(A_HEADS):
                sh = s[hh * BLOCK:(hh + 1) * BLOCK] + biases[jb == 0][hh]
                m = jnp.max(sh, axis=-1, keepdims=True)
                p = jnp.exp2(sh - m)
                ms.append(m)
                ls.append(jnp.sum(p, axis=-1, keepdims=True))
                ps.append(p.astype(BF16))
            ost = jnp.dot(jnp.concatenate(ps, axis=0), v2, preferred_element_type=F32)
            o_tiles, lse_tiles = [], []
            for t0 in range(0, A_HEADS, HEADS_PER_TILE):
                lanes = slice(t0 * HEAD_DIM, (t0 + HEADS_PER_TILE) * HEAD_DIM)
                per_head = [ost[hh * BLOCK:(hh + 1) * BLOCK, lanes] * (1.0 / ls[hh])
                            for hh in range(t0, t0 + HEADS_PER_TILE)]
                o_tiles.append(jnp.where(first_head_lanes, per_head[0], per_head[1]))
                if with_lse:
                    lse_tiles.append(jnp.where(first_head_lanes, ms[t0] + jnp.log2(ls[t0]),
                                               ms[t0 + 1] + jnp.log2(ls[t0 + 1])))
            o_ref[jb * BLOCK:(jb + 1) * BLOCK, cs] = jnp.concatenate(o_tiles, axis=1).astype(o_ref.dtype)
            if with_lse:
                lse_ref[jb * BLOCK:(jb + 1) * BLOCK, cs] = jnp.concatenate(lse_tiles, axis=1)


def _band_attn(q, k, v, *, name, rows, slabs, prev_off, sinks=None, with_lse=True):
    b, length, width = q.shape
    assert sinks is None or prev_off >= 1
    n_col = width // (slabs * A_SLAB)
    n_row = length // rows
    blk_per_step = rows // BLOCK
    cur = pl.BlockSpec((None, rows, slabs * A_SLAB), lambda i, r, j: (i, j, r))
    prev = pl.BlockSpec((None, BLOCK, slabs * A_SLAB),
                        lambda i, r, j: (i, jnp.maximum(j * blk_per_step - 1, 0), r))
    in_specs = [cur, cur, prev, cur, prev]
    args = [q, k, k, v, v]
    if sinks is not None:
        in_specs.append(pl.BlockSpec(memory_space=pltpu.SMEM))
        args.append(sinks)
    out_specs = [cur]
    out_shape = [jax.ShapeDtypeStruct((b, length, width), BF16)]
    if with_lse:
        out_specs.append(cur)
        out_shape.append(jax.ShapeDtypeStruct((b, length, width), F32))
    return pl.pallas_call(
        functools.partial(_band_attn_kernel, rows=rows, slabs=slabs, prev_off=prev_off,
                          has_sink=sinks is not None, with_lse=with_lse),
        grid=(b, n_col, n_row),
        in_specs=in_specs,
        out_specs=out_specs,
        out_shape=out_shape,
        compiler_params=pltpu.CompilerParams(dimension_semantics=("arbitrary",) * 3,
                                             vmem_limit_bytes=VMEM_LIMIT),
        name=name,
    )(*args)


def _mix_kernel(x_ref, gain_ref, oa0, la0, oa1, la1, oa2, la2, ob_ref, mq_ref, mk_ref, mv_ref,
                wg_ref, bg_ref, woa_ref, wob_ref, wom_ref, wout_ref, out_ref, *stage_refs):
    x = x_ref[...]
    h = _rms_rows(x, gain_ref[...]).astype(BF16)
    stage_refs = list(stage_refs)

    def token_major(blk_ref, dil):
        if dil == 1:
            return blk_ref[...].astype(F32)
        stage_ref = stage_refs.pop()
        rows = blk_ref.shape[0]
        for r in range(dil):
            for s in range(A_SLAB // LANES):
                c0 = r * A_SLAB + s * LANES
                stage_ref[s, pl.ds(r, rows, stride=dil), :] = blk_ref[:, c0:c0 + LANES].astype(F32)
        return jnp.concatenate([stage_ref[s] for s in range(A_SLAB // LANES)], axis=1)

    head_cols = [slice(hh * M_HEAD_DIM, (hh + 1) * M_HEAD_DIM) for hh in range(M_HEADS)]
    scores = [lax.dot_general(mq_ref[:, cs], mk_ref[:, cs], (((1,), (1,)), ((), ())),
                              preferred_element_type=F32) for cs in head_cols]
    gate_cols = [slice(i * D_MODEL, (i + 1) * D_MODEL) for i in range(N_BRANCH)]
    gates = [1.0 + jnp.tanh(jnp.dot(h, wg_ref[:, cs], preferred_element_type=F32) + 0.5 * bg_ref[:, cs])
             for cs in gate_cols]

    dils = [dil for _, dil in A_GROUPS]
    l0, l1, l2 = (token_major(r, d) for r, d in zip((la0, la1, la2), dils))
    o0, o1, o2 = (token_major(r, d) for r, d in zip((oa0, oa1, oa2), dils))
    mx = jnp.maximum(jnp.maximum(l0, l1), l2)
    e0, e1, e2 = jnp.exp2(l0 - mx), jnp.exp2(l1 - mx), jnp.exp2(l2 - mx)
    o_a = (e0 * o0 + e1 * o1 + e2 * o2) / (e0 + e1 + e2)

    o_m = []
    for s, cs in zip(scores, head_cols):
        p = jnp.exp2(s - jnp.max(s, axis=-1, keepdims=True))
        l = jnp.sum(p, axis=-1, keepdims=True)
        o_m.append(jnp.dot(p.astype(BF16), mv_ref[:, cs], preferred_element_type=F32) / l)
    o_m = jnp.concatenate(o_m, axis=1)

    merged = jnp.zeros(x.shape, F32)
    for gate, o, w_ref in zip(gates, (o_a, ob_ref[...], o_m), (woa_ref, wob_ref, wom_ref)):
        merged = merged + gate * jnp.dot(o.astype(BF16), w_ref[...], preferred_element_type=F32)
    out_ref[...] = x + jnp.dot(merged.astype(BF16), wout_ref[...], preferred_element_type=F32)


def _mix(x2, attn_gain, oa, la, ob, mq, mk, mv, w_gate, b_gate, w_o_a, w_o_b, w_o_m, w_out, seq):
    t = x2.shape[0]
    tb = TOKEN_BLOCK
    mem_len = mk.shape[1]
    per_seq = seq // tb
    tok = lambda w, d=1: pl.BlockSpec((tb // d, w * d), lambda i: (i, 0))
    mem = pl.BlockSpec((None, mem_len, M_Q_COLS), lambda i: (i // per_seq, 0, 0))
    dils = [dil for _, dil in A_GROUPS]
    return pl.pallas_call(
        _mix_kernel,
        grid=(t // tb,),
        in_specs=[tok(D_MODEL), _const_spec((1, D_MODEL))]
                 + [tok(A_SLAB, d) for d in dils for _ in range(2)]
                 + [tok(B_Q_COLS), tok(M_Q_COLS), mem, mem,
                  _const_spec((D_MODEL, N_BRANCH * D_MODEL)), _const_spec((1, N_BRANCH * D_MODEL)),
                  _const_spec((A_SLAB, D_MODEL)), _const_spec((B_Q_COLS, D_MODEL)),
                  _const_spec((M_Q_COLS, D_MODEL)), _const_spec((D_MODEL, D_MODEL))],
        out_specs=tok(D_MODEL),
        out_shape=jax.ShapeDtypeStruct((t, D_MODEL), F32),
        scratch_shapes=[pltpu.VMEM((A_SLAB // LANES, tb, LANES), F32)] * (2 * sum(d > 1 for d in dils)),
        compiler_params=pltpu.CompilerParams(dimension_semantics=("arbitrary",),
                                             vmem_limit_bytes=VMEM_LIMIT),
        name="mix",
    )(x2, attn_gain, oa[0], la[0], oa[1], la[1], oa[2], la[2], ob, mq, mk, mv,
      w_gate, b_gate, w_o_a, w_o_b, w_o_m, w_out)


def _conv_ffn_kernel(x_ref, gain_ref, wup_ref, cw_ref, cb_ref, wdown_ref, out_ref,
                     carry_ref, ext_ref, act_ref, *, per_seq):
    tb = x_ref.shape[0]
    halo = CONV_WIDTH - 1
    pad = 8

    @pl.when(pl.program_id(0) % per_seq == 0)
    def _():
        carry_ref[...] = jnp.zeros(carry_ref.shape, F32)

    x = x_ref[...]
    h = _rms_rows(x, gain_ref[...]).astype(BF16)

    def conv(c0, scale=1.0):
        cs = slice(c0, c0 + FF_CHUNK)
        u = jnp.dot(h, wup_ref[:, cs], preferred_element_type=F32)
        ext_ref[0:pad, :] = carry_ref[:, cs]
        ext_ref[pad:pad + tb, :] = u
        carry_ref[:, cs] = u[tb - pad:tb, :]
        taps = cw_ref[:, cs] * scale
        c = cb_ref[:, cs] * scale + taps[CONV_WIDTH - 1:CONV_WIDTH] * u
        for j in range(halo):
            c = c + taps[j:j + 1] * ext_ref[pad - halo + j:pad - halo + j + tb, :]
        return c

    for ch in range(D_FF // FF_CHUNK):
        half = conv(ch * FF_CHUNK, 0.5)
        g = conv(D_FF + ch * FF_CHUNK)
        act = (half + half * jnp.tanh(half)) * g
        act_ref[:, ch * FF_CHUNK:(ch + 1) * FF_CHUNK] = act.astype(BF16)
    out_ref[...] = x + jnp.dot(act_ref[...], wdown_ref[...], preferred_element_type=F32)


def _conv_ffn(x2, gain, w_up, conv_w, conv_b, w_down, seq):
    t = x2.shape[0]
    tb = TOKEN_BLOCK
    return pl.pallas_call(
        functools.partial(_conv_ffn_kernel, per_seq=seq // tb),
        grid=(t // tb,),
        in_specs=[pl.BlockSpec((tb, D_MODEL), lambda i: (i, 0)),
                  _const_spec((1, D_MODEL)),
                  _const_spec((D_MODEL, 2 * D_FF)),
                  _const_spec((CONV_WIDTH, 2 * D_FF)),
                  _const_spec((1, 2 * D_FF)),
                  _const_spec((D_FF, D_MODEL))],
        out_specs=pl.BlockSpec((tb, D_MODEL), lambda i: (i, 0)),
        out_shape=jax.ShapeDtypeStruct((t, D_MODEL), F32),
        scratch_shapes=[pltpu.VMEM((8, 2 * D_FF), F32),
                        pltpu.VMEM((tb + 8, FF_CHUNK), F32),
                        pltpu.VMEM((tb, D_FF), BF16)],
        compiler_params=pltpu.CompilerParams(dimension_semantics=("arbitrary",),
                                             vmem_limit_bytes=VMEM_LIMIT),
        name="conv_ffn",
    )(x2, gain, w_up, conv_w, conv_b, w_down)


def _tables():
    lane = np.arange(A_SLAB)
    bd64 = jnp.asarray(lane[:, None] // HEAD_DIM == lane[None, :] // HEAD_DIM, BF16)
    bd128 = jnp.asarray(lane[:, None] // M_HEAD_DIM == lane[None, :] // M_HEAD_DIM, BF16)
    freq_col = jnp.exp(jnp.arange(ROPE_HALF, dtype=F32) * (-2.0 * math.log(ROPE_THETA) / ROPE_DIM))
    freq_col = freq_col.reshape(ROPE_HALF, 1)
    in_head = np.arange(LANES) % HEAD_DIM
    j = np.arange(LANES)[:, None]
    rot = in_head[None, :] < ROPE_DIM
    cos_sel = rot & (j == in_head[None, :] % ROPE_HALF)
    sin_sel = rot & (j == ROPE_HALF + in_head[None, :] % ROPE_HALF)
    lo = in_head[None, :] < ROPE_HALF
    spread = np.concatenate([cos_sel.astype(np.float32), -(sin_sel & lo).astype(np.float32),
                             (sin_sel & ~lo).astype(np.float32)], axis=1)
    one_row = (in_head >= ROPE_DIM).astype(np.float32).reshape(1, LANES)
    return bd64, bd128, freq_col, jnp.asarray(spread, BF16), jnp.asarray(one_row)


def _layer(x2, mem, pos_rows, batch, seq, p):
    bd64, bd128, freq_col, spread, one_row = _tables()
    qk_scale = HEAD_DIM ** -0.5 * LOG2E
    ones = jnp.ones((A_SLAB,), F32)
    colgain = jnp.concatenate(
        [jnp.concatenate([jnp.tile(p["a_q_norm"][g], A_HEADS) * qk_scale,
                          jnp.tile(p["a_k_norm"][g], A_HEADS), ones]) for g in range(len(A_GROUPS))]
        + [jnp.tile(p["b_q_norm"], B_Q_HEADS) * qk_scale,
           jnp.tile(p["b_k_norm"], B_KV_HEADS), jnp.ones((B_KV_COLS,), F32),
           jnp.tile(p["m_q_norm"], M_HEADS) * (M_HEAD_DIM ** -0.5 * LOG2E)]).reshape(1, IN_COLS)

    attn_gain = p["attn_norm"].reshape(1, D_MODEL)
    later_weights = ("w_mem_kv", "w_gate", "w_o_a", "w_o_b", "w_o_m", "w_out", "w_up", "w_down")
    scales = [0.5 if name in ("w_gate", "w_out") else 1.0 for name in later_weights]
    outs = _in_proj(x2, pos_rows, attn_gain, p["w_in"].astype(BF16), colgain, freq_col, spread, one_row,
                    bd64, bd128, [p[name] for name in later_weights], scales)
    (qa0, ka0, va0, qa1, ka1, va1, qa2, ka2, va2, qb, kb, vb, mq) = outs[:13]
    wb = dict(zip(later_weights, outs[13:]))
    mk, mv = _mem_kv(mem, p["mem_norm"].reshape(1, D_MODEL), wb["w_mem_kv"],
                     p["m_k_norm"].reshape(1, M_HEAD_DIM))

    def per_batch(t):
        return t.reshape(batch, t.shape[0] // batch, t.shape[-1])

    def flat(t):
        return t.reshape(t.shape[0] * t.shape[1], t.shape[2])

    oa, la = [], []
    cfg = {1: dict(rows=2048, slabs=1), 4: dict(rows=seq // 4, slabs=2), 16: dict(rows=seq // 16, slabs=8)}
    for (window, dil), (q, k, v) in zip(A_GROUPS, ((qa0, ka0, va0), (qa1, ka1, va1), (qa2, ka2, va2))):
        assert window // dil == BLOCK
        o, lse = _band_attn(per_batch(q), per_batch(k), per_batch(v), name=f"band_attn_d{dil}",
                            prev_off=0, **cfg[dil])
        oa.append(flat(o))
        la.append(flat(lse))
    (ob,) = _band_attn(per_batch(qb), per_batch(kb), per_batch(vb), name="band_attn_swa", rows=1024,
                       slabs=B_KV_HEADS, prev_off=BLOCK - (B_WINDOW - 1), sinks=p["b_sinks"],
                       with_lse=False)
    ob = flat(ob)

    x2 = _mix(x2, attn_gain, oa, la, ob, mq, mk, mv, wb["w_gate"],
              p["b_gate"].reshape(1, N_BRANCH * D_MODEL), wb["w_o_a"], wb["w_o_b"], wb["w_o_m"],
              wb["w_out"], seq)
    return _conv_ffn(x2, p["ffn_norm"].reshape(1, D_MODEL), wb["w_up"], p["conv_w"],
                     p["conv_b"].reshape(1, 2 * D_FF), wb["w_down"], seq)


def kernel(x, mem, positions, attn_norm, w_in, a_q_norm, a_k_norm, b_q_norm, b_k_norm, b_sinks,
           mem_norm, w_mem_kv, m_q_norm, m_k_norm, w_o_a, w_o_b, w_o_m, w_gate, b_gate, w_out,
           ffn_norm, w_up, conv_w, conv_b, w_down):
    batch, seq, _ = x.shape
    params = dict(attn_norm=attn_norm, w_in=w_in, a_q_norm=a_q_norm, a_k_norm=a_k_norm,
                  b_q_norm=b_q_norm, b_k_norm=b_k_norm, b_sinks=b_sinks, mem_norm=mem_norm,
                  w_mem_kv=w_mem_kv, m_q_norm=m_q_norm, m_k_norm=m_k_norm, w_o_a=w_o_a, w_o_b=w_o_b,
                  w_o_m=w_o_m, w_gate=w_gate, b_gate=b_gate, w_out=w_out, ffn_norm=ffn_norm,
                  w_up=w_up, conv_w=conv_w, conv_b=conv_b, w_down=w_down)
    pos_rows = positions.astype(F32).reshape(batch * seq // TOKEN_BLOCK, 1, TOKEN_BLOCK)
    x2 = x.reshape(batch * seq, D_MODEL)
    for layer in range(attn_norm.shape[0]):
        x2 = _layer(x2, mem, pos_rows, batch, seq, {k: v[layer] for k, v in params.items()})
    return x2.reshape(batch, seq, D_MODEL)
```

```python
import functools
import math

import numpy as np

import jax
import jax.numpy as jnp
from jax import lax
from jax.experimental import pallas as pl
from jax.experimental.pallas import tpu as pltpu

D_MODEL = 1024
HEAD_DIM = 64
A_GROUPS = ((128, 1), (512, 4), (2048, 16))
A_HEADS = 4
A_SLAB = A_HEADS * HEAD_DIM
A_QKV_COLS = len(A_GROUPS) * 3 * A_SLAB
B_Q_HEADS = 8
B_KV_HEADS = 2
B_GROUP = B_Q_HEADS // B_KV_HEADS
B_WINDOW = 128
B_Q_COLS = B_Q_HEADS * HEAD_DIM
B_KV_COLS = B_KV_HEADS * HEAD_DIM
M_HEADS = 4
M_HEAD_DIM = 128
M_Q_COLS = M_HEADS * M_HEAD_DIM
B_Q_OFF = A_QKV_COLS
B_K_OFF = B_Q_OFF + B_Q_COLS
B_V_OFF = B_K_OFF + B_KV_COLS
M_Q_OFF = B_V_OFF + B_KV_COLS
IN_COLS = M_Q_OFF + M_Q_COLS
N_NORMED_TILES = 2 * len(A_GROUPS) + B_Q_COLS // A_SLAB + 1 + M_Q_COLS // A_SLAB
N_BRANCH = 3
D_FF = 2816
CONV_WIDTH = 3
ROPE_THETA = 500000.0
ROPE_DIM = HEAD_DIM // 4
ROPE_HALF = ROPE_DIM // 2
BLOCK = 128
EPS = 1e-6
NEG = -1e30
LOG2E = math.log2(math.e)

LANES = 128
BF16_ROWS = 16
HEADS_PER_TILE = LANES // HEAD_DIM
TOKEN_BLOCK = 512
ROW_CHUNK = 64
FF_CHUNK = 256
VMEM_LIMIT = 56 * 1024 * 1024

F32 = jnp.float32
BF16 = jnp.bfloat16


def _const_spec(shape):
    return pl.BlockSpec(shape, lambda *_: (0,) * len(shape), pipeline_mode=pl.Buffered(1))


def _rms_rows(x, gain):
    ms = jnp.mean(x * x, axis=-1, keepdims=True)
    return x * lax.rsqrt(ms + EPS) * gain


def _mem_kv_kernel(mem_ref, gain_ref, w_ref, kgain_ref, win_ref, mk_ref, mv_ref, win_out_ref):
    win_out_ref[...] = win_ref[...].astype(BF16)
    hm = _rms_rows(mem_ref[...], gain_ref[...]).astype(BF16)
    kv = jnp.dot(hm, w_ref[...].astype(BF16), preferred_element_type=F32)
    ks = []
    for h in range(M_HEADS):
        kh = kv[:, h * M_HEAD_DIM:(h + 1) * M_HEAD_DIM]
        ks.append(_rms_rows(kh, kgain_ref[...]))
    mk_ref[...] = jnp.concatenate(ks, axis=1).astype(BF16)
    mv_ref[...] = kv[:, M_Q_COLS:].astype(BF16)


def _mem_kv(mem, mem_gain, w_kv, k_gain, w_in):
    b, m, _ = mem.shape
    slab = pl.BlockSpec((_cast_rows(w_in.shape[0], b), w_in.shape[1]), lambda i: (i, 0))
    assert w_in.shape[0] == b * slab.block_shape[0]
    return pl.pallas_call(
        _mem_kv_kernel,
        grid=(b,),
        in_specs=[pl.BlockSpec((None, m, D_MODEL), lambda i: (i, 0, 0)),
                  _const_spec((1, D_MODEL)),
                  _const_spec((D_MODEL, 2 * M_Q_COLS)),
                  _const_spec((1, M_HEAD_DIM)),
                  slab],
        out_specs=[pl.BlockSpec((None, m, M_Q_COLS), lambda i: (i, 0, 0)),
                   pl.BlockSpec((None, m, M_Q_COLS), lambda i: (i, 0, 0)),
                   slab],
        out_shape=[jax.ShapeDtypeStruct((b, m, M_Q_COLS), BF16)] * 2
                  + [jax.ShapeDtypeStruct(w_in.shape, BF16)],
        compiler_params=pltpu.CompilerParams(dimension_semantics=("arbitrary",),
                                             vmem_limit_bytes=VMEM_LIMIT),
        name="mem_kv",
    )(mem, mem_gain, w_kv, k_gain, w_in)


def _in_proj_kernel(x_ref, pos_ref, gain_ref, w_ref, colgain_ref, freq_ref, spread_ref, one_ref,
                    bd64_ref, bd128_ref, *refs, cast_scales):
    n_cast = len(cast_scales)
    cast_in, refs = refs[:n_cast], refs[n_cast:]
    (qa0, ka0, va0, qa1, ka1, va1, qa2, ka2, va2, qb_ref, kb_ref, vb_ref, mq_ref) = refs[:13]
    cast_out, refs = refs[13:13 + n_cast], refs[13 + n_cast:]
    proj_ref, ss_ref, h_ref, cos_ref, sin_lo_ref, sin_hi_ref = refs[:6]
    stage_refs = list(refs[6:])
    tb = x_ref.shape[0]

    for src, dst, scale in zip(cast_in, cast_out, cast_scales):
        w = src[...]
        dst[...] = (w if scale == 1.0 else w * scale).astype(BF16)
    chunks = [slice(r, r + ROW_CHUNK) for r in range(0, tb, ROW_CHUNK)]

    @pl.when(pl.program_id(0) == 0)
    def _():
        proj_ref[...] = jnp.zeros(proj_ref.shape, F32)
        ss_ref[...] = jnp.zeros(ss_ref.shape, F32)

    for rows in chunks:
        h_ref[rows, :] = _rms_rows(x_ref[rows, :], gain_ref[...]).astype(BF16)

    ang = freq_ref[...] * pos_ref[...]
    trig = jnp.concatenate([jnp.cos(ang), jnp.sin(ang),
                            jnp.zeros((LANES - ROPE_DIM, tb), F32)], axis=0).T
    tables = jnp.zeros((tb, 3 * LANES), F32)
    for _ in range(3):
        part = trig.astype(BF16)
        tables = tables + jnp.dot(part, spread_ref[...], preferred_element_type=F32)
        trig = trig - part.astype(F32)
    cos_ref[...] = tables[:, 0:LANES] + one_ref[...]
    sin_lo_ref[...] = tables[:, LANES:2 * LANES]
    sin_hi_ref[...] = tables[:, 2 * LANES:3 * LANES]

    ss_slots = iter(range(ss_ref.shape[0]))
    pending = []

    def flush_pending():
        while pending:
            new, bd, slot = pending.pop()
            ss_ref[slot, :, 0:new.shape[1]] = jnp.dot((new * new).astype(BF16), bd,
                                                      preferred_element_type=F32)

    def proj(c0, width, bd=None):
        tile, off = divmod(c0, A_SLAB)
        assert off + width <= A_SLAB and off % LANES == 0
        t = proj_ref[tile, :, off:off + width]
        ss = None
        if bd is not None:
            slot = next(ss_slots)
            ss = ss_ref[slot, :, 0:width]
        new = jnp.dot(h_ref[...], w_ref[:, c0:c0 + width], preferred_element_type=F32)
        proj_ref[tile, :, off:off + width] = new
        flush_pending()
        if bd is not None:
            pending.append((new, bd, slot))
        return t, ss

    def rope(y, rows):
        parts = []
        for c in range(y.shape[1] // LANES):
            yc = y[:, c * LANES:(c + 1) * LANES]
            parts.append(yc * cos_ref[rows, :]
                         + pltpu.roll(yc, LANES - ROPE_HALF, 1) * sin_lo_ref[rows, :]
                         + pltpu.roll(yc, ROPE_HALF, 1) * sin_hi_ref[rows, :])
        return parts[0] if len(parts) == 1 else jnp.concatenate(parts, axis=1)

    def finish(t_ss, emit, c0, *, dim=None, rotary=False):
        t, ss = t_ss
        if ss is not None:
            gain = colgain_ref[:, c0:c0 + t.shape[1]]
        for rows in chunks:
            y = t[rows]
            if ss is not None:
                y = y * lax.rsqrt(ss[rows] * (1.0 / dim) + EPS) * gain
            if rotary:
                y = rope(y, rows)
            emit(rows, y)

    def to_ref(out_ref, col0=0):
        def emit(rows, y):
            out_ref[rows, col0:col0 + y.shape[1]] = y.astype(BF16)
        return emit

    def store_tile(out_ref, t, dil, c0, **kw):
        if dil == 1:
            finish(t, to_ref(out_ref), c0, **kw)
            return
        stage_ref = stage_refs.pop()

        def emit(rows, y):
            for s in range(A_SLAB // LANES):
                stage_ref[s, rows, :] = y[:, s * LANES:(s + 1) * LANES]
        finish(t, emit, c0, **kw)
        for r in range(dil):
            for s in range(A_SLAB // LANES):
                col = r * A_SLAB + s * LANES
                out_ref[:, col:col + LANES] = stage_ref[s, pl.ds(r, tb // dil, stride=dil), :].astype(BF16)

    bd64 = bd64_ref[...]
    qk = dict(dim=HEAD_DIM, rotary=True)
    for g, (q_ref, k_ref, v_ref) in enumerate(((qa0, ka0, va0), (qa1, ka1, va1), (qa2, ka2, va2))):
        c0 = g * 3 * A_SLAB
        dil = A_GROUPS[g][1]
        store_tile(q_ref, proj(c0, A_SLAB, bd64), dil, c0, **qk)
        store_tile(k_ref, proj(c0 + A_SLAB, A_SLAB, bd64), dil, c0 + A_SLAB, **qk)
        store_tile(v_ref, proj(c0 + 2 * A_SLAB, A_SLAB), dil, c0 + 2 * A_SLAB)

    for s in range(B_Q_COLS // A_SLAB):
        c0 = B_Q_OFF + s * A_SLAB
        finish(proj(c0, A_SLAB, bd64), to_ref(qb_ref, s * A_SLAB), c0, **qk)

    def under_query_heads(out_ref):
        def emit(rows, y):
            swapped = pltpu.roll(y, HEAD_DIM, 1)
            first = jnp.where(first_head_lanes, y, swapped).astype(BF16)
            second = jnp.where(first_head_lanes, swapped, y).astype(BF16)
            reps = B_GROUP // HEADS_PER_TILE
            out_ref[rows, :] = jnp.concatenate([first] * reps + [second] * reps, axis=1)
        return emit
    first_head_lanes = lax.broadcasted_iota(jnp.int32, (1, LANES), 1) < HEAD_DIM
    finish(proj(B_K_OFF, B_KV_COLS, bd64_ref[0:B_KV_COLS, 0:B_KV_COLS]), under_query_heads(kb_ref),
           B_K_OFF, **qk)
    finish(proj(B_V_OFF, B_KV_COLS), under_query_heads(vb_ref), B_V_OFF)

    bd128 = bd128_ref[...]
    for s in range(M_Q_COLS // A_SLAB):
        c0 = M_Q_OFF + s * A_SLAB
        finish(proj(c0, A_SLAB, bd128), to_ref(mq_ref, s * A_SLAB), c0, dim=M_HEAD_DIM)
    flush_pending()


def _cast_rows(rows, n_steps):
    for per_step in range(BF16_ROWS, rows + 1, BF16_ROWS):
        if rows % per_step == 0 and rows // per_step <= n_steps:
            return per_step
    raise ValueError(f"no bf16-aligned split of {rows} rows over {n_steps} steps")


def _in_proj(x2, pos_rows, attn_gain, w_in, colgain, freq_col, spread, one_row, bd64, bd128,
             cast_weights, cast_scales):
    t = x2.shape[0]
    tb = TOKEN_BLOCK
    dils = [dil for _, dil in A_GROUPS for _ in range(3)] + [1, 1, 1, 1]
    widths = [A_SLAB] * 9 + [B_Q_COLS, B_Q_COLS, B_Q_COLS, M_Q_COLS]
    n_staged = sum(d > 1 for d in dils)
    n = t // tb
    done = lambda i: (jnp.maximum(i - 1, 0), 0)

    def cast_spec(w):
        per_step = _cast_rows(w.shape[0], n + 1)
        last = w.shape[0] // per_step - 1
        return pl.BlockSpec((per_step, w.shape[1]), lambda i: (jnp.minimum(i, last), 0))

    cast_specs = [cast_spec(w) for w in cast_weights]
    return pl.pallas_call(
        functools.partial(_in_proj_kernel, cast_scales=tuple(cast_scales)),
        grid=(n + 1,),
        in_specs=[pl.BlockSpec((tb, D_MODEL), lambda i: (jnp.minimum(i, n - 1), 0)),
                  pl.BlockSpec((None, 1, tb), lambda i: (jnp.maximum(i - 1, 0), 0, 0)),
                  _const_spec((1, D_MODEL)),
                  _const_spec((D_MODEL, IN_COLS)),
                  _const_spec((1, IN_COLS)),
                  _const_spec((ROPE_HALF, 1)),
                  _const_spec((LANES, 3 * LANES)),
                  _const_spec((1, LANES)),
                  _const_spec((A_SLAB, A_SLAB)),
                  _const_spec((A_SLAB, A_SLAB))] + cast_specs,
        out_specs=[pl.BlockSpec((tb // d, w * d), done) for w, d in zip(widths, dils)] + cast_specs,
        out_shape=[jax.ShapeDtypeStruct((t // d, w * d), BF16) for w, d in zip(widths, dils)]
                  + [jax.ShapeDtypeStruct(w.shape, BF16) for w in cast_weights],
        scratch_shapes=[pltpu.VMEM((IN_COLS // A_SLAB, tb, A_SLAB), F32),
                        pltpu.VMEM((N_NORMED_TILES, tb, A_SLAB), F32)]
                       + [pltpu.VMEM((tb, D_MODEL), BF16)] + [pltpu.VMEM((tb, LANES), F32)] * 3
                       + [pltpu.VMEM((A_SLAB // LANES, tb, LANES), F32)] * n_staged,
        compiler_params=pltpu.CompilerParams(dimension_semantics=("arbitrary",),
                                             vmem_limit_bytes=VMEM_LIMIT),
        name="in_proj",
    )(x2, pos_rows, attn_gain, w_in, colgain, freq_col, spread, one_row, bd64, bd128, *cast_weights)


def _band_attn_kernel(*refs, rows, slabs, prev_off, has_sink, with_lse):
    q_ref, kc_ref, kp_ref, vc_ref, vp_ref = refs[:5]
    pos = 5
    sink_ref = None
    if has_sink:
        sink_ref = refs[pos]
        pos += 1
    o_ref = refs[pos]
    lse_ref = refs[pos + 1] if with_lse else None

    first_step = pl.program_id(2) == 0
    row = lax.broadcasted_iota(jnp.int32, (BLOCK, 2 * BLOCK), 0)
    col = lax.broadcasted_iota(jnp.int32, (BLOCK, 2 * BLOCK), 1)
    bias = jnp.where(col < BLOCK,
                     jnp.where(col >= row + prev_off, 0.0, NEG),
                     jnp.where(col - BLOCK <= row, 0.0, NEG)).astype(F32)
    bias_first = jnp.where(jnp.logical_and(first_step, col < BLOCK), NEG, bias)
    head_of_lane = lax.broadcasted_iota(jnp.int32, (1, A_SLAB), 1) // HEAD_DIM
    head_mask = [(head_of_lane == hh).astype(BF16) for hh in range(A_HEADS)]
    first_head_lanes = lax.broadcasted_iota(jnp.int32, (1, LANES), 1) < HEAD_DIM
    pack_row = lax.broadcasted_iota(jnp.int32, (BF16_ROWS, A_SLAB), 0)
    nblk = rows // BLOCK

    def drop_first_row(t):
        top = jnp.where(pack_row == 0, jnp.zeros((BF16_ROWS, A_SLAB), BF16), t[0:BF16_ROWS])
        return jnp.concatenate([top, t[BF16_ROWS:]], axis=0)

    for w in range(slabs):
        cs = slice(w * A_SLAB, (w + 1) * A_SLAB)
        biases = {False: [bias] * A_HEADS, True: [bias_first] * A_HEADS}
        if has_sink:
            biases = {first: [jnp.where(col == 0, sink_ref[w * A_HEADS + hh] * LOG2E, b[hh])
                              for hh in range(A_HEADS)] for first, b in biases.items()}
        for jb in range(nblk):
            q = q_ref[jb * BLOCK:(jb + 1) * BLOCK, cs]
            if jb == 0:
                k2 = jnp.concatenate([kp_ref[:, cs], kc_ref[0:BLOCK, cs]], axis=0)
                v2 = jnp.concatenate([vp_ref[:, cs], vc_ref[0:BLOCK, cs]], axis=0)
            else:
                k2 = kc_ref[(jb - 1) * BLOCK:(jb + 1) * BLOCK, cs]
                v2 = vc_ref[(jb - 1) * BLOCK:(jb + 1) * BLOCK, cs]
            if has_sink:
                k2 = drop_first_row(k2)
                v2 = drop_first_row(v2)
            qs = jnp.concatenate([q * head_mask[hh] for hh in range(A_HEADS)], axis=0)
            s = lax.dot_general(qs, k2, (((1,), (1,)), ((), ())), preferred_element_type=F32)
            ps, ms, ls = [], [], []
            for hh in range(A_HEADS):
                sh = s[hh * BLOCK:(hh + 1) * BLOCK] + biases[jb == 0][hh]
                m = jnp.max(sh, axis=-1, keepdims=True)
                p = jnp.exp2(sh - m)
                ms.append(m)
                ls.append(jnp.sum(p, axis=-1, keepdims=True))
                ps.append(p.astype(BF16))
            ost = jnp.dot(jnp.concatenate(ps, axis=0), v2, preferred_element_type=F32)
            o_tiles, lse_tiles = [], []
            for t0 in range(0, A_HEADS, HEADS_PER_TILE):
                lanes = slice(t0 * HEAD_DIM, (t0 + HEADS_PER_TILE) * HEAD_DIM)
                per_head = [ost[hh * BLOCK:(hh + 1) * BLOCK, lanes] * (1.0 / ls[hh])
                            for hh in range(t0, t0 + HEADS_PER_TILE)]
                o_tiles.append(jnp.where(first_head_lanes, per_head[0], per_head[1]))
                if with_lse:
                    lse_tiles.append(jnp.where(first_head_lanes, ms[t0] + jnp.log2(ls[t0]),
                                               ms[t0 + 1] + jnp.log2(ls[t0 + 1])))
            o_ref[jb * BLOCK:(jb + 1) * BLOCK, cs] = jnp.concatenate(o_tiles, axis=1).astype(o_ref.dtype)
            if with_lse:
                lse_ref[jb * BLOCK:(jb + 1) * BLOCK, cs] = jnp.concatenate(lse_tiles, axis=1)


def _band_attn(q, k, v, *, name, rows, slabs, prev_off, sinks=None, with_lse=True):
    b, length, width = q.shape
    assert sinks is None or prev_off >= 1
    n_col = width // (slabs * A_SLAB)
    n_row = length // rows
    blk_per_step = rows // BLOCK
    cur = pl.BlockSpec((None, rows, slabs * A_SLAB), lambda i, r, j: (i, j, r))
    prev = pl.BlockSpec((None, BLOCK, slabs * A_SLAB),
                        lambda i, r, j: (i, jnp.maximum(j * blk_per_step - 1, 0), r))
    in_specs = [cur, cur, prev, cur, prev]
    args = [q, k, k, v, v]
    if sinks is not None:
        in_specs.append(pl.BlockSpec(memory_space=pltpu.SMEM))
        args.append(sinks)
    out_specs = [cur]
    out_shape = [jax.ShapeDtypeStruct((b, length, width), BF16)]
    if with_lse:
        out_specs.append(cur)
        out_shape.append(jax.ShapeDtypeStruct((b, length, width), F32))
    return pl.pallas_call(
        functools.partial(_band_attn_kernel, rows=rows, slabs=slabs, prev_off=prev_off,
                          has_sink=sinks is not None, with_lse=with_lse),
        grid=(b, n_col, n_row),
        in_specs=in_specs,
        out_specs=out_specs,
        out_shape=out_shape,
        compiler_params=pltpu.CompilerParams(dimension_semantics=("arbitrary",) * 3,
                                             vmem_limit_bytes=VMEM_LIMIT),
        name=name,
    )(*args)


def _mix_kernel(x_ref, gain_ref, oa0, la0, oa1, la1, oa2, la2, ob_ref, mq_ref, mk_ref, mv_ref,
                wg_ref, bg_ref, woa_ref, wob_ref, wom_ref, wout_ref, out_ref, *stage_refs):
    x = x_ref[...]
    h = _rms_rows(x, gain_ref[...]).astype(BF16)
    stage_refs = list(stage_refs)

    def token_major(blk_ref, dil):
        if dil == 1:
            return blk_ref[...].astype(F32)
        stage_ref = stage_refs.pop()
        rows = blk_ref.shape[0]
        for r in range(dil):
            for s in range(A_SLAB // LANES):
                c0 = r * A_SLAB + s * LANES
                stage_ref[s, pl.ds(r, rows, stride=dil), :] = blk_ref[:, c0:c0 + LANES].astype(F32)
        return jnp.concatenate([stage_ref[s] for s in range(A_SLAB // LANES)], axis=1)

    head_cols = [slice(hh * M_HEAD_DIM, (hh + 1) * M_HEAD_DIM) for hh in range(M_HEADS)]
    scores = [lax.dot_general(mq_ref[:, cs], mk_ref[:, cs], (((1,), (1,)), ((), ())),
                              preferred_element_type=F32) for cs in head_cols]
    gate_cols = [slice(i * D_MODEL, (i + 1) * D_MODEL) for i in range(N_BRANCH)]
    gates = [1.0 + jnp.tanh(jnp.dot(h, wg_ref[:, cs], preferred_element_type=F32) + 0.5 * bg_ref[:, cs])
             for cs in gate_cols]

    dils = [dil for _, dil in A_GROUPS]
    l0, l1, l2 = (token_major(r, d) for r, d in zip((la0, la1, la2), dils))
    o0, o1, o2 = (token_major(r, d) for r, d in zip((oa0, oa1, oa2), dils))
    mx = jnp.maximum(jnp.maximum(l0, l1), l2)
    e0, e1, e2 = jnp.exp2(l0 - mx), jnp.exp2(l1 - mx), jnp.exp2(l2 - mx)
    o_a = (e0 * o0 + e1 * o1 + e2 * o2) / (e0 + e1 + e2)

    o_m = []
    for s, cs in zip(scores, head_cols):
        p = jnp.exp2(s - jnp.max(s, axis=-1, keepdims=True))
        l = jnp.sum(p, axis=-1, keepdims=True)
        o_m.append(jnp.dot(p.astype(BF16), mv_ref[:, cs], preferred_element_type=F32) / l)
    o_m = jnp.concatenate(o_m, axis=1)

    merged = jnp.zeros(x.shape, F32)
    for gate, o, w_ref in zip(gates, (o_a, ob_ref[...], o_m), (woa_ref, wob_ref, wom_ref)):
        merged = merged + gate * jnp.dot(o.astype(BF16), w_ref[...], preferred_element_type=F32)
    out_ref[...] = x + jnp.dot(merged.astype(BF16), wout_ref[...], preferred_element_type=F32)


def _mix(x2, attn_gain, oa, la, ob, mq, mk, mv, w_gate, b_gate, w_o_a, w_o_b, w_o_m, w_out, seq):
    t = x2.shape[0]
    tb = TOKEN_BLOCK
    mem_len = mk.shape[1]
    per_seq = seq // tb
    tok = lambda w, d=1: pl.BlockSpec((tb // d, w * d), lambda i: (i, 0))
    mem = pl.BlockSpec((None, mem_len, M_Q_COLS), lambda i: (i // per_seq, 0, 0))
    dils = [dil for _, dil in A_GROUPS]
    return pl.pallas_call(
        _mix_kernel,
        grid=(t // tb,),
        in_specs=[tok(D_MODEL), _const_spec((1, D_MODEL))]
                 + [tok(A_SLAB, d) for d in dils for _ in range(2)]
                 + [tok(B_Q_COLS), tok(M_Q_COLS), mem, mem,
                  _const_spec((D_MODEL, N_BRANCH * D_MODEL)), _const_spec((1, N_BRANCH * D_MODEL)),
                  _const_spec((A_SLAB, D_MODEL)), _const_spec((B_Q_COLS, D_MODEL)),
                  _const_spec((M_Q_COLS, D_MODEL)), _const_spec((D_MODEL, D_MODEL))],
        out_specs=tok(D_MODEL),
        out_shape=jax.ShapeDtypeStruct((t, D_MODEL), F32),
        scratch_shapes=[pltpu.VMEM((A_SLAB // LANES, tb, LANES), F32)] * (2 * sum(d > 1 for d in dils)),
        compiler_params=pltpu.CompilerParams(dimension_semantics=("arbitrary",),
                                             vmem_limit_bytes=VMEM_LIMIT),
        name="mix",
    )(x2, attn_gain, oa[0], la[0], oa[1], la[1], oa[2], la[2], ob, mq, mk, mv,
      w_gate, b_gate, w_o_a, w_o_b, w_o_m, w_out)


def _conv_ffn_kernel(x_ref, gain_ref, wup_ref, cw_ref, cb_ref, wdown_ref, out_ref,
                     carry_ref, ext_ref, act_ref, *, per_seq):
    tb = x_ref.shape[0]
    halo = CONV_WIDTH - 1
    pad = 8

    @pl.when(pl.program_id(0) % per_seq == 0)
    def _():
        carry_ref[...] = jnp.zeros(carry_ref.shape, F32)

    x = x_ref[...]
    h = _rms_rows(x, gain_ref[...]).astype(BF16)

    def conv(c0, scale=1.0):
        cs = slice(c0, c0 + FF_CHUNK)
        u = jnp.dot(h, wup_ref[:, cs], preferred_element_type=F32)
        ext_ref[0:pad, :] = carry_ref[:, cs]
        ext_ref[pad:pad + tb, :] = u
        carry_ref[:, cs] = u[tb - pad:tb, :]
        taps = cw_ref[:, cs] * scale
        c = cb_ref[:, cs] * scale + taps[CONV_WIDTH - 1:CONV_WIDTH] * u
        for j in range(halo):
            c = c + taps[j:j + 1] * ext_ref[pad - halo + j:pad - halo + j + tb, :]
        return c

    for ch in range(D_FF // FF_CHUNK):
        half = conv(ch * FF_CHUNK, 0.5)
        g = conv(D_FF + ch * FF_CHUNK)
        act = (half + half * jnp.tanh(half)) * g
        act_ref[:, ch * FF_CHUNK:(ch + 1) * FF_CHUNK] = act.astype(BF16)
    out_ref[...] = x + jnp.dot(act_ref[...], wdown_ref[...], preferred_element_type=F32)


def _conv_ffn(x2, gain, w_up, conv_w, conv_b, w_down, seq):
    t = x2.shape[0]
    tb = TOKEN_BLOCK
    return pl.pallas_call(
        functools.partial(_conv_ffn_kernel, per_seq=seq // tb),
        grid=(t // tb,),
        in_specs=[pl.BlockSpec((tb, D_MODEL), lambda i: (i, 0)),
                  _const_spec((1, D_MODEL)),
                  _const_spec((D_MODEL, 2 * D_FF)),
                  _const_spec((CONV_WIDTH, 2 * D_FF)),
                  _const_spec((1, 2 * D_FF)),
                  _const_spec((D_FF, D_MODEL))],
        out_specs=pl.BlockSpec((tb, D_MODEL), lambda i: (i, 0)),
        out_shape=jax.ShapeDtypeStruct((t, D_MODEL), F32),
        scratch_shapes=[pltpu.VMEM((8, 2 * D_FF), F32),
                        pltpu.VMEM((tb + 8, FF_CHUNK), F32),
                        pltpu.VMEM((tb, D_FF), BF16)],
        compiler_params=pltpu.CompilerParams(dimension_semantics=("arbitrary",),
                                             vmem_limit_bytes=VMEM_LIMIT),
        name="conv_ffn",
    )(x2, gain, w_up, conv_w, conv_b, w_down)


def _tables():
    lane = np.arange(A_SLAB)
    bd64 = jnp.asarray(lane[:, None] // HEAD_DIM == lane[None, :] // HEAD_DIM, BF16)
    bd128 = jnp.asarray(lane[:, None] // M_HEAD_DIM == lane[None, :] // M_HEAD_DIM, BF16)
    freq_col = jnp.exp(jnp.arange(ROPE_HALF, dtype=F32) * (-2.0 * math.log(ROPE_THETA) / ROPE_DIM))
    freq_col = freq_col.reshape(ROPE_HALF, 1)
    in_head = np.arange(LANES) % HEAD_DIM
    j = np.arange(LANES)[:, None]
    rot = in_head[None, :] < ROPE_DIM
    cos_sel = rot & (j == in_head[None, :] % ROPE_HALF)
    sin_sel = rot & (j == ROPE_HALF + in_head[None, :] % ROPE_HALF)
    lo = in_head[None, :] < ROPE_HALF
    spread = np.concatenate([cos_sel.astype(np.float32), -(sin_sel & lo).astype(np.float32),
                             (sin_sel & ~lo).astype(np.float32)], axis=1)
    one_row = (in_head >= ROPE_DIM).astype(np.float32).reshape(1, LANES)
    return bd64, bd128, freq_col, jnp.asarray(spread, BF16), jnp.asarray(one_row)


def _layer(x2, mem, pos_rows, batch, seq, p):
    bd64, bd128, freq_col, spread, one_row = _tables()
    n_groups = len(A_GROUPS)
    gains = jnp.concatenate([p["a_q_norm"], p["a_k_norm"], p["b_q_norm"][None], p["b_k_norm"][None],
                             p["m_q_norm"].reshape(M_HEAD_DIM // HEAD_DIM, HEAD_DIM),
                             jnp.ones((1, HEAD_DIM), F32)])
    row_bq, row_bk, row_mq = 2 * n_groups, 2 * n_groups + 1, 2 * n_groups + 2
    row_one = row_mq + M_HEAD_DIM // HEAD_DIM
    qk_scale = HEAD_DIM ** -0.5 * LOG2E
    src, scale = [], []
    for g in range(n_groups):
        src += [g] * A_HEADS + [n_groups + g] * A_HEADS + [row_one] * A_HEADS
        scale += [qk_scale] * A_HEADS + [1.0] * (2 * A_HEADS)
    src += [row_bq] * B_Q_HEADS + [row_bk] * B_KV_HEADS + [row_one] * B_KV_HEADS
    scale += [qk_scale] * B_Q_HEADS + [1.0] * (2 * B_KV_HEADS)
    src += [row_mq + i for i in range(M_HEAD_DIM // HEAD_DIM)] * M_HEADS
    scale += [M_HEAD_DIM ** -0.5 * LOG2E] * (M_Q_COLS // HEAD_DIM)
    colgain = (gains[np.asarray(src)] * np.asarray(scale, np.float32)[:, None]).reshape(1, IN_COLS)

    attn_gain = p["attn_norm"].reshape(1, D_MODEL)
    mk, mv, w_in = _mem_kv(mem, p["mem_norm"].reshape(1, D_MODEL), p["w_mem_kv"],
                           p["m_k_norm"].reshape(1, M_HEAD_DIM), p["w_in"])
    later_weights = ("w_gate", "w_o_a", "w_o_b", "w_o_m", "w_out", "w_up", "w_down")
    scales = [0.5 if name in ("w_gate", "w_out") else 1.0 for name in later_weights]
    outs = _in_proj(x2, pos_rows, attn_gain, w_in, colgain, freq_col, spread, one_row,
                    bd64, bd128, [p[name] for name in later_weights], scales)
    (qa0, ka0, va0, qa1, ka1, va1, qa2, ka2, va2, qb, kb, vb, mq) = outs[:13]
    wb = dict(zip(later_weights, outs[13:]))

    def per_batch(t):
        return t.reshape(batch, t.shape[0] // batch, t.shape[-1])

    def flat(t):
        return t.reshape(t.shape[0] * t.shape[1], t.shape[2])

    oa, la = [], []
    cfg = {1: dict(rows=2048, slabs=1), 4: dict(rows=seq // 4, slabs=2), 16: dict(rows=seq // 16, slabs=8)}
    for (window, dil), (q, k, v) in zip(A_GROUPS, ((qa0, ka0, va0), (qa1, ka1, va1), (qa2, ka2, va2))):
        assert window // dil == BLOCK
        o, lse = _band_attn(per_batch(q), per_batch(k), per_batch(v), name=f"band_attn_d{dil}",
                            prev_off=0, **cfg[dil])
        oa.append(flat(o))
        la.append(flat(lse))
    (ob,) = _band_attn(per_batch(qb), per_batch(kb), per_batch(vb), name="band_attn_swa", rows=1024,
                       slabs=B_KV_HEADS, prev_off=BLOCK - (B_WINDOW - 1), sinks=p["b_sinks"],
                       with_lse=False)
    ob = flat(ob)

    x2 = _mix(x2, attn_gain, oa, la, ob, mq, mk, mv, wb["w_gate"],
              p["b_gate"].reshape(1, N_BRANCH * D_MODEL), wb["w_o_a"], wb["w_o_b"], wb["w_o_m"],
              wb["w_out"], seq)
    return _conv_ffn(x2, p["ffn_norm"].reshape(1, D_MODEL), wb["w_up"], p["conv_w"],
                     p["conv_b"].reshape(1, 2 * D_FF), wb["w_down"], seq)


def kernel(x, mem, positions, attn_norm, w_in, a_q_norm, a_k_norm, b_q_norm, b_k_norm, b_sinks,
           mem_norm, w_mem_kv, m_q_norm, m_k_norm, w_o_a, w_o_b, w_o_m, w_gate, b_gate, w_out,
           ffn_norm, w_up, conv_w, conv_b, w_down):
    batch, seq, _ = x.shape
    params = dict(attn_norm=attn_norm, w_in=w_in, a_q_norm=a_q_norm, a_k_norm=a_k_norm,
                  b_q_norm=b_q_norm, b_k_norm=b_k_norm, b_sinks=b_sinks, mem_norm=mem_norm,
                  w_mem_kv=w_mem_kv, m_q_norm=m_q_norm, m_k_norm=m_k_norm, w_o_a=w_o_a, w_o_b=w_o_b,
                  w_o_m=w_o_m, w_gate=w_gate, b_gate=b_gate, w_out=w_out, ffn_norm=ffn_norm,
                  w_up=w_up, conv_w=conv_w, conv_b=conv_b, w_down=w_down)
    pos_rows = positions.astype(F32).reshape(batch * seq // TOKEN_BLOCK, 1, TOKEN_BLOCK)
    x2 = x.reshape(batch * seq, D_MODEL)
    for layer in range(attn_norm.shape[0]):
        x2 = _layer(x2, mem, pos_rows, batch, seq, {k: v[layer] for k, v in params.items()})
    return x2.reshape(batch, seq, D_MODEL)
```

```python
import functools
import math

import numpy as np

import jax
import jax.numpy as jnp
from jax import lax
from jax.experimental import pallas as pl
from jax.experimental.pallas import tpu as pltpu

D_MODEL = 1024
HEAD_DIM = 64
A_GROUPS = ((128, 1), (512, 4), (2048, 16))
A_HEADS = 4
A_SLAB = A_HEADS * HEAD_DIM
A_QKV_COLS = len(A_GROUPS) * 3 * A_SLAB
B_Q_HEADS = 8
B_KV_HEADS = 2
B_GROUP = B_Q_HEADS // B_KV_HEADS
B_WINDOW = 128
B_Q_COLS = B_Q_HEADS * HEAD_DIM
B_KV_COLS = B_KV_HEADS * HEAD_DIM
M_HEADS = 4
M_HEAD_DIM = 128
M_Q_COLS = M_HEADS * M_HEAD_DIM
B_Q_OFF = A_QKV_COLS
B_K_OFF = B_Q_OFF + B_Q_COLS
B_V_OFF = B_K_OFF + B_KV_COLS
M_Q_OFF = B_V_OFF + B_KV_COLS
IN_COLS = M_Q_OFF + M_Q_COLS
N_NORMED_TILES = 2 * len(A_GROUPS) + B_Q_COLS // A_SLAB + 1 + M_Q_COLS // A_SLAB
N_BRANCH = 3
D_FF = 2816
CONV_WIDTH = 3
ROPE_THETA = 500000.0
ROPE_DIM = HEAD_DIM // 4
ROPE_HALF = ROPE_DIM // 2
BLOCK = 128
EPS = 1e-6
NEG = -1e30
LOG2E = math.log2(math.e)

LANES = 128
BF16_ROWS = 16
HEADS_PER_TILE = LANES // HEAD_DIM
TOKEN_BLOCK = 512
ROW_CHUNK = 64
FF_CHUNK = 256
FF_PARTS = 2
VMEM_LIMIT = 56 * 1024 * 1024

F32 = jnp.float32
BF16 = jnp.bfloat16


def _const_spec(shape):
    return pl.BlockSpec(shape, lambda *_: (0,) * len(shape), pipeline_mode=pl.Buffered(1))


def _rms_rows(x, gain):
    ms = jnp.mean(x * x, axis=-1, keepdims=True)
    return x * lax.rsqrt(ms + EPS) * gain


def _mem_kv_kernel(mem_ref, gain_ref, w_ref, kgain_ref, win_ref, mk_ref, mv_ref, win_out_ref):
    win_out_ref[...] = win_ref[...].astype(BF16)
    hm = _rms_rows(mem_ref[...], gain_ref[...]).astype(BF16)
    kv = jnp.dot(hm, w_ref[...].astype(BF16), preferred_element_type=F32)
    ks = []
    for h in range(M_HEADS):
        kh = kv[:, h * M_HEAD_DIM:(h + 1) * M_HEAD_DIM]
        ks.append(_rms_rows(kh, kgain_ref[...]))
    mk_ref[...] = jnp.concatenate(ks, axis=1).astype(BF16)
    mv_ref[...] = kv[:, M_Q_COLS:].astype(BF16)


def _mem_kv(mem, mem_gain, w_kv, k_gain, w_in):
    b, m, _ = mem.shape
    slab = pl.BlockSpec((_cast_rows(w_in.shape[0], b), w_in.shape[1]), lambda i: (i, 0))
    assert w_in.shape[0] == b * slab.block_shape[0]
    return pl.pallas_call(
        _mem_kv_kernel,
        grid=(b,),
        in_specs=[pl.BlockSpec((None, m, D_MODEL), lambda i: (i, 0, 0)),
                  _const_spec((1, D_MODEL)),
                  _const_spec((D_MODEL, 2 * M_Q_COLS)),
                  _const_spec((1, M_HEAD_DIM)),
                  slab],
        out_specs=[pl.BlockSpec((None, m, M_Q_COLS), lambda i: (i, 0, 0)),
                   pl.BlockSpec((None, m, M_Q_COLS), lambda i: (i, 0, 0)),
                   slab],
        out_shape=[jax.ShapeDtypeStruct((b, m, M_Q_COLS), BF16)] * 2
                  + [jax.ShapeDtypeStruct(w_in.shape, BF16)],
        compiler_params=pltpu.CompilerParams(dimension_semantics=("arbitrary",),
                                             vmem_limit_bytes=VMEM_LIMIT),
        name="mem_kv",
    )(mem, mem_gain, w_kv, k_gain, w_in)


def _in_proj_kernel(x_ref, pos_ref, gain_ref, w_ref, colgain_ref, freq_ref, spread_ref, one_ref,
                    bd64_ref, bd128_ref, *refs, cast_scales):
    n_cast = len(cast_scales)
    cast_in, refs = refs[:n_cast], refs[n_cast:]
    (qa0, ka0, va0, qa1, ka1, va1, qa2, ka2, va2, qb_ref, kb_ref, vb_ref, mq_ref) = refs[:13]
    cast_out, refs = refs[13:13 + n_cast], refs[13 + n_cast:]
    proj_ref, ss_ref, h_ref, cos_ref, sin_lo_ref, sin_hi_ref = refs[:6]
    stage_refs = list(refs[6:])
    tb = x_ref.shape[0]

    for src, dst, scale in zip(cast_in, cast_out, cast_scales):
        w = src[...]
        dst[...] = (w if scale == 1.0 else w * scale).astype(BF16)
    chunks = [slice(r, r + ROW_CHUNK) for r in range(0, tb, ROW_CHUNK)]

    @pl.when(pl.program_id(0) == 0)
    def _():
        proj_ref[...] = jnp.zeros(proj_ref.shape, F32)
        ss_ref[...] = jnp.zeros(ss_ref.shape, F32)

    for rows in chunks:
        h_ref[rows, :] = _rms_rows(x_ref[rows, :], gain_ref[...]).astype(BF16)

    ang = freq_ref[...] * pos_ref[...]
    trig = jnp.concatenate([jnp.cos(ang), jnp.sin(ang),
                            jnp.zeros((LANES - ROPE_DIM, tb), F32)], axis=0).T
    tables = jnp.zeros((tb, 3 * LANES), F32)
    for _ in range(3):
        part = trig.astype(BF16)
        tables = tables + jnp.dot(part, spread_ref[...], preferred_element_type=F32)
        trig = trig - part.astype(F32)
    cos_ref[...] = tables[:, 0:LANES] + one_ref[...]
    sin_lo_ref[...] = tables[:, LANES:2 * LANES]
    sin_hi_ref[...] = tables[:, 2 * LANES:3 * LANES]

    ss_slots = iter(range(ss_ref.shape[0]))
    pending = []

    def flush_pending():
        while pending:
            new, bd, slot = pending.pop()
            ss_ref[slot, :, 0:new.shape[1]] = jnp.dot((new * new).astype(BF16), bd,
                                                      preferred_element_type=F32)

    def proj(c0, width, bd=None):
        tile, off = divmod(c0, A_SLAB)
        assert off + width <= A_SLAB and off % LANES == 0
        t = proj_ref[tile, :, off:off + width]
        ss = None
        if bd is not None:
            slot = next(ss_slots)
            ss = ss_ref[slot, :, 0:width]
        new = jnp.dot(h_ref[...], w_ref[:, c0:c0 + width], preferred_element_type=F32)
        proj_ref[tile, :, off:off + width] = new
        flush_pending()
        if bd is not None:
            pending.append((new, bd, slot))
        return t, ss

    def rope(y, rows):
        parts = []
        for c in range(y.shape[1] // LANES):
            yc = y[:, c * LANES:(c + 1) * LANES]
            parts.append(yc * cos_ref[rows, :]
                         + pltpu.roll(yc, LANES - ROPE_HALF, 1) * sin_lo_ref[rows, :]
                         + pltpu.roll(yc, ROPE_HALF, 1) * sin_hi_ref[rows, :])
        return parts[0] if len(parts) == 1 else jnp.concatenate(parts, axis=1)

    def finish(t_ss, emit, c0, *, dim=None, rotary=False):
        t, ss = t_ss
        if ss is not None:
            gain = colgain_ref[:, c0:c0 + t.shape[1]]
        for rows in chunks:
            y = t[rows]
            if ss is not None:
                y = y * lax.rsqrt(ss[rows] * (1.0 / dim) + EPS) * gain
            if rotary:
                y = rope(y, rows)
            emit(rows, y)

    def to_ref(out_ref, col0=0):
        def emit(rows, y):
            out_ref[rows, col0:col0 + y.shape[1]] = y.astype(BF16)
        return emit

    def store_tile(out_ref, t, dil, c0, **kw):
        if dil == 1:
            finish(t, to_ref(out_ref), c0, **kw)
            return
        stage_ref = stage_refs.pop()

        def emit(rows, y):
            for s in range(A_SLAB // LANES):
                stage_ref[s, rows, :] = y[:, s * LANES:(s + 1) * LANES]
        finish(t, emit, c0, **kw)
        for r in range(dil):
            for s in range(A_SLAB // LANES):
                col = r * A_SLAB + s * LANES
                out_ref[:, col:col + LANES] = stage_ref[s, pl.ds(r, tb // dil, stride=dil), :].astype(BF16)

    bd64 = bd64_ref[...]
    qk = dict(dim=HEAD_DIM, rotary=True)
    for g, (q_ref, k_ref, v_ref) in enumerate(((qa0, ka0, va0), (qa1, ka1, va1), (qa2, ka2, va2))):
        c0 = g * 3 * A_SLAB
        dil = A_GROUPS[g][1]
        store_tile(q_ref, proj(c0, A_SLAB, bd64), dil, c0, **qk)
        store_tile(k_ref, proj(c0 + A_SLAB, A_SLAB, bd64), dil, c0 + A_SLAB, **qk)
        store_tile(v_ref, proj(c0 + 2 * A_SLAB, A_SLAB), dil, c0 + 2 * A_SLAB)

    for s in range(B_Q_COLS // A_SLAB):
        c0 = B_Q_OFF + s * A_SLAB
        finish(proj(c0, A_SLAB, bd64), to_ref(qb_ref, s * A_SLAB), c0, **qk)

    def under_query_heads(out_ref):
        def emit(rows, y):
            swapped = pltpu.roll(y, HEAD_DIM, 1)
            first = jnp.where(first_head_lanes, y, swapped).astype(BF16)
            second = jnp.where(first_head_lanes, swapped, y).astype(BF16)
            reps = B_GROUP // HEADS_PER_TILE
            out_ref[rows, :] = jnp.concatenate([first] * reps + [second] * reps, axis=1)
        return emit
    first_head_lanes = lax.broadcasted_iota(jnp.int32, (1, LANES), 1) < HEAD_DIM
    finish(proj(B_K_OFF, B_KV_COLS, bd64_ref[0:B_KV_COLS, 0:B_KV_COLS]), under_query_heads(kb_ref),
           B_K_OFF, **qk)
    finish(proj(B_V_OFF, B_KV_COLS), under_query_heads(vb_ref), B_V_OFF)

    bd128 = bd128_ref[...]
    for s in range(M_Q_COLS // A_SLAB):
        c0 = M_Q_OFF + s * A_SLAB
        finish(proj(c0, A_SLAB, bd128), to_ref(mq_ref, s * A_SLAB), c0, dim=M_HEAD_DIM)
    flush_pending()


def _cast_rows(rows, n_steps):
    for per_step in range(BF16_ROWS, rows + 1, BF16_ROWS):
        if rows % per_step == 0 and rows // per_step <= n_steps:
            return per_step
    raise ValueError(f"no bf16-aligned split of {rows} rows over {n_steps} steps")


def _in_proj(x2, pos_rows, attn_gain, w_in, colgain, freq_col, spread, one_row, bd64, bd128,
             cast_weights, cast_scales):
    t = x2.shape[0]
    tb = TOKEN_BLOCK
    dils = [dil for _, dil in A_GROUPS for _ in range(3)] + [1, 1, 1, 1]
    widths = [A_SLAB] * 9 + [B_Q_COLS, B_Q_COLS, B_Q_COLS, M_Q_COLS]
    n_staged = sum(d > 1 for d in dils)
    n = t // tb
    done = lambda i: (jnp.maximum(i - 1, 0), 0)

    def cast_spec(w):
        per_step = _cast_rows(w.shape[0], n + 1)
        last = w.shape[0] // per_step - 1
        return pl.BlockSpec((per_step, w.shape[1]), lambda i: (jnp.minimum(i, last), 0))

    cast_specs = [cast_spec(w) for w in cast_weights]
    return pl.pallas_call(
        functools.partial(_in_proj_kernel, cast_scales=tuple(cast_scales)),
        grid=(n + 1,),
        in_specs=[pl.BlockSpec((tb, D_MODEL), lambda i: (jnp.minimum(i, n - 1), 0)),
                  pl.BlockSpec((None, 1, tb), lambda i: (jnp.maximum(i - 1, 0), 0, 0)),
                  _const_spec((1, D_MODEL)),
                  _const_spec((D_MODEL, IN_COLS)),
                  _const_spec((1, IN_COLS)),
                  _const_spec((ROPE_HALF, 1)),
                  _const_spec((LANES, 3 * LANES)),
                  _const_spec((1, LANES)),
                  _const_spec((A_SLAB, A_SLAB)),
                  _const_spec((A_SLAB, A_SLAB))] + cast_specs,
        out_specs=[pl.BlockSpec((tb // d, w * d), done) for w, d in zip(widths, dils)] + cast_specs,
        out_shape=[jax.ShapeDtypeStruct((t // d, w * d), BF16) for w, d in zip(widths, dils)]
                  + [jax.ShapeDtypeStruct(w.shape, BF16) for w in cast_weights],
        scratch_shapes=[pltpu.VMEM((IN_COLS // A_SLAB, tb, A_SLAB), F32),
                        pltpu.VMEM((N_NORMED_TILES, tb, A_SLAB), F32)]
                       + [pltpu.VMEM((tb, D_MODEL), BF16)] + [pltpu.VMEM((tb, LANES), F32)] * 3
                       + [pltpu.VMEM((A_SLAB // LANES, tb, LANES), F32)] * n_staged,
        compiler_params=pltpu.CompilerParams(dimension_semantics=("arbitrary",),
                                             vmem_limit_bytes=VMEM_LIMIT),
        name="in_proj",
    )(x2, pos_rows, attn_gain, w_in, colgain, freq_col, spread, one_row, bd64, bd128, *cast_weights)


def _band_attn_kernel(*refs, rows, slabs, prev_off, has_sink, with_lse):
    q_ref, kc_ref, kp_ref, vc_ref, vp_ref = refs[:5]
    pos = 5
    sink_ref = None
    if has_sink:
        sink_ref = refs[pos]
        pos += 1
    o_ref = refs[pos]
    lse_ref = refs[pos + 1] if with_lse else None

    first_step = pl.program_id(2) == 0
    row = lax.broadcasted_iota(jnp.int32, (BLOCK, 2 * BLOCK), 0)
    col = lax.broadcasted_iota(jnp.int32, (BLOCK, 2 * BLOCK), 1)
    bias = jnp.where(col < BLOCK,
                     jnp.where(col >= row + prev_off, 0.0, NEG),
                     jnp.where(col - BLOCK <= row, 0.0, NEG)).astype(F32)
    bias_first = jnp.where(jnp.logical_and(first_step, col < BLOCK), NEG, bias)
    head_of_lane = lax.broadcasted_iota(jnp.int32, (1, A_SLAB), 1) // HEAD_DIM
    head_mask = [(head_of_lane == hh).astype(BF16) for hh in range(A_HEADS)]
    first_head_lanes = lax.broadcasted_iota(jnp.int32, (1, LANES), 1) < HEAD_DIM
    pack_row = lax.broadcasted_iota(jnp.int32, (BF16_ROWS, A_SLAB), 0)
    nblk = rows // BLOCK

    def drop_first_row(t):
        top = jnp.where(pack_row == 0, jnp.zeros((BF16_ROWS, A_SLAB), BF16), t[0:BF16_ROWS])
        return jnp.concatenate([top, t[BF16_ROWS:]], axis=0)

    for w in range(slabs):
        cs = slice(w * A_SLAB, (w + 1) * A_SLAB)
        biases = {False: [bias] * A_HEADS, True: [bias_first] * A_HEADS}
        if has_sink:
            biases = {first: [jnp.where(col == 0, sink_ref[w * A_HEADS + hh] * LOG2E, b[hh])
                              for hh in range(A_HEADS)] for first, b in biases.items()}
        for jb in range(nblk):
            q = q_ref[jb * BLOCK:(jb + 1) * BLOCK, cs]
            if jb == 0:
                k2 = jnp.concatenate([kp_ref[:, cs], kc_ref[0:BLOCK, cs]], axis=0)
                v2 = jnp.concatenate([vp_ref[:, cs], vc_ref[0:BLOCK, cs]], axis=0)
            else:
                k2 = kc_ref[(jb - 1) * BLOCK:(jb + 1) * BLOCK, cs]
                v2 = vc_ref[(jb - 1) * BLOCK:(jb + 1) * BLOCK, cs]
            if has_sink:
                k2 = drop_first_row(k2)
                v2 = drop_first_row(v2)
            qs = jnp.concatenate([q * head_mask[hh] for hh in range(A_HEADS)], axis=0)
            s = lax.dot_general(qs, k2, (((1,), (1,)), ((), ())), preferred_element_type=F32)
            ps, ms, ls = [], [], []
            for hh in range(A_HEADS):
                sh = s[hh * BLOCK:(hh + 1) * BLOCK] + biases[jb == 0][hh]
                m = jnp.max(sh, axis=-1, keepdims=True)
                p = jnp.exp2(sh - m)
                ms.append(m)
                ls.append(jnp.sum(p, axis=-1, keepdims=True))
                ps.append(p.astype(BF16))
            ost = jnp.dot(jnp.concatenate(ps, axis=0), v2, preferred_element_type=F32)
            o_tiles, lse_tiles = [], []
            for t0 in range(0, A_HEADS, HEADS_PER_TILE):
                lanes = slice(t0 * HEAD_DIM, (t0 + HEADS_PER_TILE) * HEAD_DIM)
                per_head = [ost[hh * BLOCK:(hh + 1) * BLOCK, lanes] * (1.0 / ls[hh])
                            for hh in range(t0, t0 + HEADS_PER_TILE)]
                o_tiles.append(jnp.where(first_head_lanes, per_head[0], per_head[1]))
                if with_lse:
                    lse_tiles.append(jnp.where(first_head_lanes, ms[t0] + jnp.log2(ls[t0]),
                                               ms[t0 + 1] + jnp.log2(ls[t0 + 1])))
            o_ref[jb * BLOCK:(jb + 1) * BLOCK, cs] = jnp.concatenate(o_tiles, axis=1).astype(o_ref.dtype)
            if with_lse:
                lse_ref[jb * BLOCK:(jb + 1) * BLOCK, cs] = jnp.concatenate(lse_tiles, axis=1)


def _band_attn(q, k, v, *, name, rows, slabs, prev_off, sinks=None, with_lse=True):
    b, length, width = q.shape
    assert sinks is None or prev_off >= 1
    n_col = width // (slabs * A_SLAB)
    n_row = length // rows
    blk_per_step = rows // BLOCK
    cur = pl.BlockSpec((None, rows, slabs * A_SLAB), lambda i, r, j: (i, j, r))
    prev = pl.BlockSpec((None, BLOCK, slabs * A_SLAB),
                        lambda i, r, j: (i, jnp.maximum(j * blk_per_step - 1, 0), r))
    in_specs = [cur, cur, prev, cur, prev]
    args = [q, k, k, v, v]
    if sinks is not None:
        in_specs.append(pl.BlockSpec(memory_space=pltpu.SMEM))
        args.append(sinks)
    out_specs = [cur]
    out_shape = [jax.ShapeDtypeStruct((b, length, width), BF16)]
    if with_lse:
        out_specs.append(cur)
        out_shape.append(jax.ShapeDtypeStruct((b, length, width), F32))
    return pl.pallas_call(
        functools.partial(_band_attn_kernel, rows=rows, slabs=slabs, prev_off=prev_off,
                          has_sink=sinks is not None, with_lse=with_lse),
        grid=(b, n_col, n_row),
        in_specs=in_specs,
        out_specs=out_specs,
        out_shape=out_shape,
        compiler_params=pltpu.CompilerParams(dimension_semantics=("arbitrary",) * 3,
                                             vmem_limit_bytes=VMEM_LIMIT),
        name=name,
    )(*args)


def _mix_kernel(x_ref, gain_ref, oa0, la0, oa1, la1, oa2, la2, ob_ref, mq_ref, mk_ref, mv_ref,
                wg_ref, bg_ref, woa_ref, wob_ref, wom_ref, wout_ref, out_ref, *stage_refs):
    x = x_ref[...]
    h = _rms_rows(x, gain_ref[...]).astype(BF16)
    stage_refs = list(stage_refs)

    def token_major(blk_ref, dil):
        if dil == 1:
            return blk_ref[...].astype(F32)
        stage_ref = stage_refs.pop()
        rows = blk_ref.shape[0]
        for r in range(dil):
            for s in range(A_SLAB // LANES):
                c0 = r * A_SLAB + s * LANES
                stage_ref[s, pl.ds(r, rows, stride=dil), :] = blk_ref[:, c0:c0 + LANES].astype(F32)
        return jnp.concatenate([stage_ref[s] for s in range(A_SLAB // LANES)], axis=1)

    head_cols = [slice(hh * M_HEAD_DIM, (hh + 1) * M_HEAD_DIM) for hh in range(M_HEADS)]
    scores = [lax.dot_general(mq_ref[:, cs], mk_ref[:, cs], (((1,), (1,)), ((), ())),
                              preferred_element_type=F32) for cs in head_cols]
    gate_cols = [slice(i * D_MODEL, (i + 1) * D_MODEL) for i in range(N_BRANCH)]
    gates = [1.0 + jnp.tanh(jnp.dot(h, wg_ref[:, cs], preferred_element_type=F32) + 0.5 * bg_ref[:, cs])
             for cs in gate_cols]

    dils = [dil for _, dil in A_GROUPS]
    l0, l1, l2 = (token_major(r, d) for r, d in zip((la0, la1, la2), dils))
    o0, o1, o2 = (token_major(r, d) for r, d in zip((oa0, oa1, oa2), dils))
    mx = jnp.maximum(jnp.maximum(l0, l1), l2)
    e0, e1, e2 = jnp.exp2(l0 - mx), jnp.exp2(l1 - mx), jnp.exp2(l2 - mx)
    o_a = (e0 * o0 + e1 * o1 + e2 * o2) / (e0 + e1 + e2)

    o_m = []
    for s, cs in zip(scores, head_cols):
        p = jnp.exp2(s - jnp.max(s, axis=-1, keepdims=True))
        l = jnp.sum(p, axis=-1, keepdims=True)
        o_m.append(jnp.dot(p.astype(BF16), mv_ref[:, cs], preferred_element_type=F32) / l)
    o_m = jnp.concatenate(o_m, axis=1)

    merged = jnp.zeros(x.shape, F32)
    for gate, o, w_ref in zip(gates, (o_a, ob_ref[...], o_m), (woa_ref, wob_ref, wom_ref)):
        merged = merged + gate * jnp.dot(o.astype(BF16), w_ref[...], preferred_element_type=F32)
    out_ref[...] = x + jnp.dot(merged.astype(BF16), wout_ref[...], preferred_element_type=F32)


def _mix(x2, attn_gain, oa, la, ob, mq, mk, mv, w_gate, b_gate, w_o_a, w_o_b, w_o_m, w_out, seq):
    t = x2.shape[0]
    tb = TOKEN_BLOCK
    mem_len = mk.shape[1]
    per_seq = seq // tb
    tok = lambda w, d=1: pl.BlockSpec((tb // d, w * d), lambda i: (i, 0))
    mem = pl.BlockSpec((None, mem_len, M_Q_COLS), lambda i: (i // per_seq, 0, 0))
    dils = [dil for _, dil in A_GROUPS]
    return pl.pallas_call(
        _mix_kernel,
        grid=(t // tb,),
        in_specs=[tok(D_MODEL), _const_spec((1, D_MODEL))]
                 + [tok(A_SLAB, d) for d in dils for _ in range(2)]
                 + [tok(B_Q_COLS), tok(M_Q_COLS), mem, mem,
                  _const_spec((D_MODEL, N_BRANCH * D_MODEL)), _const_spec((1, N_BRANCH * D_MODEL)),
                  _const_spec((A_SLAB, D_MODEL)), _const_spec((B_Q_COLS, D_MODEL)),
                  _const_spec((M_Q_COLS, D_MODEL)), _const_spec((D_MODEL, D_MODEL))],
        out_specs=tok(D_MODEL),
        out_shape=jax.ShapeDtypeStruct((t, D_MODEL), F32),
        scratch_shapes=[pltpu.VMEM((A_SLAB // LANES, tb, LANES), F32)] * (2 * sum(d > 1 for d in dils)),
        compiler_params=pltpu.CompilerParams(dimension_semantics=("arbitrary",),
                                             vmem_limit_bytes=VMEM_LIMIT),
        name="mix",
    )(x2, attn_gain, oa[0], la[0], oa[1], la[1], oa[2], la[2], ob, mq, mk, mv,
      w_gate, b_gate, w_o_a, w_o_b, w_o_m, w_out)


def _conv_ffn_kernel(x_ref, gain_ref, wup_ref, cw_ref, cb_ref, wdown_ref, out_ref,
                     carry_ref, ext_ref, act_ref, *, per_seq):
    tb = x_ref.shape[0]
    halo = CONV_WIDTH - 1
    pad = 8

    @pl.when(pl.program_id(0) % per_seq == 0)
    def _():
        carry_ref[...] = jnp.zeros(carry_ref.shape, F32)

    part = tb // FF_PARTS
    for r0 in range(0, tb, part):
        rows = slice(r0, r0 + part)
        h = _rms_rows(x_ref[rows, :], gain_ref[...]).astype(BF16)

        def conv(c0, scale=1.0):
            cs = slice(c0, c0 + FF_CHUNK)
            u = jnp.dot(h, wup_ref[:, cs], preferred_element_type=F32)
            ext_ref[0:pad, :] = carry_ref[:, cs]
            ext_ref[pad:pad + part, :] = u
            carry_ref[:, cs] = u[part - pad:part, :]
            taps = cw_ref[:, cs] * scale
            c = cb_ref[:, cs] * scale + taps[CONV_WIDTH - 1:CONV_WIDTH] * u
            for j in range(halo):
                c = c + taps[j:j + 1] * ext_ref[pad - halo + j:pad - halo + j + part, :]
            return c

        for ch in range(D_FF // FF_CHUNK):
            half = conv(ch * FF_CHUNK, 0.5)
            g = conv(D_FF + ch * FF_CHUNK)
            act = (half + half * jnp.tanh(half)) * g
            act_ref[rows, ch * FF_CHUNK:(ch + 1) * FF_CHUNK] = act.astype(BF16)
    for r0 in range(0, tb, part):
        rows = slice(r0, r0 + part)
        out_ref[rows, :] = x_ref[rows, :] + jnp.dot(act_ref[rows, :], wdown_ref[...],
                                                    preferred_element_type=F32)


def _conv_ffn(x2, gain, w_up, conv_w, conv_b, w_down, seq):
    t = x2.shape[0]
    tb = TOKEN_BLOCK
    return pl.pallas_call(
        functools.partial(_conv_ffn_kernel, per_seq=seq // tb),
        grid=(t // tb,),
        in_specs=[pl.BlockSpec((tb, D_MODEL), lambda i: (i, 0)),
                  _const_spec((1, D_MODEL)),
                  _const_spec((D_MODEL, 2 * D_FF)),
                  _const_spec((CONV_WIDTH, 2 * D_FF)),
                  _const_spec((1, 2 * D_FF)),
                  _const_spec((D_FF, D_MODEL))],
        out_specs=pl.BlockSpec((tb, D_MODEL), lambda i: (i, 0)),
        out_shape=jax.ShapeDtypeStruct((t, D_MODEL), F32),
        scratch_shapes=[pltpu.VMEM((8, 2 * D_FF), F32),
                        pltpu.VMEM((tb // FF_PARTS + 8, FF_CHUNK), F32),
                        pltpu.VMEM((tb, D_FF), BF16)],
        compiler_params=pltpu.CompilerParams(dimension_semantics=("arbitrary",),
                                             vmem_limit_bytes=VMEM_LIMIT),
        name="conv_ffn",
    )(x2, gain, w_up, conv_w, conv_b, w_down)


def _tables():
    lane = np.arange(A_SLAB)
    bd64 = jnp.asarray(lane[:, None] // HEAD_DIM == lane[None, :] // HEAD_DIM, BF16)
    bd128 = jnp.asarray(lane[:, None] // M_HEAD_DIM == lane[None, :] // M_HEAD_DIM, BF16)
    freq_col = jnp.exp(jnp.arange(ROPE_HALF, dtype=F32) * (-2.0 * math.log(ROPE_THETA) / ROPE_DIM))
    freq_col = freq_col.reshape(ROPE_HALF, 1)
    in_head = np.arange(LANES) % HEAD_DIM
    j = np.arange(LANES)[:, None]
    rot = in_head[None, :] < ROPE_DIM
    cos_sel = rot & (j == in_head[None, :] % ROPE_HALF)
    sin_sel = rot & (j == ROPE_HALF + in_head[None, :] % ROPE_HALF)
    lo = in_head[None, :] < ROPE_HALF
    spread = np.concatenate([cos_sel.astype(np.float32), -(sin_sel & lo).astype(np.float32),
                             (sin_sel & ~lo).astype(np.float32)], axis=1)
    one_row = (in_head >= ROPE_DIM).astype(np.float32).reshape(1, LANES)
    return bd64, bd128, freq_col, jnp.asarray(spread, BF16), jnp.asarray(one_row)


def _layer(x2, mem, pos_rows, batch, seq, p):
    bd64, bd128, freq_col, spread, one_row = _tables()
    n_groups = len(A_GROUPS)
    gains = jnp.concatenate([p["a_q_norm"], p["a_k_norm"], p["b_q_norm"][None], p["b_k_norm"][None],
                             p["m_q_norm"].reshape(M_HEAD_DIM // HEAD_DIM, HEAD_DIM),
                             jnp.ones((1, HEAD_DIM), F32)])
    row_bq, row_bk, row_mq = 2 * n_groups, 2 * n_groups + 1, 2 * n_groups + 2
    row_one = row_mq + M_HEAD_DIM // HEAD_DIM
    qk_scale = HEAD_DIM ** -0.5 * LOG2E
    src, scale = [], []
    for g in range(n_groups):
        src += [g] * A_HEADS + [n_groups + g] * A_HEADS + [row_one] * A_HEADS
        scale += [qk_scale] * A_HEADS + [1.0] * (2 * A_HEADS)
    src += [row_bq] * B_Q_HEADS + [row_bk] * B_KV_HEADS + [row_one] * B_KV_HEADS
    scale += [qk_scale] * B_Q_HEADS + [1.0] * (2 * B_KV_HEADS)
    src += [row_mq + i for i in range(M_HEAD_DIM // HEAD_DIM)] * M_HEADS
    scale += [M_HEAD_DIM ** -0.5 * LOG2E] * (M_Q_COLS // HEAD_DIM)
    colgain = (gains[np.asarray(src)] * np.asarray(scale, np.float32)[:, None]).reshape(1, IN_COLS)

    attn_gain = p["attn_norm"].reshape(1, D_MODEL)
    mk, mv, w_in = _mem_kv(mem, p["mem_norm"].reshape(1, D_MODEL), p["w_mem_kv"],
                           p["m_k_norm"].reshape(1, M_HEAD_DIM), p["w_in"])
    later_weights = ("w_gate", "w_o_a", "w_o_b", "w_o_m", "w_out", "w_up", "w_down")
    scales = [0.5 if name in ("w_gate", "w_out") else 1.0 for name in later_weights]
    outs = _in_proj(x2, pos_rows, attn_gain, w_in, colgain, freq_col, spread, one_row,
                    bd64, bd128, [p[name] for name in later_weights], scales)
    (qa0, ka0, va0, qa1, ka1, va1, qa2, ka2, va2, qb, kb, vb, mq) = outs[:13]
    wb = dict(zip(later_weights, outs[13:]))

    def per_batch(t):
        return t.reshape(batch, t.shape[0] // batch, t.shape[-1])

    def flat(t):
        return t.reshape(t.shape[0] * t.shape[1], t.shape[2])

    oa, la = [], []
    cfg = {1: dict(rows=2048, slabs=1), 4: dict(rows=seq // 4, slabs=2), 16: dict(rows=seq // 16, slabs=8)}
    for (window, dil), (q, k, v) in zip(A_GROUPS, ((qa0, ka0, va0), (qa1, ka1, va1), (qa2, ka2, va2))):
        assert window // dil == BLOCK
        o, lse = _band_attn(per_batch(q), per_batch(k), per_batch(v), name=f"band_attn_d{dil}",
                            prev_off=0, **cfg[dil])
        oa.append(flat(o))
        la.append(flat(lse))
    (ob,) = _band_attn(per_batch(qb), per_batch(kb), per_batch(vb), name="band_attn_swa", rows=1024,
                       slabs=B_KV_HEADS, prev_off=BLOCK - (B_WINDOW - 1), sinks=p["b_sinks"],
                       with_lse=False)
    ob = flat(ob)

    x2 = _mix(x2, attn_gain, oa, la, ob, mq, mk, mv, wb["w_gate"],
              p["b_gate"].reshape(1, N_BRANCH * D_MODEL), wb["w_o_a"], wb["w_o_b"], wb["w_o_m"],
              wb["w_out"], seq)
    return _conv_ffn(x2, p["ffn_norm"].reshape(1, D_MODEL), wb["w_up"], p["conv_w"],
                     p["conv_b"].reshape(1, 2 * D_FF), wb["w_down"], seq)


def kernel(x, mem, positions, attn_norm, w_in, a_q_norm, a_k_norm, b_q_norm, b_k_norm, b_sinks,
           mem_norm, w_mem_kv, m_q_norm, m_k_norm, w_o_a, w_o_b, w_o_m, w_gate, b_gate, w_out,
           ffn_norm, w_up, conv_w, conv_b, w_down):
    batch, seq, _ = x.shape
    params = dict(attn_norm=attn_norm, w_in=w_in, a_q_norm=a_q_norm, a_k_norm=a_k_norm,
                  b_q_norm=b_q_norm, b_k_norm=b_k_norm, b_sinks=b_sinks, mem_norm=mem_norm,
                  w_mem_kv=w_mem_kv, m_q_norm=m_q_norm, m_k_norm=m_k_norm, w_o_a=w_o_a, w_o_b=w_o_b,
                  w_o_m=w_o_m, w_gate=w_gate, b_gate=b_gate, w_out=w_out, ffn_norm=ffn_norm,
                  w_up=w_up, conv_w=conv_w, conv_b=conv_b, w_down=w_down)
    pos_rows = positions.astype(F32).reshape(batch * seq // TOKEN_BLOCK, 1, TOKEN_BLOCK)
    x2 = x.reshape(batch * seq, D_MODEL)
    for layer in range(attn_norm.shape[0]):
        x2 = _layer(x2, mem, pos_rows, batch, seq, {k: v[layer] for k, v in params.items()})
    return x2.reshape(batch, seq, D_MODEL)
```

```python
import functools
import math

import numpy as np

import jax
import jax.numpy as jnp
from jax import lax
from jax.experimental import pallas as pl
from jax.experimental.pallas import tpu as pltpu

D_MODEL = 1024
HEAD_DIM = 64
A_GROUPS = ((128, 1), (512, 4), (2048, 16))
A_HEADS = 4
A_SLAB = A_HEADS * HEAD_DIM
A_QKV_COLS = len(A_GROUPS) * 3 * A_SLAB
B_Q_HEADS = 8
B_KV_HEADS = 2
B_GROUP = B_Q_HEADS // B_KV_HEADS
B_WINDOW = 128
B_Q_COLS = B_Q_HEADS * HEAD_DIM
B_KV_COLS = B_KV_HEADS * HEAD_DIM
M_HEADS = 4
M_HEAD_DIM = 128
M_Q_COLS = M_HEADS * M_HEAD_DIM
B_Q_OFF = A_QKV_COLS
B_K_OFF = B_Q_OFF + B_Q_COLS
B_V_OFF = B_K_OFF + B_KV_COLS
M_Q_OFF = B_V_OFF + B_KV_COLS
IN_COLS = M_Q_OFF + M_Q_COLS
N_NORMED_TILES = 2 * len(A_GROUPS) + B_Q_COLS // A_SLAB + 1 + M_Q_COLS // A_SLAB
N_BRANCH = 3
D_FF = 2816
CONV_WIDTH = 3
ROPE_THETA = 500000.0
ROPE_DIM = HEAD_DIM // 4
ROPE_HALF = ROPE_DIM // 2
BLOCK = 128
EPS = 1e-6
NEG = -1e30
LOG2E = math.log2(math.e)

LANES = 128
BF16_ROWS = 16
HEADS_PER_TILE = LANES // HEAD_DIM
TOKEN_BLOCK = 512
ROW_CHUNK = 64
FF_CHUNK = 256
VMEM_LIMIT = 56 * 1024 * 1024

F32 = jnp.float32
BF16 = jnp.bfloat16


def _const_spec(shape):
    return pl.BlockSpec(shape, lambda *_: (0,) * len(shape), pipeline_mode=pl.Buffered(1))


def _rms_rows(x, gain):
    ms = jnp.mean(x * x, axis=-1, keepdims=True)
    return x * lax.rsqrt(ms + EPS) * gain


def _mem_kv_kernel(mem_ref, gain_ref, w_ref, kgain_ref, win_ref, mk_ref, mv_ref, win_out_ref):
    win_out_ref[...] = win_ref[...].astype(BF16)
    hm = _rms_rows(mem_ref[...], gain_ref[...]).astype(BF16)
    kv = jnp.dot(hm, w_ref[...].astype(BF16), preferred_element_type=F32)
    ks = []
    for h in range(M_HEADS):
        kh = kv[:, h * M_HEAD_DIM:(h + 1) * M_HEAD_DIM]
        ks.append(_rms_rows(kh, kgain_ref[...]))
    mk_ref[...] = jnp.concatenate(ks, axis=1).astype(BF16)
    mv_ref[...] = kv[:, M_Q_COLS:].astype(BF16)


def _mem_kv(mem, mem_gain, w_kv, k_gain, w_in):
    b, m, _ = mem.shape
    slab = pl.BlockSpec((_cast_rows(w_in.shape[0], b), w_in.shape[1]), lambda i: (i, 0))
    assert w_in.shape[0] == b * slab.block_shape[0]
    return pl.pallas_call(
        _mem_kv_kernel,
        grid=(b,),
        in_specs=[pl.BlockSpec((None, m, D_MODEL), lambda i: (i, 0, 0)),
                  _const_spec((1, D_MODEL)),
                  _const_spec((D_MODEL, 2 * M_Q_COLS)),
                  _const_spec((1, M_HEAD_DIM)),
                  slab],
        out_specs=[pl.BlockSpec((None, m, M_Q_COLS), lambda i: (i, 0, 0)),
                   pl.BlockSpec((None, m, M_Q_COLS), lambda i: (i, 0, 0)),
                   slab],
        out_shape=[jax.ShapeDtypeStruct((b, m, M_Q_COLS), BF16)] * 2
                  + [jax.ShapeDtypeStruct(w_in.shape, BF16)],
        compiler_params=pltpu.CompilerParams(dimension_semantics=("arbitrary",),
                                             vmem_limit_bytes=VMEM_LIMIT),
        name="mem_kv",
    )(mem, mem_gain, w_kv, k_gain, w_in)


def _in_proj_kernel(x_ref, pos_ref, gain_ref, w_ref, colgain_ref, freq_ref, spread_ref, one_ref,
                    bd64_ref, bd128_ref, *refs, cast_scales):
    n_cast = len(cast_scales)
    cast_in, refs = refs[:n_cast], refs[n_cast:]
    (qa0, ka0, va0, qa1, ka1, va1, qa2, ka2, va2, qb_ref, kb_ref, vb_ref, mq_ref) = refs[:13]
    cast_out, refs = refs[13:13 + n_cast], refs[13 + n_cast:]
    proj_ref, ss_ref, h_ref, cos_ref, sin_lo_ref, sin_hi_ref = refs[:6]
    stage_refs = list(refs[6:])
    tb = x_ref.shape[0]

    for src, dst, scale in zip(cast_in, cast_out, cast_scales):
        w = src[...]
        dst[...] = (w if scale == 1.0 else w * scale).astype(BF16)
    chunks = [slice(r, r + ROW_CHUNK) for r in range(0, tb, ROW_CHUNK)]

    @pl.when(pl.program_id(0) == 0)
    def _():
        proj_ref[...] = jnp.zeros(proj_ref.shape, F32)
        ss_ref[...] = jnp.zeros(ss_ref.shape, F32)

    for rows in chunks:
        h_ref[rows, :] = _rms_rows(x_ref[rows, :], gain_ref[...]).astype(BF16)

    ang = freq_ref[...] * pos_ref[...]
    trig = jnp.concatenate([jnp.cos(ang), jnp.sin(ang),
                            jnp.zeros((LANES - ROPE_DIM, tb), F32)], axis=0).T
    tables = jnp.zeros((tb, 3 * LANES), F32)
    for _ in range(3):
        part = trig.astype(BF16)
        tables = tables + jnp.dot(part, spread_ref[...], preferred_element_type=F32)
        trig = trig - part.astype(F32)
    cos_ref[...] = tables[:, 0:LANES] + one_ref[...]
    sin_lo_ref[...] = tables[:, LANES:2 * LANES]
    sin_hi_ref[...] = tables[:, 2 * LANES:3 * LANES]

    ss_slots = iter(range(ss_ref.shape[0]))
    pending = []

    def flush_pending():
        while pending:
            new, bd, slot = pending.pop()
            ss_ref[slot, :, 0:new.shape[1]] = jnp.dot((new * new).astype(BF16), bd,
                                                      preferred_element_type=F32)

    def proj(c0, width, bd=None):
        tile, off = divmod(c0, A_SLAB)
        assert off + width <= A_SLAB and off % LANES == 0
        t = proj_ref[tile, :, off:off + width]
        ss = None
        if bd is not None:
            slot = next(ss_slots)
            ss = ss_ref[slot, :, 0:width]
        new = jnp.dot(h_ref[...], w_ref[:, c0:c0 + width], preferred_element_type=F32)
        proj_ref[tile, :, off:off + width] = new
        flush_pending()
        if bd is not None:
            pending.append((new, bd, slot))
        return t, ss

    def rope(y, rows):
        parts = []
        for c in range(y.shape[1] // LANES):
            yc = y[:, c * LANES:(c + 1) * LANES]
            parts.append(yc * cos_ref[rows, :]
                         + pltpu.roll(yc, LANES - ROPE_HALF, 1) * sin_lo_ref[rows, :]
                         + pltpu.roll(yc, ROPE_HALF, 1) * sin_hi_ref[rows, :])
        return parts[0] if len(parts) == 1 else jnp.concatenate(parts, axis=1)

    def finish(t_ss, emit, c0, *, dim=None, rotary=False):
        t, ss = t_ss
        if ss is not None:
            gain = colgain_ref[:, c0:c0 + t.shape[1]]
        for rows in chunks:
            y = t[rows]
            if ss is not None:
                y = y * lax.rsqrt(ss[rows] * (1.0 / dim) + EPS) * gain
            if rotary:
                y = rope(y, rows)
            emit(rows, y)

    def to_ref(out_ref, col0=0):
        def emit(rows, y):
            out_ref[rows, col0:col0 + y.shape[1]] = y.astype(BF16)
        return emit

    def store_tile(out_ref, t, dil, c0, **kw):
        if dil == 1:
            finish(t, to_ref(out_ref), c0, **kw)
            return
        stage_ref = stage_refs.pop()

        def emit(rows, y):
            for s in range(A_SLAB // LANES):
                stage_ref[s, rows, :] = y[:, s * LANES:(s + 1) * LANES]
        finish(t, emit, c0, **kw)
        for r in range(dil):
            for s in range(A_SLAB // LANES):
                col = r * A_SLAB + s * LANES
                out_ref[:, col:col + LANES] = stage_ref[s, pl.ds(r, tb // dil, stride=dil), :].astype(BF16)

    bd64 = bd64_ref[...]
    qk = dict(dim=HEAD_DIM, rotary=True)
    for g, (q_ref, k_ref, v_ref) in enumerate(((qa0, ka0, va0), (qa1, ka1, va1), (qa2, ka2, va2))):
        c0 = g * 3 * A_SLAB
        dil = A_GROUPS[g][1]
        store_tile(q_ref, proj(c0, A_SLAB, bd64), dil, c0, **qk)
        store_tile(k_ref, proj(c0 + A_SLAB, A_SLAB, bd64), dil, c0 + A_SLAB, **qk)
        store_tile(v_ref, proj(c0 + 2 * A_SLAB, A_SLAB), dil, c0 + 2 * A_SLAB)

    for s in range(B_Q_COLS // A_SLAB):
        c0 = B_Q_OFF + s * A_SLAB
        finish(proj(c0, A_SLAB, bd64), to_ref(qb_ref, s * A_SLAB), c0, **qk)

    def under_query_heads(out_ref):
        def emit(rows, y):
            swapped = pltpu.roll(y, HEAD_DIM, 1)
            first = jnp.where(first_head_lanes, y, swapped).astype(BF16)
            second = jnp.where(first_head_lanes, swapped, y).astype(BF16)
            reps = B_GROUP // HEADS_PER_TILE
            out_ref[rows, :] = jnp.concatenate([first] * reps + [second] * reps, axis=1)
        return emit
    first_head_lanes = lax.broadcasted_iota(jnp.int32, (1, LANES), 1) < HEAD_DIM
    finish(proj(B_K_OFF, B_KV_COLS, bd64_ref[0:B_KV_COLS, 0:B_KV_COLS]), under_query_heads(kb_ref),
           B_K_OFF, **qk)
    finish(proj(B_V_OFF, B_KV_COLS), under_query_heads(vb_ref), B_V_OFF)

    bd128 = bd128_ref[...]
    for s in range(M_Q_COLS // A_SLAB):
        c0 = M_Q_OFF + s * A_SLAB
        finish(proj(c0, A_SLAB, bd128), to_ref(mq_ref, s * A_SLAB), c0, dim=M_HEAD_DIM)
    flush_pending()


def _cast_rows(rows, n_steps):
    for per_step in range(BF16_ROWS, rows + 1, BF16_ROWS):
        if rows % per_step == 0 and rows // per_step <= n_steps:
            return per_step
    raise ValueError(f"no bf16-aligned split of {rows} rows over {n_steps} steps")


def _in_proj(x2, pos_rows, attn_gain, w_in, colgain, freq_col, spread, one_row, bd64, bd128,
             cast_weights, cast_scales):
    t = x2.shape[0]
    tb = TOKEN_BLOCK
    dils = [dil for _, dil in A_GROUPS for _ in range(3)] + [1, 1, 1, 1]
    widths = [A_SLAB] * 9 + [B_Q_COLS, B_Q_COLS, B_Q_COLS, M_Q_COLS]
    n_staged = sum(d > 1 for d in dils)
    n = t // tb
    done = lambda i: (jnp.maximum(i - 1, 0), 0)

    def cast_spec(w):
        per_step = _cast_rows(w.shape[0], n + 1)
        last = w.shape[0] // per_step - 1
        return pl.BlockSpec((per_step, w.shape[1]), lambda i: (jnp.minimum(i, last), 0))

    cast_specs = [cast_spec(w) for w in cast_weights]
    return pl.pallas_call(
        functools.partial(_in_proj_kernel, cast_scales=tuple(cast_scales)),
        grid=(n + 1,),
        in_specs=[pl.BlockSpec((tb, D_MODEL), lambda i: (jnp.minimum(i, n - 1), 0)),
                  pl.BlockSpec((None, 1, tb), lambda i: (jnp.maximum(i - 1, 0), 0, 0)),
                  _const_spec((1, D_MODEL)),
                  _const_spec((D_MODEL, IN_COLS)),
                  _const_spec((1, IN_COLS)),
                  _const_spec((ROPE_HALF, 1)),
                  _const_spec((LANES, 3 * LANES)),
                  _const_spec((1, LANES)),
                  _const_spec((A_SLAB, A_SLAB)),
                  _const_spec((A_SLAB, A_SLAB))] + cast_specs,
        out_specs=[pl.BlockSpec((tb // d, w * d), done) for w, d in zip(widths, dils)] + cast_specs,
        out_shape=[jax.ShapeDtypeStruct((t // d, w * d), BF16) for w, d in zip(widths, dils)]
                  + [jax.ShapeDtypeStruct(w.shape, BF16) for w in cast_weights],
        scratch_shapes=[pltpu.VMEM((IN_COLS // A_SLAB, tb, A_SLAB), F32),
                        pltpu.VMEM((N_NORMED_TILES, tb, A_SLAB), F32)]
                       + [pltpu.VMEM((tb, D_MODEL), BF16)] + [pltpu.VMEM((tb, LANES), F32)] * 3
                       + [pltpu.VMEM((A_SLAB // LANES, tb, LANES), F32)] * n_staged,
        compiler_params=pltpu.CompilerParams(dimension_semantics=("arbitrary",),
                                             vmem_limit_bytes=VMEM_LIMIT),
        name="in_proj",
    )(x2, pos_rows, attn_gain, w_in, colgain, freq_col, spread, one_row, bd64, bd128, *cast_weights)


def _band_attn_kernel(*refs, rows, slabs, prev_off, has_sink, with_lse):
    q_ref, kc_ref, kp_ref, vc_ref, vp_ref = refs[:5]
    pos = 5
    sink_ref = None
    if has_sink:
        sink_ref = refs[pos]
        pos += 1
    o_ref = refs[pos]
    lse_ref = refs[pos + 1] if with_lse else None

    first_step = pl.program_id(2) == 0
    row = lax.broadcasted_iota(jnp.int32, (BLOCK, 2 * BLOCK), 0)
    col = lax.broadcasted_iota(jnp.int32, (BLOCK, 2 * BLOCK), 1)
    bias = jnp.where(col < BLOCK,
                     jnp.where(col >= row + prev_off, 0.0, NEG),
                     jnp.where(col - BLOCK <= row, 0.0, NEG)).astype(F32)
    bias_first = jnp.where(jnp.logical_and(first_step, col < BLOCK), NEG, bias)
    head_of_lane = lax.broadcasted_iota(jnp.int32, (1, A_SLAB), 1) // HEAD_DIM
    head_mask = [(head_of_lane == hh).astype(BF16) for hh in range(A_HEADS)]
    first_head_lanes = lax.broadcasted_iota(jnp.int32, (1, LANES), 1) < HEAD_DIM
    pack_row = lax.broadcasted_iota(jnp.int32, (BF16_ROWS, A_SLAB), 0)
    nblk = rows // BLOCK

    def drop_first_row(t):
        top = jnp.where(pack_row == 0, jnp.zeros((BF16_ROWS, A_SLAB), BF16), t[0:BF16_ROWS])
        return jnp.concatenate([top, t[BF16_ROWS:]], axis=0)

    for w in range(slabs):
        cs = slice(w * A_SLAB, (w + 1) * A_SLAB)
        biases = {False: [bias] * A_HEADS, True: [bias_first] * A_HEADS}
        if has_sink:
            biases = {first: [jnp.where(col == 0, sink_ref[w * A_HEADS + hh] * LOG2E, b[hh])
                              for hh in range(A_HEADS)] for first, b in biases.items()}
        for jb in range(nblk):
            q = q_ref[jb * BLOCK:(jb + 1) * BLOCK, cs]
            if jb == 0:
                k2 = jnp.concatenate([kp_ref[:, cs], kc_ref[0:BLOCK, cs]], axis=0)
                v2 = jnp.concatenate([vp_ref[:, cs], vc_ref[0:BLOCK, cs]], axis=0)
            else:
                k2 = kc_ref[(jb - 1) * BLOCK:(jb + 1) * BLOCK, cs]
                v2 = vc_ref[(jb - 1) * BLOCK:(jb + 1) * BLOCK, cs]
            if has_sink:
                k2 = drop_first_row(k2)
                v2 = drop_first_row(v2)
            qs = jnp.concatenate([q * head_mask[hh] for hh in range(A_HEADS)], axis=0)
            s = lax.dot_general(qs, k2, (((1,), (1,)), ((), ())), preferred_element_type=F32)
            ps, ms, ls = [], [], []
            for hh in range(A_HEADS):
                sh = s[hh * BLOCK:(hh + 1) * BLOCK] + biases[jb == 0][hh]
                m = jnp.max(sh, axis=-1, keepdims=True)
                p = jnp.exp2(sh - m)
                ms.append(m)
                ls.append(jnp.sum(p, axis=-1, keepdims=True))
                ps.append(p.astype(BF16))
            ost = jnp.dot(jnp.concatenate(ps, axis=0), v2, preferred_element_type=F32)
            o_tiles, lse_tiles = [], []
            for t0 in range(0, A_HEADS, HEADS_PER_TILE):
                lanes = slice(t0 * HEAD_DIM, (t0 + HEADS_PER_TILE) * HEAD_DIM)
                per_head = [ost[hh * BLOCK:(hh + 1) * BLOCK, lanes] * (1.0 / ls[hh])
                            for hh in range(t0, t0 + HEADS_PER_TILE)]
                o_tiles.append(jnp.where(first_head_lanes, per_head[0], per_head[1]))
                if with_lse:
                    lse_tiles.append(jnp.where(first_head_lanes, ms[t0] + jnp.log2(ls[t0]),
                                               ms[t0 + 1] + jnp.log2(ls[t0 + 1])))
            o_ref[jb * BLOCK:(jb + 1) * BLOCK, cs] = jnp.concatenate(o_tiles, axis=1).astype(o_ref.dtype)
            if with_lse:
                lse_ref[jb * BLOCK:(jb + 1) * BLOCK, cs] = jnp.concatenate(lse_tiles, axis=1)


def _band_attn(q, k, v, *, name, rows, slabs, prev_off, sinks=None, with_lse=True):
    b, length, width = q.shape
    assert sinks is None or prev_off >= 1
    n_col = width // (slabs * A_SLAB)
    n_row = length // rows
    blk_per_step = rows // BLOCK
    cur = pl.BlockSpec((None, rows, slabs * A_SLAB), lambda i, r, j: (i, j, r))
    prev = pl.BlockSpec((None, BLOCK, slabs * A_SLAB),
                        lambda i, r, j: (i, jnp.maximum(j * blk_per_step - 1, 0), r))
    in_specs = [cur, cur, prev, cur, prev]
    args = [q, k, k, v, v]
    if sinks is not None:
        in_specs.append(pl.BlockSpec(memory_space=pltpu.SMEM))
        args.append(sinks)
    out_specs = [cur]
    out_shape = [jax.ShapeDtypeStruct((b, length, width), BF16)]
    if with_lse:
        out_specs.append(cur)
        out_shape.append(jax.ShapeDtypeStruct((b, length, width), F32))
    return pl.pallas_call(
        functools.partial(_band_attn_kernel, rows=rows, slabs=slabs, prev_off=prev_off,
                          has_sink=sinks is not None, with_lse=with_lse),
        grid=(b, n_col, n_row),
        in_specs=in_specs,
        out_specs=out_specs,
        out_shape=out_shape,
        compiler_params=pltpu.CompilerParams(dimension_semantics=("arbitrary",) * 3,
                                             vmem_limit_bytes=VMEM_LIMIT),
        name=name,
    )(*args)


def _mix_kernel(x_ref, gain_ref, oa0, la0, oa1, la1, oa2, la2, ob_ref, mq_ref, mk_ref, mv_ref,
                wg_ref, bg_ref, woa_ref, wob_ref, wom_ref, wout_ref, out_ref, *stage_refs):
    x = x_ref[...]
    h = _rms_rows(x, gain_ref[...]).astype(BF16)
    stage_refs = list(stage_refs)

    def token_major(blk_ref, dil):
        if dil == 1:
            return blk_ref[...].astype(F32)
        stage_ref = stage_refs.pop()
        rows = blk_ref.shape[0]
        for r in range(dil):
            for s in range(A_SLAB // LANES):
                c0 = r * A_SLAB + s * LANES
                stage_ref[s, pl.ds(r, rows, stride=dil), :] = blk_ref[:, c0:c0 + LANES].astype(F32)
        return jnp.concatenate([stage_ref[s] for s in range(A_SLAB // LANES)], axis=1)

    head_cols = [slice(hh * M_HEAD_DIM, (hh + 1) * M_HEAD_DIM) for hh in range(M_HEADS)]
    scores = [lax.dot_general(mq_ref[:, cs], mk_ref[:, cs], (((1,), (1,)), ((), ())),
                              preferred_element_type=F32) for cs in head_cols]
    gate_cols = [slice(i * D_MODEL, (i + 1) * D_MODEL) for i in range(N_BRANCH)]
    gates = [1.0 + jnp.tanh(jnp.dot(h, wg_ref[:, cs], preferred_element_type=F32) + 0.5 * bg_ref[:, cs])
             for cs in gate_cols]

    dils = [dil for _, dil in A_GROUPS]
    l0, l1, l2 = (token_major(r, d) for r, d in zip((la0, la1, la2), dils))
    o0, o1, o2 = (token_major(r, d) for r, d in zip((oa0, oa1, oa2), dils))
    mx = jnp.maximum(jnp.maximum(l0, l1), l2)
    e0, e1, e2 = jnp.exp2(l0 - mx), jnp.exp2(l1 - mx), jnp.exp2(l2 - mx)
    o_a = (e0 * o0 + e1 * o1 + e2 * o2) / (e0 + e1 + e2)

    o_m = []
    for s, cs in zip(scores, head_cols):
        p = jnp.exp2(s - jnp.max(s, axis=-1, keepdims=True))
        l = jnp.sum(p, axis=-1, keepdims=True)
        o_m.append(jnp.dot(p.astype(BF16), mv_ref[:, cs], preferred_element_type=F32) / l)
    o_m = jnp.concatenate(o_m, axis=1)

    merged = jnp.zeros(x.shape, F32)
    for gate, o, w_ref in zip(gates, (o_a, ob_ref[...], o_m), (woa_ref, wob_ref, wom_ref)):
        merged = merged + gate * jnp.dot(o.astype(BF16), w_ref[...], preferred_element_type=F32)
    out_ref[...] = x + jnp.dot(merged.astype(BF16), wout_ref[...], preferred_element_type=F32)


def _mix(x2, attn_gain, oa, la, ob, mq, mk, mv, w_gate, b_gate, w_o_a, w_o_b, w_o_m, w_out, seq):
    t = x2.shape[0]
    tb = TOKEN_BLOCK
    mem_len = mk.shape[1]
    per_seq = seq // tb
    tok = lambda w, d=1: pl.BlockSpec((tb // d, w * d), lambda i: (i, 0))
    mem = pl.BlockSpec((None, mem_len, M_Q_COLS), lambda i: (i // per_seq, 0, 0))
    dils = [dil for _, dil in A_GROUPS]
    return pl.pallas_call(
        _mix_kernel,
        grid=(t // tb,),
        in_specs=[tok(D_MODEL), _const_spec((1, D_MODEL))]
                 + [tok(A_SLAB, d) for d in dils for _ in range(2)]
                 + [tok(B_Q_COLS), tok(M_Q_COLS), mem, mem,
                  _const_spec((D_MODEL, N_BRANCH * D_MODEL)), _const_spec((1, N_BRANCH * D_MODEL)),
                  _const_spec((A_SLAB, D_MODEL)), _const_spec((B_Q_COLS, D_MODEL)),
                  _const_spec((M_Q_COLS, D_MODEL)), _const_spec((D_MODEL, D_MODEL))],
        out_specs=tok(D_MODEL),
        out_shape=jax.ShapeDtypeStruct((t, D_MODEL), F32),
        scratch_shapes=[pltpu.VMEM((A_SLAB // LANES, tb, LANES), F32)] * (2 * sum(d > 1 for d in dils)),
        compiler_params=pltpu.CompilerParams(dimension_semantics=("arbitrary",),
                                             vmem_limit_bytes=VMEM_LIMIT),
        name="mix",
    )(x2, attn_gain, oa[0], la[0], oa[1], la[1], oa[2], la[2], ob, mq, mk, mv,
      w_gate, b_gate, w_o_a, w_o_b, w_o_m, w_out)


def _conv_ffn_kernel(x_ref, gain_ref, wup_ref, cw_ref, cb_ref, wdown_ref, out_ref,
                     carry_ref, ext_ref, act_ref, *, per_seq):
    tb = x_ref.shape[0]
    halo = CONV_WIDTH - 1
    pad = 8

    @pl.when(pl.program_id(0) % per_seq == 0)
    def _():
        carry_ref[...] = jnp.zeros(carry_ref.shape, F32)

    x = x_ref[...]
    h = _rms_rows(x, gain_ref[...]).astype(BF16)

    def conv(c0, scale=1.0):
        cs = slice(c0, c0 + FF_CHUNK)
        u = jnp.dot(h, wup_ref[:, cs], preferred_element_type=F32)
        ext_ref[0:pad, :] = carry_ref[:, cs]
        ext_ref[pad:pad + tb, :] = u
        carry_ref[:, cs] = u[tb - pad:tb, :]
        taps = cw_ref[:, cs] * scale
        c = cb_ref[:, cs] * scale + taps[CONV_WIDTH - 1:CONV_WIDTH] * u
        for j in range(halo):
            c = c + taps[j:j + 1] * ext_ref[pad - halo + j:pad - halo + j + tb, :]
        return c

    for ch in range(D_FF // FF_CHUNK):
        half = conv(ch * FF_CHUNK, 0.5)
        g = conv(D_FF + ch * FF_CHUNK)
        act = (half + half * jnp.tanh(half)) * g
        act_ref[:, ch * FF_CHUNK:(ch + 1) * FF_CHUNK] = act.astype(BF16)
    out_ref[...] = x + jnp.dot(act_ref[...], wdown_ref[...], preferred_element_type=F32)


def _conv_ffn(x2, gain, w_up, conv_w, conv_b, w_down, seq):
    t = x2.shape[0]
    tb = TOKEN_BLOCK
    return pl.pallas_call(
        functools.partial(_conv_ffn_kernel, per_seq=seq // tb),
        grid=(t // tb,),
        in_specs=[pl.BlockSpec((tb, D_MODEL), lambda i: (i, 0)),
                  _const_spec((1, D_MODEL)),
                  _const_spec((D_MODEL, 2 * D_FF)),
                  _const_spec((CONV_WIDTH, 2 * D_FF)),
                  _const_spec((1, 2 * D_FF)),
                  _const_spec((D_FF, D_MODEL))],
        out_specs=pl.BlockSpec((tb, D_MODEL), lambda i: (i, 0)),
        out_shape=jax.ShapeDtypeStruct((t, D_MODEL), F32),
        scratch_shapes=[pltpu.VMEM((8, 2 * D_FF), F32),
                        pltpu.VMEM((tb + 8, FF_CHUNK), F32),
                        pltpu.VMEM((tb, D_FF), BF16)],
        compiler_params=pltpu.CompilerParams(dimension_semantics=("arbitrary",),
                                             vmem_limit_bytes=VMEM_LIMIT),
        name="conv_ffn",
    )(x2, gain, w_up, conv_w, conv_b, w_down)


def _tables():
    lane = np.arange(A_SLAB)
    bd64 = jnp.asarray(lane[:, None] // HEAD_DIM == lane[None, :] // HEAD_DIM, BF16)
    bd128 = jnp.asarray(lane[:, None] // M_HEAD_DIM == lane[None, :] // M_HEAD_DIM, BF16)
    freq_col = jnp.exp(jnp.arange(ROPE_HALF, dtype=F32) * (-2.0 * math.log(ROPE_THETA) / ROPE_DIM))
    freq_col = freq_col.reshape(ROPE_HALF, 1)
    in_head = np.arange(LANES) % HEAD_DIM
    j = np.arange(LANES)[:, None]
    rot = in_head[None, :] < ROPE_DIM
    cos_sel = rot & (j == in_head[None, :] % ROPE_HALF)
    sin_sel = rot & (j == ROPE_HALF + in_head[None, :] % ROPE_HALF)
    lo = in_head[None, :] < ROPE_HALF
    spread = np.concatenate([cos_sel.astype(np.float32), -(sin_sel & lo).astype(np.float32),
                             (sin_sel & ~lo).astype(np.float32)], axis=1)
    one_row = (in_head >= ROPE_DIM).astype(np.float32).reshape(1, LANES)
    return bd64, bd128, freq_col, jnp.asarray(spread, BF16), jnp.asarray(one_row)


def _layer(x2, mem, pos_rows, batch, seq, p):
    bd64, bd128, freq_col, spread, one_row = _tables()
    n_groups = len(A_GROUPS)
    gains = jnp.concatenate([p["a_q_norm"], p["a_k_norm"], p["b_q_norm"][None], p["b_k_norm"][None],
                             p["m_q_norm"].reshape(M_HEAD_DIM // HEAD_DIM, HEAD_DIM),
                             jnp.ones((1, HEAD_DIM), F32)])
    row_bq, row_bk, row_mq = 2 * n_groups, 2 * n_groups + 1, 2 * n_groups + 2
    row_one = row_mq + M_HEAD_DIM // HEAD_DIM
    qk_scale = HEAD_DIM ** -0.5 * LOG2E
    src, scale = [], []
    for g in range(n_groups):
        src += [g] * A_HEADS + [n_groups + g] * A_HEADS + [row_one] * A_HEADS
        scale += [qk_scale] * A_HEADS + [1.0] * (2 * A_HEADS)
    src += [row_bq] * B_Q_HEADS + [row_bk] * B_KV_HEADS + [row_one] * B_KV_HEADS
    scale += [qk_scale] * B_Q_HEADS + [1.0] * (2 * B_KV_HEADS)
    src += [row_mq + i for i in range(M_HEAD_DIM // HEAD_DIM)] * M_HEADS
    scale += [M_HEAD_DIM ** -0.5 * LOG2E] * (M_Q_COLS // HEAD_DIM)
    colgain = (gains[np.asarray(src)] * np.asarray(scale, np.float32)[:, None]).reshape(1, IN_COLS)

    attn_gain = p["attn_norm"].reshape(1, D_MODEL)
    mk, mv, w_in = _mem_kv(mem, p["mem_norm"].reshape(1, D_MODEL), p["w_mem_kv"],
                           p["m_k_norm"].reshape(1, M_HEAD_DIM), p["w_in"])
    later_weights = ("w_gate", "w_o_a", "w_o_b", "w_o_m", "w_out", "w_up", "w_down")
    scales = [0.5 if name in ("w_gate", "w_out") else 1.0 for name in later_weights]
    outs = _in_proj(x2, pos_rows, attn_gain, w_in, colgain, freq_col, spread, one_row,
                    bd64, bd128, [p[name] for name in later_weights], scales)
    (qa0, ka0, va0, qa1, ka1, va1, qa2, ka2, va2, qb, kb, vb, mq) = outs[:13]
    wb = dict(zip(later_weights, outs[13:]))

    def per_batch(t):
        return t.reshape(batch, t.shape[0] // batch, t.shape[-1])

    def flat(t):
        return t.reshape(t.shape[0] * t.shape[1], t.shape[2])

    oa, la = [], []
    cfg = {1: dict(rows=seq, slabs=1), 4: dict(rows=seq // 4, slabs=4), 16: dict(rows=seq // 16, slabs=16)}
    for (window, dil), (q, k, v) in zip(A_GROUPS, ((qa0, ka0, va0), (qa1, ka1, va1), (qa2, ka2, va2))):
        assert window // dil == BLOCK
        o, lse = _band_attn(per_batch(q), per_batch(k), per_batch(v), name=f"band_attn_d{dil}",
                            prev_off=0, **cfg[dil])
        oa.append(flat(o))
        la.append(flat(lse))
    (ob,) = _band_attn(per_batch(qb), per_batch(kb), per_batch(vb), name="band_attn_swa", rows=2048,
                       slabs=B_KV_HEADS, prev_off=BLOCK - (B_WINDOW - 1), sinks=p["b_sinks"],
                       with_lse=False)
    ob = flat(ob)

    x2 = _mix(x2, attn_gain, oa, la, ob, mq, mk, mv, wb["w_gate"],
              p["b_gate"].reshape(1, N_BRANCH * D_MODEL), wb["w_o_a"], wb["w_o_b"], wb["w_o_m"],
              wb["w_out"], seq)
    return _conv_ffn(x2, p["ffn_norm"].reshape(1, D_MODEL), wb["w_up"], p["conv_w"],
                     p["conv_b"].reshape(1, 2 * D_FF), wb["w_down"], seq)


def kernel(x, mem, positions, attn_norm, w_in, a_q_norm, a_k_norm, b_q_norm, b_k_norm, b_sinks,
           mem_norm, w_mem_kv, m_q_norm, m_k_norm, w_o_a, w_o_b, w_o_m, w_gate, b_gate, w_out,
           ffn_norm, w_up, conv_w, conv_b, w_down):
    batch, seq, _ = x.shape
    params = dict(attn_norm=attn_norm, w_in=w_in, a_q_norm=a_q_norm, a_k_norm=a_k_norm,
                  b_q_norm=b_q_norm, b_k_norm=b_k_norm, b_sinks=b_sinks, mem_norm=mem_norm,
                  w_mem_kv=w_mem_kv, m_q_norm=m_q_norm, m_k_norm=m_k_norm, w_o_a=w_o_a, w_o_b=w_o_b,
                  w_o_m=w_o_m, w_gate=w_gate, b_gate=b_gate, w_out=w_out, ffn_norm=ffn_norm,
                  w_up=w_up, conv_w=conv_w, conv_b=conv_b, w_down=w_down)
    pos_rows = positions.astype(F32).reshape(batch * seq // TOKEN_BLOCK, 1, TOKEN_BLOCK)
    x2 = x.reshape(batch * seq, D_MODEL)
    for layer in range(attn_norm.shape[0]):
        x2 = _layer(x2, mem, pos_rows, batch, seq, {k: v[layer] for k, v in params.items()})
    return x2.reshape(batch, seq, D_MODEL)
```

```python
import functools
import math

import numpy as np

import jax
import jax.numpy as jnp
from jax import lax
from jax.experimental import pallas as pl
from jax.experimental.pallas import tpu as pltpu

D_MODEL = 1024
HEAD_DIM = 64
A_GROUPS = ((128, 1), (512, 4), (2048, 16))
A_HEADS = 4
A_SLAB = A_HEADS * HEAD_DIM
A_QKV_COLS = len(A_GROUPS) * 3 * A_SLAB
B_Q_HEADS = 8
B_KV_HEADS = 2
B_GROUP = B_Q_HEADS // B_KV_HEADS
B_WINDOW = 128
B_Q_COLS = B_Q_HEADS * HEAD_DIM
B_KV_COLS = B_KV_HEADS * HEAD_DIM
M_HEADS = 4
M_HEAD_DIM = 128
M_Q_COLS = M_HEADS * M_HEAD_DIM
B_Q_OFF = A_QKV_COLS
B_K_OFF = B_Q_OFF + B_Q_COLS
B_V_OFF = B_K_OFF + B_KV_COLS
M_Q_OFF = B_V_OFF + B_KV_COLS
IN_COLS = M_Q_OFF + M_Q_COLS
N_NORMED_TILES = 2 * len(A_GROUPS) + B_Q_COLS // A_SLAB + 1 + M_Q_COLS // A_SLAB
N_BRANCH = 3
D_FF = 2816
CONV_WIDTH = 3
ROPE_THETA = 500000.0
ROPE_DIM = HEAD_DIM // 4
ROPE_HALF = ROPE_DIM // 2
BLOCK = 128
EPS = 1e-6
NEG = -1e30
LOG2E = math.log2(math.e)

LANES = 128
SUBLANES = 8
BF16_ROWS = 16
HEADS_PER_TILE = LANES // HEAD_DIM
TOKEN_BLOCK = 512
ROW_CHUNK = 32
FF_CHUNK = 256
VMEM_LIMIT = 56 * 1024 * 1024

F32 = jnp.float32
BF16 = jnp.bfloat16

assert all(window // dil == BLOCK for window, dil in A_GROUPS) and B_WINDOW <= BLOCK
assert B_KV_COLS == LANES and B_KV_HEADS == HEADS_PER_TILE and B_GROUP % HEADS_PER_TILE == 0
assert M_HEAD_DIM == LANES and D_FF % FF_CHUNK == 0 and CONV_WIDTH - 1 <= SUBLANES


def _const_spec(shape):
    return pl.BlockSpec(shape, lambda *_: (0,) * len(shape), pipeline_mode=pl.Buffered(1))


def _rms_rows(x, gain):
    ms = jnp.mean(x * x, axis=-1, keepdims=True)
    return x * lax.rsqrt(ms + EPS) * gain


def _mem_kv_kernel(mem_ref, gain_ref, w_ref, kgain_ref, win_ref, mk_ref, mv_ref, win_out_ref):
    win_out_ref[...] = win_ref[...].astype(BF16)
    hm = _rms_rows(mem_ref[...], gain_ref[...]).astype(BF16)
    kv = jnp.dot(hm, w_ref[...].astype(BF16), preferred_element_type=F32)
    ks = []
    for h in range(M_HEADS):
        kh = kv[:, h * M_HEAD_DIM:(h + 1) * M_HEAD_DIM]
        ks.append(_rms_rows(kh, kgain_ref[...]))
    mk_ref[...] = jnp.concatenate(ks, axis=1).astype(BF16)
    mv_ref[...] = kv[:, M_Q_COLS:].astype(BF16)


def _mem_kv(mem, mem_gain, w_kv, k_gain, w_in):
    b, m, _ = mem.shape
    slab = pl.BlockSpec((_cast_rows(w_in.shape[0], b), w_in.shape[1]), lambda i: (i, 0))
    assert w_in.shape[0] == b * slab.block_shape[0]
    return pl.pallas_call(
        _mem_kv_kernel,
        grid=(b,),
        in_specs=[pl.BlockSpec((None, m, D_MODEL), lambda i: (i, 0, 0)),
                  _const_spec((1, D_MODEL)),
                  _const_spec((D_MODEL, 2 * M_Q_COLS)),
                  _const_spec((1, M_HEAD_DIM)),
                  slab],
        out_specs=[pl.BlockSpec((None, m, M_Q_COLS), lambda i: (i, 0, 0)),
                   pl.BlockSpec((None, m, M_Q_COLS), lambda i: (i, 0, 0)),
                   slab],
        out_shape=[jax.ShapeDtypeStruct((b, m, M_Q_COLS), BF16)] * 2
                  + [jax.ShapeDtypeStruct(w_in.shape, BF16)],
        compiler_params=pltpu.CompilerParams(dimension_semantics=("arbitrary",),
                                             vmem_limit_bytes=VMEM_LIMIT),
        name="mem_kv",
    )(mem, mem_gain, w_kv, k_gain, w_in)


def _in_proj_kernel(x_ref, pos_ref, gain_ref, w_ref, colgain_ref, freq_ref, spread_ref, one_ref,
                    bd64_ref, bd128_ref, *refs, cast_scales):
    n_cast = len(cast_scales)
    cast_in, refs = refs[:n_cast], refs[n_cast:]
    (qa0, ka0, va0, qa1, ka1, va1, qa2, ka2, va2, qb_ref, kb_ref, vb_ref, mq_ref) = refs[:13]
    cast_out, refs = refs[13:13 + n_cast], refs[13 + n_cast:]
    proj_ref, ss_ref, h_ref, cos_ref, sin_lo_ref, sin_hi_ref = refs[:6]
    stage_refs = list(refs[6:])
    tb = x_ref.shape[0]

    for src, dst, scale in zip(cast_in, cast_out, cast_scales):
        w = src[...]
        dst[...] = (w if scale == 1.0 else w * scale).astype(BF16)
    chunks = [slice(r, r + ROW_CHUNK) for r in range(0, tb, ROW_CHUNK)]

    @pl.when(pl.program_id(0) == 0)
    def _():
        proj_ref[...] = jnp.zeros(proj_ref.shape, F32)
        ss_ref[...] = jnp.zeros(ss_ref.shape, F32)

    for rows in chunks:
        h_ref[rows, :] = _rms_rows(x_ref[rows, :], gain_ref[...]).astype(BF16)

    ang = freq_ref[...] * pos_ref[...]
    trig = jnp.concatenate([jnp.cos(ang), jnp.sin(ang),
                            jnp.zeros((LANES - ROPE_DIM, tb), F32)], axis=0).T
    tables = jnp.zeros((tb, 3 * LANES), F32)
    for _ in range(3):
        part = trig.astype(BF16)
        tables = tables + jnp.dot(part, spread_ref[...], preferred_element_type=F32)
        trig = trig - part.astype(F32)
    cos_ref[...] = tables[:, 0:LANES] + one_ref[...]
    sin_lo_ref[...] = tables[:, LANES:2 * LANES]
    sin_hi_ref[...] = tables[:, 2 * LANES:3 * LANES]

    ss_slots = iter(range(ss_ref.shape[0]))
    pending = []

    def flush_pending():
        while pending:
            new, bd, slot = pending.pop()
            ss_ref[slot, :, 0:new.shape[1]] = jnp.dot((new * new).astype(BF16), bd,
                                                      preferred_element_type=F32)

    def proj(c0, width, bd=None):
        tile, off = divmod(c0, A_SLAB)
        assert off + width <= A_SLAB and off % LANES == 0
        t = proj_ref[tile, :, off:off + width]
        ss = None
        if bd is not None:
            slot = next(ss_slots)
            ss = ss_ref[slot, :, 0:width]
        new = jnp.dot(h_ref[...], w_ref[:, c0:c0 + width], preferred_element_type=F32)
        proj_ref[tile, :, off:off + width] = new
        flush_pending()
        if bd is not None:
            pending.append((new, bd, slot))
        return t, ss

    def rope(y, rows):
        parts = []
        for c in range(y.shape[1] // LANES):
            yc = y[:, c * LANES:(c + 1) * LANES]
            parts.append(yc * cos_ref[rows, :]
                         + pltpu.roll(yc, LANES - ROPE_HALF, 1) * sin_lo_ref[rows, :]
                         + pltpu.roll(yc, ROPE_HALF, 1) * sin_hi_ref[rows, :])
        return parts[0] if len(parts) == 1 else jnp.concatenate(parts, axis=1)

    def finish(t_ss, emit, c0, *, dim=None, rotary=False):
        t, ss = t_ss
        if ss is not None:
            gain = colgain_ref[:, c0:c0 + t.shape[1]]
        for rows in chunks:
            y = t[rows]
            if ss is not None:
                y = y * lax.rsqrt(ss[rows] * (1.0 / dim) + EPS) * gain
            if rotary:
                y = rope(y, rows)
            emit(rows, y)

    def to_ref(out_ref, col0=0):
        def emit(rows, y):
            out_ref[rows, col0:col0 + y.shape[1]] = y.astype(BF16)
        return emit

    def store_tile(out_ref, t, dil, c0, **kw):
        if dil == 1:
            finish(t, to_ref(out_ref), c0, **kw)
            return
        stage_ref = stage_refs.pop()

        def emit(rows, y):
            for s in range(A_SLAB // LANES):
                stage_ref[s, rows, :] = y[:, s * LANES:(s + 1) * LANES]
        finish(t, emit, c0, **kw)
        for r in range(dil):
            for s in range(A_SLAB // LANES):
                col = r * A_SLAB + s * LANES
                out_ref[:, col:col + LANES] = stage_ref[s, pl.ds(r, tb // dil, stride=dil), :].astype(BF16)

    bd64 = bd64_ref[...]
    qk = dict(dim=HEAD_DIM, rotary=True)
    for g, (q_ref, k_ref, v_ref) in enumerate(((qa0, ka0, va0), (qa1, ka1, va1), (qa2, ka2, va2))):
        c0 = g * 3 * A_SLAB
        dil = A_GROUPS[g][1]
        store_tile(q_ref, proj(c0, A_SLAB, bd64), dil, c0, **qk)
        store_tile(k_ref, proj(c0 + A_SLAB, A_SLAB, bd64), dil, c0 + A_SLAB, **qk)
        store_tile(v_ref, proj(c0 + 2 * A_SLAB, A_SLAB), dil, c0 + 2 * A_SLAB)

    for s in range(B_Q_COLS // A_SLAB):
        c0 = B_Q_OFF + s * A_SLAB
        finish(proj(c0, A_SLAB, bd64), to_ref(qb_ref, s * A_SLAB), c0, **qk)

    def under_query_heads(out_ref):
        def emit(rows, y):
            swapped = pltpu.roll(y, HEAD_DIM, 1)
            first = jnp.where(first_head_lanes, y, swapped).astype(BF16)
            second = jnp.where(first_head_lanes, swapped, y).astype(BF16)
            reps = B_GROUP // HEADS_PER_TILE
            out_ref[rows, :] = jnp.concatenate([first] * reps + [second] * reps, axis=1)
        return emit
    first_head_lanes = lax.broadcasted_iota(jnp.int32, (1, LANES), 1) < HEAD_DIM
    finish(proj(B_K_OFF, B_KV_COLS, bd64_ref[0:B_KV_COLS, 0:B_KV_COLS]), under_query_heads(kb_ref),
           B_K_OFF, **qk)
    finish(proj(B_V_OFF, B_KV_COLS), under_query_heads(vb_ref), B_V_OFF)

    bd128 = bd128_ref[...]
    for s in range(M_Q_COLS // A_SLAB):
        c0 = M_Q_OFF + s * A_SLAB
        finish(proj(c0, A_SLAB, bd128), to_ref(mq_ref, s * A_SLAB), c0, dim=M_HEAD_DIM)
    flush_pending()


def _cast_rows(rows, n_steps):
    for per_step in range(BF16_ROWS, rows + 1, BF16_ROWS):
        if rows % per_step == 0 and rows // per_step <= n_steps:
            return per_step
    raise ValueError(f"no bf16-aligned split of {rows} rows over {n_steps} steps")


def _in_proj(x2, pos_rows, attn_gain, w_in, colgain, freq_col, spread, one_row, bd64, bd128,
             cast_weights, cast_scales):
    t = x2.shape[0]
    tb = TOKEN_BLOCK
    dils = [dil for _, dil in A_GROUPS for _ in range(3)] + [1, 1, 1, 1]
    widths = [A_SLAB] * 9 + [B_Q_COLS, B_Q_COLS, B_Q_COLS, M_Q_COLS]
    n_staged = sum(d > 1 for d in dils)
    n = t // tb
    done = lambda i: (jnp.maximum(i - 1, 0), 0)

    def cast_spec(w):
        per_step = _cast_rows(w.shape[0], n + 1)
        last = w.shape[0] // per_step - 1
        return pl.BlockSpec((per_step, w.shape[1]), lambda i: (jnp.minimum(i, last), 0))

    cast_specs = [cast_spec(w) for w in cast_weights]
    return pl.pallas_call(
        functools.partial(_in_proj_kernel, cast_scales=tuple(cast_scales)),
        grid=(n + 1,),
        in_specs=[pl.BlockSpec((tb, D_MODEL), lambda i: (jnp.minimum(i, n - 1), 0)),
                  pl.BlockSpec((None, 1, tb), lambda i: (jnp.maximum(i - 1, 0), 0, 0)),
                  _const_spec((1, D_MODEL)),
                  _const_spec((D_MODEL, IN_COLS)),
                  _const_spec((1, IN_COLS)),
                  _const_spec((ROPE_HALF, 1)),
                  _const_spec((LANES, 3 * LANES)),
                  _const_spec((1, LANES)),
                  _const_spec((A_SLAB, A_SLAB)),
                  _const_spec((A_SLAB, A_SLAB))] + cast_specs,
        out_specs=[pl.BlockSpec((tb // d, w * d), done) for w, d in zip(widths, dils)] + cast_specs,
        out_shape=[jax.ShapeDtypeStruct((t // d, w * d), BF16) for w, d in zip(widths, dils)]
                  + [jax.ShapeDtypeStruct(w.shape, BF16) for w in cast_weights],
        scratch_shapes=[pltpu.VMEM((IN_COLS // A_SLAB, tb, A_SLAB), F32),
                        pltpu.VMEM((N_NORMED_TILES, tb, A_SLAB), F32)]
                       + [pltpu.VMEM((tb, D_MODEL), BF16)] + [pltpu.VMEM((tb, LANES), F32)] * 3
                       + [pltpu.VMEM((A_SLAB // LANES, tb, LANES), F32)] * n_staged,
        compiler_params=pltpu.CompilerParams(dimension_semantics=("arbitrary",),
                                             vmem_limit_bytes=VMEM_LIMIT),
        name="in_proj",
    )(x2, pos_rows, attn_gain, w_in, colgain, freq_col, spread, one_row, bd64, bd128, *cast_weights)


def _band_attn_kernel(*refs, rows, slabs, prev_off, has_sink, with_lse):
    q_ref, kc_ref, kp_ref, vc_ref, vp_ref = refs[:5]
    pos = 5
    sink_ref = None
    if has_sink:
        sink_ref = refs[pos]
        pos += 1
    o_ref = refs[pos]
    lse_ref = refs[pos + 1] if with_lse else None

    first_step = pl.program_id(2) == 0
    row = lax.broadcasted_iota(jnp.int32, (BLOCK, 2 * BLOCK), 0)
    col = lax.broadcasted_iota(jnp.int32, (BLOCK, 2 * BLOCK), 1)
    bias = jnp.where(col < BLOCK,
                     jnp.where(col >= row + prev_off, 0.0, NEG),
                     jnp.where(col - BLOCK <= row, 0.0, NEG)).astype(F32)
    bias_first = jnp.where(jnp.logical_and(first_step, col < BLOCK), NEG, bias)
    head_of_lane = lax.broadcasted_iota(jnp.int32, (1, A_SLAB), 1) // HEAD_DIM
    head_mask = [(head_of_lane == hh).astype(BF16) for hh in range(A_HEADS)]
    first_head_lanes = lax.broadcasted_iota(jnp.int32, (1, LANES), 1) < HEAD_DIM
    pack_row = lax.broadcasted_iota(jnp.int32, (BF16_ROWS, A_SLAB), 0)
    nblk = rows // BLOCK

    def drop_first_row(t):
        top = jnp.where(pack_row == 0, jnp.zeros((BF16_ROWS, A_SLAB), BF16), t[0:BF16_ROWS])
        return jnp.concatenate([top, t[BF16_ROWS:]], axis=0)

    for w in range(slabs):
        cs = slice(w * A_SLAB, (w + 1) * A_SLAB)
        biases = {False: [bias] * A_HEADS, True: [bias_first] * A_HEADS}
        if has_sink:
            biases = {first: [jnp.where(col == 0, sink_ref[w * A_HEADS + hh] * LOG2E, b[hh])
                              for hh in range(A_HEADS)] for first, b in biases.items()}
        for jb in range(nblk):
            q = q_ref[jb * BLOCK:(jb + 1) * BLOCK, cs]
            if jb == 0:
                k2 = jnp.concatenate([kp_ref[:, cs], kc_ref[0:BLOCK, cs]], axis=0)
                v2 = jnp.concatenate([vp_ref[:, cs], vc_ref[0:BLOCK, cs]], axis=0)
            else:
                k2 = kc_ref[(jb - 1) * BLOCK:(jb + 1) * BLOCK, cs]
                v2 = vc_ref[(jb - 1) * BLOCK:(jb + 1) * BLOCK, cs]
            if has_sink:
                k2 = drop_first_row(k2)
                v2 = drop_first_row(v2)
            qs = jnp.concatenate([q * head_mask[hh] for hh in range(A_HEADS)], axis=0)
            s = lax.dot_general(qs, k2, (((1,), (1,)), ((), ())), preferred_element_type=F32)
            ps, ms, ls = [], [], []
            for hh in range(A_HEADS):
                sh = s[hh * BLOCK:(hh + 1) * BLOCK] + biases[jb == 0][hh]
                m = jnp.max(sh, axis=-1, keepdims=True)
                p = jnp.exp2(sh - m)
                ms.append(m)
                ls.append(jnp.sum(p, axis=-1, keepdims=True))
                ps.append(p.astype(BF16))
            ost = jnp.dot(jnp.concatenate(ps, axis=0), v2, preferred_element_type=F32)
            o_tiles, lse_tiles = [], []
            for t0 in range(0, A_HEADS, HEADS_PER_TILE):
                lanes = slice(t0 * HEAD_DIM, (t0 + HEADS_PER_TILE) * HEAD_DIM)
                per_head = [ost[hh * BLOCK:(hh + 1) * BLOCK, lanes] * (1.0 / ls[hh])
                            for hh in range(t0, t0 + HEADS_PER_TILE)]
                o_tiles.append(jnp.where(first_head_lanes, per_head[0], per_head[1]))
                if with_lse:
                    lse_tiles.append(jnp.where(first_head_lanes, ms[t0] + jnp.log2(ls[t0]),
                                               ms[t0 + 1] + jnp.log2(ls[t0 + 1])))
            o_ref[jb * BLOCK:(jb + 1) * BLOCK, cs] = jnp.concatenate(o_tiles, axis=1).astype(o_ref.dtype)
            if with_lse:
                lse_ref[jb * BLOCK:(jb + 1) * BLOCK, cs] = jnp.concatenate(lse_tiles, axis=1)


def _band_attn(q, k, v, *, name, rows, slabs, prev_off, sinks=None, with_lse=True):
    b, length, width = q.shape
    assert sinks is None or prev_off >= 1
    n_col = width // (slabs * A_SLAB)
    n_row = length // rows
    blk_per_step = rows // BLOCK
    cur = pl.BlockSpec((None, rows, slabs * A_SLAB), lambda i, r, j: (i, j, r))
    prev = pl.BlockSpec((None, BLOCK, slabs * A_SLAB),
                        lambda i, r, j: (i, jnp.maximum(j * blk_per_step - 1, 0), r))
    in_specs = [cur, cur, prev, cur, prev]
    args = [q, k, k, v, v]
    if sinks is not None:
        in_specs.append(pl.BlockSpec(memory_space=pltpu.SMEM))
        args.append(sinks)
    out_specs = [cur]
    out_shape = [jax.ShapeDtypeStruct((b, length, width), BF16)]
    if with_lse:
        out_specs.append(cur)
        out_shape.append(jax.ShapeDtypeStruct((b, length, width), F32))
    return pl.pallas_call(
        functools.partial(_band_attn_kernel, rows=rows, slabs=slabs, prev_off=prev_off,
                          has_sink=sinks is not None, with_lse=with_lse),
        grid=(b, n_col, n_row),
        in_specs=in_specs,
        out_specs=out_specs,
        out_shape=out_shape,
        compiler_params=pltpu.CompilerParams(dimension_semantics=("arbitrary",) * 3,
                                             vmem_limit_bytes=VMEM_LIMIT),
        name=name,
    )(*args)


def _mix_kernel(x_ref, gain_ref, oa0, la0, oa1, la1, oa2, la2, ob_ref, mq_ref, mk_ref, mv_ref,
                wg_ref, bg_ref, woa_ref, wob_ref, wom_ref, wout_ref, out_ref, *stage_refs):
    x = x_ref[...]
    h = _rms_rows(x, gain_ref[...]).astype(BF16)
    stage_refs = list(stage_refs)

    def token_major(blk_ref, dil):
        if dil == 1:
            return blk_ref[...].astype(F32)
        stage_ref = stage_refs.pop()
        rows = blk_ref.shape[0]
        for r in range(dil):
            for s in range(A_SLAB // LANES):
                c0 = r * A_SLAB + s * LANES
                stage_ref[s, pl.ds(r, rows, stride=dil), :] = blk_ref[:, c0:c0 + LANES].astype(F32)
        return jnp.concatenate([stage_ref[s] for s in range(A_SLAB // LANES)], axis=1)

    head_cols = [slice(hh * M_HEAD_DIM, (hh + 1) * M_HEAD_DIM) for hh in range(M_HEADS)]
    scores = [lax.dot_general(mq_ref[:, cs], mk_ref[:, cs], (((1,), (1,)), ((), ())),
                              preferred_element_type=F32) for cs in head_cols]
    gate_cols = [slice(i * D_MODEL, (i + 1) * D_MODEL) for i in range(N_BRANCH)]
    gates = [1.0 + jnp.tanh(jnp.dot(h, wg_ref[:, cs], preferred_element_type=F32) + 0.5 * bg_ref[:, cs])
             for cs in gate_cols]

    dils = [dil for _, dil in A_GROUPS]
    l0, l1, l2 = (token_major(r, d) for r, d in zip((la0, la1, la2), dils))
    o0, o1, o2 = (token_major(r, d) for r, d in zip((oa0, oa1, oa2), dils))
    mx = jnp.maximum(jnp.maximum(l0, l1), l2)
    e0, e1, e2 = jnp.exp2(l0 - mx), jnp.exp2(l1 - mx), jnp.exp2(l2 - mx)
    o_a = (e0 * o0 + e1 * o1 + e2 * o2) / (e0 + e1 + e2)

    o_m = []
    for s, cs in zip(scores, head_cols):
        p = jnp.exp2(s - jnp.max(s, axis=-1, keepdims=True))
        l = jnp.sum(p, axis=-1, keepdims=True)
        o_m.append(jnp.dot(p.astype(BF16), mv_ref[:, cs], preferred_element_type=F32) / l)
    o_m = jnp.concatenate(o_m, axis=1)

    merged = jnp.zeros(x.shape, F32)
    for gate, o, w_ref in zip(gates, (o_a, ob_ref[...], o_m), (woa_ref, wob_ref, wom_ref)):
        merged = merged + gate * jnp.dot(o.astype(BF16), w_ref[...], preferred_element_type=F32)
    out_ref[...] = x + jnp.dot(merged.astype(BF16), wout_ref[...], preferred_element_type=F32)


def _mix(x2, attn_gain, oa, la, ob, mq, mk, mv, w_gate, b_gate, w_o_a, w_o_b, w_o_m, w_out, seq):
    t = x2.shape[0]
    tb = TOKEN_BLOCK
    mem_len = mk.shape[1]
    per_seq = seq // tb
    tok = lambda w, d=1: pl.BlockSpec((tb // d, w * d), lambda i: (i, 0))
    mem = pl.BlockSpec((None, mem_len, M_Q_COLS), lambda i: (i // per_seq, 0, 0))
    dils = [dil for _, dil in A_GROUPS]
    return pl.pallas_call(
        _mix_kernel,
        grid=(t // tb,),
        in_specs=[tok(D_MODEL), _const_spec((1, D_MODEL))]
                 + [tok(A_SLAB, d) for d in dils for _ in range(2)]
                 + [tok(B_Q_COLS), tok(M_Q_COLS), mem, mem,
                  _const_spec((D_MODEL, N_BRANCH * D_MODEL)), _const_spec((1, N_BRANCH * D_MODEL)),
                  _const_spec((A_SLAB, D_MODEL)), _const_spec((B_Q_COLS, D_MODEL)),
                  _const_spec((M_Q_COLS, D_MODEL)), _const_spec((D_MODEL, D_MODEL))],
        out_specs=tok(D_MODEL),
        out_shape=jax.ShapeDtypeStruct((t, D_MODEL), F32),
        scratch_shapes=[pltpu.VMEM((A_SLAB // LANES, tb, LANES), F32)] * (2 * sum(d > 1 for d in dils)),
        compiler_params=pltpu.CompilerParams(dimension_semantics=("arbitrary",),
                                             vmem_limit_bytes=VMEM_LIMIT),
        name="mix",
    )(x2, attn_gain, oa[0], la[0], oa[1], la[1], oa[2], la[2], ob, mq, mk, mv,
      w_gate, b_gate, w_o_a, w_o_b, w_o_m, w_out)


def _conv_ffn_kernel(x_ref, gain_ref, wup_ref, cw_ref, cb_ref, wdown_ref, out_ref,
                     carry_ref, ext_ref, act_ref, *, per_seq):
    tb = x_ref.shape[0]
    halo = CONV_WIDTH - 1
    pad = SUBLANES

    @pl.when(pl.program_id(0) % per_seq == 0)
    def _():
        carry_ref[...] = jnp.zeros(carry_ref.shape, F32)

    x = x_ref[...]
    h = _rms_rows(x, gain_ref[...]).astype(BF16)

    def conv(c0, scale=1.0):
        cs = slice(c0, c0 + FF_CHUNK)
        u = jnp.dot(h, wup_ref[:, cs], preferred_element_type=F32)
        ext_ref[0:pad, :] = carry_ref[:, cs]
        ext_ref[pad:pad + tb, :] = u
        carry_ref[:, cs] = u[tb - pad:tb, :]
        taps = cw_ref[:, cs] * scale
        c = cb_ref[:, cs] * scale + taps[CONV_WIDTH - 1:CONV_WIDTH] * u
        for j in range(halo):
            c = c + taps[j:j + 1] * ext_ref[pad - halo + j:pad - halo + j + tb, :]
        return c

    for ch in range(D_FF // FF_CHUNK):
        half = conv(ch * FF_CHUNK, 0.5)
        g = conv(D_FF + ch * FF_CHUNK)
        act = (half + half * jnp.tanh(half)) * g
        act_ref[:, ch * FF_CHUNK:(ch + 1) * FF_CHUNK] = act.astype(BF16)
    out_ref[...] = x + jnp.dot(act_ref[...], wdown_ref[...], preferred_element_type=F32)


def _conv_ffn(x2, gain, w_up, conv_w, conv_b, w_down, seq):
    t = x2.shape[0]
    tb = TOKEN_BLOCK
    return pl.pallas_call(
        functools.partial(_conv_ffn_kernel, per_seq=seq // tb),
        grid=(t // tb,),
        in_specs=[pl.BlockSpec((tb, D_MODEL), lambda i: (i, 0)),
                  _const_spec((1, D_MODEL)),
                  _const_spec((D_MODEL, 2 * D_FF)),
                  _const_spec((CONV_WIDTH, 2 * D_FF)),
                  _const_spec((1, 2 * D_FF)),
                  _const_spec((D_FF, D_MODEL))],
        out_specs=pl.BlockSpec((tb, D_MODEL), lambda i: (i, 0)),
        out_shape=jax.ShapeDtypeStruct((t, D_MODEL), F32),
        scratch_shapes=[pltpu.VMEM((SUBLANES, 2 * D_FF), F32),
                        pltpu.VMEM((tb + SUBLANES, FF_CHUNK), F32),
                        pltpu.VMEM((tb, D_FF), BF16)],
        compiler_params=pltpu.CompilerParams(dimension_semantics=("arbitrary",),
                                             vmem_limit_bytes=VMEM_LIMIT),
        name="conv_ffn",
    )(x2, gain, w_up, conv_w, conv_b, w_down)


def _tables():
    lane = np.arange(A_SLAB)
    bd64 = jnp.asarray(lane[:, None] // HEAD_DIM == lane[None, :] // HEAD_DIM, BF16)
    bd128 = jnp.asarray(lane[:, None] // M_HEAD_DIM == lane[None, :] // M_HEAD_DIM, BF16)
    freq_col = jnp.exp(jnp.arange(ROPE_HALF, dtype=F32) * (-2.0 * math.log(ROPE_THETA) / ROPE_DIM))
    freq_col = freq_col.reshape(ROPE_HALF, 1)
    in_head = np.arange(LANES) % HEAD_DIM
    j = np.arange(LANES)[:, None]
    rot = in_head[None, :] < ROPE_DIM
    cos_sel = rot & (j == in_head[None, :] % ROPE_HALF)
    sin_sel = rot & (j == ROPE_HALF + in_head[None, :] % ROPE_HALF)
    lo = in_head[None, :] < ROPE_HALF
    spread = np.concatenate([cos_sel.astype(np.float32), -(sin_sel & lo).astype(np.float32),
                             (sin_sel & ~lo).astype(np.float32)], axis=1)
    one_row = (in_head >= ROPE_DIM).astype(np.float32).reshape(1, LANES)
    return bd64, bd128, freq_col, jnp.asarray(spread, BF16), jnp.asarray(one_row)


def _layer(x2, mem, pos_rows, batch, seq, p):
    bd64, bd128, freq_col, spread, one_row = _tables()
    n_groups = len(A_GROUPS)
    gains = jnp.concatenate([p["a_q_norm"], p["a_k_norm"], p["b_q_norm"][None], p["b_k_norm"][None],
                             p["m_q_norm"].reshape(M_HEAD_DIM // HEAD_DIM, HEAD_DIM),
                             jnp.ones((1, HEAD_DIM), F32)])
    row_bq, row_bk, row_mq = 2 * n_groups, 2 * n_groups + 1, 2 * n_groups + 2
    row_one = row_mq + M_HEAD_DIM // HEAD_DIM
    qk_scale = HEAD_DIM ** -0.5 * LOG2E
    src, scale = [], []
    for g in range(n_groups):
        src += [g] * A_HEADS + [n_groups + g] * A_HEADS + [row_one] * A_HEADS
        scale += [qk_scale] * A_HEADS + [1.0] * (2 * A_HEADS)
    src += [row_bq] * B_Q_HEADS + [row_bk] * B_KV_HEADS + [row_one] * B_KV_HEADS
    scale += [qk_scale] * B_Q_HEADS + [1.0] * (2 * B_KV_HEADS)
    src += [row_mq + i for i in range(M_HEAD_DIM // HEAD_DIM)] * M_HEADS
    scale += [M_HEAD_DIM ** -0.5 * LOG2E] * (M_Q_COLS // HEAD_DIM)
    colgain = (gains[np.asarray(src)] * np.asarray(scale, np.float32)[:, None]).reshape(1, IN_COLS)

    attn_gain = p["attn_norm"].reshape(1, D_MODEL)
    mk, mv, w_in = _mem_kv(mem, p["mem_norm"].reshape(1, D_MODEL), p["w_mem_kv"],
                           p["m_k_norm"].reshape(1, M_HEAD_DIM), p["w_in"])
    later_weights = ("w_gate", "w_o_a", "w_o_b", "w_o_m", "w_out", "w_up", "w_down")
    scales = [0.5 if name in ("w_gate", "w_out") else 1.0 for name in later_weights]
    outs = _in_proj(x2, pos_rows, attn_gain, w_in, colgain, freq_col, spread, one_row,
                    bd64, bd128, [p[name] for name in later_weights], scales)
    (qa0, ka0, va0, qa1, ka1, va1, qa2, ka2, va2, qb, kb, vb, mq) = outs[:13]
    wb = dict(zip(later_weights, outs[13:]))

    def per_batch(t):
        return t.reshape(batch, t.shape[0] // batch, t.shape[-1])

    def flat(t):
        return t.reshape(t.shape[0] * t.shape[1], t.shape[2])

    oa, la = [], []
    cfg = {1: dict(rows=seq, slabs=1), 4: dict(rows=seq // 4, slabs=4), 16: dict(rows=seq // 16, slabs=16)}
    for (window, dil), (q, k, v) in zip(A_GROUPS, ((qa0, ka0, va0), (qa1, ka1, va1), (qa2, ka2, va2))):
        assert window // dil == BLOCK
        o, lse = _band_attn(per_batch(q), per_batch(k), per_batch(v), name=f"band_attn_d{dil}",
                            prev_off=0, **cfg[dil])
        oa.append(flat(o))
        la.append(flat(lse))
    (ob,) = _band_attn(per_batch(qb), per_batch(kb), per_batch(vb), name="band_attn_swa", rows=2048,
                       slabs=B_KV_HEADS, prev_off=BLOCK - (B_WINDOW - 1), sinks=p["b_sinks"],
                       with_lse=False)
    ob = flat(ob)

    x2 = _mix(x2, attn_gain, oa, la, ob, mq, mk, mv, wb["w_gate"],
              p["b_gate"].reshape(1, N_BRANCH * D_MODEL), wb["w_o_a"], wb["w_o_b"], wb["w_o_m"],
              wb["w_out"], seq)
    return _conv_ffn(x2, p["ffn_norm"].reshape(1, D_MODEL), wb["w_up"], p["conv_w"],
                     p["conv_b"].reshape(1, 2 * D_FF), wb["w_down"], seq)


def kernel(x, mem, positions, attn_norm, w_in, a_q_norm, a_k_norm, b_q_norm, b_k_norm, b_sinks,
           mem_norm, w_mem_kv, m_q_norm, m_k_norm, w_o_a, w_o_b, w_o_m, w_gate, b_gate, w_out,
           ffn_norm, w_up, conv_w, conv_b, w_down):
    batch, seq, _ = x.shape
    params = dict(attn_norm=attn_norm, w_in=w_in, a_q_norm=a_q_norm, a_k_norm=a_k_norm,
                  b_q_norm=b_q_norm, b_k_norm=b_k_norm, b_sinks=b_sinks, mem_norm=mem_norm,
                  w_mem_kv=w_mem_kv, m_q_norm=m_q_norm, m_k_norm=m_k_norm, w_o_a=w_o_a, w_o_b=w_o_b,
                  w_o_m=w_o_m, w_gate=w_gate, b_gate=b_gate, w_out=w_out, ffn_norm=ffn_norm,
                  w_up=w_up, conv_w=conv_w, conv_b=conv_b, w_down=w_down)
    pos_rows = positions.astype(F32).reshape(batch * seq // TOKEN_BLOCK, 1, TOKEN_BLOCK)
    x2 = x.reshape(batch * seq, D_MODEL)
    for layer in range(attn_norm.shape[0]):
        x2 = _layer(x2, mem, pos_rows, batch, seq, {k: v[layer] for k, v in params.items()})
    return x2.reshape(batch, seq, D_MODEL)
```

```python
import functools
import math

import numpy as np

import jax
import jax.numpy as jnp
from jax import lax
from jax.experimental import pallas as pl
from jax.experimental.pallas import tpu as pltpu

D_MODEL = 1024
HEAD_DIM = 64
A_GROUPS = ((128, 1), (512, 4), (2048, 16))
A_HEADS = 4
A_SLAB = A_HEADS * HEAD_DIM
A_QKV_COLS = len(A_GROUPS) * 3 * A_SLAB
B_Q_HEADS = 8
B_KV_HEADS = 2
B_GROUP = B_Q_HEADS // B_KV_HEADS
B_WINDOW = 128
B_Q_COLS = B_Q_HEADS * HEAD_DIM
B_KV_COLS = B_KV_HEADS * HEAD_DIM
M_HEADS = 4
M_HEAD_DIM = 128
M_Q_COLS = M_HEADS * M_HEAD_DIM
B_Q_OFF = A_QKV_COLS
B_K_OFF = B_Q_OFF + B_Q_COLS
B_V_OFF = B_K_OFF + B_KV_COLS
M_Q_OFF = B_V_OFF + B_KV_COLS
IN_COLS = M_Q_OFF + M_Q_COLS
N_NORMED_TILES = 2 * len(A_GROUPS) + B_Q_COLS // A_SLAB + 1 + M_Q_COLS // A_SLAB
N_BRANCH = 3
D_FF = 2816
CONV_WIDTH = 3
ROPE_THETA = 500000.0
ROPE_DIM = HEAD_DIM // 4
ROPE_HALF = ROPE_DIM // 2
BLOCK = 128
EPS = 1e-6
NEG = -1e30
LOG2E = math.log2(math.e)

LANES = 128
SUBLANES = 8
BF16_ROWS = 16
HEADS_PER_TILE = LANES // HEAD_DIM
TOKEN_BLOCK = 512
ROW_CHUNK = 32
ATTN_STEPS = 4
FF_CHUNK = 256
VMEM_LIMIT = 56 * 1024 * 1024

F32 = jnp.float32
BF16 = jnp.bfloat16

assert all(window // dil == BLOCK for window, dil in A_GROUPS) and B_WINDOW <= BLOCK
assert B_KV_COLS == LANES and B_KV_HEADS == HEADS_PER_TILE and B_GROUP % HEADS_PER_TILE == 0
assert M_HEAD_DIM == LANES and D_FF % FF_CHUNK == 0 and CONV_WIDTH - 1 <= SUBLANES


def _const_spec(shape):
    return pl.BlockSpec(shape, lambda *_: (0,) * len(shape), pipeline_mode=pl.Buffered(1))


def _rms_rows(x, gain):
    ms = jnp.mean(x * x, axis=-1, keepdims=True)
    return x * lax.rsqrt(ms + EPS) * gain


def _mem_kv_kernel(mem_ref, gain_ref, w_ref, kgain_ref, win_ref, mk_ref, mv_ref, win_out_ref):
    win_out_ref[...] = win_ref[...].astype(BF16)
    hm = _rms_rows(mem_ref[...], gain_ref[...]).astype(BF16)
    kv = jnp.dot(hm, w_ref[...].astype(BF16), preferred_element_type=F32)
    ks = []
    for h in range(M_HEADS):
        kh = kv[:, h * M_HEAD_DIM:(h + 1) * M_HEAD_DIM]
        ks.append(_rms_rows(kh, kgain_ref[...]))
    mk_ref[...] = jnp.concatenate(ks, axis=1).astype(BF16)
    mv_ref[...] = kv[:, M_Q_COLS:].astype(BF16)


def _mem_kv(mem, mem_gain, w_kv, k_gain, w_in):
    b, m, _ = mem.shape
    slab = pl.BlockSpec((_cast_rows(w_in.shape[0], b), w_in.shape[1]), lambda i: (i, 0))
    assert w_in.shape[0] == b * slab.block_shape[0]
    return pl.pallas_call(
        _mem_kv_kernel,
        grid=(b,),
        in_specs=[pl.BlockSpec((None, m, D_MODEL), lambda i: (i, 0, 0)),
                  _const_spec((1, D_MODEL)),
                  _const_spec((D_MODEL, 2 * M_Q_COLS)),
                  _const_spec((1, M_HEAD_DIM)),
                  slab],
        out_specs=[pl.BlockSpec((None, m, M_Q_COLS), lambda i: (i, 0, 0)),
                   pl.BlockSpec((None, m, M_Q_COLS), lambda i: (i, 0, 0)),
                   slab],
        out_shape=[jax.ShapeDtypeStruct((b, m, M_Q_COLS), BF16)] * 2
                  + [jax.ShapeDtypeStruct(w_in.shape, BF16)],
        compiler_params=pltpu.CompilerParams(dimension_semantics=("arbitrary",),
                                             vmem_limit_bytes=VMEM_LIMIT),
        name="mem_kv",
    )(mem, mem_gain, w_kv, k_gain, w_in)


def _in_proj_kernel(x_ref, pos_ref, gain_ref, w_ref, colgain_ref, freq_ref, spread_ref, one_ref,
                    bd64_ref, bd128_ref, *refs, cast_scales):
    n_cast = len(cast_scales)
    cast_in, refs = refs[:n_cast], refs[n_cast:]
    (qa0, ka0, va0, qa1, ka1, va1, qa2, ka2, va2, qb_ref, kb_ref, vb_ref, mq_ref) = refs[:13]
    cast_out, refs = refs[13:13 + n_cast], refs[13 + n_cast:]
    proj_ref, ss_ref, h_ref, cos_ref, sin_lo_ref, sin_hi_ref = refs[:6]
    stage_refs = list(refs[6:])
    tb = x_ref.shape[0]

    for src, dst, scale in zip(cast_in, cast_out, cast_scales):
        w = src[...]
        dst[...] = (w if scale == 1.0 else w * scale).astype(BF16)
    chunks = [slice(r, r + ROW_CHUNK) for r in range(0, tb, ROW_CHUNK)]

    @pl.when(pl.program_id(0) == 0)
    def _():
        proj_ref[...] = jnp.zeros(proj_ref.shape, F32)
        ss_ref[...] = jnp.zeros(ss_ref.shape, F32)

    for rows in chunks:
        h_ref[rows, :] = _rms_rows(x_ref[rows, :], gain_ref[...]).astype(BF16)

    ang = freq_ref[...] * pos_ref[...]
    trig = jnp.concatenate([jnp.cos(ang), jnp.sin(ang),
                            jnp.zeros((LANES - ROPE_DIM, tb), F32)], axis=0).T
    tables = jnp.zeros((tb, 3 * LANES), F32)
    for _ in range(3):
        part = trig.astype(BF16)
        tables = tables + jnp.dot(part, spread_ref[...], preferred_element_type=F32)
        trig = trig - part.astype(F32)
    cos_ref[...] = tables[:, 0:LANES] + one_ref[...]
    sin_lo_ref[...] = tables[:, LANES:2 * LANES]
    sin_hi_ref[...] = tables[:, 2 * LANES:3 * LANES]

    ss_slots = iter(range(ss_ref.shape[0]))
    pending = []

    def flush_pending():
        while pending:
            new, bd, slot = pending.pop()
            ss_ref[slot, :, 0:new.shape[1]] = jnp.dot((new * new).astype(BF16), bd,
                                                      preferred_element_type=F32)

    def proj(c0, width, bd=None):
        tile, off = divmod(c0, A_SLAB)
        assert off + width <= A_SLAB and off % LANES == 0
        t = proj_ref[tile, :, off:off + width]
        ss = None
        if bd is not None:
            slot = next(ss_slots)
            ss = ss_ref[slot, :, 0:width]
        new = jnp.dot(h_ref[...], w_ref[:, c0:c0 + width], preferred_element_type=F32)
        proj_ref[tile, :, off:off + width] = new
        flush_pending()
        if bd is not None:
            pending.append((new, bd, slot))
        return t, ss

    def rope(y, rows):
        parts = []
        for c in range(y.shape[1] // LANES):
            yc = y[:, c * LANES:(c + 1) * LANES]
            parts.append(yc * cos_ref[rows, :]
                         + pltpu.roll(yc, LANES - ROPE_HALF, 1) * sin_lo_ref[rows, :]
                         + pltpu.roll(yc, ROPE_HALF, 1) * sin_hi_ref[rows, :])
        return parts[0] if len(parts) == 1 else jnp.concatenate(parts, axis=1)

    def finish(t_ss, emit, c0, *, dim=None, rotary=False):
        t, ss = t_ss
        if ss is not None:
            gain = colgain_ref[:, c0:c0 + t.shape[1]]
        for rows in chunks:
            y = t[rows]
            if ss is not None:
                y = y * lax.rsqrt(ss[rows] * (1.0 / dim) + EPS) * gain
            if rotary:
                y = rope(y, rows)
            emit(rows, y)

    def to_ref(out_ref, col0=0):
        def emit(rows, y):
            out_ref[rows, col0:col0 + y.shape[1]] = y.astype(BF16)
        return emit

    def store_tile(out_ref, t, dil, c0, **kw):
        if dil == 1:
            finish(t, to_ref(out_ref), c0, **kw)
            return
        stage_ref = stage_refs.pop()

        def emit(rows, y):
            for s in range(A_SLAB // LANES):
                stage_ref[s, rows, :] = y[:, s * LANES:(s + 1) * LANES]
        finish(t, emit, c0, **kw)
        for r in range(dil):
            for s in range(A_SLAB // LANES):
                col = r * A_SLAB + s * LANES
                out_ref[:, col:col + LANES] = stage_ref[s, pl.ds(r, tb // dil, stride=dil), :].astype(BF16)

    bd64 = bd64_ref[...]
    qk = dict(dim=HEAD_DIM, rotary=True)
    for g, (q_ref, k_ref, v_ref) in enumerate(((qa0, ka0, va0), (qa1, ka1, va1), (qa2, ka2, va2))):
        c0 = g * 3 * A_SLAB
        dil = A_GROUPS[g][1]
        store_tile(q_ref, proj(c0, A_SLAB, bd64), dil, c0, **qk)
        store_tile(k_ref, proj(c0 + A_SLAB, A_SLAB, bd64), dil, c0 + A_SLAB, **qk)
        store_tile(v_ref, proj(c0 + 2 * A_SLAB, A_SLAB), dil, c0 + 2 * A_SLAB)

    for s in range(B_Q_COLS // A_SLAB):
        c0 = B_Q_OFF + s * A_SLAB
        finish(proj(c0, A_SLAB, bd64), to_ref(qb_ref, s * A_SLAB), c0, **qk)

    def under_query_heads(out_ref):
        def emit(rows, y):
            swapped = pltpu.roll(y, HEAD_DIM, 1)
            first = jnp.where(first_head_lanes, y, swapped).astype(BF16)
            second = jnp.where(first_head_lanes, swapped, y).astype(BF16)
            reps = B_GROUP // HEADS_PER_TILE
            out_ref[rows, :] = jnp.concatenate([first] * reps + [second] * reps, axis=1)
        return emit
    first_head_lanes = lax.broadcasted_iota(jnp.int32, (1, LANES), 1) < HEAD_DIM
    finish(proj(B_K_OFF, B_KV_COLS, bd64_ref[0:B_KV_COLS, 0:B_KV_COLS]), under_query_heads(kb_ref),
           B_K_OFF, **qk)
    finish(proj(B_V_OFF, B_KV_COLS), under_query_heads(vb_ref), B_V_OFF)

    bd128 = bd128_ref[...]
    for s in range(M_Q_COLS // A_SLAB):
        c0 = M_Q_OFF + s * A_SLAB
        finish(proj(c0, A_SLAB, bd128), to_ref(mq_ref, s * A_SLAB), c0, dim=M_HEAD_DIM)
    flush_pending()


def _cast_rows(rows, n_steps):
    for per_step in range(BF16_ROWS, rows + 1, BF16_ROWS):
        if rows % per_step == 0 and rows // per_step <= n_steps:
            return per_step
    raise ValueError(f"no bf16-aligned split of {rows} rows over {n_steps} steps")


def _in_proj(x2, pos_rows, attn_gain, w_in, colgain, freq_col, spread, one_row, bd64, bd128,
             cast_weights, cast_scales):
    t = x2.shape[0]
    tb = TOKEN_BLOCK
    dils = [dil for _, dil in A_GROUPS for _ in range(3)] + [1, 1, 1, 1]
    widths = [A_SLAB] * 9 + [B_Q_COLS, B_Q_COLS, B_Q_COLS, M_Q_COLS]
    n_staged = sum(d > 1 for d in dils)
    n = t // tb
    done = lambda i: (jnp.maximum(i - 1, 0), 0)

    def cast_spec(w):
        per_step = _cast_rows(w.shape[0], n + 1)
        last = w.shape[0] // per_step - 1
        return pl.BlockSpec((per_step, w.shape[1]), lambda i: (jnp.minimum(i, last), 0))

    cast_specs = [cast_spec(w) for w in cast_weights]
    return pl.pallas_call(
        functools.partial(_in_proj_kernel, cast_scales=tuple(cast_scales)),
        grid=(n + 1,),
        in_specs=[pl.BlockSpec((tb, D_MODEL), lambda i: (jnp.minimum(i, n - 1), 0)),
                  pl.BlockSpec((None, 1, tb), lambda i: (jnp.maximum(i - 1, 0), 0, 0)),
                  _const_spec((1, D_MODEL)),
                  _const_spec((D_MODEL, IN_COLS)),
                  _const_spec((1, IN_COLS)),
                  _const_spec((ROPE_HALF, 1)),
                  _const_spec((LANES, 3 * LANES)),
                  _const_spec((1, LANES)),
                  _const_spec((A_SLAB, A_SLAB)),
                  _const_spec((A_SLAB, A_SLAB))] + cast_specs,
        out_specs=[pl.BlockSpec((tb // d, w * d), done) for w, d in zip(widths, dils)] + cast_specs,
        out_shape=[jax.ShapeDtypeStruct((t // d, w * d), BF16) for w, d in zip(widths, dils)]
                  + [jax.ShapeDtypeStruct(w.shape, BF16) for w in cast_weights],
        scratch_shapes=[pltpu.VMEM((IN_COLS // A_SLAB, tb, A_SLAB), F32),
                        pltpu.VMEM((N_NORMED_TILES, tb, A_SLAB), F32)]
                       + [pltpu.VMEM((tb, D_MODEL), BF16)] + [pltpu.VMEM((tb, LANES), F32)] * 3
                       + [pltpu.VMEM((A_SLAB // LANES, tb, LANES), F32)] * n_staged,
        compiler_params=pltpu.CompilerParams(dimension_semantics=("arbitrary",),
                                             vmem_limit_bytes=VMEM_LIMIT),
        name="in_proj",
    )(x2, pos_rows, attn_gain, w_in, colgain, freq_col, spread, one_row, bd64, bd128, *cast_weights)


def _band_attn_kernel(*refs, kinds):
    refs = list(refs)
    ins = [[refs.pop(0) for _ in range(5 + has_sink)] for _, _, _, has_sink, _, _ in kinds]
    outs = [[refs.pop(0) for _ in range(1 + with_lse)] for _, _, _, _, with_lse, _ in kinds]
    for (rows, slabs, prev_off, has_sink, with_lse, starts), in_refs, out_refs in zip(kinds, ins, outs):
        _attend(*in_refs[:5], in_refs[5] if has_sink else None, out_refs[0],
                out_refs[1] if with_lse else None,
                first_step=True if starts else pl.program_id(1) == 0,
                rows=rows, slabs=slabs, prev_off=prev_off)


def _attend(q_ref, kc_ref, kp_ref, vc_ref, vp_ref, sink_ref, o_ref, lse_ref, *, first_step, rows, slabs,
            prev_off):
    has_sink = sink_ref is not None
    with_lse = lse_ref is not None
    row = lax.broadcasted_iota(jnp.int32, (BLOCK, 2 * BLOCK), 0)
    col = lax.broadcasted_iota(jnp.int32, (BLOCK, 2 * BLOCK), 1)
    bias = jnp.where(col < BLOCK,
                     jnp.where(col >= row + prev_off, 0.0, NEG),
                     jnp.where(col - BLOCK <= row, 0.0, NEG)).astype(F32)
    bias_first = jnp.where(jnp.logical_and(first_step, col < BLOCK), NEG, bias)
    head_of_lane = lax.broadcasted_iota(jnp.int32, (1, A_SLAB), 1) // HEAD_DIM
    head_mask = [(head_of_lane == hh).astype(BF16) for hh in range(A_HEADS)]
    first_head_lanes = lax.broadcasted_iota(jnp.int32, (1, LANES), 1) < HEAD_DIM
    pack_row = lax.broadcasted_iota(jnp.int32, (BF16_ROWS, A_SLAB), 0)
    nblk = rows // BLOCK

    def drop_first_row(t):
        top = jnp.where(pack_row == 0, jnp.zeros((BF16_ROWS, A_SLAB), BF16), t[0:BF16_ROWS])
        return jnp.concatenate([top, t[BF16_ROWS:]], axis=0)

    for w in range(slabs):
        cs = slice(w * A_SLAB, (w + 1) * A_SLAB)
        biases = {False: [bias] * A_HEADS, True: [bias_first] * A_HEADS}
        if has_sink:
            biases = {first: [jnp.where(col == 0, sink_ref[w * A_HEADS + hh] * LOG2E, b[hh])
                              for hh in range(A_HEADS)] for first, b in biases.items()}
        for jb in range(nblk):
            q = q_ref[jb * BLOCK:(jb + 1) * BLOCK, cs]
            if jb == 0:
                k2 = jnp.concatenate([kp_ref[:, cs], kc_ref[0:BLOCK, cs]], axis=0)
                v2 = jnp.concatenate([vp_ref[:, cs], vc_ref[0:BLOCK, cs]], axis=0)
            else:
                k2 = kc_ref[(jb - 1) * BLOCK:(jb + 1) * BLOCK, cs]
                v2 = vc_ref[(jb - 1) * BLOCK:(jb + 1) * BLOCK, cs]
            if has_sink:
                k2 = drop_first_row(k2)
                v2 = drop_first_row(v2)
            qs = jnp.concatenate([q * head_mask[hh] for hh in range(A_HEADS)], axis=0)
            s = lax.dot_general(qs, k2, (((1,), (1,)), ((), ())), preferred_element_type=F32)
            ps, ms, ls = [], [], []
            for hh in range(A_HEADS):
                sh = s[hh * BLOCK:(hh + 1) * BLOCK] + biases[jb == 0][hh]
                m = jnp.max(sh, axis=-1, keepdims=True)
                p = jnp.exp2(sh - m)
                ms.append(m)
                ls.append(jnp.sum(p, axis=-1, keepdims=True))
                ps.append(p.astype(BF16))
            ost = jnp.dot(jnp.concatenate(ps, axis=0), v2, preferred_element_type=F32)
            o_tiles, lse_tiles = [], []
            for t0 in range(0, A_HEADS, HEADS_PER_TILE):
                lanes = slice(t0 * HEAD_DIM, (t0 + HEADS_PER_TILE) * HEAD_DIM)
                per_head = [ost[hh * BLOCK:(hh + 1) * BLOCK, lanes] * (1.0 / ls[hh])
                            for hh in range(t0, t0 + HEADS_PER_TILE)]
                o_tiles.append(jnp.where(first_head_lanes, per_head[0], per_head[1]))
                if with_lse:
                    lse_tiles.append(jnp.where(first_head_lanes, ms[t0] + jnp.log2(ls[t0]),
                                               ms[t0 + 1] + jnp.log2(ls[t0 + 1])))
            o_ref[jb * BLOCK:(jb + 1) * BLOCK, cs] = jnp.concatenate(o_tiles, axis=1).astype(o_ref.dtype)
            if with_lse:
                lse_ref[jb * BLOCK:(jb + 1) * BLOCK, cs] = jnp.concatenate(lse_tiles, axis=1)


def _band_attn(jobs, n_steps):
    kinds, in_specs, args, out_specs, out_shape = [], [], [], [], []
    for job in jobs:
        q = job["q"]
        b, length, width = q.shape
        sinks, with_lse = job.get("sinks"), job["with_lse"]
        assert sinks is None or job["prev_off"] >= 1
        if job["along"] == "rows":
            rows, slabs = length // n_steps, width // A_SLAB
            per_step = rows // BLOCK
            cur = pl.BlockSpec((None, rows, width), lambda i, j: (i, j, 0))
            prev = pl.BlockSpec((None, BLOCK, width),
                                lambda i, j, per_step=per_step: (i, jnp.maximum(j * per_step - 1, 0), 0))
        else:
            rows, slabs = length, width // A_SLAB // n_steps
            cur = pl.BlockSpec((None, rows, slabs * A_SLAB), lambda i, j: (i, 0, j))
            prev = pl.BlockSpec((None, BLOCK, slabs * A_SLAB), lambda i, j: (i, 0, j))
        kinds.append((rows, slabs, job["prev_off"], sinks is not None, with_lse, job["along"] == "slabs"))
        in_specs += [cur, cur, prev, cur, prev]
        args += [q, job["k"], job["k"], job["v"], job["v"]]
        if sinks is not None:
            in_specs.append(pl.BlockSpec(memory_space=pltpu.SMEM))
            args.append(sinks)
        out_specs.append(cur)
        out_shape.append(jax.ShapeDtypeStruct(q.shape, BF16))
        if with_lse:
            out_specs.append(cur)
            out_shape.append(jax.ShapeDtypeStruct(q.shape, F32))
    outs = list(pl.pallas_call(
        functools.partial(_band_attn_kernel, kinds=tuple(kinds)),
        grid=(jobs[0]["q"].shape[0], n_steps),
        in_specs=in_specs,
        out_specs=out_specs,
        out_shape=out_shape,
        compiler_params=pltpu.CompilerParams(dimension_semantics=("arbitrary",) * 2,
                                             vmem_limit_bytes=VMEM_LIMIT),
        name="band_attn",
    )(*args))
    return [[outs.pop(0) for _ in range(1 + job["with_lse"])] for job in jobs]


def _mix_kernel(x_ref, gain_ref, oa0, la0, oa1, la1, oa2, la2, ob_ref, mq_ref, mk_ref, mv_ref,
                wg_ref, bg_ref, woa_ref, wob_ref, wom_ref, wout_ref, out_ref, *stage_refs):
    x = x_ref[...]
    h = _rms_rows(x, gain_ref[...]).astype(BF16)
    stage_refs = list(stage_refs)

    def token_major(blk_ref, dil):
        if dil == 1:
            return blk_ref[...].astype(F32)
        stage_ref = stage_refs.pop()
        rows = blk_ref.shape[0]
        for r in range(dil):
            for s in range(A_SLAB // LANES):
                c0 = r * A_SLAB + s * LANES
                stage_ref[s, pl.ds(r, rows, stride=dil), :] = blk_ref[:, c0:c0 + LANES].astype(F32)
        return jnp.concatenate([stage_ref[s] for s in range(A_SLAB // LANES)], axis=1)

    head_cols = [slice(hh * M_HEAD_DIM, (hh + 1) * M_HEAD_DIM) for hh in range(M_HEADS)]
    scores = [lax.dot_general(mq_ref[:, cs], mk_ref[:, cs], (((1,), (1,)), ((), ())),
                              preferred_element_type=F32) for cs in head_cols]
    gate_cols = [slice(i * D_MODEL, (i + 1) * D_MODEL) for i in range(N_BRANCH)]
    gates = [1.0 + jnp.tanh(jnp.dot(h, wg_ref[:, cs], preferred_element_type=F32) + 0.5 * bg_ref[:, cs])
             for cs in gate_cols]

    dils = [dil for _, dil in A_GROUPS]
    l0, l1, l2 = (token_major(r, d) for r, d in zip((la0, la1, la2), dils))
    o0, o1, o2 = (token_major(r, d) for r, d in zip((oa0, oa1, oa2), dils))
    mx = jnp.maximum(jnp.maximum(l0, l1), l2)
    e0, e1, e2 = jnp.exp2(l0 - mx), jnp.exp2(l1 - mx), jnp.exp2(l2 - mx)
    o_a = (e0 * o0 + e1 * o1 + e2 * o2) / (e0 + e1 + e2)

    o_m = []
    for s, cs in zip(scores, head_cols):
        p = jnp.exp2(s - jnp.max(s, axis=-1, keepdims=True))
        l = jnp.sum(p, axis=-1, keepdims=True)
        o_m.append(jnp.dot(p.astype(BF16), mv_ref[:, cs], preferred_element_type=F32) / l)
    o_m = jnp.concatenate(o_m, axis=1)

    merged = jnp.zeros(x.shape, F32)
    for gate, o, w_ref in zip(gates, (o_a, ob_ref[...], o_m), (woa_ref, wob_ref, wom_ref)):
        merged = merged + gate * jnp.dot(o.astype(BF16), w_ref[...], preferred_element_type=F32)
    out_ref[...] = x + jnp.dot(merged.astype(BF16), wout_ref[...], preferred_element_type=F32)


def _mix(x2, attn_gain, oa, la, ob, mq, mk, mv, w_gate, b_gate, w_o_a, w_o_b, w_o_m, w_out, seq):
    t = x2.shape[0]
    tb = TOKEN_BLOCK
    mem_len = mk.shape[1]
    per_seq = seq // tb
    tok = lambda w, d=1: pl.BlockSpec((tb // d, w * d), lambda i: (i, 0))
    mem = pl.BlockSpec((None, mem_len, M_Q_COLS), lambda i: (i // per_seq, 0, 0))
    dils = [dil for _, dil in A_GROUPS]
    return pl.pallas_call(
        _mix_kernel,
        grid=(t // tb,),
        in_specs=[tok(D_MODEL), _const_spec((1, D_MODEL))]
                 + [tok(A_SLAB, d) for d in dils for _ in range(2)]
                 + [tok(B_Q_COLS), tok(M_Q_COLS), mem, mem,
                  _const_spec((D_MODEL, N_BRANCH * D_MODEL)), _const_spec((1, N_BRANCH * D_MODEL)),
                  _const_spec((A_SLAB, D_MODEL)), _const_spec((B_Q_COLS, D_MODEL)),
                  _const_spec((M_Q_COLS, D_MODEL)), _const_spec((D_MODEL, D_MODEL))],
        out_specs=tok(D_MODEL),
        out_shape=jax.ShapeDtypeStruct((t, D_MODEL), F32),
        scratch_shapes=[pltpu.VMEM((A_SLAB // LANES, tb, LANES), F32)] * (2 * sum(d > 1 for d in dils)),
        compiler_params=pltpu.CompilerParams(dimension_semantics=("arbitrary",),
                                             vmem_limit_bytes=VMEM_LIMIT),
        name="mix",
    )(x2, attn_gain, oa[0], la[0], oa[1], la[1], oa[2], la[2], ob, mq, mk, mv,
      w_gate, b_gate, w_o_a, w_o_b, w_o_m, w_out)


def _conv_ffn_kernel(x_ref, gain_ref, wup_ref, cw_ref, cb_ref, wdown_ref, out_ref,
                     carry_ref, ext_ref, act_ref, *, per_seq):
    tb = x_ref.shape[0]
    halo = CONV_WIDTH - 1
    pad = SUBLANES

    @pl.when(pl.program_id(0) % per_seq == 0)
    def _():
        carry_ref[...] = jnp.zeros(carry_ref.shape, F32)

    x = x_ref[...]
    h = _rms_rows(x, gain_ref[...]).astype(BF16)

    def conv(c0, scale=1.0):
        cs = slice(c0, c0 + FF_CHUNK)
        u = jnp.dot(h, wup_ref[:, cs], preferred_element_type=F32)
        ext_ref[0:pad, :] = carry_ref[:, cs]
        ext_ref[pad:pad + tb, :] = u
        carry_ref[:, cs] = u[tb - pad:tb, :]
        taps = cw_ref[:, cs] * scale
        c = cb_ref[:, cs] * scale + taps[CONV_WIDTH - 1:CONV_WIDTH] * u
        for j in range(halo):
            c = c + taps[j:j + 1] * ext_ref[pad - halo + j:pad - halo + j + tb, :]
        return c

    for ch in range(D_FF // FF_CHUNK):
        half = conv(ch * FF_CHUNK, 0.5)
        g = conv(D_FF + ch * FF_CHUNK)
        act = (half + half * jnp.tanh(half)) * g
        act_ref[:, ch * FF_CHUNK:(ch + 1) * FF_CHUNK] = act.astype(BF16)
    out_ref[...] = x + jnp.dot(act_ref[...], wdown_ref[...], preferred_element_type=F32)


def _conv_ffn(x2, gain, w_up, conv_w, conv_b, w_down, seq):
    t = x2.shape[0]
    tb = TOKEN_BLOCK
    return pl.pallas_call(
        functools.partial(_conv_ffn_kernel, per_seq=seq // tb),
        grid=(t // tb,),
        in_specs=[pl.BlockSpec((tb, D_MODEL), lambda i: (i, 0)),
                  _const_spec((1, D_MODEL)),
                  _const_spec((D_MODEL, 2 * D_FF)),
                  _const_spec((CONV_WIDTH, 2 * D_FF)),
                  _const_spec((1, 2 * D_FF)),
                  _const_spec((D_FF, D_MODEL))],
        out_specs=pl.BlockSpec((tb, D_MODEL), lambda i: (i, 0)),
        out_shape=jax.ShapeDtypeStruct((t, D_MODEL), F32),
        scratch_shapes=[pltpu.VMEM((SUBLANES, 2 * D_FF), F32),
                        pltpu.VMEM((tb + SUBLANES, FF_CHUNK), F32),
                        pltpu.VMEM((tb, D_FF), BF16)],
        compiler_params=pltpu.CompilerParams(dimension_semantics=("arbitrary",),
                                             vmem_limit_bytes=VMEM_LIMIT),
        name="conv_ffn",
    )(x2, gain, w_up, conv_w, conv_b, w_down)


def _tables():
    lane = np.arange(A_SLAB)
    bd64 = jnp.asarray(lane[:, None] // HEAD_DIM == lane[None, :] // HEAD_DIM, BF16)
    bd128 = jnp.asarray(lane[:, None] // M_HEAD_DIM == lane[None, :] // M_HEAD_DIM, BF16)
    freq_col = jnp.exp(jnp.arange(ROPE_HALF, dtype=F32) * (-2.0 * math.log(ROPE_THETA) / ROPE_DIM))
    freq_col = freq_col.reshape(ROPE_HALF, 1)
    in_head = np.arange(LANES) % HEAD_DIM
    j = np.arange(LANES)[:, None]
    rot = in_head[None, :] < ROPE_DIM
    cos_sel = rot & (j == in_head[None, :] % ROPE_HALF)
    sin_sel = rot & (j == ROPE_HALF + in_head[None, :] % ROPE_HALF)
    lo = in_head[None, :] < ROPE_HALF
    spread = np.concatenate([cos_sel.astype(np.float32), -(sin_sel & lo).astype(np.float32),
                             (sin_sel & ~lo).astype(np.float32)], axis=1)
    one_row = (in_head >= ROPE_DIM).astype(np.float32).reshape(1, LANES)
    return bd64, bd128, freq_col, jnp.asarray(spread, BF16), jnp.asarray(one_row)


def _layer(x2, mem, pos_rows, batch, seq, p):
    bd64, bd128, freq_col, spread, one_row = _tables()
    n_groups = len(A_GROUPS)
    gains = jnp.concatenate([p["a_q_norm"], p["a_k_norm"], p["b_q_norm"][None], p["b_k_norm"][None],
                             p["m_q_norm"].reshape(M_HEAD_DIM // HEAD_DIM, HEAD_DIM),
                             jnp.ones((1, HEAD_DIM), F32)])
    row_bq, row_bk, row_mq = 2 * n_groups, 2 * n_groups + 1, 2 * n_groups + 2
    row_one = row_mq + M_HEAD_DIM // HEAD_DIM
    qk_scale = HEAD_DIM ** -0.5 * LOG2E
    src, scale = [], []
    for g in range(n_groups):
        src += [g] * A_HEADS + [n_groups + g] * A_HEADS + [row_one] * A_HEADS
        scale += [qk_scale] * A_HEADS + [1.0] * (2 * A_HEADS)
    src += [row_bq] * B_Q_HEADS + [row_bk] * B_KV_HEADS + [row_one] * B_KV_HEADS
    scale += [qk_scale] * B_Q_HEADS + [1.0] * (2 * B_KV_HEADS)
    src += [row_mq + i for i in range(M_HEAD_DIM // HEAD_DIM)] * M_HEADS
    scale += [M_HEAD_DIM ** -0.5 * LOG2E] * (M_Q_COLS // HEAD_DIM)
    colgain = (gains[np.asarray(src)] * np.asarray(scale, np.float32)[:, None]).reshape(1, IN_COLS)

    attn_gain = p["attn_norm"].reshape(1, D_MODEL)
    mk, mv, w_in = _mem_kv(mem, p["mem_norm"].reshape(1, D_MODEL), p["w_mem_kv"],
                           p["m_k_norm"].reshape(1, M_HEAD_DIM), p["w_in"])
    later_weights = ("w_gate", "w_o_a", "w_o_b", "w_o_m", "w_out", "w_up", "w_down")
    scales = [0.5 if name in ("w_gate", "w_out") else 1.0 for name in later_weights]
    outs = _in_proj(x2, pos_rows, attn_gain, w_in, colgain, freq_col, spread, one_row,
                    bd64, bd128, [p[name] for name in later_weights], scales)
    (qa0, ka0, va0, qa1, ka1, va1, qa2, ka2, va2, qb, kb, vb, mq) = outs[:13]
    wb = dict(zip(later_weights, outs[13:]))

    def per_batch(t):
        return t.reshape(batch, t.shape[0] // batch, t.shape[-1])

    def flat(t):
        return t.reshape(t.shape[0] * t.shape[1], t.shape[2])

    jobs = [dict(q=per_batch(q), k=per_batch(k), v=per_batch(v), prev_off=0, with_lse=True,
                 along="rows" if dil == 1 else "slabs")
            for (_, dil), (q, k, v) in zip(A_GROUPS, ((qa0, ka0, va0), (qa1, ka1, va1), (qa2, ka2, va2)))]
    jobs.append(dict(q=per_batch(qb), k=per_batch(kb), v=per_batch(vb), prev_off=BLOCK - (B_WINDOW - 1),
                     sinks=p["b_sinks"], with_lse=False, along="rows"))
    *a_outs, (ob,) = _band_attn(jobs, ATTN_STEPS)
    oa = [flat(o) for o, _ in a_outs]
    la = [flat(lse) for _, lse in a_outs]
    ob = flat(ob)

    x2 = _mix(x2, attn_gain, oa, la, ob, mq, mk, mv, wb["w_gate"],
              p["b_gate"].reshape(1, N_BRANCH * D_MODEL), wb["w_o_a"], wb["w_o_b"], wb["w_o_m"],
              wb["w_out"], seq)
    return _conv_ffn(x2, p["ffn_norm"].reshape(1, D_MODEL), wb["w_up"], p["conv_w"],
                     p["conv_b"].reshape(1, 2 * D_FF), wb["w_down"], seq)


def kernel(x, mem, positions, attn_norm, w_in, a_q_norm, a_k_norm, b_q_norm, b_k_norm, b_sinks,
           mem_norm, w_mem_kv, m_q_norm, m_k_norm, w_o_a, w_o_b, w_o_m, w_gate, b_gate, w_out,
           ffn_norm, w_up, conv_w, conv_b, w_down):
    batch, seq, _ = x.shape
    params = dict(attn_norm=attn_norm, w_in=w_in, a_q_norm=a_q_norm, a_k_norm=a_k_norm,
                  b_q_norm=b_q_norm, b_k_norm=b_k_norm, b_sinks=b_sinks, mem_norm=mem_norm,
                  w_mem_kv=w_mem_kv, m_q_norm=m_q_norm, m_k_norm=m_k_norm, w_o_a=w_o_a, w_o_b=w_o_b,
                  w_o_m=w_o_m, w_gate=w_gate, b_gate=b_gate, w_out=w_out, ffn_norm=ffn_norm,
                  w_up=w_up, conv_w=conv_w, conv_b=conv_b, w_down=w_down)
    pos_rows = positions.astype(F32).reshape(batch * seq // TOKEN_BLOCK, 1, TOKEN_BLOCK)
    x2 = x.reshape(batch * seq, D_MODEL)
    for layer in range(attn_norm.shape[0]):
        x2 = _layer(x2, mem, pos_rows, batch, seq, {k: v[layer] for k, v in params.items()})
    return x2.reshape(batch, seq, D_MODEL)
```

```python
import functools
import math

import numpy as np

import jax
import jax.numpy as jnp
from jax import lax
from jax.experimental import pallas as pl
from jax.experimental.pallas import tpu as pltpu

D_MODEL = 1024
HEAD_DIM = 64
A_GROUPS = ((128, 1), (512, 4), (2048, 16))
A_HEADS = 4
A_SLAB = A_HEADS * HEAD_DIM
A_QKV_COLS = len(A_GROUPS) * 3 * A_SLAB
B_Q_HEADS = 8
B_KV_HEADS = 2
B_GROUP = B_Q_HEADS // B_KV_HEADS
B_WINDOW = 128
B_Q_COLS = B_Q_HEADS * HEAD_DIM
B_KV_COLS = B_KV_HEADS * HEAD_DIM
M_HEADS = 4
M_HEAD_DIM = 128
M_Q_COLS = M_HEADS * M_HEAD_DIM
B_Q_OFF = A_QKV_COLS
B_K_OFF = B_Q_OFF + B_Q_COLS
B_V_OFF = B_K_OFF + B_KV_COLS
M_Q_OFF = B_V_OFF + B_KV_COLS
IN_COLS = M_Q_OFF + M_Q_COLS
N_NORMED_TILES = 2 * len(A_GROUPS) + B_Q_COLS // A_SLAB + 1 + M_Q_COLS // A_SLAB
N_BRANCH = 3
D_FF = 2816
CONV_WIDTH = 3
ROPE_THETA = 500000.0
ROPE_DIM = HEAD_DIM // 4
ROPE_HALF = ROPE_DIM // 2
BLOCK = 128
EPS = 1e-6
NEG = -1e30
LOG2E = math.log2(math.e)

LANES = 128
SUBLANES = 8
BF16_ROWS = 16
HEADS_PER_TILE = LANES // HEAD_DIM
TOKEN_BLOCK = 512
ROW_CHUNK = 32
ATTN_STEPS = 4
FF_CHUNK = 256
VMEM_LIMIT = 56 * 1024 * 1024

F32 = jnp.float32
BF16 = jnp.bfloat16

assert all(window // dil == BLOCK for window, dil in A_GROUPS) and B_WINDOW <= BLOCK
assert B_KV_COLS == LANES and B_KV_HEADS == HEADS_PER_TILE and B_GROUP % HEADS_PER_TILE == 0
assert M_HEAD_DIM == LANES and D_FF % FF_CHUNK == 0 and CONV_WIDTH - 1 <= SUBLANES


def _const_spec(shape):
    return pl.BlockSpec(shape, lambda *_: (0,) * len(shape), pipeline_mode=pl.Buffered(1))


def _rms_rows(x, gain):
    ms = jnp.mean(x * x, axis=-1, keepdims=True)
    return x * lax.rsqrt(ms + EPS) * gain


def _mem_kv_kernel(mem_ref, gain_ref, w_ref, kgain_ref, win_ref, mk_ref, mv_ref, win_out_ref):
    win_out_ref[...] = win_ref[...].astype(BF16)
    hm = _rms_rows(mem_ref[...], gain_ref[...]).astype(BF16)
    kv = jnp.dot(hm, w_ref[...].astype(BF16), preferred_element_type=F32)
    ks = []
    for h in range(M_HEADS):
        kh = kv[:, h * M_HEAD_DIM:(h + 1) * M_HEAD_DIM]
        ks.append(_rms_rows(kh, kgain_ref[...]))
    mk_ref[...] = jnp.concatenate(ks, axis=1).astype(BF16)
    mv_ref[...] = kv[:, M_Q_COLS:].astype(BF16)


def _mem_kv(mem, mem_gain, w_kv, k_gain, w_in):
    b, m, _ = mem.shape
    slab = pl.BlockSpec((_cast_rows(w_in.shape[0], b), w_in.shape[1]), lambda i: (i, 0))
    assert w_in.shape[0] == b * slab.block_shape[0]
    return pl.pallas_call(
        _mem_kv_kernel,
        grid=(b,),
        in_specs=[pl.BlockSpec((None, m, D_MODEL), lambda i: (i, 0, 0)),
                  _const_spec((1, D_MODEL)),
                  _const_spec((D_MODEL, 2 * M_Q_COLS)),
                  _const_spec((1, M_HEAD_DIM)),
                  slab],
        out_specs=[pl.BlockSpec((None, m, M_Q_COLS), lambda i: (i, 0, 0)),
                   pl.BlockSpec((None, m, M_Q_COLS), lambda i: (i, 0, 0)),
                   slab],
        out_shape=[jax.ShapeDtypeStruct((b, m, M_Q_COLS), BF16)] * 2
                  + [jax.ShapeDtypeStruct(w_in.shape, BF16)],
        compiler_params=pltpu.CompilerParams(dimension_semantics=("arbitrary",),
                                             vmem_limit_bytes=VMEM_LIMIT),
        name="mem_kv",
    )(mem, mem_gain, w_kv, k_gain, w_in)


def _in_proj_kernel(x_ref, pos_ref, gain_ref, w_ref, colgain_ref, freq_ref, spread_ref, one_ref,
                    bd64_ref, bd128_ref, *refs, cast_scales):
    n_cast = len(cast_scales)
    cast_in, refs = refs[:n_cast], refs[n_cast:]
    (qa0, ka0, va0, qa1, ka1, va1, qa2, ka2, va2, qb_ref, kb_ref, vb_ref, mq_ref) = refs[:13]
    cast_out, refs = refs[13:13 + n_cast], refs[13 + n_cast:]
    proj_ref, ss_ref, h_ref, cos_ref, sin_lo_ref, sin_hi_ref = refs[:6]
    stage_refs = list(refs[6:])
    tb = x_ref.shape[0]

    for src, dst, scale in zip(cast_in, cast_out, cast_scales):
        w = src[...]
        dst[...] = (w if scale == 1.0 else w * scale).astype(BF16)
    chunks = [slice(r, r + ROW_CHUNK) for r in range(0, tb, ROW_CHUNK)]

    @pl.when(pl.program_id(0) == 0)
    def _():
        proj_ref[...] = jnp.zeros(proj_ref.shape, F32)
        ss_ref[...] = jnp.zeros(ss_ref.shape, F32)

    for rows in chunks:
        h_ref[rows, :] = _rms_rows(x_ref[rows, :], gain_ref[...]).astype(BF16)

    ang = freq_ref[...] * pos_ref[...]
    trig = jnp.concatenate([jnp.cos(ang), jnp.sin(ang),
                            jnp.zeros((LANES - ROPE_DIM, tb), F32)], axis=0).T
    tables = jnp.zeros((tb, 3 * LANES), F32)
    for _ in range(3):
        part = trig.astype(BF16)
        tables = tables + jnp.dot(part, spread_ref[...], preferred_element_type=F32)
        trig = trig - part.astype(F32)
    cos_ref[...] = tables[:, 0:LANES] + one_ref[...]
    sin_lo_ref[...] = tables[:, LANES:2 * LANES]
    sin_hi_ref[...] = tables[:, 2 * LANES:3 * LANES]

    ss_slots = iter(range(ss_ref.shape[0]))
    pending = []

    def flush_pending():
        while pending:
            new, bd, slot = pending.pop()
            ss_ref[slot, :, 0:new.shape[1]] = jnp.dot((new * new).astype(BF16), bd,
                                                      preferred_element_type=F32)

    def proj(c0, width, bd=None):
        tile, off = divmod(c0, A_SLAB)
        assert off + width <= A_SLAB and off % LANES == 0
        t = proj_ref[tile, :, off:off + width]
        ss = None
        if bd is not None:
            slot = next(ss_slots)
            ss = ss_ref[slot, :, 0:width]
        new = jnp.dot(h_ref[...], w_ref[:, c0:c0 + width], preferred_element_type=F32)
        proj_ref[tile, :, off:off + width] = new
        flush_pending()
        if bd is not None:
            pending.append((new, bd, slot))
        return t, ss

    def rope(y, rows):
        parts = []
        for c in range(y.shape[1] // LANES):
            yc = y[:, c * LANES:(c + 1) * LANES]
            parts.append(yc * cos_ref[rows, :]
                         + pltpu.roll(yc, LANES - ROPE_HALF, 1) * sin_lo_ref[rows, :]
                         + pltpu.roll(yc, ROPE_HALF, 1) * sin_hi_ref[rows, :])
        return parts[0] if len(parts) == 1 else jnp.concatenate(parts, axis=1)

    def finish(t_ss, emit, c0, *, dim=None, rotary=False):
        t, ss = t_ss
        if ss is not None:
            gain = colgain_ref[:, c0:c0 + t.shape[1]]
        for rows in chunks:
            y = t[rows]
            if ss is not None:
                y = y * lax.rsqrt(ss[rows] * (1.0 / dim) + EPS) * gain
            if rotary:
                y = rope(y, rows)
            emit(rows, y)

    def to_ref(out_ref, col0=0):
        def emit(rows, y):
            out_ref[rows, col0:col0 + y.shape[1]] = y.astype(BF16)
        return emit

    def store_tile(out_ref, t, dil, c0, **kw):
        if dil == 1:
            finish(t, to_ref(out_ref), c0, **kw)
            return
        stage_ref = stage_refs.pop()

        def emit(rows, y):
            for s in range(A_SLAB // LANES):
                stage_ref[s, rows, :] = y[:, s * LANES:(s + 1) * LANES]
        finish(t, emit, c0, **kw)
        for r in range(dil):
            for s in range(A_SLAB // LANES):
                col = r * A_SLAB + s * LANES
                out_ref[:, col:col + LANES] = stage_ref[s, pl.ds(r, tb // dil, stride=dil), :].astype(BF16)

    bd64 = bd64_ref[...]
    qk = dict(dim=HEAD_DIM, rotary=True)
    for g, (q_ref, k_ref, v_ref) in enumerate(((qa0, ka0, va0), (qa1, ka1, va1), (qa2, ka2, va2))):
        c0 = g * 3 * A_SLAB
        dil = A_GROUPS[g][1]
        store_tile(q_ref, proj(c0, A_SLAB, bd64), dil, c0, **qk)
        store_tile(k_ref, proj(c0 + A_SLAB, A_SLAB, bd64), dil, c0 + A_SLAB, **qk)
        store_tile(v_ref, proj(c0 + 2 * A_SLAB, A_SLAB), dil, c0 + 2 * A_SLAB)

    for s in range(B_Q_COLS // A_SLAB):
        c0 = B_Q_OFF + s * A_SLAB
        finish(proj(c0, A_SLAB, bd64), to_ref(qb_ref, s * A_SLAB), c0, **qk)

    def under_query_heads(out_ref):
        def emit(rows, y):
            swapped = pltpu.roll(y, HEAD_DIM, 1)
            first = jnp.where(first_head_lanes, y, swapped).astype(BF16)
            second = jnp.where(first_head_lanes, swapped, y).astype(BF16)
            reps = B_GROUP // HEADS_PER_TILE
            out_ref[rows, :] = jnp.concatenate([first] * reps + [second] * reps, axis=1)
        return emit
    first_head_lanes = lax.broadcasted_iota(jnp.int32, (1, LANES), 1) < HEAD_DIM
    finish(proj(B_K_OFF, B_KV_COLS, bd64_ref[0:B_KV_COLS, 0:B_KV_COLS]), under_query_heads(kb_ref),
           B_K_OFF, **qk)
    finish(proj(B_V_OFF, B_KV_COLS), under_query_heads(vb_ref), B_V_OFF)

    bd128 = bd128_ref[...]
    for s in range(M_Q_COLS // A_SLAB):
        c0 = M_Q_OFF + s * A_SLAB
        finish(proj(c0, A_SLAB, bd128), to_ref(mq_ref, s * A_SLAB), c0, dim=M_HEAD_DIM)
    flush_pending()


def _cast_rows(rows, n_steps):
    for per_step in range(BF16_ROWS, rows + 1, BF16_ROWS):
        if rows % per_step == 0 and rows // per_step <= n_steps:
            return per_step
    raise ValueError(f"no bf16-aligned split of {rows} rows over {n_steps} steps")


def _in_proj(x2, pos_rows, attn_gain, w_in, colgain, freq_col, spread, one_row, bd64, bd128,
             cast_weights, cast_scales):
    t = x2.shape[0]
    tb = TOKEN_BLOCK
    dils = [dil for _, dil in A_GROUPS for _ in range(3)] + [1, 1, 1, 1]
    widths = [A_SLAB] * 9 + [B_Q_COLS, B_Q_COLS, B_Q_COLS, M_Q_COLS]
    n_staged = sum(d > 1 for d in dils)
    n = t // tb
    done = lambda i: (jnp.maximum(i - 1, 0), 0)

    def cast_spec(w):
        per_step = _cast_rows(w.shape[0], n + 1)
        last = w.shape[0] // per_step - 1
        return pl.BlockSpec((per_step, w.shape[1]), lambda i: (jnp.minimum(i, last), 0))

    cast_specs = [cast_spec(w) for w in cast_weights]
    return pl.pallas_call(
        functools.partial(_in_proj_kernel, cast_scales=tuple(cast_scales)),
        grid=(n + 1,),
        in_specs=[pl.BlockSpec((tb, D_MODEL), lambda i: (jnp.minimum(i, n - 1), 0)),
                  pl.BlockSpec((None, 1, tb), lambda i: (jnp.maximum(i - 1, 0), 0, 0)),
                  _const_spec((1, D_MODEL)),
                  _const_spec((D_MODEL, IN_COLS)),
                  _const_spec((1, IN_COLS)),
                  _const_spec((ROPE_HALF, 1)),
                  _const_spec((LANES, 3 * LANES)),
                  _const_spec((1, LANES)),
                  _const_spec((A_SLAB, A_SLAB)),
                  _const_spec((A_SLAB, A_SLAB))] + cast_specs,
        out_specs=[pl.BlockSpec((tb // d, w * d), done) for w, d in zip(widths, dils)] + cast_specs,
        out_shape=[jax.ShapeDtypeStruct((t // d, w * d), BF16) for w, d in zip(widths, dils)]
                  + [jax.ShapeDtypeStruct(w.shape, BF16) for w in cast_weights],
        scratch_shapes=[pltpu.VMEM((IN_COLS // A_SLAB, tb, A_SLAB), F32),
                        pltpu.VMEM((N_NORMED_TILES, tb, A_SLAB), F32)]
                       + [pltpu.VMEM((tb, D_MODEL), BF16)] + [pltpu.VMEM((tb, LANES), F32)] * 3
                       + [pltpu.VMEM((A_SLAB // LANES, tb, LANES), F32)] * n_staged,
        compiler_params=pltpu.CompilerParams(dimension_semantics=("arbitrary",),
                                             vmem_limit_bytes=VMEM_LIMIT),
        name="in_proj",
    )(x2, pos_rows, attn_gain, w_in, colgain, freq_col, spread, one_row, bd64, bd128, *cast_weights)


def _band_attn_kernel(*refs, kinds):
    refs = list(refs)
    ins = [[refs.pop(0) for _ in range(5 + has_sink)] for _, _, _, has_sink, _, _ in kinds]
    outs = [[refs.pop(0) for _ in range(1 + with_lse)] for _, _, _, _, with_lse, _ in kinds]
    for (rows, slabs, prev_off, has_sink, with_lse, starts), in_refs, out_refs in zip(kinds, ins, outs):
        _attend(*in_refs[:5], in_refs[5] if has_sink else None, out_refs[0],
                out_refs[1] if with_lse else None,
                first_step=True if starts else pl.program_id(1) == 0,
                rows=rows, slabs=slabs, prev_off=prev_off)


def _attend(q_ref, kc_ref, kp_ref, vc_ref, vp_ref, sink_ref, o_ref, lse_ref, *, first_step, rows, slabs,
            prev_off):
    has_sink = sink_ref is not None
    with_lse = lse_ref is not None
    row = lax.broadcasted_iota(jnp.int32, (BLOCK, 2 * BLOCK), 0)
    col = lax.broadcasted_iota(jnp.int32, (BLOCK, 2 * BLOCK), 1)
    bias = jnp.where(col < BLOCK,
                     jnp.where(col >= row + prev_off, 0.0, NEG),
                     jnp.where(col - BLOCK <= row, 0.0, NEG)).astype(F32)
    bias_first = jnp.where(jnp.logical_and(first_step, col < BLOCK), NEG, bias)
    head_of_lane = lax.broadcasted_iota(jnp.int32, (1, A_SLAB), 1) // HEAD_DIM
    head_mask = [(head_of_lane == hh).astype(BF16) for hh in range(A_HEADS)]
    first_head_lanes = lax.broadcasted_iota(jnp.int32, (1, LANES), 1) < HEAD_DIM
    pack_row = lax.broadcasted_iota(jnp.int32, (BF16_ROWS, A_SLAB), 0)
    nblk = rows // BLOCK

    def drop_first_row(t):
        top = jnp.where(pack_row == 0, jnp.zeros((BF16_ROWS, A_SLAB), BF16), t[0:BF16_ROWS])
        return jnp.concatenate([top, t[BF16_ROWS:]], axis=0)

    for w in range(slabs):
        cs = slice(w * A_SLAB, (w + 1) * A_SLAB)
        biases = {False: [bias] * A_HEADS, True: [bias_first] * A_HEADS}
        if has_sink:
            biases = {first: [jnp.where(col == 0, sink_ref[w * A_HEADS + hh] * LOG2E, b[hh])
                              for hh in range(A_HEADS)] for first, b in biases.items()}
        for jb in range(nblk):
            q = q_ref[jb * BLOCK:(jb + 1) * BLOCK, cs]
            if jb == 0:
                k2 = jnp.concatenate([kp_ref[:, cs], kc_ref[0:BLOCK, cs]], axis=0)
                v2 = jnp.concatenate([vp_ref[:, cs], vc_ref[0:BLOCK, cs]], axis=0)
            else:
                k2 = kc_ref[(jb - 1) * BLOCK:(jb + 1) * BLOCK, cs]
                v2 = vc_ref[(jb - 1) * BLOCK:(jb + 1) * BLOCK, cs]
            if has_sink:
                k2 = drop_first_row(k2)
                v2 = drop_first_row(v2)
            qs = jnp.concatenate([q * head_mask[hh] for hh in range(A_HEADS)], axis=0)
            s = lax.dot_general(qs, k2, (((1,), (1,)), ((), ())), preferred_element_type=F32)
            ps, ms, ls = [], [], []
            for hh in range(A_HEADS):
                sh = s[hh * BLOCK:(hh + 1) * BLOCK] + biases[jb == 0][hh]
                m = jnp.max(sh, axis=-1, keepdims=True)
                p = jnp.exp2(sh - m)
                ms.append(m)
                ls.append(jnp.sum(p, axis=-1, keepdims=True))
                ps.append(p.astype(BF16))
            ost = jnp.dot(jnp.concatenate(ps, axis=0), v2, preferred_element_type=F32)
            o_tiles, lse_tiles = [], []
            for t0 in range(0, A_HEADS, HEADS_PER_TILE):
                lanes = slice(t0 * HEAD_DIM, (t0 + HEADS_PER_TILE) * HEAD_DIM)
                l_tile = jnp.where(first_head_lanes, ls[t0], ls[t0 + 1])
                o_tiles.append(jnp.where(first_head_lanes, ost[t0 * BLOCK:(t0 + 1) * BLOCK, lanes],
                                         ost[(t0 + 1) * BLOCK:(t0 + 2) * BLOCK, lanes]) * (1.0 / l_tile))
                if with_lse:
                    lse_tiles.append(jnp.where(first_head_lanes, ms[t0], ms[t0 + 1]) + jnp.log2(l_tile))
            o_ref[jb * BLOCK:(jb + 1) * BLOCK, cs] = jnp.concatenate(o_tiles, axis=1).astype(o_ref.dtype)
            if with_lse:
                lse_ref[jb * BLOCK:(jb + 1) * BLOCK, cs] = jnp.concatenate(lse_tiles, axis=1)


def _band_attn(jobs, n_steps):
    kinds, in_specs, args, out_specs, out_shape = [], [], [], [], []
    for job in jobs:
        q = job["q"]
        b, length, width = q.shape
        sinks, with_lse = job.get("sinks"), job["with_lse"]
        assert sinks is None or job["prev_off"] >= 1
        if job["along"] == "rows":
            rows, slabs = length // n_steps, width // A_SLAB
            per_step = rows // BLOCK
            cur = pl.BlockSpec((None, rows, width), lambda i, j: (i, j, 0))
            prev = pl.BlockSpec((None, BLOCK, width),
                                lambda i, j, per_step=per_step: (i, jnp.maximum(j * per_step - 1, 0), 0))
        else:
            rows, slabs = length, width // A_SLAB // n_steps
            cur = pl.BlockSpec((None, rows, slabs * A_SLAB), lambda i, j: (i, 0, j))
            prev = pl.BlockSpec((None, BLOCK, slabs * A_SLAB), lambda i, j: (i, 0, j))
        kinds.append((rows, slabs, job["prev_off"], sinks is not None, with_lse, job["along"] == "slabs"))
        in_specs += [cur, cur, prev, cur, prev]
        args += [q, job["k"], job["k"], job["v"], job["v"]]
        if sinks is not None:
            in_specs.append(pl.BlockSpec(memory_space=pltpu.SMEM))
            args.append(sinks)
        out_specs.append(cur)
        out_shape.append(jax.ShapeDtypeStruct(q.shape, BF16))
        if with_lse:
            out_specs.append(cur)
            out_shape.append(jax.ShapeDtypeStruct(q.shape, F32))
    outs = list(pl.pallas_call(
        functools.partial(_band_attn_kernel, kinds=tuple(kinds)),
        grid=(jobs[0]["q"].shape[0], n_steps),
        in_specs=in_specs,
        out_specs=out_specs,
        out_shape=out_shape,
        compiler_params=pltpu.CompilerParams(dimension_semantics=("arbitrary",) * 2,
                                             vmem_limit_bytes=VMEM_LIMIT),
        name="band_attn",
    )(*args))
    return [[outs.pop(0) for _ in range(1 + job["with_lse"])] for job in jobs]


def _mix_kernel(x_ref, gain_ref, oa0, la0, oa1, la1, oa2, la2, ob_ref, mq_ref, mk_ref, mv_ref,
                wg_ref, bg_ref, woa_ref, wob_ref, wom_ref, wout_ref, out_ref, *stage_refs):
    x = x_ref[...]
    h = _rms_rows(x, gain_ref[...]).astype(BF16)
    stage_refs = list(stage_refs)

    def token_major(blk_ref, dil):
        if dil == 1:
            return blk_ref[...].astype(F32)
        stage_ref = stage_refs.pop()
        rows = blk_ref.shape[0]
        for r in range(dil):
            for s in range(A_SLAB // LANES):
                c0 = r * A_SLAB + s * LANES
                stage_ref[s, pl.ds(r, rows, stride=dil), :] = blk_ref[:, c0:c0 + LANES].astype(F32)
        return jnp.concatenate([stage_ref[s] for s in range(A_SLAB // LANES)], axis=1)

    head_cols = [slice(hh * M_HEAD_DIM, (hh + 1) * M_HEAD_DIM) for hh in range(M_HEADS)]
    scores = [lax.dot_general(mq_ref[:, cs], mk_ref[:, cs], (((1,), (1,)), ((), ())),
                              preferred_element_type=F32) for cs in head_cols]
    gate_cols = [slice(i * D_MODEL, (i + 1) * D_MODEL) for i in range(N_BRANCH)]
    gates = [1.0 + jnp.tanh(jnp.dot(h, wg_ref[:, cs], preferred_element_type=F32) + 0.5 * bg_ref[:, cs])
             for cs in gate_cols]

    dils = [dil for _, dil in A_GROUPS]
    l0, l1, l2 = (token_major(r, d) for r, d in zip((la0, la1, la2), dils))
    o0, o1, o2 = (token_major(r, d) for r, d in zip((oa0, oa1, oa2), dils))
    mx = jnp.maximum(jnp.maximum(l0, l1), l2)
    e0, e1, e2 = jnp.exp2(l0 - mx), jnp.exp2(l1 - mx), jnp.exp2(l2 - mx)
    o_a = (e0 * o0 + e1 * o1 + e2 * o2) / (e0 + e1 + e2)

    o_m = []
    for s, cs in zip(scores, head_cols):
        p = jnp.exp2(s - jnp.max(s, axis=-1, keepdims=True))
        l = jnp.sum(p, axis=-1, keepdims=True)
        o_m.append(jnp.dot(p.astype(BF16), mv_ref[:, cs], preferred_element_type=F32) / l)
    o_m = jnp.concatenate(o_m, axis=1)

    merged = jnp.zeros(x.shape, F32)
    for gate, o, w_ref in zip(gates, (o_a, ob_ref[...], o_m), (woa_ref, wob_ref, wom_ref)):
        merged = merged + gate * jnp.dot(o.astype(BF16), w_ref[...], preferred_element_type=F32)
    out_ref[...] = x + jnp.dot(merged.astype(BF16), wout_ref[...], preferred_element_type=F32)


def _mix(x2, attn_gain, oa, la, ob, mq, mk, mv, w_gate, b_gate, w_o_a, w_o_b, w_o_m, w_out, seq):
    t = x2.shape[0]
    tb = TOKEN_BLOCK
    mem_len = mk.shape[1]
    per_seq = seq // tb
    tok = lambda w, d=1: pl.BlockSpec((tb // d, w * d), lambda i: (i, 0))
    mem = pl.BlockSpec((None, mem_len, M_Q_COLS), lambda i: (i // per_seq, 0, 0))
    dils = [dil for _, dil in A_GROUPS]
    return pl.pallas_call(
        _mix_kernel,
        grid=(t // tb,),
        in_specs=[tok(D_MODEL), _const_spec((1, D_MODEL))]
                 + [tok(A_SLAB, d) for d in dils for _ in range(2)]
                 + [tok(B_Q_COLS), tok(M_Q_COLS), mem, mem,
                  _const_spec((D_MODEL, N_BRANCH * D_MODEL)), _const_spec((1, N_BRANCH * D_MODEL)),
                  _const_spec((A_SLAB, D_MODEL)), _const_spec((B_Q_COLS, D_MODEL)),
                  _const_spec((M_Q_COLS, D_MODEL)), _const_spec((D_MODEL, D_MODEL))],
        out_specs=tok(D_MODEL),
        out_shape=jax.ShapeDtypeStruct((t, D_MODEL), F32),
        scratch_shapes=[pltpu.VMEM((A_SLAB // LANES, tb, LANES), F32)] * (2 * sum(d > 1 for d in dils)),
        compiler_params=pltpu.CompilerParams(dimension_semantics=("arbitrary",),
                                             vmem_limit_bytes=VMEM_LIMIT),
        name="mix",
    )(x2, attn_gain, oa[0], la[0], oa[1], la[1], oa[2], la[2], ob, mq, mk, mv,
      w_gate, b_gate, w_o_a, w_o_b, w_o_m, w_out)


def _conv_ffn_kernel(x_ref, gain_ref, wup_ref, cw_ref, cb_ref, wdown_ref, out_ref,
                     carry_ref, ext_ref, act_ref, *, per_seq):
    tb = x_ref.shape[0]
    halo = CONV_WIDTH - 1
    pad = SUBLANES

    @pl.when(pl.program_id(0) % per_seq == 0)
    def _():
        carry_ref[...] = jnp.zeros(carry_ref.shape, F32)

    x = x_ref[...]
    h = _rms_rows(x, gain_ref[...]).astype(BF16)

    def conv(c0, scale=1.0):
        cs = slice(c0, c0 + FF_CHUNK)
        u = jnp.dot(h, wup_ref[:, cs], preferred_element_type=F32)
        ext_ref[0:pad, :] = carry_ref[:, cs]
        ext_ref[pad:pad + tb, :] = u
        carry_ref[:, cs] = u[tb - pad:tb, :]
        taps = cw_ref[:, cs] * scale
        c = cb_ref[:, cs] * scale + taps[CONV_WIDTH - 1:CONV_WIDTH] * u
        for j in range(halo):
            c = c + taps[j:j + 1] * ext_ref[pad - halo + j:pad - halo + j + tb, :]
        return c

    for ch in range(D_FF // FF_CHUNK):
        half = conv(ch * FF_CHUNK, 0.5)
        g = conv(D_FF + ch * FF_CHUNK)
        act = (half + half * jnp.tanh(half)) * g
        act_ref[:, ch * FF_CHUNK:(ch + 1) * FF_CHUNK] = act.astype(BF16)
    out_ref[...] = x + jnp.dot(act_ref[...], wdown_ref[...], preferred_element_type=F32)


def _conv_ffn(x2, gain, w_up, conv_w, conv_b, w_down, seq):
    t = x2.shape[0]
    tb = TOKEN_BLOCK
    return pl.pallas_call(
        functools.partial(_conv_ffn_kernel, per_seq=seq // tb),
        grid=(t // tb,),
        in_specs=[pl.BlockSpec((tb, D_MODEL), lambda i: (i, 0)),
                  _const_spec((1, D_MODEL)),
                  _const_spec((D_MODEL, 2 * D_FF)),
                  _const_spec((CONV_WIDTH, 2 * D_FF)),
                  _const_spec((1, 2 * D_FF)),
                  _const_spec((D_FF, D_MODEL))],
        out_specs=pl.BlockSpec((tb, D_MODEL), lambda i: (i, 0)),
        out_shape=jax.ShapeDtypeStruct((t, D_MODEL), F32),
        scratch_shapes=[pltpu.VMEM((SUBLANES, 2 * D_FF), F32),
                        pltpu.VMEM((tb + SUBLANES, FF_CHUNK), F32),
                        pltpu.VMEM((tb, D_FF), BF16)],
        compiler_params=pltpu.CompilerParams(dimension_semantics=("arbitrary",),
                                             vmem_limit_bytes=VMEM_LIMIT),
        name="conv_ffn",
    )(x2, gain, w_up, conv_w, conv_b, w_down)


def _tables():
    lane = np.arange(A_SLAB)
    bd64 = jnp.asarray(lane[:, None] // HEAD_DIM == lane[None, :] // HEAD_DIM, BF16)
    bd128 = jnp.asarray(lane[:, None] // M_HEAD_DIM == lane[None, :] // M_HEAD_DIM, BF16)
    freq_col = jnp.exp(jnp.arange(ROPE_HALF, dtype=F32) * (-2.0 * math.log(ROPE_THETA) / ROPE_DIM))
    freq_col = freq_col.reshape(ROPE_HALF, 1)
    in_head = np.arange(LANES) % HEAD_DIM
    j = np.arange(LANES)[:, None]
    rot = in_head[None, :] < ROPE_DIM
    cos_sel = rot & (j == in_head[None, :] % ROPE_HALF)
    sin_sel = rot & (j == ROPE_HALF + in_head[None, :] % ROPE_HALF)
    lo = in_head[None, :] < ROPE_HALF
    spread = np.concatenate([cos_sel.astype(np.float32), -(sin_sel & lo).astype(np.float32),
                             (sin_sel & ~lo).astype(np.float32)], axis=1)
    one_row = (in_head >= ROPE_DIM).astype(np.float32).reshape(1, LANES)
    return bd64, bd128, freq_col, jnp.asarray(spread, BF16), jnp.asarray(one_row)


def _layer(x2, mem, pos_rows, batch, seq, p):
    bd64, bd128, freq_col, spread, one_row = _tables()
    n_groups = len(A_GROUPS)
    gains = jnp.concatenate([p["a_q_norm"], p["a_k_norm"], p["b_q_norm"][None], p["b_k_norm"][None],
                             p["m_q_norm"].reshape(M_HEAD_DIM // HEAD_DIM, HEAD_DIM),
                             jnp.ones((1, HEAD_DIM), F32)])
    row_bq, row_bk, row_mq = 2 * n_groups, 2 * n_groups + 1, 2 * n_groups + 2
    row_one = row_mq + M_HEAD_DIM // HEAD_DIM
    qk_scale = HEAD_DIM ** -0.5 * LOG2E
    src, scale = [], []
    for g in range(n_groups):
        src += [g] * A_HEADS + [n_groups + g] * A_HEADS + [row_one] * A_HEADS
        scale += [qk_scale] * A_HEADS + [1.0] * (2 * A_HEADS)
    src += [row_bq] * B_Q_HEADS + [row_bk] * B_KV_HEADS + [row_one] * B_KV_HEADS
    scale += [qk_scale] * B_Q_HEADS + [1.0] * (2 * B_KV_HEADS)
    src += [row_mq + i for i in range(M_HEAD_DIM // HEAD_DIM)] * M_HEADS
    scale += [M_HEAD_DIM ** -0.5 * LOG2E] * (M_Q_COLS // HEAD_DIM)
    colgain = (gains[np.asarray(src)] * np.asarray(scale, np.float32)[:, None]).reshape(1, IN_COLS)

    attn_gain = p["attn_norm"].reshape(1, D_MODEL)
    mk, mv, w_in = _mem_kv(mem, p["mem_norm"].reshape(1, D_MODEL), p["w_mem_kv"],
                           p["m_k_norm"].reshape(1, M_HEAD_DIM), p["w_in"])
    later_weights = ("w_gate", "w_o_a", "w_o_b", "w_o_m", "w_out", "w_up", "w_down")
    scales = [0.5 if name in ("w_gate", "w_out") else 1.0 for name in later_weights]
    outs = _in_proj(x2, pos_rows, attn_gain, w_in, colgain, freq_col, spread, one_row,
                    bd64, bd128, [p[name] for name in later_weights], scales)
    (qa0, ka0, va0, qa1, ka1, va1, qa2, ka2, va2, qb, kb, vb, mq) = outs[:13]
    wb = dict(zip(later_weights, outs[13:]))

    def per_batch(t):
        return t.reshape(batch, t.shape[0] // batch, t.shape[-1])

    def flat(t):
        return t.reshape(t.shape[0] * t.shape[1], t.shape[2])

    jobs = [dict(q=per_batch(q), k=per_batch(k), v=per_batch(v), prev_off=0, with_lse=True,
                 along="rows" if dil == 1 else "slabs")
            for (_, dil), (q, k, v) in zip(A_GROUPS, ((qa0, ka0, va0), (qa1, ka1, va1), (qa2, ka2, va2)))]
    jobs.append(dict(q=per_batch(qb), k=per_batch(kb), v=per_batch(vb), prev_off=BLOCK - (B_WINDOW - 1),
                     sinks=p["b_sinks"], with_lse=False, along="rows"))
    *a_outs, (ob,) = _band_attn(jobs, ATTN_STEPS)
    oa = [flat(o) for o, _ in a_outs]
    la = [flat(lse) for _, lse in a_outs]
    ob = flat(ob)

    x2 = _mix(x2, attn_gain, oa, la, ob, mq, mk, mv, wb["w_gate"],
              p["b_gate"].reshape(1, N_BRANCH * D_MODEL), wb["w_o_a"], wb["w_o_b"], wb["w_o_m"],
              wb["w_out"], seq)
    return _conv_ffn(x2, p["ffn_norm"].reshape(1, D_MODEL), wb["w_up"], p["conv_w"],
                     p["conv_b"].reshape(1, 2 * D_FF), wb["w_down"], seq)


def kernel(x, mem, positions, attn_norm, w_in, a_q_norm, a_k_norm, b_q_norm, b_k_norm, b_sinks,
           mem_norm, w_mem_kv, m_q_norm, m_k_norm, w_o_a, w_o_b, w_o_m, w_gate, b_gate, w_out,
           ffn_norm, w_up, conv_w, conv_b, w_down):
    batch, seq, _ = x.shape
    params = dict(attn_norm=attn_norm, w_in=w_in, a_q_norm=a_q_norm, a_k_norm=a_k_norm,
                  b_q_norm=b_q_norm, b_k_norm=b_k_norm, b_sinks=b_sinks, mem_norm=mem_norm,
                  w_mem_kv=w_mem_kv, m_q_norm=m_q_norm, m_k_norm=m_k_norm, w_o_a=w_o_a, w_o_b=w_o_b,
                  w_o_m=w_o_m, w_gate=w_gate, b_gate=b_gate, w_out=w_out, ffn_norm=ffn_norm,
                  w_up=w_up, conv_w=conv_w, conv_b=conv_b, w_down=w_down)
    pos_rows = positions.astype(F32).reshape(batch * seq // TOKEN_BLOCK, 1, TOKEN_BLOCK)
    x2 = x.reshape(batch * seq, D_MODEL)
    for layer in range(attn_norm.shape[0]):
        x2 = _layer(x2, mem, pos_rows, batch, seq, {k: v[layer] for k, v in params.items()})
    return x2.reshape(batch, seq, D_MODEL)
```

```python
import functools
import math

import numpy as np

import jax
import jax.numpy as jnp
from jax import lax
from jax.experimental import pallas as pl
from jax.experimental.pallas import tpu as pltpu

D_MODEL = 1024
HEAD_DIM = 64
A_GROUPS = ((128, 1), (512, 4), (2048, 16))
A_HEADS = 4
A_SLAB = A_HEADS * HEAD_DIM
A_QKV_COLS = len(A_GROUPS) * 3 * A_SLAB
B_Q_HEADS = 8
B_KV_HEADS = 2
B_GROUP = B_Q_HEADS // B_KV_HEADS
B_WINDOW = 128
B_Q_COLS = B_Q_HEADS * HEAD_DIM
B_KV_COLS = B_KV_HEADS * HEAD_DIM
M_HEADS = 4
M_HEAD_DIM = 128
M_Q_COLS = M_HEADS * M_HEAD_DIM
B_Q_OFF = A_QKV_COLS
B_K_OFF = B_Q_OFF + B_Q_COLS
B_V_OFF = B_K_OFF + B_KV_COLS
M_Q_OFF = B_V_OFF + B_KV_COLS
IN_COLS = M_Q_OFF + M_Q_COLS
N_NORMED_TILES = 2 * len(A_GROUPS) + B_Q_COLS // A_SLAB + 1 + M_Q_COLS // A_SLAB
N_BRANCH = 3
D_FF = 2816
CONV_WIDTH = 3
ROPE_THETA = 500000.0
ROPE_DIM = HEAD_DIM // 4
ROPE_HALF = ROPE_DIM // 2
BLOCK = 128
EPS = 1e-6
NEG = -1e30
LOG2E = math.log2(math.e)

LANES = 128
SUBLANES = 8
FAST_STRIDE = 4
BF16_ROWS = 16
HEADS_PER_TILE = LANES // HEAD_DIM
TOKEN_BLOCK = 512
ROW_CHUNK = 32
ATTN_STEPS = 4
FF_CHUNK = 256
VMEM_LIMIT = 56 * 1024 * 1024

F32 = jnp.float32
BF16 = jnp.bfloat16

assert all(window // dil == BLOCK for window, dil in A_GROUPS) and B_WINDOW <= BLOCK
assert B_KV_COLS == LANES and B_KV_HEADS == HEADS_PER_TILE and B_GROUP % HEADS_PER_TILE == 0
assert M_HEAD_DIM == LANES and D_FF % FF_CHUNK == 0 and CONV_WIDTH - 1 <= SUBLANES


def _const_spec(shape):
    return pl.BlockSpec(shape, lambda *_: (0,) * len(shape), pipeline_mode=pl.Buffered(1))


def _rms_rows(x, gain):
    ms = jnp.mean(x * x, axis=-1, keepdims=True)
    return x * lax.rsqrt(ms + EPS) * gain


def _mem_kv_kernel(mem_ref, gain_ref, w_ref, kgain_ref, win_ref, mk_ref, mv_ref, win_out_ref):
    win_out_ref[...] = win_ref[...].astype(BF16)
    hm = _rms_rows(mem_ref[...], gain_ref[...]).astype(BF16)
    kv = jnp.dot(hm, w_ref[...].astype(BF16), preferred_element_type=F32)
    ks = []
    for h in range(M_HEADS):
        kh = kv[:, h * M_HEAD_DIM:(h + 1) * M_HEAD_DIM]
        ks.append(_rms_rows(kh, kgain_ref[...]))
    mk_ref[...] = jnp.concatenate(ks, axis=1).astype(BF16)
    mv_ref[...] = kv[:, M_Q_COLS:].astype(BF16)


def _mem_kv(mem, mem_gain, w_kv, k_gain, w_in):
    b, m, _ = mem.shape
    slab = pl.BlockSpec((_cast_rows(w_in.shape[0], b), w_in.shape[1]), lambda i: (i, 0))
    assert w_in.shape[0] == b * slab.block_shape[0]
    return pl.pallas_call(
        _mem_kv_kernel,
        grid=(b,),
        in_specs=[pl.BlockSpec((None, m, D_MODEL), lambda i: (i, 0, 0)),
                  _const_spec((1, D_MODEL)),
                  _const_spec((D_MODEL, 2 * M_Q_COLS)),
                  _const_spec((1, M_HEAD_DIM)),
                  slab],
        out_specs=[pl.BlockSpec((None, m, M_Q_COLS), lambda i: (i, 0, 0)),
                   pl.BlockSpec((None, m, M_Q_COLS), lambda i: (i, 0, 0)),
                   slab],
        out_shape=[jax.ShapeDtypeStruct((b, m, M_Q_COLS), BF16)] * 2
                  + [jax.ShapeDtypeStruct(w_in.shape, BF16)],
        compiler_params=pltpu.CompilerParams(dimension_semantics=("arbitrary",),
                                             vmem_limit_bytes=VMEM_LIMIT),
        name="mem_kv",
    )(mem, mem_gain, w_kv, k_gain, w_in)


def _in_proj_kernel(x_ref, pos_ref, gain_ref, w_ref, colgain_ref, freq_ref, spread_ref, one_ref,
                    bd64_ref, bd128_ref, *refs, cast_scales):
    n_cast = len(cast_scales)
    cast_in, refs = refs[:n_cast], refs[n_cast:]
    (qa0, ka0, va0, qa1, ka1, va1, qa2, ka2, va2, qb_ref, kb_ref, vb_ref, mq_ref) = refs[:13]
    cast_out, refs = refs[13:13 + n_cast], refs[13 + n_cast:]
    proj_ref, ss_ref, h_ref, cos_ref, sin_lo_ref, sin_hi_ref = refs[:6]
    stage_refs = list(refs[6:])
    tb = x_ref.shape[0]

    for src, dst, scale in zip(cast_in, cast_out, cast_scales):
        w = src[...]
        dst[...] = (w if scale == 1.0 else w * scale).astype(BF16)
    chunks = [slice(r, r + ROW_CHUNK) for r in range(0, tb, ROW_CHUNK)]

    @pl.when(pl.program_id(0) == 0)
    def _():
        proj_ref[...] = jnp.zeros(proj_ref.shape, F32)
        ss_ref[...] = jnp.zeros(ss_ref.shape, F32)

    for rows in chunks:
        h_ref[rows, :] = _rms_rows(x_ref[rows, :], gain_ref[...]).astype(BF16)

    ang = freq_ref[...] * pos_ref[...]
    trig = jnp.concatenate([jnp.cos(ang), jnp.sin(ang),
                            jnp.zeros((LANES - ROPE_DIM, tb), F32)], axis=0).T
    tables = jnp.zeros((tb, 3 * LANES), F32)
    for _ in range(3):
        part = trig.astype(BF16)
        tables = tables + jnp.dot(part, spread_ref[...], preferred_element_type=F32)
        trig = trig - part.astype(F32)
    cos_ref[...] = tables[:, 0:LANES] + one_ref[...]
    sin_lo_ref[...] = tables[:, LANES:2 * LANES]
    sin_hi_ref[...] = tables[:, 2 * LANES:3 * LANES]

    ss_slots = iter(range(ss_ref.shape[0]))
    pending = []

    def flush_pending():
        while pending:
            new, bd, slot = pending.pop()
            ss_ref[slot, :, 0:new.shape[1]] = jnp.dot((new * new).astype(BF16), bd,
                                                      preferred_element_type=F32)

    def proj(c0, width, bd=None):
        tile, off = divmod(c0, A_SLAB)
        assert off + width <= A_SLAB and off % LANES == 0
        t = proj_ref[tile, :, off:off + width]
        ss = None
        if bd is not None:
            slot = next(ss_slots)
            ss = ss_ref[slot, :, 0:width]
        new = jnp.dot(h_ref[...], w_ref[:, c0:c0 + width], preferred_element_type=F32)
        proj_ref[tile, :, off:off + width] = new
        flush_pending()
        if bd is not None:
            pending.append((new, bd, slot))
        return t, ss

    def rope(y, rows):
        parts = []
        for c in range(y.shape[1] // LANES):
            yc = y[:, c * LANES:(c + 1) * LANES]
            parts.append(yc * cos_ref[rows, :]
                         + pltpu.roll(yc, LANES - ROPE_HALF, 1) * sin_lo_ref[rows, :]
                         + pltpu.roll(yc, ROPE_HALF, 1) * sin_hi_ref[rows, :])
        return parts[0] if len(parts) == 1 else jnp.concatenate(parts, axis=1)

    def finish(t_ss, emit, c0, *, dim=None, rotary=False):
        t, ss = t_ss
        if ss is not None:
            gain = colgain_ref[:, c0:c0 + t.shape[1]]
        for rows in chunks:
            y = t[rows]
            if ss is not None:
                y = y * lax.rsqrt(ss[rows] * (1.0 / dim) + EPS) * gain
            if rotary:
                y = rope(y, rows)
            emit(rows, y)

    def to_ref(out_ref, col0=0):
        def emit(rows, y):
            out_ref[rows, col0:col0 + y.shape[1]] = y.astype(BF16)
        return emit

    def store_tile(out_ref, t, dil, c0, **kw):
        if dil == 1:
            finish(t, to_ref(out_ref), c0, **kw)
            return
        stage_ref = stage_refs.pop()

        def emit(rows, y):
            for s in range(A_SLAB // LANES):
                stage_ref[s, rows, :] = y[:, s * LANES:(s + 1) * LANES]
        finish(t, emit, c0, **kw)
        for r in range(dil):
            for s in range(A_SLAB // LANES):
                col = r * A_SLAB + s * LANES
                out_ref[:, col:col + LANES] = stage_ref[s, pl.ds(r, tb // dil, stride=dil), :].astype(BF16)

    bd64 = bd64_ref[...]
    qk = dict(dim=HEAD_DIM, rotary=True)
    for g, (q_ref, k_ref, v_ref) in enumerate(((qa0, ka0, va0), (qa1, ka1, va1), (qa2, ka2, va2))):
        c0 = g * 3 * A_SLAB
        dil = A_GROUPS[g][1]
        store_tile(q_ref, proj(c0, A_SLAB, bd64), dil, c0, **qk)
        store_tile(k_ref, proj(c0 + A_SLAB, A_SLAB, bd64), dil, c0 + A_SLAB, **qk)
        store_tile(v_ref, proj(c0 + 2 * A_SLAB, A_SLAB), dil, c0 + 2 * A_SLAB)

    for s in range(B_Q_COLS // A_SLAB):
        c0 = B_Q_OFF + s * A_SLAB
        finish(proj(c0, A_SLAB, bd64), to_ref(qb_ref, s * A_SLAB), c0, **qk)

    def under_query_heads(out_ref):
        def emit(rows, y):
            swapped = pltpu.roll(y, HEAD_DIM, 1)
            first = jnp.where(first_head_lanes, y, swapped).astype(BF16)
            second = jnp.where(first_head_lanes, swapped, y).astype(BF16)
            reps = B_GROUP // HEADS_PER_TILE
            out_ref[rows, :] = jnp.concatenate([first] * reps + [second] * reps, axis=1)
        return emit
    first_head_lanes = lax.broadcasted_iota(jnp.int32, (1, LANES), 1) < HEAD_DIM
    finish(proj(B_K_OFF, B_KV_COLS, bd64_ref[0:B_KV_COLS, 0:B_KV_COLS]), under_query_heads(kb_ref),
           B_K_OFF, **qk)
    finish(proj(B_V_OFF, B_KV_COLS), under_query_heads(vb_ref), B_V_OFF)

    bd128 = bd128_ref[...]
    for s in range(M_Q_COLS // A_SLAB):
        c0 = M_Q_OFF + s * A_SLAB
        finish(proj(c0, A_SLAB, bd128), to_ref(mq_ref, s * A_SLAB), c0, dim=M_HEAD_DIM)
    flush_pending()


def _cast_rows(rows, n_steps):
    for per_step in range(BF16_ROWS, rows + 1, BF16_ROWS):
        if rows % per_step == 0 and rows // per_step <= n_steps:
            return per_step
    raise ValueError(f"no bf16-aligned split of {rows} rows over {n_steps} steps")


def _in_proj(x2, pos_rows, attn_gain, w_in, colgain, freq_col, spread, one_row, bd64, bd128,
             cast_weights, cast_scales):
    t = x2.shape[0]
    tb = TOKEN_BLOCK
    dils = [dil for _, dil in A_GROUPS for _ in range(3)] + [1, 1, 1, 1]
    widths = [A_SLAB] * 9 + [B_Q_COLS, B_Q_COLS, B_Q_COLS, M_Q_COLS]
    n_staged = sum(d > 1 for d in dils)
    n = t // tb
    done = lambda i: (jnp.maximum(i - 1, 0), 0)

    def cast_spec(w):
        per_step = _cast_rows(w.shape[0], n + 1)
        last = w.shape[0] // per_step - 1
        return pl.BlockSpec((per_step, w.shape[1]), lambda i: (jnp.minimum(i, last), 0))

    cast_specs = [cast_spec(w) for w in cast_weights]
    return pl.pallas_call(
        functools.partial(_in_proj_kernel, cast_scales=tuple(cast_scales)),
        grid=(n + 1,),
        in_specs=[pl.BlockSpec((tb, D_MODEL), lambda i: (jnp.minimum(i, n - 1), 0)),
                  pl.BlockSpec((None, 1, tb), lambda i: (jnp.maximum(i - 1, 0), 0, 0)),
                  _const_spec((1, D_MODEL)),
                  _const_spec((D_MODEL, IN_COLS)),
                  _const_spec((1, IN_COLS)),
                  _const_spec((ROPE_HALF, 1)),
                  _const_spec((LANES, 3 * LANES)),
                  _const_spec((1, LANES)),
                  _const_spec((A_SLAB, A_SLAB)),
                  _const_spec((A_SLAB, A_SLAB))] + cast_specs,
        out_specs=[pl.BlockSpec((tb // d, w * d), done) for w, d in zip(widths, dils)] + cast_specs,
        out_shape=[jax.ShapeDtypeStruct((t // d, w * d), BF16) for w, d in zip(widths, dils)]
                  + [jax.ShapeDtypeStruct(w.shape, BF16) for w in cast_weights],
        scratch_shapes=[pltpu.VMEM((IN_COLS // A_SLAB, tb, A_SLAB), F32),
                        pltpu.VMEM((N_NORMED_TILES, tb, A_SLAB), F32)]
                       + [pltpu.VMEM((tb, D_MODEL), BF16)] + [pltpu.VMEM((tb, LANES), F32)] * 3
                       + [pltpu.VMEM((A_SLAB // LANES, tb, LANES), F32)] * n_staged,
        compiler_params=pltpu.CompilerParams(dimension_semantics=("arbitrary",),
                                             vmem_limit_bytes=VMEM_LIMIT),
        name="in_proj",
    )(x2, pos_rows, attn_gain, w_in, colgain, freq_col, spread, one_row, bd64, bd128, *cast_weights)


def _band_attn_kernel(*refs, kinds):
    refs = list(refs)
    ins = [[refs.pop(0) for _ in range(5 + has_sink)] for _, _, _, has_sink, _, _ in kinds]
    outs = [[refs.pop(0) for _ in range(1 + with_lse)] for _, _, _, _, with_lse, _ in kinds]
    for (rows, slabs, prev_off, has_sink, with_lse, starts), in_refs, out_refs in zip(kinds, ins, outs):
        _attend(*in_refs[:5], in_refs[5] if has_sink else None, out_refs[0],
                out_refs[1] if with_lse else None,
                first_step=True if starts else pl.program_id(1) == 0,
                rows=rows, slabs=slabs, prev_off=prev_off)


def _attend(q_ref, kc_ref, kp_ref, vc_ref, vp_ref, sink_ref, o_ref, lse_ref, *, first_step, rows, slabs,
            prev_off):
    has_sink = sink_ref is not None
    with_lse = lse_ref is not None
    row = lax.broadcasted_iota(jnp.int32, (BLOCK, 2 * BLOCK), 0)
    col = lax.broadcasted_iota(jnp.int32, (BLOCK, 2 * BLOCK), 1)
    bias = jnp.where(col < BLOCK,
                     jnp.where(col >= row + prev_off, 0.0, NEG),
                     jnp.where(col - BLOCK <= row, 0.0, NEG)).astype(F32)
    bias_first = jnp.where(jnp.logical_and(first_step, col < BLOCK), NEG, bias)
    head_of_lane = lax.broadcasted_iota(jnp.int32, (1, A_SLAB), 1) // HEAD_DIM
    head_mask = [(head_of_lane == hh).astype(BF16) for hh in range(A_HEADS)]
    first_head_lanes = lax.broadcasted_iota(jnp.int32, (1, LANES), 1) < HEAD_DIM
    pack_row = lax.broadcasted_iota(jnp.int32, (BF16_ROWS, A_SLAB), 0)
    nblk = rows // BLOCK

    def drop_first_row(t):
        top = jnp.where(pack_row == 0, jnp.zeros((BF16_ROWS, A_SLAB), BF16), t[0:BF16_ROWS])
        return jnp.concatenate([top, t[BF16_ROWS:]], axis=0)

    for w in range(slabs):
        cs = slice(w * A_SLAB, (w + 1) * A_SLAB)
        biases = {False: [bias] * A_HEADS, True: [bias_first] * A_HEADS}
        if has_sink:
            biases = {first: [jnp.where(col == 0, sink_ref[w * A_HEADS + hh] * LOG2E, b[hh])
                              for hh in range(A_HEADS)] for first, b in biases.items()}
        for jb in range(nblk):
            q = q_ref[jb * BLOCK:(jb + 1) * BLOCK, cs]
            if jb == 0:
                k2 = jnp.concatenate([kp_ref[:, cs], kc_ref[0:BLOCK, cs]], axis=0)
                v2 = jnp.concatenate([vp_ref[:, cs], vc_ref[0:BLOCK, cs]], axis=0)
            else:
                k2 = kc_ref[(jb - 1) * BLOCK:(jb + 1) * BLOCK, cs]
                v2 = vc_ref[(jb - 1) * BLOCK:(jb + 1) * BLOCK, cs]
            if has_sink:
                k2 = drop_first_row(k2)
                v2 = drop_first_row(v2)
            qs = jnp.concatenate([q * head_mask[hh] for hh in range(A_HEADS)], axis=0)
            s = lax.dot_general(qs, k2, (((1,), (1,)), ((), ())), preferred_element_type=F32)
            ps, ms, ls = [], [], []
            for hh in range(A_HEADS):
                sh = s[hh * BLOCK:(hh + 1) * BLOCK] + biases[jb == 0][hh]
                m = jnp.max(sh, axis=-1, keepdims=True)
                p = jnp.exp2(sh - m)
                ms.append(m)
                ls.append(jnp.sum(p, axis=-1, keepdims=True))
                ps.append(p.astype(BF16))
            ost = jnp.dot(jnp.concatenate(ps, axis=0), v2, preferred_element_type=F32)
            o_tiles, lse_tiles = [], []
            for t0 in range(0, A_HEADS, HEADS_PER_TILE):
                lanes = slice(t0 * HEAD_DIM, (t0 + HEADS_PER_TILE) * HEAD_DIM)
                l_tile = jnp.where(first_head_lanes, ls[t0], ls[t0 + 1])
                o_tiles.append(jnp.where(first_head_lanes, ost[t0 * BLOCK:(t0 + 1) * BLOCK, lanes],
                                         ost[(t0 + 1) * BLOCK:(t0 + 2) * BLOCK, lanes]) * (1.0 / l_tile))
                if with_lse:
                    lse_tiles.append(jnp.where(first_head_lanes, ms[t0], ms[t0 + 1]) + jnp.log2(l_tile))
            o_ref[jb * BLOCK:(jb + 1) * BLOCK, cs] = jnp.concatenate(o_tiles, axis=1).astype(o_ref.dtype)
            if with_lse:
                lse_ref[jb * BLOCK:(jb + 1) * BLOCK, cs] = jnp.concatenate(lse_tiles, axis=1)


def _band_attn(jobs, n_steps):
    kinds, in_specs, args, out_specs, out_shape = [], [], [], [], []
    for job in jobs:
        q = job["q"]
        b, length, width = q.shape
        sinks, with_lse = job.get("sinks"), job["with_lse"]
        assert sinks is None or job["prev_off"] >= 1
        if job["along"] == "rows":
            rows, slabs = length // n_steps, width // A_SLAB
            per_step = rows // BLOCK
            cur = pl.BlockSpec((None, rows, width), lambda i, j: (i, j, 0))
            prev = pl.BlockSpec((None, BLOCK, width),
                                lambda i, j, per_step=per_step: (i, jnp.maximum(j * per_step - 1, 0), 0))
        else:
            rows, slabs = length, width // A_SLAB // n_steps
            cur = pl.BlockSpec((None, rows, slabs * A_SLAB), lambda i, j: (i, 0, j))
            prev = pl.BlockSpec((None, BLOCK, slabs * A_SLAB), lambda i, j: (i, 0, j))
        kinds.append((rows, slabs, job["prev_off"], sinks is not None, with_lse, job["along"] == "slabs"))
        in_specs += [cur, cur, prev, cur, prev]
        args += [q, job["k"], job["k"], job["v"], job["v"]]
        if sinks is not None:
            in_specs.append(pl.BlockSpec(memory_space=pltpu.SMEM))
            args.append(sinks)
        out_specs.append(cur)
        out_shape.append(jax.ShapeDtypeStruct(q.shape, BF16))
        if with_lse:
            out_specs.append(cur)
            out_shape.append(jax.ShapeDtypeStruct(q.shape, F32))
    outs = list(pl.pallas_call(
        functools.partial(_band_attn_kernel, kinds=tuple(kinds)),
        grid=(jobs[0]["q"].shape[0], n_steps),
        in_specs=in_specs,
        out_specs=out_specs,
        out_shape=out_shape,
        compiler_params=pltpu.CompilerParams(dimension_semantics=("arbitrary",) * 2,
                                             vmem_limit_bytes=VMEM_LIMIT),
        name="band_attn",
    )(*args))
    return [[outs.pop(0) for _ in range(1 + job["with_lse"])] for job in jobs]


def _mix_kernel(x_ref, gain_ref, oa0, la0, oa1, la1, oa2, la2, ob_ref, mq_ref, mk_ref, mv_ref,
                wg_ref, bg_ref, woa_ref, wob_ref, wom_ref, wout_ref, out_ref, *stage_refs):
    x = x_ref[...]
    h = _rms_rows(x, gain_ref[...]).astype(BF16)
    stage_refs = list(stage_refs)

    def token_major(blk_ref, dil):
        if dil == 1:
            return blk_ref[...].astype(F32)
        stage_ref = stage_refs.pop()
        rows = blk_ref.shape[0]
        if dil <= FAST_STRIDE:
            for r in range(dil):
                for s in range(A_SLAB // LANES):
                    c0 = r * A_SLAB + s * LANES
                    stage_ref[s, pl.ds(r, rows, stride=dil), :] = blk_ref[:, c0:c0 + LANES].astype(F32)
        else:
            half_ref = stage_refs.pop()
            part = rows * dil // FAST_STRIDE
            for r in range(dil):
                a, b = divmod(r, FAST_STRIDE)
                for s in range(A_SLAB // LANES):
                    c0 = r * A_SLAB + s * LANES
                    half_ref[s, pl.ds(b * part + a, rows, stride=dil // FAST_STRIDE), :] = (
                        blk_ref[:, c0:c0 + LANES].astype(F32))
            for b in range(FAST_STRIDE):
                for s in range(A_SLAB // LANES):
                    stage_ref[s, pl.ds(b, part, stride=FAST_STRIDE), :] = half_ref[s, b * part:(b + 1) * part, :]
        return jnp.concatenate([stage_ref[s] for s in range(A_SLAB // LANES)], axis=1)

    head_cols = [slice(hh * M_HEAD_DIM, (hh + 1) * M_HEAD_DIM) for hh in range(M_HEADS)]
    scores = [lax.dot_general(mq_ref[:, cs], mk_ref[:, cs], (((1,), (1,)), ((), ())),
                              preferred_element_type=F32) for cs in head_cols]
    gate_cols = [slice(i * D_MODEL, (i + 1) * D_MODEL) for i in range(N_BRANCH)]
    def gate(i):
        cs = gate_cols[i]
        return 1.0 + jnp.tanh(jnp.dot(h, wg_ref[:, cs], preferred_element_type=F32) + 0.5 * bg_ref[:, cs])

    gates = [gate(0)]
    branch_b = jnp.dot(ob_ref[...], wob_ref[...], preferred_element_type=F32)
    dils = [dil for _, dil in A_GROUPS]
    l0, l1, l2 = (token_major(r, d) for r, d in zip((la0, la1, la2), dils))
    o0, o1, o2 = (token_major(r, d) for r, d in zip((oa0, oa1, oa2), dils))
    mx = jnp.maximum(jnp.maximum(l0, l1), l2)
    e0, e1, e2 = jnp.exp2(l0 - mx), jnp.exp2(l1 - mx), jnp.exp2(l2 - mx)
    o_a = (e0 * o0 + e1 * o1 + e2 * o2) / (e0 + e1 + e2)

    branch_a = jnp.dot(o_a.astype(BF16), woa_ref[...], preferred_element_type=F32)

    gates.append(gate(1))
    o_m = []
    for s, cs in zip(scores, head_cols):
        p = jnp.exp2(s - jnp.max(s, axis=-1, keepdims=True))
        l = jnp.sum(p, axis=-1, keepdims=True)
        o_m.append(jnp.dot(p.astype(BF16), mv_ref[:, cs], preferred_element_type=F32) / l)
    o_m = jnp.concatenate(o_m, axis=1)
    branch_m = jnp.dot(o_m.astype(BF16), wom_ref[...], preferred_element_type=F32)
    gates.append(gate(2))

    merged = jnp.zeros(x.shape, F32)
    for g, branch in zip(gates, (branch_a, branch_b, branch_m)):
        merged = merged + g * branch
    out_ref[...] = x + jnp.dot(merged.astype(BF16), wout_ref[...], preferred_element_type=F32)


def _mix(x2, attn_gain, oa, la, ob, mq, mk, mv, w_gate, b_gate, w_o_a, w_o_b, w_o_m, w_out, seq):
    t = x2.shape[0]
    tb = TOKEN_BLOCK
    mem_len = mk.shape[1]
    per_seq = seq // tb
    tok = lambda w, d=1: pl.BlockSpec((tb // d, w * d), lambda i: (i, 0))
    mem = pl.BlockSpec((None, mem_len, M_Q_COLS), lambda i: (i // per_seq, 0, 0))
    dils = [dil for _, dil in A_GROUPS]
    return pl.pallas_call(
        _mix_kernel,
        grid=(t // tb,),
        in_specs=[tok(D_MODEL), _const_spec((1, D_MODEL))]
                 + [tok(A_SLAB, d) for d in dils for _ in range(2)]
                 + [tok(B_Q_COLS), tok(M_Q_COLS), mem, mem,
                  _const_spec((D_MODEL, N_BRANCH * D_MODEL)), _const_spec((1, N_BRANCH * D_MODEL)),
                  _const_spec((A_SLAB, D_MODEL)), _const_spec((B_Q_COLS, D_MODEL)),
                  _const_spec((M_Q_COLS, D_MODEL)), _const_spec((D_MODEL, D_MODEL))],
        out_specs=tok(D_MODEL),
        out_shape=jax.ShapeDtypeStruct((t, D_MODEL), F32),
        scratch_shapes=[pltpu.VMEM((A_SLAB // LANES, tb, LANES), F32)]
                       * (2 * sum((d > 1) + (d > FAST_STRIDE) for d in dils)),
        compiler_params=pltpu.CompilerParams(dimension_semantics=("arbitrary",),
                                             vmem_limit_bytes=VMEM_LIMIT),
        name="mix",
    )(x2, attn_gain, oa[0], la[0], oa[1], la[1], oa[2], la[2], ob, mq, mk, mv,
      w_gate, b_gate, w_o_a, w_o_b, w_o_m, w_out)


def _conv_ffn_kernel(x_ref, gain_ref, wup_ref, cw_ref, cb_ref, wdown_ref, out_ref,
                     carry_ref, ext_ref, act_ref, *, per_seq):
    tb = x_ref.shape[0]
    halo = CONV_WIDTH - 1
    pad = SUBLANES

    @pl.when(pl.program_id(0) % per_seq == 0)
    def _():
        carry_ref[...] = jnp.zeros(carry_ref.shape, F32)

    x = x_ref[...]
    h = _rms_rows(x, gain_ref[...]).astype(BF16)

    def conv(c0, scale=1.0):
        cs = slice(c0, c0 + FF_CHUNK)
        u = jnp.dot(h, wup_ref[:, cs], preferred_element_type=F32)
        ext_ref[0:pad, :] = carry_ref[:, cs]
        ext_ref[pad:pad + tb, :] = u
        carry_ref[:, cs] = u[tb - pad:tb, :]
        taps = cw_ref[:, cs] * scale
        c = cb_ref[:, cs] * scale + taps[CONV_WIDTH - 1:CONV_WIDTH] * u
        for j in range(halo):
            c = c + taps[j:j + 1] * ext_ref[pad - halo + j:pad - halo + j + tb, :]
        return c

    for ch in range(D_FF // FF_CHUNK):
        half = conv(ch * FF_CHUNK, 0.5)
        g = conv(D_FF + ch * FF_CHUNK)
        act = (half + half * jnp.tanh(half)) * g
        act_ref[:, ch * FF_CHUNK:(ch + 1) * FF_CHUNK] = act.astype(BF16)
    out_ref[...] = x + jnp.dot(act_ref[...], wdown_ref[...], preferred_element_type=F32)


def _conv_ffn(x2, gain, w_up, conv_w, conv_b, w_down, seq):
    t = x2.shape[0]
    tb = TOKEN_BLOCK
    return pl.pallas_call(
        functools.partial(_conv_ffn_kernel, per_seq=seq // tb),
        grid=(t // tb,),
        in_specs=[pl.BlockSpec((tb, D_MODEL), lambda i: (i, 0)),
                  _const_spec((1, D_MODEL)),
                  _const_spec((D_MODEL, 2 * D_FF)),
                  _const_spec((CONV_WIDTH, 2 * D_FF)),
                  _const_spec((1, 2 * D_FF)),
                  _const_spec((D_FF, D_MODEL))],
        out_specs=pl.BlockSpec((tb, D_MODEL), lambda i: (i, 0)),
        out_shape=jax.ShapeDtypeStruct((t, D_MODEL), F32),
        scratch_shapes=[pltpu.VMEM((SUBLANES, 2 * D_FF), F32),
                        pltpu.VMEM((tb + SUBLANES, FF_CHUNK), F32),
                        pltpu.VMEM((tb, D_FF), BF16)],
        compiler_params=pltpu.CompilerParams(dimension_semantics=("arbitrary",),
                                             vmem_limit_bytes=VMEM_LIMIT),
        name="conv_ffn",
    )(x2, gain, w_up, conv_w, conv_b, w_down)


def _tables():
    lane = np.arange(A_SLAB)
    bd64 = jnp.asarray(lane[:, None] // HEAD_DIM == lane[None, :] // HEAD_DIM, BF16)
    bd128 = jnp.asarray(lane[:, None] // M_HEAD_DIM == lane[None, :] // M_HEAD_DIM, BF16)
    freq_col = jnp.exp(jnp.arange(ROPE_HALF, dtype=F32) * (-2.0 * math.log(ROPE_THETA) / ROPE_DIM))
    freq_col = freq_col.reshape(ROPE_HALF, 1)
    in_head = np.arange(LANES) % HEAD_DIM
    j = np.arange(LANES)[:, None]
    rot = in_head[None, :] < ROPE_DIM
    cos_sel = rot & (j == in_head[None, :] % ROPE_HALF)
    sin_sel = rot & (j == ROPE_HALF + in_head[None, :] % ROPE_HALF)
    lo = in_head[None, :] < ROPE_HALF
    spread = np.concatenate([cos_sel.astype(np.float32), -(sin_sel & lo).astype(np.float32),
                             (sin_sel & ~lo).astype(np.float32)], axis=1)
    one_row = (in_head >= ROPE_DIM).astype(np.float32).reshape(1, LANES)
    return bd64, bd128, freq_col, jnp.asarray(spread, BF16), jnp.asarray(one_row)


def _layer(x2, mem, pos_rows, batch, seq, p):
    bd64, bd128, freq_col, spread, one_row = _tables()
    n_groups = len(A_GROUPS)
    gains = jnp.concatenate([p["a_q_norm"], p["a_k_norm"], p["b_q_norm"][None], p["b_k_norm"][None],
                             p["m_q_norm"].reshape(M_HEAD_DIM // HEAD_DIM, HEAD_DIM),
                             jnp.ones((1, HEAD_DIM), F32)])
    row_bq, row_bk, row_mq = 2 * n_groups, 2 * n_groups + 1, 2 * n_groups + 2
    row_one = row_mq + M_HEAD_DIM // HEAD_DIM
    qk_scale = HEAD_DIM ** -0.5 * LOG2E
    src, scale = [], []
    for g in range(n_groups):
        src += [g] * A_HEADS + [n_groups + g] * A_HEADS + [row_one] * A_HEADS
        scale += [qk_scale] * A_HEADS + [1.0] * (2 * A_HEADS)
    src += [row_bq] * B_Q_HEADS + [row_bk] * B_KV_HEADS + [row_one] * B_KV_HEADS
    scale += [qk_scale] * B_Q_HEADS + [1.0] * (2 * B_KV_HEADS)
    src += [row_mq + i for i in range(M_HEAD_DIM // HEAD_DIM)] * M_HEADS
    scale += [M_HEAD_DIM ** -0.5 * LOG2E] * (M_Q_COLS // HEAD_DIM)
    colgain = (gains[np.asarray(src)] * np.asarray(scale, np.float32)[:, None]).reshape(1, IN_COLS)

    attn_gain = p["attn_norm"].reshape(1, D_MODEL)
    mk, mv, w_in = _mem_kv(mem, p["mem_norm"].reshape(1, D_MODEL), p["w_mem_kv"],
                           p["m_k_norm"].reshape(1, M_HEAD_DIM), p["w_in"])
    later_weights = ("w_gate", "w_o_a", "w_o_b", "w_o_m", "w_out", "w_up", "w_down")
    scales = [0.5 if name in ("w_gate", "w_out") else 1.0 for name in later_weights]
    outs = _in_proj(x2, pos_rows, attn_gain, w_in, colgain, freq_col, spread, one_row,
                    bd64, bd128, [p[name] for name in later_weights], scales)
    (qa0, ka0, va0, qa1, ka1, va1, qa2, ka2, va2, qb, kb, vb, mq) = outs[:13]
    wb = dict(zip(later_weights, outs[13:]))

    def per_batch(t):
        return t.reshape(batch, t.shape[0] // batch, t.shape[-1])

    def flat(t):
        return t.reshape(t.shape[0] * t.shape[1], t.shape[2])

    jobs = [dict(q=per_batch(q), k=per_batch(k), v=per_batch(v), prev_off=0, with_lse=True,
                 along="rows" if dil == 1 else "slabs")
            for (_, dil), (q, k, v) in zip(A_GROUPS, ((qa0, ka0, va0), (qa1, ka1, va1), (qa2, ka2, va2)))]
    jobs.append(dict(q=per_batch(qb), k=per_batch(kb), v=per_batch(vb), prev_off=BLOCK - (B_WINDOW - 1),
                     sinks=p["b_sinks"], with_lse=False, along="rows"))
    *a_outs, (ob,) = _band_attn(jobs, ATTN_STEPS)
    oa = [flat(o) for o, _ in a_outs]
    la = [flat(lse) for _, lse in a_outs]
    ob = flat(ob)

    x2 = _mix(x2, attn_gain, oa, la, ob, mq, mk, mv, wb["w_gate"],
              p["b_gate"].reshape(1, N_BRANCH * D_MODEL), wb["w_o_a"], wb["w_o_b"], wb["w_o_m"],
              wb["w_out"], seq)
    return _conv_ffn(x2, p["ffn_norm"].reshape(1, D_MODEL), wb["w_up"], p["conv_w"],
                     p["conv_b"].reshape(1, 2 * D_FF), wb["w_down"], seq)


def kernel(x, mem, positions, attn_norm, w_in, a_q_norm, a_k_norm, b_q_norm, b_k_norm, b_sinks,
           mem_norm, w_mem_kv, m_q_norm, m_k_norm, w_o_a, w_o_b, w_o_m, w_gate, b_gate, w_out,
           ffn_norm, w_up, conv_w, conv_b, w_down):
    batch, seq, _ = x.shape
    params = dict(attn_norm=attn_norm, w_in=w_in, a_q_norm=a_q_norm, a_k_norm=a_k_norm,
                  b_q_norm=b_q_norm, b_k_norm=b_k_norm, b_sinks=b_sinks, mem_norm=mem_norm,
                  w_mem_kv=w_mem_kv, m_q_norm=m_q_norm, m_k_norm=m_k_norm, w_o_a=w_o_a, w_o_b=w_o_b,
                  w_o_m=w_o_m, w_gate=w_gate, b_gate=b_gate, w_out=w_out, ffn_norm=ffn_norm,
                  w_up=w_up, conv_w=conv_w, conv_b=conv_b, w_down=w_down)
    pos_rows = positions.astype(F32).reshape(batch * seq // TOKEN_BLOCK, 1, TOKEN_BLOCK)
    x2 = x.reshape(batch * seq, D_MODEL)
    for layer in range(attn_norm.shape[0]):
        x2 = _layer(x2, mem, pos_rows, batch, seq, {k: v[layer] for k, v in params.items()})
    return x2.reshape(batch, seq, D_MODEL)
```

```python
import functools
import math

import numpy as np

import jax
import jax.numpy as jnp
from jax import lax
from jax.experimental import pallas as pl
from jax.experimental.pallas import tpu as pltpu

D_MODEL = 1024
HEAD_DIM = 64
A_GROUPS = ((128, 1), (512, 4), (2048, 16))
A_HEADS = 4
A_SLAB = A_HEADS * HEAD_DIM
A_QKV_COLS = len(A_GROUPS) * 3 * A_SLAB
B_Q_HEADS = 8
B_KV_HEADS = 2
B_GROUP = B_Q_HEADS // B_KV_HEADS
B_WINDOW = 128
B_Q_COLS = B_Q_HEADS * HEAD_DIM
B_KV_COLS = B_KV_HEADS * HEAD_DIM
M_HEADS = 4
M_HEAD_DIM = 128
M_Q_COLS = M_HEADS * M_HEAD_DIM
B_Q_OFF = A_QKV_COLS
B_K_OFF = B_Q_OFF + B_Q_COLS
B_V_OFF = B_K_OFF + B_KV_COLS
M_Q_OFF = B_V_OFF + B_KV_COLS
IN_COLS = M_Q_OFF + M_Q_COLS
N_NORMED_TILES = 2 * len(A_GROUPS) + B_Q_COLS // A_SLAB + 1 + M_Q_COLS // A_SLAB
N_BRANCH = 3
D_FF = 2816
CONV_WIDTH = 3
ROPE_THETA = 500000.0
ROPE_DIM = HEAD_DIM // 4
ROPE_HALF = ROPE_DIM // 2
BLOCK = 128
EPS = 1e-6
NEG = -1e30
LOG2E = math.log2(math.e)

LANES = 128
SUBLANES = 8
FAST_STRIDE = 4
BF16_ROWS = 16
HEADS_PER_TILE = LANES // HEAD_DIM
TOKEN_BLOCK = 512
ROW_CHUNK = 32
ATTN_STEPS = 4
FF_CHUNK = 256
VMEM_LIMIT = 56 * 1024 * 1024

F32 = jnp.float32
BF16 = jnp.bfloat16

assert all(window // dil == BLOCK for window, dil in A_GROUPS) and B_WINDOW <= BLOCK
assert B_KV_COLS == LANES and B_KV_HEADS == HEADS_PER_TILE and B_GROUP % HEADS_PER_TILE == 0
assert M_HEAD_DIM == LANES and D_FF % FF_CHUNK == 0 and CONV_WIDTH - 1 <= SUBLANES


def _const_spec(shape):
    return pl.BlockSpec(shape, lambda *_: (0,) * len(shape), pipeline_mode=pl.Buffered(1))


def _rms_rows(x, gain):
    ms = jnp.mean(x * x, axis=-1, keepdims=True)
    return x * lax.rsqrt(ms + EPS) * gain


def _mem_kv_kernel(mem_ref, gain_ref, w_ref, kgain_ref, win_ref, mk_ref, mv_ref, win_out_ref):
    win_out_ref[...] = win_ref[...].astype(BF16)
    hm = _rms_rows(mem_ref[...], gain_ref[...]).astype(BF16)
    kv = jnp.dot(hm, w_ref[...].astype(BF16), preferred_element_type=F32)
    ks = []
    for h in range(M_HEADS):
        kh = kv[:, h * M_HEAD_DIM:(h + 1) * M_HEAD_DIM]
        ks.append(_rms_rows(kh, kgain_ref[...]))
    mk_ref[...] = jnp.concatenate(ks, axis=1).astype(BF16)
    mv_ref[...] = kv[:, M_Q_COLS:].astype(BF16)


def _mem_kv(mem, mem_gain, w_kv, k_gain, w_in):
    b, m, _ = mem.shape
    slab = pl.BlockSpec((_cast_rows(w_in.shape[0], b), w_in.shape[1]), lambda i: (i, 0))
    assert w_in.shape[0] == b * slab.block_shape[0]
    return pl.pallas_call(
        _mem_kv_kernel,
        grid=(b,),
        in_specs=[pl.BlockSpec((None, m, D_MODEL), lambda i: (i, 0, 0)),
                  _const_spec((1, D_MODEL)),
                  _const_spec((D_MODEL, 2 * M_Q_COLS)),
                  _const_spec((1, M_HEAD_DIM)),
                  slab],
        out_specs=[pl.BlockSpec((None, m, M_Q_COLS), lambda i: (i, 0, 0)),
                   pl.BlockSpec((None, m, M_Q_COLS), lambda i: (i, 0, 0)),
                   slab],
        out_shape=[jax.ShapeDtypeStruct((b, m, M_Q_COLS), BF16)] * 2
                  + [jax.ShapeDtypeStruct(w_in.shape, BF16)],
        compiler_params=pltpu.CompilerParams(dimension_semantics=("arbitrary",),
                                             vmem_limit_bytes=VMEM_LIMIT),
        name="mem_kv",
    )(mem, mem_gain, w_kv, k_gain, w_in)


def _in_proj_kernel(x_ref, pos_ref, gain_ref, w_ref, colgain_ref, freq_ref, spread_ref, one_ref,
                    bd64_ref, bd128_ref, *refs, cast_scales):
    n_cast = len(cast_scales)
    cast_in, refs = refs[:n_cast], refs[n_cast:]
    (qa0, ka0, va0, qa1, ka1, va1, qa2, ka2, va2, qb_ref, kb_ref, vb_ref, mq_ref) = refs[:13]
    cast_out, refs = refs[13:13 + n_cast], refs[13 + n_cast:]
    proj_ref, ss_ref, h_ref, cos_ref, sin_lo_ref, sin_hi_ref = refs[:6]
    stage_refs = list(refs[6:])
    tb = x_ref.shape[0]

    for src, dst, scale in zip(cast_in, cast_out, cast_scales):
        w = src[...]
        dst[...] = (w if scale == 1.0 else w * scale).astype(BF16)
    chunks = [slice(r, r + ROW_CHUNK) for r in range(0, tb, ROW_CHUNK)]

    @pl.when(pl.program_id(0) == 0)
    def _():
        proj_ref[...] = jnp.zeros(proj_ref.shape, F32)
        ss_ref[...] = jnp.zeros(ss_ref.shape, F32)

    for rows in chunks:
        h_ref[rows, :] = _rms_rows(x_ref[rows, :], gain_ref[...]).astype(BF16)

    ang = freq_ref[...] * pos_ref[...]
    trig = jnp.concatenate([jnp.cos(ang), jnp.sin(ang),
                            jnp.zeros((LANES - ROPE_DIM, tb), F32)], axis=0).T
    tables = jnp.zeros((tb, 3 * LANES), F32)
    for _ in range(3):
        part = trig.astype(BF16)
        tables = tables + jnp.dot(part, spread_ref[...], preferred_element_type=F32)
        trig = trig - part.astype(F32)
    cos_ref[...] = tables[:, 0:LANES] + one_ref[...]
    sin_lo_ref[...] = tables[:, LANES:2 * LANES]
    sin_hi_ref[...] = tables[:, 2 * LANES:3 * LANES]

    ss_slots = iter(range(ss_ref.shape[0]))
    pending = []

    def flush_pending():
        while pending:
            new, bd, slot = pending.pop()
            ss_ref[slot, :, 0:new.shape[1]] = jnp.dot((new * new).astype(BF16), bd,
                                                      preferred_element_type=F32)

    def proj(c0, width, bd=None):
        tile, off = divmod(c0, A_SLAB)
        assert off + width <= A_SLAB and off % LANES == 0
        t = proj_ref[tile, :, off:off + width]
        ss = None
        if bd is not None:
            slot = next(ss_slots)
            ss = ss_ref[slot, :, 0:width]
        new = jnp.dot(h_ref[...], w_ref[:, c0:c0 + width], preferred_element_type=F32)
        proj_ref[tile, :, off:off + width] = new
        flush_pending()
        if bd is not None:
            pending.append((new, bd, slot))
        return t, ss

    def rope(y, rows):
        parts = []
        for c in range(y.shape[1] // LANES):
            yc = y[:, c * LANES:(c + 1) * LANES]
            parts.append(yc * cos_ref[rows, :]
                         + pltpu.roll(yc, LANES - ROPE_HALF, 1) * sin_lo_ref[rows, :]
                         + pltpu.roll(yc, ROPE_HALF, 1) * sin_hi_ref[rows, :])
        return parts[0] if len(parts) == 1 else jnp.concatenate(parts, axis=1)

    def finish(t_ss, emit, c0, *, dim=None, rotary=False):
        t, ss = t_ss
        if ss is not None:
            gain = colgain_ref[:, c0:c0 + t.shape[1]]
        for rows in chunks:
            y = t[rows]
            if ss is not None:
                y = y * lax.rsqrt(ss[rows] * (1.0 / dim) + EPS) * gain
            if rotary:
                y = rope(y, rows)
            emit(rows, y)

    def to_ref(out_ref, col0=0):
        def emit(rows, y):
            out_ref[rows, col0:col0 + y.shape[1]] = y.astype(BF16)
        return emit

    def store_tile(out_ref, t, dil, c0, **kw):
        if dil == 1:
            finish(t, to_ref(out_ref), c0, **kw)
            return
        stage_ref = stage_refs.pop()

        def emit(rows, y):
            for s in range(A_SLAB // LANES):
                stage_ref[s, rows, :] = y[:, s * LANES:(s + 1) * LANES]
        finish(t, emit, c0, **kw)
        src_ref, step, part = stage_ref, dil, 0
        if dil > FAST_STRIDE:
            src_ref, step, part = stage_refs.pop(), dil // FAST_STRIDE, tb // FAST_STRIDE
            for b in range(FAST_STRIDE):
                for s in range(A_SLAB // LANES):
                    src_ref[s, b * part:(b + 1) * part, :] = stage_ref[s, pl.ds(b, part, stride=FAST_STRIDE), :]
        for r in range(dil):
            a, b = divmod(r, dil // step)
            for s in range(A_SLAB // LANES):
                col = r * A_SLAB + s * LANES
                out_ref[:, col:col + LANES] = src_ref[s, pl.ds(b * part + a, tb // dil, stride=step), :].astype(BF16)

    bd64 = bd64_ref[...]
    qk = dict(dim=HEAD_DIM, rotary=True)
    for g, (q_ref, k_ref, v_ref) in enumerate(((qa0, ka0, va0), (qa1, ka1, va1), (qa2, ka2, va2))):
        c0 = g * 3 * A_SLAB
        dil = A_GROUPS[g][1]
        store_tile(q_ref, proj(c0, A_SLAB, bd64), dil, c0, **qk)
        store_tile(k_ref, proj(c0 + A_SLAB, A_SLAB, bd64), dil, c0 + A_SLAB, **qk)
        store_tile(v_ref, proj(c0 + 2 * A_SLAB, A_SLAB), dil, c0 + 2 * A_SLAB)

    for s in range(B_Q_COLS // A_SLAB):
        c0 = B_Q_OFF + s * A_SLAB
        finish(proj(c0, A_SLAB, bd64), to_ref(qb_ref, s * A_SLAB), c0, **qk)

    def under_query_heads(out_ref):
        def emit(rows, y):
            swapped = pltpu.roll(y, HEAD_DIM, 1)
            first = jnp.where(first_head_lanes, y, swapped).astype(BF16)
            second = jnp.where(first_head_lanes, swapped, y).astype(BF16)
            reps = B_GROUP // HEADS_PER_TILE
            out_ref[rows, :] = jnp.concatenate([first] * reps + [second] * reps, axis=1)
        return emit
    first_head_lanes = lax.broadcasted_iota(jnp.int32, (1, LANES), 1) < HEAD_DIM
    finish(proj(B_K_OFF, B_KV_COLS, bd64_ref[0:B_KV_COLS, 0:B_KV_COLS]), under_query_heads(kb_ref),
           B_K_OFF, **qk)
    finish(proj(B_V_OFF, B_KV_COLS), under_query_heads(vb_ref), B_V_OFF)

    bd128 = bd128_ref[...]
    for s in range(M_Q_COLS // A_SLAB):
        c0 = M_Q_OFF + s * A_SLAB
        finish(proj(c0, A_SLAB, bd128), to_ref(mq_ref, s * A_SLAB), c0, dim=M_HEAD_DIM)
    flush_pending()


def _cast_rows(rows, n_steps):
    for per_step in range(BF16_ROWS, rows + 1, BF16_ROWS):
        if rows % per_step == 0 and rows // per_step <= n_steps:
            return per_step
    raise ValueError(f"no bf16-aligned split of {rows} rows over {n_steps} steps")


def _in_proj(x2, pos_rows, attn_gain, w_in, colgain, freq_col, spread, one_row, bd64, bd128,
             cast_weights, cast_scales):
    t = x2.shape[0]
    tb = TOKEN_BLOCK
    dils = [dil for _, dil in A_GROUPS for _ in range(3)] + [1, 1, 1, 1]
    widths = [A_SLAB] * 9 + [B_Q_COLS, B_Q_COLS, B_Q_COLS, M_Q_COLS]
    n_staged = sum((d > 1) + (d > FAST_STRIDE) for d in dils)
    n = t // tb
    done = lambda i: (jnp.maximum(i - 1, 0), 0)

    def cast_spec(w):
        per_step = _cast_rows(w.shape[0], n + 1)
        last = w.shape[0] // per_step - 1
        return pl.BlockSpec((per_step, w.shape[1]), lambda i: (jnp.minimum(i, last), 0))

    cast_specs = [cast_spec(w) for w in cast_weights]
    return pl.pallas_call(
        functools.partial(_in_proj_kernel, cast_scales=tuple(cast_scales)),
        grid=(n + 1,),
        in_specs=[pl.BlockSpec((tb, D_MODEL), lambda i: (jnp.minimum(i, n - 1), 0)),
                  pl.BlockSpec((None, 1, tb), lambda i: (jnp.maximum(i - 1, 0), 0, 0)),
                  _const_spec((1, D_MODEL)),
                  _const_spec((D_MODEL, IN_COLS)),
                  _const_spec((1, IN_COLS)),
                  _const_spec((ROPE_HALF, 1)),
                  _const_spec((LANES, 3 * LANES)),
                  _const_spec((1, LANES)),
                  _const_spec((A_SLAB, A_SLAB)),
                  _const_spec((A_SLAB, A_SLAB))] + cast_specs,
        out_specs=[pl.BlockSpec((tb // d, w * d), done) for w, d in zip(widths, dils)] + cast_specs,
        out_shape=[jax.ShapeDtypeStruct((t // d, w * d), BF16) for w, d in zip(widths, dils)]
                  + [jax.ShapeDtypeStruct(w.shape, BF16) for w in cast_weights],
        scratch_shapes=[pltpu.VMEM((IN_COLS // A_SLAB, tb, A_SLAB), F32),
                        pltpu.VMEM((N_NORMED_TILES, tb, A_SLAB), F32)]
                       + [pltpu.VMEM((tb, D_MODEL), BF16)] + [pltpu.VMEM((tb, LANES), F32)] * 3
                       + [pltpu.VMEM((A_SLAB // LANES, tb, LANES), F32)] * n_staged,
        compiler_params=pltpu.CompilerParams(dimension_semantics=("arbitrary",),
                                             vmem_limit_bytes=VMEM_LIMIT),
        name="in_proj",
    )(x2, pos_rows, attn_gain, w_in, colgain, freq_col, spread, one_row, bd64, bd128, *cast_weights)


def _band_attn_kernel(*refs, kinds):
    refs = list(refs)
    ins = [[refs.pop(0) for _ in range(5 + has_sink)] for _, _, _, has_sink, _, _ in kinds]
    outs = [[refs.pop(0) for _ in range(1 + with_lse)] for _, _, _, _, with_lse, _ in kinds]
    for (rows, slabs, prev_off, has_sink, with_lse, starts), in_refs, out_refs in zip(kinds, ins, outs):
        _attend(*in_refs[:5], in_refs[5] if has_sink else None, out_refs[0],
                out_refs[1] if with_lse else None,
                first_step=True if starts else pl.program_id(1) == 0,
                rows=rows, slabs=slabs, prev_off=prev_off)


def _attend(q_ref, kc_ref, kp_ref, vc_ref, vp_ref, sink_ref, o_ref, lse_ref, *, first_step, rows, slabs,
            prev_off):
    has_sink = sink_ref is not None
    with_lse = lse_ref is not None
    row = lax.broadcasted_iota(jnp.int32, (BLOCK, 2 * BLOCK), 0)
    col = lax.broadcasted_iota(jnp.int32, (BLOCK, 2 * BLOCK), 1)
    bias = jnp.where(col < BLOCK,
                     jnp.where(col >= row + prev_off, 0.0, NEG),
                     jnp.where(col - BLOCK <= row, 0.0, NEG)).astype(F32)
    bias_first = jnp.where(jnp.logical_and(first_step, col < BLOCK), NEG, bias)
    head_of_lane = lax.broadcasted_iota(jnp.int32, (1, A_SLAB), 1) // HEAD_DIM
    head_mask = [(head_of_lane == hh).astype(BF16) for hh in range(A_HEADS)]
    first_head_lanes = lax.broadcasted_iota(jnp.int32, (1, LANES), 1) < HEAD_DIM
    pack_row = lax.broadcasted_iota(jnp.int32, (BF16_ROWS, A_SLAB), 0)
    nblk = rows // BLOCK

    def drop_first_row(t):
        top = jnp.where(pack_row == 0, jnp.zeros((BF16_ROWS, A_SLAB), BF16), t[0:BF16_ROWS])
        return jnp.concatenate([top, t[BF16_ROWS:]], axis=0)

    for w in range(slabs):
        cs = slice(w * A_SLAB, (w + 1) * A_SLAB)
        biases = {False: [bias] * A_HEADS, True: [bias_first] * A_HEADS}
        if has_sink:
            biases = {first: [jnp.where(col == 0, sink_ref[w * A_HEADS + hh] * LOG2E, b[hh])
                              for hh in range(A_HEADS)] for first, b in biases.items()}
        for jb in range(nblk):
            q = q_ref[jb * BLOCK:(jb + 1) * BLOCK, cs]
            if jb == 0:
                k2 = jnp.concatenate([kp_ref[:, cs], kc_ref[0:BLOCK, cs]], axis=0)
                v2 = jnp.concatenate([vp_ref[:, cs], vc_ref[0:BLOCK, cs]], axis=0)
            else:
                k2 = kc_ref[(jb - 1) * BLOCK:(jb + 1) * BLOCK, cs]
                v2 = vc_ref[(jb - 1) * BLOCK:(jb + 1) * BLOCK, cs]
            if has_sink:
                k2 = drop_first_row(k2)
                v2 = drop_first_row(v2)
            qs = jnp.concatenate([q * head_mask[hh] for hh in range(A_HEADS)], axis=0)
            s = lax.dot_general(qs, k2, (((1,), (1,)), ((), ())), preferred_element_type=F32)
            ps, ms, ls = [], [], []
            for hh in range(A_HEADS):
                sh = s[hh * BLOCK:(hh + 1) * BLOCK] + biases[jb == 0][hh]
                m = jnp.max(sh, axis=-1, keepdims=True)
                p = jnp.exp2(sh - m)
                ms.append(m)
                ls.append(jnp.sum(p, axis=-1, keepdims=True))
                ps.append(p.astype(BF16))
            ost = jnp.dot(jnp.concatenate(ps, axis=0), v2, preferred_element_type=F32)
            o_tiles, lse_tiles = [], []
            for t0 in range(0, A_HEADS, HEADS_PER_TILE):
                lanes = slice(t0 * HEAD_DIM, (t0 + HEADS_PER_TILE) * HEAD_DIM)
                l_tile = jnp.where(first_head_lanes, ls[t0], ls[t0 + 1])
                o_tiles.append(jnp.where(first_head_lanes, ost[t0 * BLOCK:(t0 + 1) * BLOCK, lanes],
                                         ost[(t0 + 1) * BLOCK:(t0 + 2) * BLOCK, lanes]) * (1.0 / l_tile))
                if with_lse:
                    lse_tiles.append(jnp.where(first_head_lanes, ms[t0], ms[t0 + 1]) + jnp.log2(l_tile))
            o_ref[jb * BLOCK:(jb + 1) * BLOCK, cs] = jnp.concatenate(o_tiles, axis=1).astype(o_ref.dtype)
            if with_lse:
                lse_ref[jb * BLOCK:(jb + 1) * BLOCK, cs] = jnp.concatenate(lse_tiles, axis=1)


def _band_attn(jobs, n_steps):
    kinds, in_specs, args, out_specs, out_shape = [], [], [], [], []
    for job in jobs:
        q = job["q"]
        b, length, width = q.shape
        sinks, with_lse = job.get("sinks"), job["with_lse"]
        assert sinks is None or job["prev_off"] >= 1
        if job["along"] == "rows":
            rows, slabs = length // n_steps, width // A_SLAB
            per_step = rows // BLOCK
            cur = pl.BlockSpec((None, rows, width), lambda i, j: (i, j, 0))
            prev = pl.BlockSpec((None, BLOCK, width),
                                lambda i, j, per_step=per_step: (i, jnp.maximum(j * per_step - 1, 0), 0))
        else:
            rows, slabs = length, width // A_SLAB // n_steps
            cur = pl.BlockSpec((None, rows, slabs * A_SLAB), lambda i, j: (i, 0, j))
            prev = pl.BlockSpec((None, BLOCK, slabs * A_SLAB), lambda i, j: (i, 0, j))
        kinds.append((rows, slabs, job["prev_off"], sinks is not None, with_lse, job["along"] == "slabs"))
        in_specs += [cur, cur, prev, cur, prev]
        args += [q, job["k"], job["k"], job["v"], job["v"]]
        if sinks is not None:
            in_specs.append(pl.BlockSpec(memory_space=pltpu.SMEM))
            args.append(sinks)
        out_specs.append(cur)
        out_shape.append(jax.ShapeDtypeStruct(q.shape, BF16))
        if with_lse:
            out_specs.append(cur)
            out_shape.append(jax.ShapeDtypeStruct(q.shape, F32))
    outs = list(pl.pallas_call(
        functools.partial(_band_attn_kernel, kinds=tuple(kinds)),
        grid=(jobs[0]["q"].shape[0], n_steps),
        in_specs=in_specs,
        out_specs=out_specs,
        out_shape=out_shape,
        compiler_params=pltpu.CompilerParams(dimension_semantics=("arbitrary",) * 2,
                                             vmem_limit_bytes=VMEM_LIMIT),
        name="band_attn",
    )(*args))
    return [[outs.pop(0) for _ in range(1 + job["with_lse"])] for job in jobs]


def _mix_kernel(x_ref, gain_ref, oa0, la0, oa1, la1, oa2, la2, ob_ref, mq_ref, mk_ref, mv_ref,
                wg_ref, bg_ref, woa_ref, wob_ref, wom_ref, wout_ref, out_ref, *stage_refs):
    x = x_ref[...]
    h = _rms_rows(x, gain_ref[...]).astype(BF16)
    stage_refs = list(stage_refs)

    def token_major(blk_ref, dil):
        if dil == 1:
            return blk_ref[...].astype(F32)
        stage_ref = stage_refs.pop()
        rows = blk_ref.shape[0]
        if dil <= FAST_STRIDE:
            for r in range(dil):
                for s in range(A_SLAB // LANES):
                    c0 = r * A_SLAB + s * LANES
                    stage_ref[s, pl.ds(r, rows, stride=dil), :] = blk_ref[:, c0:c0 + LANES].astype(F32)
        else:
            half_ref = stage_refs.pop()
            part = rows * dil // FAST_STRIDE
            for r in range(dil):
                a, b = divmod(r, FAST_STRIDE)
                for s in range(A_SLAB // LANES):
                    c0 = r * A_SLAB + s * LANES
                    half_ref[s, pl.ds(b * part + a, rows, stride=dil // FAST_STRIDE), :] = (
                        blk_ref[:, c0:c0 + LANES].astype(F32))
            for b in range(FAST_STRIDE):
                for s in range(A_SLAB // LANES):
                    stage_ref[s, pl.ds(b, part, stride=FAST_STRIDE), :] = half_ref[s, b * part:(b + 1) * part, :]
        return jnp.concatenate([stage_ref[s] for s in range(A_SLAB // LANES)], axis=1)

    head_cols = [slice(hh * M_HEAD_DIM, (hh + 1) * M_HEAD_DIM) for hh in range(M_HEADS)]
    scores = [lax.dot_general(mq_ref[:, cs], mk_ref[:, cs], (((1,), (1,)), ((), ())),
                              preferred_element_type=F32) for cs in head_cols]
    gate_cols = [slice(i * D_MODEL, (i + 1) * D_MODEL) for i in range(N_BRANCH)]
    def gate(i):
        cs = gate_cols[i]
        return 1.0 + jnp.tanh(jnp.dot(h, wg_ref[:, cs], preferred_element_type=F32) + 0.5 * bg_ref[:, cs])

    gates = [gate(0)]
    branch_b = jnp.dot(ob_ref[...], wob_ref[...], preferred_element_type=F32)
    dils = [dil for _, dil in A_GROUPS]
    l0, l1, l2 = (token_major(r, d) for r, d in zip((la0, la1, la2), dils))
    o0, o1, o2 = (token_major(r, d) for r, d in zip((oa0, oa1, oa2), dils))
    mx = jnp.maximum(jnp.maximum(l0, l1), l2)
    e0, e1, e2 = jnp.exp2(l0 - mx), jnp.exp2(l1 - mx), jnp.exp2(l2 - mx)
    o_a = (e0 * o0 + e1 * o1 + e2 * o2) / (e0 + e1 + e2)

    branch_a = jnp.dot(o_a.astype(BF16), woa_ref[...], preferred_element_type=F32)

    gates.append(gate(1))
    o_m = []
    for s, cs in zip(scores, head_cols):
        p = jnp.exp2(s - jnp.max(s, axis=-1, keepdims=True))
        l = jnp.sum(p, axis=-1, keepdims=True)
        o_m.append(jnp.dot(p.astype(BF16), mv_ref[:, cs], preferred_element_type=F32) / l)
    o_m = jnp.concatenate(o_m, axis=1)
    branch_m = jnp.dot(o_m.astype(BF16), wom_ref[...], preferred_element_type=F32)
    gates.append(gate(2))

    merged = jnp.zeros(x.shape, F32)
    for g, branch in zip(gates, (branch_a, branch_b, branch_m)):
        merged = merged + g * branch
    out_ref[...] = x + jnp.dot(merged.astype(BF16), wout_ref[...], preferred_element_type=F32)


def _mix(x2, attn_gain, oa, la, ob, mq, mk, mv, w_gate, b_gate, w_o_a, w_o_b, w_o_m, w_out, seq):
    t = x2.shape[0]
    tb = TOKEN_BLOCK
    mem_len = mk.shape[1]
    per_seq = seq // tb
    tok = lambda w, d=1: pl.BlockSpec((tb // d, w * d), lambda i: (i, 0))
    mem = pl.BlockSpec((None, mem_len, M_Q_COLS), lambda i: (i // per_seq, 0, 0))
    dils = [dil for _, dil in A_GROUPS]
    return pl.pallas_call(
        _mix_kernel,
        grid=(t // tb,),
        in_specs=[tok(D_MODEL), _const_spec((1, D_MODEL))]
                 + [tok(A_SLAB, d) for d in dils for _ in range(2)]
                 + [tok(B_Q_COLS), tok(M_Q_COLS), mem, mem,
                  _const_spec((D_MODEL, N_BRANCH * D_MODEL)), _const_spec((1, N_BRANCH * D_MODEL)),
                  _const_spec((A_SLAB, D_MODEL)), _const_spec((B_Q_COLS, D_MODEL)),
                  _const_spec((M_Q_COLS, D_MODEL)), _const_spec((D_MODEL, D_MODEL))],
        out_specs=tok(D_MODEL),
        out_shape=jax.ShapeDtypeStruct((t, D_MODEL), F32),
        scratch_shapes=[pltpu.VMEM((A_SLAB // LANES, tb, LANES), F32)]
                       * (2 * sum((d > 1) + (d > FAST_STRIDE) for d in dils)),
        compiler_params=pltpu.CompilerParams(dimension_semantics=("arbitrary",),
                                             vmem_limit_bytes=VMEM_LIMIT),
        name="mix",
    )(x2, attn_gain, oa[0], la[0], oa[1], la[1], oa[2], la[2], ob, mq, mk, mv,
      w_gate, b_gate, w_o_a, w_o_b, w_o_m, w_out)


def _conv_ffn_kernel(x_ref, gain_ref, wup_ref, cw_ref, cb_ref, wdown_ref, out_ref,
                     carry_ref, ext_ref, act_ref, *, per_seq):
    tb = x_ref.shape[0]
    halo = CONV_WIDTH - 1
    pad = SUBLANES

    @pl.when(pl.program_id(0) % per_seq == 0)
    def _():
        carry_ref[...] = jnp.zeros(carry_ref.shape, F32)

    x = x_ref[...]
    h = _rms_rows(x, gain_ref[...]).astype(BF16)

    def conv(c0, scale=1.0):
        cs = slice(c0, c0 + FF_CHUNK)
        u = jnp.dot(h, wup_ref[:, cs], preferred_element_type=F32)
        ext_ref[0:pad, :] = carry_ref[:, cs]
        ext_ref[pad:pad + tb, :] = u
        carry_ref[:, cs] = u[tb - pad:tb, :]
        taps = cw_ref[:, cs] * scale
        c = cb_ref[:, cs] * scale + taps[CONV_WIDTH - 1:CONV_WIDTH] * u
        for j in range(halo):
            c = c + taps[j:j + 1] * ext_ref[pad - halo + j:pad - halo + j + tb, :]
        return c

    for ch in range(D_FF // FF_CHUNK):
        half = conv(ch * FF_CHUNK, 0.5)
        g = conv(D_FF + ch * FF_CHUNK)
        act = (half + half * jnp.tanh(half)) * g
        act_ref[:, ch * FF_CHUNK:(ch + 1) * FF_CHUNK] = act.astype(BF16)
    out_ref[...] = x + jnp.dot(act_ref[...], wdown_ref[...], preferred_element_type=F32)


def _conv_ffn(x2, gain, w_up, conv_w, conv_b, w_down, seq):
    t = x2.shape[0]
    tb = TOKEN_BLOCK
    return pl.pallas_call(
        functools.partial(_conv_ffn_kernel, per_seq=seq // tb),
        grid=(t // tb,),
        in_specs=[pl.BlockSpec((tb, D_MODEL), lambda i: (i, 0)),
                  _const_spec((1, D_MODEL)),
                  _const_spec((D_MODEL, 2 * D_FF)),
                  _const_spec((CONV_WIDTH, 2 * D_FF)),
                  _const_spec((1, 2 * D_FF)),
                  _const_spec((D_FF, D_MODEL))],
        out_specs=pl.BlockSpec((tb, D_MODEL), lambda i: (i, 0)),
        out_shape=jax.ShapeDtypeStruct((t, D_MODEL), F32),
        scratch_shapes=[pltpu.VMEM((SUBLANES, 2 * D_FF), F32),
                        pltpu.VMEM((tb + SUBLANES, FF_CHUNK), F32),
                        pltpu.VMEM((tb, D_FF), BF16)],
        compiler_params=pltpu.CompilerParams(dimension_semantics=("arbitrary",),
                                             vmem_limit_bytes=VMEM_LIMIT),
        name="conv_ffn",
    )(x2, gain, w_up, conv_w, conv_b, w_down)


def _tables():
    lane = np.arange(A_SLAB)
    bd64 = jnp.asarray(lane[:, None] // HEAD_DIM == lane[None, :] // HEAD_DIM, BF16)
    bd128 = jnp.asarray(lane[:, None] // M_HEAD_DIM == lane[None, :] // M_HEAD_DIM, BF16)
    freq_col = jnp.exp(jnp.arange(ROPE_HALF, dtype=F32) * (-2.0 * math.log(ROPE_THETA) / ROPE_DIM))
    freq_col = freq_col.reshape(ROPE_HALF, 1)
    in_head = np.arange(LANES) % HEAD_DIM
    j = np.arange(LANES)[:, None]
    rot = in_head[None, :] < ROPE_DIM
    cos_sel = rot & (j == in_head[None, :] % ROPE_HALF)
    sin_sel = rot & (j == ROPE_HALF + in_head[None, :] % ROPE_HALF)
    lo = in_head[None, :] < ROPE_HALF
    spread = np.concatenate([cos_sel.astype(np.float32), -(sin_sel & lo).astype(np.float32),
                             (sin_sel & ~lo).astype(np.float32)], axis=1)
    one_row = (in_head >= ROPE_DIM).astype(np.float32).reshape(1, LANES)
    return bd64, bd128, freq_col, jnp.asarray(spread, BF16), jnp.asarray(one_row)


def _layer(x2, mem, pos_rows, batch, seq, p):
    bd64, bd128, freq_col, spread, one_row = _tables()
    n_groups = len(A_GROUPS)
    gains = jnp.concatenate([p["a_q_norm"], p["a_k_norm"], p["b_q_norm"][None], p["b_k_norm"][None],
                             p["m_q_norm"].reshape(M_HEAD_DIM // HEAD_DIM, HEAD_DIM),
                             jnp.ones((1, HEAD_DIM), F32)])
    row_bq, row_bk, row_mq = 2 * n_groups, 2 * n_groups + 1, 2 * n_groups + 2
    row_one = row_mq + M_HEAD_DIM // HEAD_DIM
    qk_scale = HEAD_DIM ** -0.5 * LOG2E
    src, scale = [], []
    for g in range(n_groups):
        src += [g] * A_HEADS + [n_groups + g] * A_HEADS + [row_one] * A_HEADS
        scale += [qk_scale] * A_HEADS + [1.0] * (2 * A_HEADS)
    src += [row_bq] * B_Q_HEADS + [row_bk] * B_KV_HEADS + [row_one] * B_KV_HEADS
    scale += [qk_scale] * B_Q_HEADS + [1.0] * (2 * B_KV_HEADS)
    src += [row_mq + i for i in range(M_HEAD_DIM // HEAD_DIM)] * M_HEADS
    scale += [M_HEAD_DIM ** -0.5 * LOG2E] * (M_Q_COLS // HEAD_DIM)
    colgain = (gains[np.asarray(src)] * np.asarray(scale, np.float32)[:, None]).reshape(1, IN_COLS)

    attn_gain = p["attn_norm"].reshape(1, D_MODEL)
    mk, mv, w_in = _mem_kv(mem, p["mem_norm"].reshape(1, D_MODEL), p["w_mem_kv"],
                           p["m_k_norm"].reshape(1, M_HEAD_DIM), p["w_in"])
    later_weights = ("w_gate", "w_o_a", "w_o_b", "w_o_m", "w_out", "w_up", "w_down")
    scales = [0.5 if name in ("w_gate", "w_out") else 1.0 for name in later_weights]
    outs = _in_proj(x2, pos_rows, attn_gain, w_in, colgain, freq_col, spread, one_row,
                    bd64, bd128, [p[name] for name in later_weights], scales)
    (qa0, ka0, va0, qa1, ka1, va1, qa2, ka2, va2, qb, kb, vb, mq) = outs[:13]
    wb = dict(zip(later_weights, outs[13:]))

    def per_batch(t):
        return t.reshape(batch, t.shape[0] // batch, t.shape[-1])

    def flat(t):
        return t.reshape(t.shape[0] * t.shape[1], t.shape[2])

    jobs = [dict(q=per_batch(q), k=per_batch(k), v=per_batch(v), prev_off=0, with_lse=True,
                 along="rows" if dil == 1 else "slabs")
            for (_, dil), (q, k, v) in zip(A_GROUPS, ((qa0, ka0, va0), (qa1, ka1, va1), (qa2, ka2, va2)))]
    jobs.append(dict(q=per_batch(qb), k=per_batch(kb), v=per_batch(vb), prev_off=BLOCK - (B_WINDOW - 1),
                     sinks=p["b_sinks"], with_lse=False, along="rows"))
    *a_outs, (ob,) = _band_attn(jobs, ATTN_STEPS)
    oa = [flat(o) for o, _ in a_outs]
    la = [flat(lse) for _, lse in a_outs]
    ob = flat(ob)

    x2 = _mix(x2, attn_gain, oa, la, ob, mq, mk, mv, wb["w_gate"],
              p["b_gate"].reshape(1, N_BRANCH * D_MODEL), wb["w_o_a"], wb["w_o_b"], wb["w_o_m"],
              wb["w_out"], seq)
    return _conv_ffn(x2, p["ffn_norm"].reshape(1, D_MODEL), wb["w_up"], p["conv_w"],
                     p["conv_b"].reshape(1, 2 * D_FF), wb["w_down"], seq)


def kernel(x, mem, positions, attn_norm, w_in, a_q_norm, a_k_norm, b_q_norm, b_k_norm, b_sinks,
           mem_norm, w_mem_kv, m_q_norm, m_k_norm, w_o_a, w_o_b, w_o_m, w_gate, b_gate, w_out,
           ffn_norm, w_up, conv_w, conv_b, w_down):
    batch, seq, _ = x.shape
    params = dict(attn_norm=attn_norm, w_in=w_in, a_q_norm=a_q_norm, a_k_norm=a_k_norm,
                  b_q_norm=b_q_norm, b_k_norm=b_k_norm, b_sinks=b_sinks, mem_norm=mem_norm,
                  w_mem_kv=w_mem_kv, m_q_norm=m_q_norm, m_k_norm=m_k_norm, w_o_a=w_o_a, w_o_b=w_o_b,
                  w_o_m=w_o_m, w_gate=w_gate, b_gate=b_gate, w_out=w_out, ffn_norm=ffn_norm,
                  w_up=w_up, conv_w=conv_w, conv_b=conv_b, w_down=w_down)
    pos_rows = positions.astype(F32).reshape(batch * seq // TOKEN_BLOCK, 1, TOKEN_BLOCK)
    x2 = x.reshape(batch * seq, D_MODEL)
    for layer in range(attn_norm.shape[0]):
        x2 = _layer(x2, mem, pos_rows, batch, seq, {k: v[layer] for k, v in params.items()})
    return x2.reshape(batch, seq, D_MODEL)
```

```python
import functools
import math

import numpy as np

import jax
import jax.numpy as jnp
from jax import lax
from jax.experimental import pallas as pl
from jax.experimental.pallas import tpu as pltpu

D_MODEL = 1024
HEAD_DIM = 64
A_GROUPS = ((128, 1), (512, 4), (2048, 16))
A_HEADS = 4
A_SLAB = A_HEADS * HEAD_DIM
A_QKV_COLS = len(A_GROUPS) * 3 * A_SLAB
B_Q_HEADS = 8
B_KV_HEADS = 2
B_GROUP = B_Q_HEADS // B_KV_HEADS
B_WINDOW = 128
B_Q_COLS = B_Q_HEADS * HEAD_DIM
B_KV_COLS = B_KV_HEADS * HEAD_DIM
M_HEADS = 4
M_HEAD_DIM = 128
M_Q_COLS = M_HEADS * M_HEAD_DIM
B_Q_OFF = A_QKV_COLS
B_K_OFF = B_Q_OFF + B_Q_COLS
B_V_OFF = B_K_OFF + B_KV_COLS
M_Q_OFF = B_V_OFF + B_KV_COLS
IN_COLS = M_Q_OFF + M_Q_COLS
N_NORMED_TILES = 2 * len(A_GROUPS) + B_Q_COLS // A_SLAB + 1 + M_Q_COLS // A_SLAB
N_BRANCH = 3
D_FF = 2816
CONV_WIDTH = 3
ROPE_THETA = 500000.0
ROPE_DIM = HEAD_DIM // 4
ROPE_HALF = ROPE_DIM // 2
BLOCK = 128
EPS = 1e-6
NEG = -1e30
LOG2E = math.log2(math.e)

LANES = 128
SUBLANES = 8
FAST_STRIDE = 4
BF16_ROWS = 16
HEADS_PER_TILE = LANES // HEAD_DIM
TOKEN_BLOCK = 512
ROW_CHUNK = 32
ATTN_STEPS = 4
FF_CHUNK = 256
VMEM_LIMIT = 56 * 1024 * 1024

F32 = jnp.float32
BF16 = jnp.bfloat16

assert all(window // dil == BLOCK for window, dil in A_GROUPS) and B_WINDOW <= BLOCK
assert B_KV_COLS == LANES and B_KV_HEADS == HEADS_PER_TILE and B_GROUP % HEADS_PER_TILE == 0
assert M_HEAD_DIM == LANES and D_FF % FF_CHUNK == 0 and CONV_WIDTH - 1 <= SUBLANES


def _const_spec(shape):
    return pl.BlockSpec(shape, lambda *_: (0,) * len(shape), pipeline_mode=pl.Buffered(1))


def _rms_rows(x, gain):
    ms = jnp.mean(x * x, axis=-1, keepdims=True)
    return x * lax.rsqrt(ms + EPS) * gain


def _mem_kv_kernel(mem_ref, gain_ref, w_ref, kgain_ref, win_ref, mk_ref, mv_ref, win_out_ref):
    win_out_ref[...] = win_ref[...].astype(BF16)
    hm = _rms_rows(mem_ref[...], gain_ref[...]).astype(BF16)
    kv = jnp.dot(hm, w_ref[...].astype(BF16), preferred_element_type=F32)
    ks = []
    for h in range(M_HEADS):
        kh = kv[:, h * M_HEAD_DIM:(h + 1) * M_HEAD_DIM]
        ks.append(_rms_rows(kh, kgain_ref[...]))
    mk_ref[...] = jnp.concatenate(ks, axis=1).astype(BF16)
    mv_ref[...] = kv[:, M_Q_COLS:].astype(BF16)


def _mem_kv(mem, mem_gain, w_kv, k_gain, w_in):
    b, m, _ = mem.shape
    slab = pl.BlockSpec((_cast_rows(w_in.shape[0], b), w_in.shape[1]), lambda i: (i, 0))
    assert w_in.shape[0] == b * slab.block_shape[0]
    return pl.pallas_call(
        _mem_kv_kernel,
        grid=(b,),
        in_specs=[pl.BlockSpec((None, m, D_MODEL), lambda i: (i, 0, 0)),
                  _const_spec((1, D_MODEL)),
                  _const_spec((D_MODEL, 2 * M_Q_COLS)),
                  _const_spec((1, M_HEAD_DIM)),
                  slab],
        out_specs=[pl.BlockSpec((None, m, M_Q_COLS), lambda i: (i, 0, 0)),
                   pl.BlockSpec((None, m, M_Q_COLS), lambda i: (i, 0, 0)),
                   slab],
        out_shape=[jax.ShapeDtypeStruct((b, m, M_Q_COLS), BF16)] * 2
                  + [jax.ShapeDtypeStruct(w_in.shape, BF16)],
        compiler_params=pltpu.CompilerParams(dimension_semantics=("arbitrary",),
                                             vmem_limit_bytes=VMEM_LIMIT),
        name="mem_kv",
    )(mem, mem_gain, w_kv, k_gain, w_in)


def _in_proj_kernel(x_ref, pos_ref, gain_ref, w_ref, colgain_ref, freq_ref, spread_ref, one_ref,
                    bd64_ref, bd128_ref, *refs, cast_scales):
    n_cast = len(cast_scales)
    cast_in, refs = refs[:n_cast], refs[n_cast:]
    (qa0, ka0, va0, qa1, ka1, va1, qa2, ka2, va2, qb_ref, kb_ref, vb_ref, mq_ref) = refs[:13]
    cast_out, refs = refs[13:13 + n_cast], refs[13 + n_cast:]
    proj_ref, ss_ref, h_ref, cos_ref, sin_lo_ref, sin_hi_ref = refs[:6]
    stage_refs = list(refs[6:])
    tb = x_ref.shape[0]

    for src, dst, scale in zip(cast_in, cast_out, cast_scales):
        w = src[...]
        dst[...] = (w if scale == 1.0 else w * scale).astype(BF16)
    chunks = [slice(r, r + ROW_CHUNK) for r in range(0, tb, ROW_CHUNK)]

    @pl.when(pl.program_id(0) == 0)
    def _():
        proj_ref[...] = jnp.zeros(proj_ref.shape, F32)
        ss_ref[...] = jnp.zeros(ss_ref.shape, F32)

    for rows in chunks:
        h_ref[rows, :] = _rms_rows(x_ref[rows, :], gain_ref[...]).astype(BF16)

    ang = freq_ref[...] * pos_ref[...]
    trig = jnp.concatenate([jnp.cos(ang), jnp.sin(ang),
                            jnp.zeros((LANES - ROPE_DIM, tb), F32)], axis=0).T
    tables = jnp.zeros((tb, 3 * LANES), F32)
    for _ in range(3):
        part = trig.astype(BF16)
        tables = tables + jnp.dot(part, spread_ref[...], preferred_element_type=F32)
        trig = trig - part.astype(F32)
    cos_ref[...] = tables[:, 0:LANES] + one_ref[...]
    sin_lo_ref[...] = tables[:, LANES:2 * LANES]
    sin_hi_ref[...] = tables[:, 2 * LANES:3 * LANES]

    ss_slots = iter(range(ss_ref.shape[0]))
    pending = []

    def flush_pending():
        while pending:
            new, bd, slot = pending.pop()
            ss_ref[slot, :, 0:new.shape[1]] = jnp.dot((new * new).astype(BF16), bd,
                                                      preferred_element_type=F32)

    def proj(c0, width, bd=None):
        tile, off = divmod(c0, A_SLAB)
        assert off + width <= A_SLAB and off % LANES == 0
        t = proj_ref[tile, :, off:off + width]
        ss = None
        if bd is not None:
            slot = next(ss_slots)
            ss = ss_ref[slot, :, 0:width]
        new = jnp.dot(h_ref[...], w_ref[:, c0:c0 + width], preferred_element_type=F32)
        proj_ref[tile, :, off:off + width] = new
        flush_pending()
        if bd is not None:
            pending.append((new, bd, slot))
        return t, ss

    def rope(y, rows):
        parts = []
        for c in range(y.shape[1] // LANES):
            yc = y[:, c * LANES:(c + 1) * LANES]
            parts.append(yc * cos_ref[rows, :]
                         + pltpu.roll(yc, LANES - ROPE_HALF, 1) * sin_lo_ref[rows, :]
                         + pltpu.roll(yc, ROPE_HALF, 1) * sin_hi_ref[rows, :])
        return parts[0] if len(parts) == 1 else jnp.concatenate(parts, axis=1)

    def finish(t_ss, emit, c0, *, dim=None, rotary=False):
        t, ss = t_ss
        if ss is not None:
            gain = colgain_ref[:, c0:c0 + t.shape[1]]
        for rows in chunks:
            y = t[rows]
            if ss is not None:
                y = y * lax.rsqrt(ss[rows] * (1.0 / dim) + EPS) * gain
            if rotary:
                y = rope(y, rows)
            emit(rows, y)

    def to_ref(out_ref, col0=0):
        def emit(rows, y):
            out_ref[rows, col0:col0 + y.shape[1]] = y.astype(BF16)
        return emit

    def store_tile(out_ref, t, dil, c0, **kw):
        if dil == 1:
            finish(t, to_ref(out_ref), c0, **kw)
            return
        stage_ref = stage_refs.pop()

        def emit(rows, y):
            for s in range(A_SLAB // LANES):
                stage_ref[s, rows, :] = y[:, s * LANES:(s + 1) * LANES]
        finish(t, emit, c0, **kw)
        src_ref, step, part = stage_ref, dil, 0
        if dil > FAST_STRIDE:
            src_ref, step, part = stage_refs.pop(), dil // FAST_STRIDE, tb // FAST_STRIDE
            for b in range(FAST_STRIDE):
                for s in range(A_SLAB // LANES):
                    src_ref[s, b * part:(b + 1) * part, :] = stage_ref[s, pl.ds(b, part, stride=FAST_STRIDE), :]
        for r in range(dil):
            a, b = divmod(r, dil // step)
            for s in range(A_SLAB // LANES):
                col = r * A_SLAB + s * LANES
                out_ref[:, col:col + LANES] = src_ref[s, pl.ds(b * part + a, tb // dil, stride=step), :].astype(BF16)

    bd64 = bd64_ref[...]
    qk = dict(dim=HEAD_DIM, rotary=True)
    for g, (q_ref, k_ref, v_ref) in enumerate(((qa0, ka0, va0), (qa1, ka1, va1), (qa2, ka2, va2))):
        c0 = g * 3 * A_SLAB
        dil = A_GROUPS[g][1]
        store_tile(q_ref, proj(c0, A_SLAB, bd64), dil, c0, **qk)
        store_tile(k_ref, proj(c0 + A_SLAB, A_SLAB, bd64), dil, c0 + A_SLAB, **qk)
        store_tile(v_ref, proj(c0 + 2 * A_SLAB, A_SLAB), dil, c0 + 2 * A_SLAB)

    for s in range(B_Q_COLS // A_SLAB):
        c0 = B_Q_OFF + s * A_SLAB
        finish(proj(c0, A_SLAB, bd64), to_ref(qb_ref, s * A_SLAB), c0, **qk)

    def under_query_heads(out_ref):
        def emit(rows, y):
            swapped = pltpu.roll(y, HEAD_DIM, 1)
            first = jnp.where(first_head_lanes, y, swapped).astype(BF16)
            second = jnp.where(first_head_lanes, swapped, y).astype(BF16)
            reps = B_GROUP // HEADS_PER_TILE
            out_ref[rows, :] = jnp.concatenate([first] * reps + [second] * reps, axis=1)
        return emit
    first_head_lanes = lax.broadcasted_iota(jnp.int32, (1, LANES), 1) < HEAD_DIM
    finish(proj(B_K_OFF, B_KV_COLS, bd64_ref[0:B_KV_COLS, 0:B_KV_COLS]), under_query_heads(kb_ref),
           B_K_OFF, **qk)
    finish(proj(B_V_OFF, B_KV_COLS), under_query_heads(vb_ref), B_V_OFF)

    bd128 = bd128_ref[...]
    for s in range(M_Q_COLS // A_SLAB):
        c0 = M_Q_OFF + s * A_SLAB
        finish(proj(c0, A_SLAB, bd128), to_ref(mq_ref, s * A_SLAB), c0, dim=M_HEAD_DIM)
    flush_pending()


def _cast_rows(rows, n_steps):
    for per_step in range(BF16_ROWS, rows + 1, BF16_ROWS):
        if rows % per_step == 0 and rows // per_step <= n_steps:
            return per_step
    raise ValueError(f"no bf16-aligned split of {rows} rows over {n_steps} steps")


def _in_proj(x2, pos_rows, attn_gain, w_in, colgain, freq_col, spread, one_row, bd64, bd128,
             cast_weights, cast_scales):
    t = x2.shape[0]
    tb = TOKEN_BLOCK
    dils = [dil for _, dil in A_GROUPS for _ in range(3)] + [1, 1, 1, 1]
    widths = [A_SLAB] * 9 + [B_Q_COLS, B_Q_COLS, B_Q_COLS, M_Q_COLS]
    n_staged = sum((d > 1) + (d > FAST_STRIDE) for d in dils)
    n = t // tb
    done = lambda i: (jnp.maximum(i - 1, 0), 0)

    def cast_spec(w):
        per_step = _cast_rows(w.shape[0], n + 1)
        last = w.shape[0] // per_step - 1
        return pl.BlockSpec((per_step, w.shape[1]), lambda i: (jnp.minimum(i, last), 0))

    cast_specs = [cast_spec(w) for w in cast_weights]
    return pl.pallas_call(
        functools.partial(_in_proj_kernel, cast_scales=tuple(cast_scales)),
        grid=(n + 1,),
        in_specs=[pl.BlockSpec((tb, D_MODEL), lambda i: (jnp.minimum(i, n - 1), 0)),
                  pl.BlockSpec((None, 1, tb), lambda i: (jnp.maximum(i - 1, 0), 0, 0)),
                  _const_spec((1, D_MODEL)),
                  _const_spec((D_MODEL, IN_COLS)),
                  _const_spec((1, IN_COLS)),
                  _const_spec((ROPE_HALF, 1)),
                  _const_spec((LANES, 3 * LANES)),
                  _const_spec((1, LANES)),
                  _const_spec((A_SLAB, A_SLAB)),
                  _const_spec((A_SLAB, A_SLAB))] + cast_specs,
        out_specs=[pl.BlockSpec((tb // d, w * d), done) for w, d in zip(widths, dils)] + cast_specs,
        out_shape=[jax.ShapeDtypeStruct((t // d, w * d), BF16) for w, d in zip(widths, dils)]
                  + [jax.ShapeDtypeStruct(w.shape, BF16) for w in cast_weights],
        scratch_shapes=[pltpu.VMEM((IN_COLS // A_SLAB, tb, A_SLAB), F32),
                        pltpu.VMEM((N_NORMED_TILES, tb, A_SLAB), F32)]
                       + [pltpu.VMEM((tb, D_MODEL), BF16)] + [pltpu.VMEM((tb, LANES), F32)] * 3
                       + [pltpu.VMEM((A_SLAB // LANES, tb, LANES), F32)] * n_staged,
        compiler_params=pltpu.CompilerParams(dimension_semantics=("arbitrary",),
                                             vmem_limit_bytes=VMEM_LIMIT),
        name="in_proj",
    )(x2, pos_rows, attn_gain, w_in, colgain, freq_col, spread, one_row, bd64, bd128, *cast_weights)


def _band_attn_kernel(*refs, kinds):
    refs = list(refs)
    ins = [[refs.pop(0) for _ in range(5 + has_sink)] for _, _, _, has_sink, _, _ in kinds]
    outs = [[refs.pop(0) for _ in range(1 + with_lse)] for _, _, _, _, with_lse, _ in kinds]
    for (rows, slabs, prev_off, has_sink, with_lse, starts), in_refs, out_refs in zip(kinds, ins, outs):
        _attend(*in_refs[:5], in_refs[5] if has_sink else None, out_refs[0],
                out_refs[1] if with_lse else None,
                first_step=True if starts else pl.program_id(1) == 0,
                rows=rows, slabs=slabs, prev_off=prev_off)


def _attend(q_ref, kc_ref, kp_ref, vc_ref, vp_ref, sink_ref, o_ref, lse_ref, *, first_step, rows, slabs,
            prev_off):
    has_sink = sink_ref is not None
    with_lse = lse_ref is not None
    row = lax.broadcasted_iota(jnp.int32, (BLOCK, 2 * BLOCK), 0)
    col = lax.broadcasted_iota(jnp.int32, (BLOCK, 2 * BLOCK), 1)
    bias = jnp.where(col < BLOCK,
                     jnp.where(col >= row + prev_off, 0.0, NEG),
                     jnp.where(col - BLOCK <= row, 0.0, NEG)).astype(F32)
    bias_first = jnp.where(jnp.logical_and(first_step, col < BLOCK), NEG, bias)
    head_of_lane = lax.broadcasted_iota(jnp.int32, (1, A_SLAB), 1) // HEAD_DIM
    head_mask = [(head_of_lane == hh).astype(BF16) for hh in range(A_HEADS)]
    first_head_lanes = lax.broadcasted_iota(jnp.int32, (1, LANES), 1) < HEAD_DIM
    pack_row = lax.broadcasted_iota(jnp.int32, (BF16_ROWS, A_SLAB), 0)
    nblk = rows // BLOCK

    def drop_first_row(t):
        top = jnp.where(pack_row == 0, jnp.zeros((BF16_ROWS, A_SLAB), BF16), t[0:BF16_ROWS])
        return jnp.concatenate([top, t[BF16_ROWS:]], axis=0)

    for w in range(slabs):
        cs = slice(w * A_SLAB, (w + 1) * A_SLAB)
        biases = {False: [bias] * A_HEADS, True: [bias_first] * A_HEADS}
        if has_sink:
            biases = {first: [jnp.where(col == 0, sink_ref[w * A_HEADS + hh] * LOG2E, b[hh])
                              for hh in range(A_HEADS)] for first, b in biases.items()}
        for jb in range(nblk):
            q = q_ref[jb * BLOCK:(jb + 1) * BLOCK, cs]
            if jb == 0:
                k2 = jnp.concatenate([kp_ref[:, cs], kc_ref[0:BLOCK, cs]], axis=0)
                v2 = jnp.concatenate([vp_ref[:, cs], vc_ref[0:BLOCK, cs]], axis=0)
            else:
                k2 = kc_ref[(jb - 1) * BLOCK:(jb + 1) * BLOCK, cs]
                v2 = vc_ref[(jb - 1) * BLOCK:(jb + 1) * BLOCK, cs]
            if has_sink:
                k2 = drop_first_row(k2)
                v2 = drop_first_row(v2)
            qs = jnp.concatenate([q * head_mask[hh] for hh in range(A_HEADS)], axis=0)
            s = lax.dot_general(qs, k2, (((1,), (1,)), ((), ())), preferred_element_type=F32)
            ps, ms, ls = [], [], []
            for hh in range(A_HEADS):
                sh = s[hh * BLOCK:(hh + 1) * BLOCK] + biases[jb == 0][hh]
                m = jnp.max(sh, axis=-1, keepdims=True)
                p = jnp.exp2(sh - m)
                ms.append(m)
                ls.append(jnp.sum(p, axis=-1, keepdims=True))
                ps.append(p.astype(BF16))
            ost = jnp.dot(jnp.concatenate(ps, axis=0), v2, preferred_element_type=F32)
            o_tiles, lse_tiles = [], []
            for t0 in range(0, A_HEADS, HEADS_PER_TILE):
                lanes = slice(t0 * HEAD_DIM, (t0 + HEADS_PER_TILE) * HEAD_DIM)
                l_tile = jnp.where(first_head_lanes, ls[t0], ls[t0 + 1])
                o_tiles.append(jnp.where(first_head_lanes, ost[t0 * BLOCK:(t0 + 1) * BLOCK, lanes],
                                         ost[(t0 + 1) * BLOCK:(t0 + 2) * BLOCK, lanes]) * (1.0 / l_tile))
                if with_lse:
                    lse_tiles.append(jnp.where(first_head_lanes, ms[t0], ms[t0 + 1]) + jnp.log2(l_tile))
            o_ref[jb * BLOCK:(jb + 1) * BLOCK, cs] = jnp.concatenate(o_tiles, axis=1).astype(o_ref.dtype)
            if with_lse:
                lse_ref[jb * BLOCK:(jb + 1) * BLOCK, cs] = jnp.concatenate(lse_tiles, axis=1)


def _band_attn(jobs, n_steps):
    kinds, in_specs, args, out_specs, out_shape = [], [], [], [], []
    for job in jobs:
        q = job["q"]
        b, length, width = q.shape
        sinks, with_lse = job.get("sinks"), job["with_lse"]
        assert sinks is None or job["prev_off"] >= 1
        if job["along"] == "rows":
            rows, slabs = length // n_steps, width // A_SLAB
            per_step = rows // BLOCK
            cur = pl.BlockSpec((None, rows, width), lambda i, j: (i, j, 0))
            prev = pl.BlockSpec((None, BLOCK, width),
                                lambda i, j, per_step=per_step: (i, jnp.maximum(j * per_step - 1, 0), 0))
        else:
            rows, slabs = length, width // A_SLAB // n_steps
            cur = pl.BlockSpec((None, rows, slabs * A_SLAB), lambda i, j: (i, 0, j))
            prev = pl.BlockSpec((None, BLOCK, slabs * A_SLAB), lambda i, j: (i, 0, j))
        kinds.append((rows, slabs, job["prev_off"], sinks is not None, with_lse, job["along"] == "slabs"))
        in_specs += [cur, cur, prev, cur, prev]
        args += [q, job["k"], job["k"], job["v"], job["v"]]
        if sinks is not None:
            in_specs.append(pl.BlockSpec(memory_space=pltpu.SMEM))
            args.append(sinks)
        out_specs.append(cur)
        out_shape.append(jax.ShapeDtypeStruct(q.shape, BF16))
        if with_lse:
            out_specs.append(cur)
            out_shape.append(jax.ShapeDtypeStruct(q.shape, F32))
    outs = list(pl.pallas_call(
        functools.partial(_band_attn_kernel, kinds=tuple(kinds)),
        grid=(jobs[0]["q"].shape[0], n_steps),
        in_specs=in_specs,
        out_specs=out_specs,
        out_shape=out_shape,
        compiler_params=pltpu.CompilerParams(dimension_semantics=("arbitrary",) * 2,
                                             vmem_limit_bytes=VMEM_LIMIT),
        name="band_attn",
    )(*args))
    return [[outs.pop(0) for _ in range(1 + job["with_lse"])] for job in jobs]


def _mix_kernel(x_ref, gain_ref, oa0, la0, oa1, la1, oa2, la2, ob_ref, mq_ref, mk_ref, mv_ref,
                wg_ref, bg_ref, woa_ref, wob_ref, wom_ref, wout_ref, out_ref, *stage_refs):
    x = x_ref[...]
    h = _rms_rows(x, gain_ref[...]).astype(BF16)
    stage_refs = list(stage_refs)

    def token_major(blk_ref, dil):
        if dil == 1:
            return lambda rows: blk_ref[rows, :].astype(F32)
        stage_ref = stage_refs.pop()
        rows = blk_ref.shape[0]
        if dil <= FAST_STRIDE:
            for r in range(dil):
                for s in range(A_SLAB // LANES):
                    c0 = r * A_SLAB + s * LANES
                    stage_ref[s, pl.ds(r, rows, stride=dil), :] = blk_ref[:, c0:c0 + LANES].astype(F32)
        else:
            half_ref = stage_refs.pop()
            part = rows * dil // FAST_STRIDE
            for r in range(dil):
                a, b = divmod(r, FAST_STRIDE)
                for s in range(A_SLAB // LANES):
                    c0 = r * A_SLAB + s * LANES
                    half_ref[s, pl.ds(b * part + a, rows, stride=dil // FAST_STRIDE), :] = (
                        blk_ref[:, c0:c0 + LANES].astype(F32))
            for b in range(FAST_STRIDE):
                for s in range(A_SLAB // LANES):
                    stage_ref[s, pl.ds(b, part, stride=FAST_STRIDE), :] = half_ref[s, b * part:(b + 1) * part, :]
        return lambda rows: jnp.concatenate([stage_ref[s, rows, :] for s in range(A_SLAB // LANES)], axis=1)

    head_cols = [slice(hh * M_HEAD_DIM, (hh + 1) * M_HEAD_DIM) for hh in range(M_HEADS)]
    scores = [lax.dot_general(mq_ref[:, cs], mk_ref[:, cs], (((1,), (1,)), ((), ())),
                              preferred_element_type=F32) for cs in head_cols]
    gate_cols = [slice(i * D_MODEL, (i + 1) * D_MODEL) for i in range(N_BRANCH)]
    def gate(i, cols=slice(0, D_MODEL)):
        cs = slice(gate_cols[i].start + cols.start, gate_cols[i].start + cols.stop)
        return 1.0 + jnp.tanh(jnp.dot(h, wg_ref[:, cs], preferred_element_type=F32) + 0.5 * bg_ref[:, cs])

    gates = [gate(0)]
    branch_b = jnp.dot(ob_ref[...], wob_ref[...], preferred_element_type=F32)
    dils = [dil for _, dil in A_GROUPS]
    lse = [token_major(r, d) for r, d in zip((la0, la1, la2), dils)]
    outs = [token_major(r, d) for r, d in zip((oa0, oa1, oa2), dils)]
    n_parts = D_MODEL // A_SLAB
    branch_a, gate_1 = [], []
    for part in range(n_parts):
        rows = slice(part * (x.shape[0] // n_parts), (part + 1) * (x.shape[0] // n_parts))
        l0, l1, l2 = (read(rows) for read in lse)
        o0, o1, o2 = (read(rows) for read in outs)
        mx = jnp.maximum(jnp.maximum(l0, l1), l2)
        e0, e1, e2 = jnp.exp2(l0 - mx), jnp.exp2(l1 - mx), jnp.exp2(l2 - mx)
        o_a = (e0 * o0 + e1 * o1 + e2 * o2) / (e0 + e1 + e2)
        gate_1.append(gate(1, slice(part * A_SLAB, (part + 1) * A_SLAB)))
        branch_a.append(jnp.dot(o_a.astype(BF16), woa_ref[...], preferred_element_type=F32))
    branch_a = jnp.concatenate(branch_a, axis=0)
    gates.append(jnp.concatenate(gate_1, axis=1))
    o_m, gate_2 = [], []
    for hh, (s, cs) in enumerate(zip(scores, head_cols)):
        p = jnp.exp2(s - jnp.max(s, axis=-1, keepdims=True))
        l = jnp.sum(p, axis=-1, keepdims=True)
        gate_2.append(gate(2, slice(hh * (D_MODEL // M_HEADS), (hh + 1) * (D_MODEL // M_HEADS))))
        o_m.append(jnp.dot(p.astype(BF16), mv_ref[:, cs], preferred_element_type=F32) / l)
    o_m = jnp.concatenate(o_m, axis=1)
    branch_m = jnp.dot(o_m.astype(BF16), wom_ref[...], preferred_element_type=F32)
    gates.append(jnp.concatenate(gate_2, axis=1))

    merged = jnp.zeros(x.shape, F32)
    for g, branch in zip(gates, (branch_a, branch_b, branch_m)):
        merged = merged + g * branch
    out_ref[...] = x + jnp.dot(merged.astype(BF16), wout_ref[...], preferred_element_type=F32)


def _mix(x2, attn_gain, oa, la, ob, mq, mk, mv, w_gate, b_gate, w_o_a, w_o_b, w_o_m, w_out, seq):
    t = x2.shape[0]
    tb = TOKEN_BLOCK
    mem_len = mk.shape[1]
    per_seq = seq // tb
    tok = lambda w, d=1: pl.BlockSpec((tb // d, w * d), lambda i: (i, 0))
    mem = pl.BlockSpec((None, mem_len, M_Q_COLS), lambda i: (i // per_seq, 0, 0))
    dils = [dil for _, dil in A_GROUPS]
    return pl.pallas_call(
        _mix_kernel,
        grid=(t // tb,),
        in_specs=[tok(D_MODEL), _const_spec((1, D_MODEL))]
                 + [tok(A_SLAB, d) for d in dils for _ in range(2)]
                 + [tok(B_Q_COLS), tok(M_Q_COLS), mem, mem,
                  _const_spec((D_MODEL, N_BRANCH * D_MODEL)), _const_spec((1, N_BRANCH * D_MODEL)),
                  _const_spec((A_SLAB, D_MODEL)), _const_spec((B_Q_COLS, D_MODEL)),
                  _const_spec((M_Q_COLS, D_MODEL)), _const_spec((D_MODEL, D_MODEL))],
        out_specs=tok(D_MODEL),
        out_shape=jax.ShapeDtypeStruct((t, D_MODEL), F32),
        scratch_shapes=[pltpu.VMEM((A_SLAB // LANES, tb, LANES), F32)]
                       * (2 * sum((d > 1) + (d > FAST_STRIDE) for d in dils)),
        compiler_params=pltpu.CompilerParams(dimension_semantics=("arbitrary",),
                                             vmem_limit_bytes=VMEM_LIMIT),
        name="mix",
    )(x2, attn_gain, oa[0], la[0], oa[1], la[1], oa[2], la[2], ob, mq, mk, mv,
      w_gate, b_gate, w_o_a, w_o_b, w_o_m, w_out)


def _conv_ffn_kernel(x_ref, gain_ref, wup_ref, cw_ref, cb_ref, wdown_ref, out_ref,
                     carry_ref, ext_ref, act_ref, *, per_seq):
    tb = x_ref.shape[0]
    halo = CONV_WIDTH - 1
    pad = SUBLANES

    @pl.when(pl.program_id(0) % per_seq == 0)
    def _():
        carry_ref[...] = jnp.zeros(carry_ref.shape, F32)

    x = x_ref[...]
    h = _rms_rows(x, gain_ref[...]).astype(BF16)

    def conv(c0, scale=1.0):
        cs = slice(c0, c0 + FF_CHUNK)
        u = jnp.dot(h, wup_ref[:, cs], preferred_element_type=F32)
        ext_ref[0:pad, :] = carry_ref[:, cs]
        ext_ref[pad:pad + tb, :] = u
        carry_ref[:, cs] = u[tb - pad:tb, :]
        taps = cw_ref[:, cs] * scale
        c = cb_ref[:, cs] * scale + taps[CONV_WIDTH - 1:CONV_WIDTH] * u
        for j in range(halo):
            c = c + taps[j:j + 1] * ext_ref[pad - halo + j:pad - halo + j + tb, :]
        return c

    for ch in range(D_FF // FF_CHUNK):
        half = conv(ch * FF_CHUNK, 0.5)
        g = conv(D_FF + ch * FF_CHUNK)
        act = (half + half * jnp.tanh(half)) * g
        act_ref[:, ch * FF_CHUNK:(ch + 1) * FF_CHUNK] = act.astype(BF16)
    out_ref[...] = x + jnp.dot(act_ref[...], wdown_ref[...], preferred_element_type=F32)


def _conv_ffn(x2, gain, w_up, conv_w, conv_b, w_down, seq):
    t = x2.shape[0]
    tb = TOKEN_BLOCK
    return pl.pallas_call(
        functools.partial(_conv_ffn_kernel, per_seq=seq // tb),
        grid=(t // tb,),
        in_specs=[pl.BlockSpec((tb, D_MODEL), lambda i: (i, 0)),
                  _const_spec((1, D_MODEL)),
                  _const_spec((D_MODEL, 2 * D_FF)),
                  _const_spec((CONV_WIDTH, 2 * D_FF)),
                  _const_spec((1, 2 * D_FF)),
                  _const_spec((D_FF, D_MODEL))],
        out_specs=pl.BlockSpec((tb, D_MODEL), lambda i: (i, 0)),
        out_shape=jax.ShapeDtypeStruct((t, D_MODEL), F32),
        scratch_shapes=[pltpu.VMEM((SUBLANES, 2 * D_FF), F32),
                        pltpu.VMEM((tb + SUBLANES, FF_CHUNK), F32),
                        pltpu.VMEM((tb, D_FF), BF16)],
        compiler_params=pltpu.CompilerParams(dimension_semantics=("arbitrary",),
                                             vmem_limit_bytes=VMEM_LIMIT),
        name="conv_ffn",
    )(x2, gain, w_up, conv_w, conv_b, w_down)


def _tables():
    lane = np.arange(A_SLAB)
    bd64 = jnp.asarray(lane[:, None] // HEAD_DIM == lane[None, :] // HEAD_DIM, BF16)
    bd128 = jnp.asarray(lane[:, None] // M_HEAD_DIM == lane[None, :] // M_HEAD_DIM, BF16)
    freq_col = jnp.exp(jnp.arange(ROPE_HALF, dtype=F32) * (-2.0 * math.log(ROPE_THETA) / ROPE_DIM))
    freq_col = freq_col.reshape(ROPE_HALF, 1)
    in_head = np.arange(LANES) % HEAD_DIM
    j = np.arange(LANES)[:, None]
    rot = in_head[None, :] < ROPE_DIM
    cos_sel = rot & (j == in_head[None, :] % ROPE_HALF)
    sin_sel = rot & (j == ROPE_HALF + in_head[None, :] % ROPE_HALF)
    lo = in_head[None, :] < ROPE_HALF
    spread = np.concatenate([cos_sel.astype(np.float32), -(sin_sel & lo).astype(np.float32),
                             (sin_sel & ~lo).astype(np.float32)], axis=1)
    one_row = (in_head >= ROPE_DIM).astype(np.float32).reshape(1, LANES)
    return bd64, bd128, freq_col, jnp.asarray(spread, BF16), jnp.asarray(one_row)


def _layer(x2, mem, pos_rows, batch, seq, p):
    bd64, bd128, freq_col, spread, one_row = _tables()
    n_groups = len(A_GROUPS)
    gains = jnp.concatenate([p["a_q_norm"], p["a_k_norm"], p["b_q_norm"][None], p["b_k_norm"][None],
                             p["m_q_norm"].reshape(M_HEAD_DIM // HEAD_DIM, HEAD_DIM),
                             jnp.ones((1, HEAD_DIM), F32)])
    row_bq, row_bk, row_mq = 2 * n_groups, 2 * n_groups + 1, 2 * n_groups + 2
    row_one = row_mq + M_HEAD_DIM // HEAD_DIM
    qk_scale = HEAD_DIM ** -0.5 * LOG2E
    src, scale = [], []
    for g in range(n_groups):
        src += [g] * A_HEADS + [n_groups + g] * A_HEADS + [row_one] * A_HEADS
        scale += [qk_scale] * A_HEADS + [1.0] * (2 * A_HEADS)
    src += [row_bq] * B_Q_HEADS + [row_bk] * B_KV_HEADS + [row_one] * B_KV_HEADS
    scale += [qk_scale] * B_Q_HEADS + [1.0] * (2 * B_KV_HEADS)
    src += [row_mq + i for i in range(M_HEAD_DIM // HEAD_DIM)] * M_HEADS
    scale += [M_HEAD_DIM ** -0.5 * LOG2E] * (M_Q_COLS // HEAD_DIM)
    colgain = (gains[np.asarray(src)] * np.asarray(scale, np.float32)[:, None]).reshape(1, IN_COLS)

    attn_gain = p["attn_norm"].reshape(1, D_MODEL)
    mk, mv, w_in = _mem_kv(mem, p["mem_norm"].reshape(1, D_MODEL), p["w_mem_kv"],
                           p["m_k_norm"].reshape(1, M_HEAD_DIM), p["w_in"])
    later_weights = ("w_gate", "w_o_a", "w_o_b", "w_o_m", "w_out", "w_up", "w_down")
    scales = [0.5 if name in ("w_gate", "w_out") else 1.0 for name in later_weights]
    outs = _in_proj(x2, pos_rows, attn_gain, w_in, colgain, freq_col, spread, one_row,
                    bd64, bd128, [p[name] for name in later_weights], scales)
    (qa0, ka0, va0, qa1, ka1, va1, qa2, ka2, va2, qb, kb, vb, mq) = outs[:13]
    wb = dict(zip(later_weights, outs[13:]))

    def per_batch(t):
        return t.reshape(batch, t.shape[0] // batch, t.shape[-1])

    def flat(t):
        return t.reshape(t.shape[0] * t.shape[1], t.shape[2])

    jobs = [dict(q=per_batch(q), k=per_batch(k), v=per_batch(v), prev_off=0, with_lse=True,
                 along="rows" if dil == 1 else "slabs")
            for (_, dil), (q, k, v) in zip(A_GROUPS, ((qa0, ka0, va0), (qa1, ka1, va1), (qa2, ka2, va2)))]
    jobs.append(dict(q=per_batch(qb), k=per_batch(kb), v=per_batch(vb), prev_off=BLOCK - (B_WINDOW - 1),
                     sinks=p["b_sinks"], with_lse=False, along="rows"))
    *a_outs, (ob,) = _band_attn(jobs, ATTN_STEPS)
    oa = [flat(o) for o, _ in a_outs]
    la = [flat(lse) for _, lse in a_outs]
    ob = flat(ob)

    x2 = _mix(x2, attn_gain, oa, la, ob, mq, mk, mv, wb["w_gate"],
              p["b_gate"].reshape(1, N_BRANCH * D_MODEL), wb["w_o_a"], wb["w_o_b"], wb["w_o_m"],
              wb["w_out"], seq)
    return _conv_ffn(x2, p["ffn_norm"].reshape(1, D_MODEL), wb["w_up"], p["conv_w"],
                     p["conv_b"].reshape(1, 2 * D_FF), wb["w_down"], seq)


def kernel(x, mem, positions, attn_norm, w_in, a_q_norm, a_k_norm, b_q_norm, b_k_norm, b_sinks,
           mem_norm, w_mem_kv, m_q_norm, m_k_norm, w_o_a, w_o_b, w_o_m, w_gate, b_gate, w_out,
           ffn_norm, w_up, conv_w, conv_b, w_down):
    batch, seq, _ = x.shape
    params = dict(attn_norm=attn_norm, w_in=w_in, a_q_norm=a_q_norm, a_k_norm=a_k_norm,
                  b_q_norm=b_q_norm, b_k_norm=b_k_norm, b_sinks=b_sinks, mem_norm=mem_norm,
                  w_mem_kv=w_mem_kv, m_q_norm=m_q_norm, m_k_norm=m_k_norm, w_o_a=w_o_a, w_o_b=w_o_b,
                  w_o_m=w_o_m, w_gate=w_gate, b_gate=b_gate, w_out=w_out, ffn_norm=ffn_norm,
                  w_up=w_up, conv_w=conv_w, conv_b=conv_b, w_down=w_down)
    pos_rows = positions.astype(F32).reshape(batch * seq // TOKEN_BLOCK, 1, TOKEN_BLOCK)
    x2 = x.reshape(batch * seq, D_MODEL)
    for layer in range(attn_norm.shape[0]):
        x2 = _layer(x2, mem, pos_rows, batch, seq, {k: v[layer] for k, v in params.items()})
    return x2.reshape(batch, seq, D_MODEL)
```

```python
import functools
import math

import numpy as np

import jax
import jax.numpy as jnp
from jax import lax
from jax.experimental import pallas as pl
from jax.experimental.pallas import tpu as pltpu

D_MODEL = 1024
HEAD_DIM = 64
A_GROUPS = ((128, 1), (512, 4), (2048, 16))
A_HEADS = 4
A_SLAB = A_HEADS * HEAD_DIM
A_QKV_COLS = len(A_GROUPS) * 3 * A_SLAB
B_Q_HEADS = 8
B_KV_HEADS = 2
B_GROUP = B_Q_HEADS // B_KV_HEADS
B_WINDOW = 128
B_Q_COLS = B_Q_HEADS * HEAD_DIM
B_KV_COLS = B_KV_HEADS * HEAD_DIM
M_HEADS = 4
M_HEAD_DIM = 128
M_Q_COLS = M_HEADS * M_HEAD_DIM
B_Q_OFF = A_QKV_COLS
B_K_OFF = B_Q_OFF + B_Q_COLS
B_V_OFF = B_K_OFF + B_KV_COLS
M_Q_OFF = B_V_OFF + B_KV_COLS
IN_COLS = M_Q_OFF + M_Q_COLS
N_NORMED_TILES = 2 * len(A_GROUPS) + B_Q_COLS // A_SLAB + 1 + M_Q_COLS // A_SLAB
N_BRANCH = 3
D_FF = 2816
CONV_WIDTH = 3
ROPE_THETA = 500000.0
ROPE_DIM = HEAD_DIM // 4
ROPE_HALF = ROPE_DIM // 2
BLOCK = 128
EPS = 1e-6
NEG = -1e30
LOG2E = math.log2(math.e)

LANES = 128
SUBLANES = 8
FAST_STRIDE = 4
BF16_ROWS = 16
HEADS_PER_TILE = LANES // HEAD_DIM
TOKEN_BLOCK = 512
ROW_CHUNK = 32
ATTN_STEPS = 4
FF_CHUNK = 256
VMEM_LIMIT = 56 * 1024 * 1024

F32 = jnp.float32
BF16 = jnp.bfloat16

assert all(window // dil == BLOCK for window, dil in A_GROUPS) and B_WINDOW <= BLOCK
assert B_KV_COLS == LANES and B_KV_HEADS == HEADS_PER_TILE and B_GROUP % HEADS_PER_TILE == 0
assert M_HEAD_DIM == LANES and D_FF % FF_CHUNK == 0 and CONV_WIDTH - 1 <= SUBLANES


def _const_spec(shape):
    return pl.BlockSpec(shape, lambda *_: (0,) * len(shape), pipeline_mode=pl.Buffered(1))


def _rms_rows(x, gain):
    ms = jnp.mean(x * x, axis=-1, keepdims=True)
    return x * lax.rsqrt(ms + EPS) * gain


def _mem_kv_kernel(mem_ref, gain_ref, w_ref, kgain_ref, win_ref, mk_ref, mv_ref, win_out_ref):
    win_out_ref[...] = win_ref[...].astype(BF16)
    hm = _rms_rows(mem_ref[...], gain_ref[...]).astype(BF16)
    kv = jnp.dot(hm, w_ref[...].astype(BF16), preferred_element_type=F32)
    ks = []
    for h in range(M_HEADS):
        kh = kv[:, h * M_HEAD_DIM:(h + 1) * M_HEAD_DIM]
        ks.append(_rms_rows(kh, kgain_ref[...]))
    mk_ref[...] = jnp.concatenate(ks, axis=1).astype(BF16)
    mv_ref[...] = kv[:, M_Q_COLS:].astype(BF16)


def _mem_kv(mem, mem_gain, w_kv, k_gain, w_in):
    b, m, _ = mem.shape
    slab = pl.BlockSpec((_cast_rows(w_in.shape[0], b), w_in.shape[1]), lambda i: (i, 0))
    assert w_in.shape[0] == b * slab.block_shape[0]
    return pl.pallas_call(
        _mem_kv_kernel,
        grid=(b,),
        in_specs=[pl.BlockSpec((None, m, D_MODEL), lambda i: (i, 0, 0)),
                  _const_spec((1, D_MODEL)),
                  _const_spec((D_MODEL, 2 * M_Q_COLS)),
                  _const_spec((1, M_HEAD_DIM)),
                  slab],
        out_specs=[pl.BlockSpec((None, m, M_Q_COLS), lambda i: (i, 0, 0)),
                   pl.BlockSpec((None, m, M_Q_COLS), lambda i: (i, 0, 0)),
                   slab],
        out_shape=[jax.ShapeDtypeStruct((b, m, M_Q_COLS), BF16)] * 2
                  + [jax.ShapeDtypeStruct(w_in.shape, BF16)],
        compiler_params=pltpu.CompilerParams(dimension_semantics=("arbitrary",),
                                             vmem_limit_bytes=VMEM_LIMIT),
        name="mem_kv",
    )(mem, mem_gain, w_kv, k_gain, w_in)


def _in_proj_kernel(x_ref, pos_ref, gain_ref, w_ref, colgain_ref, freq_ref, spread_ref, one_ref,
                    bd64_ref, bd128_ref, *refs, cast_scales):
    n_cast = len(cast_scales)
    cast_in, refs = refs[:n_cast], refs[n_cast:]
    (qa0, ka0, va0, qa1, ka1, va1, qa2, ka2, va2, qb_ref, kb_ref, vb_ref, mq_ref) = refs[:13]
    cast_out, refs = refs[13:13 + n_cast], refs[13 + n_cast:]
    proj_ref, ss_ref, h_ref, cos_ref, sin_lo_ref, sin_hi_ref = refs[:6]
    stage_refs = list(refs[6:])
    tb = x_ref.shape[0]

    for src, dst, scale in zip(cast_in, cast_out, cast_scales):
        w = src[...]
        dst[...] = (w if scale == 1.0 else w * scale).astype(BF16)
    chunks = [slice(r, r + ROW_CHUNK) for r in range(0, tb, ROW_CHUNK)]

    @pl.when(pl.program_id(0) == 0)
    def _():
        proj_ref[...] = jnp.zeros(proj_ref.shape, F32)
        ss_ref[...] = jnp.zeros(ss_ref.shape, F32)

    for rows in chunks:
        h_ref[rows, :] = _rms_rows(x_ref[rows, :], gain_ref[...]).astype(BF16)

    ang = freq_ref[...] * pos_ref[...]
    trig = jnp.concatenate([jnp.cos(ang), jnp.sin(ang),
                            jnp.zeros((LANES - ROPE_DIM, tb), F32)], axis=0).T
    tables = jnp.zeros((tb, 3 * LANES), F32)
    for _ in range(3):
        part = trig.astype(BF16)
        tables = tables + jnp.dot(part, spread_ref[...], preferred_element_type=F32)
        trig = trig - part.astype(F32)
    cos_ref[...] = tables[:, 0:LANES] + one_ref[...]
    sin_lo_ref[...] = tables[:, LANES:2 * LANES]
    sin_hi_ref[...] = tables[:, 2 * LANES:3 * LANES]

    ss_slots = iter(range(ss_ref.shape[0]))
    pending = []

    def flush_pending():
        while pending:
            new, bd, slot = pending.pop()
            ss_ref[slot, :, 0:new.shape[1]] = jnp.dot((new * new).astype(BF16), bd,
                                                      preferred_element_type=F32)

    def proj(c0, width, bd=None):
        tile, off = divmod(c0, A_SLAB)
        assert off + width <= A_SLAB and off % LANES == 0
        t = proj_ref[tile, :, off:off + width]
        ss = None
        if bd is not None:
            slot = next(ss_slots)
            ss = ss_ref[slot, :, 0:width]
        new = jnp.dot(h_ref[...], w_ref[:, c0:c0 + width], preferred_element_type=F32)
        proj_ref[tile, :, off:off + width] = new
        flush_pending()
        if bd is not None:
            pending.append((new, bd, slot))
        return t, ss

    def rope(y, rows):
        parts = []
        for c in range(y.shape[1] // LANES):
            yc = y[:, c * LANES:(c + 1) * LANES]
            parts.append(yc * cos_ref[rows, :]
                         + pltpu.roll(yc, LANES - ROPE_HALF, 1) * sin_lo_ref[rows, :]
                         + pltpu.roll(yc, ROPE_HALF, 1) * sin_hi_ref[rows, :])
        return parts[0] if len(parts) == 1 else jnp.concatenate(parts, axis=1)

    def finish(t_ss, emit, c0, *, dim=None, rotary=False):
        t, ss = t_ss
        if ss is not None:
            gain = colgain_ref[:, c0:c0 + t.shape[1]]
        for rows in chunks:
            y = t[rows]
            if ss is not None:
                y = y * lax.rsqrt(ss[rows] * (1.0 / dim) + EPS) * gain
            if rotary:
                y = rope(y, rows)
            emit(rows, y)

    def to_ref(out_ref, col0=0):
        def emit(rows, y):
            out_ref[rows, col0:col0 + y.shape[1]] = y.astype(BF16)
        return emit

    def store_tile(out_ref, t, dil, c0, **kw):
        if dil == 1:
            finish(t, to_ref(out_ref), c0, **kw)
            return
        stage_ref = stage_refs.pop()

        def emit(rows, y):
            for s in range(A_SLAB // LANES):
                stage_ref[s, rows, :] = y[:, s * LANES:(s + 1) * LANES]
        finish(t, emit, c0, **kw)
        src_ref, step, part = stage_ref, dil, 0
        if dil > FAST_STRIDE:
            src_ref, step, part = stage_refs.pop(), dil // FAST_STRIDE, tb // FAST_STRIDE
            for b in range(FAST_STRIDE):
                for s in range(A_SLAB // LANES):
                    src_ref[s, b * part:(b + 1) * part, :] = stage_ref[s, pl.ds(b, part, stride=FAST_STRIDE), :]
        for r in range(dil):
            a, b = divmod(r, dil // step)
            for s in range(A_SLAB // LANES):
                col = r * A_SLAB + s * LANES
                out_ref[:, col:col + LANES] = src_ref[s, pl.ds(b * part + a, tb // dil, stride=step), :].astype(BF16)

    bd64 = bd64_ref[...]
    qk = dict(dim=HEAD_DIM, rotary=True)
    for g, (q_ref, k_ref, v_ref) in enumerate(((qa0, ka0, va0), (qa1, ka1, va1), (qa2, ka2, va2))):
        c0 = g * 3 * A_SLAB
        dil = A_GROUPS[g][1]
        store_tile(q_ref, proj(c0, A_SLAB, bd64), dil, c0, **qk)
        store_tile(k_ref, proj(c0 + A_SLAB, A_SLAB, bd64), dil, c0 + A_SLAB, **qk)
        store_tile(v_ref, proj(c0 + 2 * A_SLAB, A_SLAB), dil, c0 + 2 * A_SLAB)

    for s in range(B_Q_COLS // A_SLAB):
        c0 = B_Q_OFF + s * A_SLAB
        finish(proj(c0, A_SLAB, bd64), to_ref(qb_ref, s * A_SLAB), c0, **qk)

    def under_query_heads(out_ref):
        def emit(rows, y):
            swapped = pltpu.roll(y, HEAD_DIM, 1)
            first = jnp.where(first_head_lanes, y, swapped).astype(BF16)
            second = jnp.where(first_head_lanes, swapped, y).astype(BF16)
            reps = B_GROUP // HEADS_PER_TILE
            out_ref[rows, :] = jnp.concatenate([first] * reps + [second] * reps, axis=1)
        return emit
    first_head_lanes = lax.broadcasted_iota(jnp.int32, (1, LANES), 1) < HEAD_DIM
    finish(proj(B_K_OFF, B_KV_COLS, bd64_ref[0:B_KV_COLS, 0:B_KV_COLS]), under_query_heads(kb_ref),
           B_K_OFF, **qk)
    finish(proj(B_V_OFF, B_KV_COLS), under_query_heads(vb_ref), B_V_OFF)

    bd128 = bd128_ref[...]
    for s in range(M_Q_COLS // A_SLAB):
        c0 = M_Q_OFF + s * A_SLAB
        finish(proj(c0, A_SLAB, bd128), to_ref(mq_ref, s * A_SLAB), c0, dim=M_HEAD_DIM)
    flush_pending()


def _cast_rows(rows, n_steps):
    for per_step in range(BF16_ROWS, rows + 1, BF16_ROWS):
        if rows % per_step == 0 and rows // per_step <= n_steps:
            return per_step
    raise ValueError(f"no bf16-aligned split of {rows} rows over {n_steps} steps")


def _in_proj(x2, pos_rows, attn_gain, w_in, colgain, freq_col, spread, one_row, bd64, bd128,
             cast_weights, cast_scales):
    t = x2.shape[0]
    tb = TOKEN_BLOCK
    dils = [dil for _, dil in A_GROUPS for _ in range(3)] + [1, 1, 1, 1]
    widths = [A_SLAB] * 9 + [B_Q_COLS, B_Q_COLS, B_Q_COLS, M_Q_COLS]
    n_staged = sum((d > 1) + (d > FAST_STRIDE) for d in dils)
    n = t // tb
    done = lambda i: (jnp.maximum(i - 1, 0), 0)

    def cast_spec(w):
        per_step = _cast_rows(w.shape[0], n + 1)
        last = w.shape[0] // per_step - 1
        return pl.BlockSpec((per_step, w.shape[1]), lambda i: (jnp.minimum(i, last), 0))

    cast_specs = [cast_spec(w) for w in cast_weights]
    return pl.pallas_call(
        functools.partial(_in_proj_kernel, cast_scales=tuple(cast_scales)),
        grid=(n + 1,),
        in_specs=[pl.BlockSpec((tb, D_MODEL), lambda i: (jnp.minimum(i, n - 1), 0)),
                  pl.BlockSpec((None, 1, tb), lambda i: (jnp.maximum(i - 1, 0), 0, 0)),
                  _const_spec((1, D_MODEL)),
                  _const_spec((D_MODEL, IN_COLS)),
                  _const_spec((1, IN_COLS)),
                  _const_spec((ROPE_HALF, 1)),
                  _const_spec((LANES, 3 * LANES)),
                  _const_spec((1, LANES)),
                  _const_spec((A_SLAB, A_SLAB)),
                  _const_spec((A_SLAB, A_SLAB))] + cast_specs,
        out_specs=[pl.BlockSpec((tb // d, w * d), done) for w, d in zip(widths, dils)] + cast_specs,
        out_shape=[jax.ShapeDtypeStruct((t // d, w * d), BF16) for w, d in zip(widths, dils)]
                  + [jax.ShapeDtypeStruct(w.shape, BF16) for w in cast_weights],
        scratch_shapes=[pltpu.VMEM((IN_COLS // A_SLAB, tb, A_SLAB), F32),
                        pltpu.VMEM((N_NORMED_TILES, tb, A_SLAB), F32)]
                       + [pltpu.VMEM((tb, D_MODEL), BF16)] + [pltpu.VMEM((tb, LANES), F32)] * 3
                       + [pltpu.VMEM((A_SLAB // LANES, tb, LANES), F32)] * n_staged,
        compiler_params=pltpu.CompilerParams(dimension_semantics=("arbitrary",),
                                             vmem_limit_bytes=VMEM_LIMIT),
        name="in_proj",
    )(x2, pos_rows, attn_gain, w_in, colgain, freq_col, spread, one_row, bd64, bd128, *cast_weights)


def _band_attn_kernel(*refs, kinds):
    refs = list(refs)
    ins = [[refs.pop(0) for _ in range(5 + has_sink)] for _, _, _, has_sink, _, _ in kinds]
    outs = [[refs.pop(0) for _ in range(1 + with_lse)] for _, _, _, _, with_lse, _ in kinds]
    for (rows, slabs, prev_off, has_sink, with_lse, starts), in_refs, out_refs in zip(kinds, ins, outs):
        _attend(*in_refs[:5], in_refs[5] if has_sink else None, out_refs[0],
                out_refs[1] if with_lse else None,
                first_step=True if starts else pl.program_id(1) == 0,
                rows=rows, slabs=slabs, prev_off=prev_off)


def _attend(q_ref, kc_ref, kp_ref, vc_ref, vp_ref, sink_ref, o_ref, lse_ref, *, first_step, rows, slabs,
            prev_off):
    has_sink = sink_ref is not None
    with_lse = lse_ref is not None
    row = lax.broadcasted_iota(jnp.int32, (BLOCK, 2 * BLOCK), 0)
    col = lax.broadcasted_iota(jnp.int32, (BLOCK, 2 * BLOCK), 1)
    bias = jnp.where(col < BLOCK,
                     jnp.where(col >= row + prev_off, 0.0, NEG),
                     jnp.where(col - BLOCK <= row, 0.0, NEG)).astype(F32)
    bias_first = jnp.where(jnp.logical_and(first_step, col < BLOCK), NEG, bias)
    head_of_lane = lax.broadcasted_iota(jnp.int32, (1, A_SLAB), 1) // HEAD_DIM
    head_mask = [(head_of_lane == hh).astype(BF16) for hh in range(A_HEADS)]
    first_head_lanes = lax.broadcasted_iota(jnp.int32, (1, LANES), 1) < HEAD_DIM
    pack_row = lax.broadcasted_iota(jnp.int32, (BF16_ROWS, A_SLAB), 0)
    nblk = rows // BLOCK

    def drop_first_row(t):
        top = jnp.where(pack_row == 0, jnp.zeros((BF16_ROWS, A_SLAB), BF16), t[0:BF16_ROWS])
        return jnp.concatenate([top, t[BF16_ROWS:]], axis=0)

    for w in range(slabs):
        cs = slice(w * A_SLAB, (w + 1) * A_SLAB)
        biases = {False: [bias] * A_HEADS, True: [bias_first] * A_HEADS}
        if has_sink:
            biases = {first: [jnp.where(col == 0, sink_ref[w * A_HEADS + hh] * LOG2E, b[hh])
                              for hh in range(A_HEADS)] for first, b in biases.items()}
        for jb in range(nblk):
            q = q_ref[jb * BLOCK:(jb + 1) * BLOCK, cs]
            if jb == 0:
                k2 = jnp.concatenate([kp_ref[:, cs], kc_ref[0:BLOCK, cs]], axis=0)
                v2 = jnp.concatenate([vp_ref[:, cs], vc_ref[0:BLOCK, cs]], axis=0)
            else:
                k2 = kc_ref[(jb - 1) * BLOCK:(jb + 1) * BLOCK, cs]
                v2 = vc_ref[(jb - 1) * BLOCK:(jb + 1) * BLOCK, cs]
            if has_sink:
                k2 = drop_first_row(k2)
                v2 = drop_first_row(v2)
            qs = jnp.concatenate([q * head_mask[hh] for hh in range(A_HEADS)], axis=0)
            s = lax.dot_general(qs, k2, (((1,), (1,)), ((), ())), preferred_element_type=F32)
            ps, ms, ls = [], [], []
            for hh in range(A_HEADS):
                sh = s[hh * BLOCK:(hh + 1) * BLOCK] + biases[jb == 0][hh]
                m = jnp.max(sh, axis=-1, keepdims=True)
                p = jnp.exp2(sh - m)
                ms.append(m)
                ls.append(jnp.sum(p, axis=-1, keepdims=True))
                ps.append(p.astype(BF16))
            ost = jnp.dot(jnp.concatenate(ps, axis=0), v2, preferred_element_type=F32)
            o_tiles, lse_tiles = [], []
            for t0 in range(0, A_HEADS, HEADS_PER_TILE):
                lanes = slice(t0 * HEAD_DIM, (t0 + HEADS_PER_TILE) * HEAD_DIM)
                l_tile = jnp.where(first_head_lanes, ls[t0], ls[t0 + 1])
                o_tiles.append(jnp.where(first_head_lanes, ost[t0 * BLOCK:(t0 + 1) * BLOCK, lanes],
                                         ost[(t0 + 1) * BLOCK:(t0 + 2) * BLOCK, lanes]) * (1.0 / l_tile))
                if with_lse:
                    lse_tiles.append(jnp.where(first_head_lanes, ms[t0], ms[t0 + 1]) + jnp.log2(l_tile))
            o_ref[jb * BLOCK:(jb + 1) * BLOCK, cs] = jnp.concatenate(o_tiles, axis=1).astype(o_ref.dtype)
            if with_lse:
                lse_ref[jb * BLOCK:(jb + 1) * BLOCK, cs] = jnp.concatenate(lse_tiles, axis=1)


def _band_attn(jobs, n_steps):
    kinds, in_specs, args, out_specs, out_shape = [], [], [], [], []
    for job in jobs:
        q = job["q"]
        b, length, width = q.shape
        sinks, with_lse = job.get("sinks"), job["with_lse"]
        assert sinks is None or job["prev_off"] >= 1
        if job["along"] == "rows":
            rows, slabs = length // n_steps, width // A_SLAB
            per_step = rows // BLOCK
            cur = pl.BlockSpec((None, rows, width), lambda i, j: (i, j, 0))
            prev = pl.BlockSpec((None, BLOCK, width),
                                lambda i, j, per_step=per_step: (i, jnp.maximum(j * per_step - 1, 0), 0))
        else:
            rows, slabs = length, width // A_SLAB // n_steps
            cur = pl.BlockSpec((None, rows, slabs * A_SLAB), lambda i, j: (i, 0, j))
            prev = pl.BlockSpec((None, BLOCK, slabs * A_SLAB), lambda i, j: (i, 0, j))
        kinds.append((rows, slabs, job["prev_off"], sinks is not None, with_lse, job["along"] == "slabs"))
        in_specs += [cur, cur, prev, cur, prev]
        args += [q, job["k"], job["k"], job["v"], job["v"]]
        if sinks is not None:
            in_specs.append(pl.BlockSpec(memory_space=pltpu.SMEM))
            args.append(sinks)
        out_specs.append(cur)
        out_shape.append(jax.ShapeDtypeStruct(q.shape, BF16))
        if with_lse:
            out_specs.append(cur)
            out_shape.append(jax.ShapeDtypeStruct(q.shape, F32))
    outs = list(pl.pallas_call(
        functools.partial(_band_attn_kernel, kinds=tuple(kinds)),
        grid=(jobs[0]["q"].shape[0], n_steps),
        in_specs=in_specs,
        out_specs=out_specs,
        out_shape=out_shape,
        compiler_params=pltpu.CompilerParams(dimension_semantics=("arbitrary",) * 2,
                                             vmem_limit_bytes=VMEM_LIMIT),
        name="band_attn",
    )(*args))
    return [[outs.pop(0) for _ in range(1 + job["with_lse"])] for job in jobs]


def _mix_kernel(x_ref, gain_ref, oa0, la0, oa1, la1, oa2, la2, ob_ref, mq_ref, mk_ref, mv_ref,
                wg_ref, bg_ref, woa_ref, wob_ref, wom_ref, wout_ref, out_ref, *stage_refs):
    x = x_ref[...]
    h = _rms_rows(x, gain_ref[...]).astype(BF16)
    stage_refs = list(stage_refs)

    def token_major(blk_ref, dil):
        if dil == 1:
            return lambda rows: blk_ref[rows, :].astype(F32)
        stage_ref = stage_refs.pop()
        rows = blk_ref.shape[0]
        if dil <= FAST_STRIDE:
            for r in range(dil):
                for s in range(A_SLAB // LANES):
                    c0 = r * A_SLAB + s * LANES
                    stage_ref[s, pl.ds(r, rows, stride=dil), :] = blk_ref[:, c0:c0 + LANES].astype(F32)
        else:
            half_ref = stage_refs.pop()
            part = rows * dil // FAST_STRIDE
            for r in range(dil):
                a, b = divmod(r, FAST_STRIDE)
                for s in range(A_SLAB // LANES):
                    c0 = r * A_SLAB + s * LANES
                    half_ref[s, pl.ds(b * part + a, rows, stride=dil // FAST_STRIDE), :] = (
                        blk_ref[:, c0:c0 + LANES].astype(F32))
            for b in range(FAST_STRIDE):
                for s in range(A_SLAB // LANES):
                    stage_ref[s, pl.ds(b, part, stride=FAST_STRIDE), :] = half_ref[s, b * part:(b + 1) * part, :]
        return lambda rows: jnp.concatenate([stage_ref[s, rows, :] for s in range(A_SLAB // LANES)], axis=1)

    head_cols = [slice(hh * M_HEAD_DIM, (hh + 1) * M_HEAD_DIM) for hh in range(M_HEADS)]
    scores = [lax.dot_general(mq_ref[:, cs], mk_ref[:, cs], (((1,), (1,)), ((), ())),
                              preferred_element_type=F32) for cs in head_cols]
    gate_cols = [slice(i * D_MODEL, (i + 1) * D_MODEL) for i in range(N_BRANCH)]
    def gate(i, cols=slice(0, D_MODEL)):
        cs = slice(gate_cols[i].start + cols.start, gate_cols[i].start + cols.stop)
        return 1.0 + jnp.tanh(jnp.dot(h, wg_ref[:, cs], preferred_element_type=F32) + 0.5 * bg_ref[:, cs])

    gates = [gate(0)]
    branch_b = jnp.dot(ob_ref[...], wob_ref[...], preferred_element_type=F32)
    dils = [dil for _, dil in A_GROUPS]
    lse = [token_major(r, d) for r, d in zip((la0, la1, la2), dils)]
    outs = [token_major(r, d) for r, d in zip((oa0, oa1, oa2), dils)]
    n_parts = D_MODEL // A_SLAB
    branch_a, gate_1 = [], []
    for part in range(n_parts):
        rows = slice(part * (x.shape[0] // n_parts), (part + 1) * (x.shape[0] // n_parts))
        l0, l1, l2 = (read(rows) for read in lse)
        o0, o1, o2 = (read(rows) for read in outs)
        mx = jnp.maximum(jnp.maximum(l0, l1), l2)
        e0, e1, e2 = jnp.exp2(l0 - mx), jnp.exp2(l1 - mx), jnp.exp2(l2 - mx)
        o_a = (e0 * o0 + e1 * o1 + e2 * o2) / (e0 + e1 + e2)
        gate_1.append(gate(1, slice(part * A_SLAB, (part + 1) * A_SLAB)))
        branch_a.append(jnp.dot(o_a.astype(BF16), woa_ref[...], preferred_element_type=F32))
    branch_a = jnp.concatenate(branch_a, axis=0)
    gates.append(jnp.concatenate(gate_1, axis=1))
    o_m, gate_2 = [], []
    for hh, (s, cs) in enumerate(zip(scores, head_cols)):
        p = jnp.exp2(s - jnp.max(s, axis=-1, keepdims=True))
        l = jnp.sum(p, axis=-1, keepdims=True)
        gate_2.append(gate(2, slice(hh * (D_MODEL // M_HEADS), (hh + 1) * (D_MODEL // M_HEADS))))
        o_m.append(jnp.dot(p.astype(BF16), mv_ref[:, cs], preferred_element_type=F32) / l)
    o_m = jnp.concatenate(o_m, axis=1)
    branch_m = jnp.dot(o_m.astype(BF16), wom_ref[...], preferred_element_type=F32)
    gates.append(jnp.concatenate(gate_2, axis=1))

    merged = jnp.zeros(x.shape, F32)
    for g, branch in zip(gates, (branch_a, branch_b, branch_m)):
        merged = merged + g * branch
    out_ref[...] = x + jnp.dot(merged.astype(BF16), wout_ref[...], preferred_element_type=F32)


def _mix(x2, attn_gain, oa, la, ob, mq, mk, mv, w_gate, b_gate, w_o_a, w_o_b, w_o_m, w_out, seq):
    t = x2.shape[0]
    tb = TOKEN_BLOCK
    mem_len = mk.shape[1]
    per_seq = seq // tb
    tok = lambda w, d=1: pl.BlockSpec((tb // d, w * d), lambda i: (i, 0))
    mem = pl.BlockSpec((None, mem_len, M_Q_COLS), lambda i: (i // per_seq, 0, 0))
    dils = [dil for _, dil in A_GROUPS]
    return pl.pallas_call(
        _mix_kernel,
        grid=(t // tb,),
        in_specs=[tok(D_MODEL), _const_spec((1, D_MODEL))]
                 + [tok(A_SLAB, d) for d in dils for _ in range(2)]
                 + [tok(B_Q_COLS), tok(M_Q_COLS), mem, mem,
                  _const_spec((D_MODEL, N_BRANCH * D_MODEL)), _const_spec((1, N_BRANCH * D_MODEL)),
                  _const_spec((A_SLAB, D_MODEL)), _const_spec((B_Q_COLS, D_MODEL)),
                  _const_spec((M_Q_COLS, D_MODEL)), _const_spec((D_MODEL, D_MODEL))],
        out_specs=tok(D_MODEL),
        out_shape=jax.ShapeDtypeStruct((t, D_MODEL), F32),
        scratch_shapes=[pltpu.VMEM((A_SLAB // LANES, tb, LANES), F32)]
                       * (2 * sum((d > 1) + (d > FAST_STRIDE) for d in dils)),
        compiler_params=pltpu.CompilerParams(dimension_semantics=("arbitrary",),
                                             vmem_limit_bytes=VMEM_LIMIT),
        name="mix",
    )(x2, attn_gain, oa[0], la[0], oa[1], la[1], oa[2], la[2], ob, mq, mk, mv,
      w_gate, b_gate, w_o_a, w_o_b, w_o_m, w_out)


def _conv_ffn_kernel(x_ref, gain_ref, wup_ref, cw_ref, cb_ref, wdown_ref, out_ref,
                     carry_ref, ext_ref, act_ref, *, per_seq):
    tb = x_ref.shape[0]
    halo = CONV_WIDTH - 1
    pad = SUBLANES

    @pl.when(pl.program_id(0) % per_seq == 0)
    def _():
        carry_ref[...] = jnp.zeros(carry_ref.shape, F32)

    x = x_ref[...]
    h = _rms_rows(x, gain_ref[...]).astype(BF16)

    def project(c0, buf):
        cs = slice(c0, c0 + FF_CHUNK)
        u = jnp.dot(h, wup_ref[:, cs], preferred_element_type=F32)
        ext_ref[buf, 0:pad, :] = carry_ref[:, cs]
        ext_ref[buf, pad:pad + tb, :] = u
        carry_ref[:, cs] = u[tb - pad:tb, :]

    def conv(c0, buf, lanes, scale=1.0):
        cs = slice(c0 + lanes.start, c0 + lanes.stop)
        taps = cw_ref[:, cs] * scale
        c = cb_ref[:, cs] * scale + taps[CONV_WIDTH - 1:CONV_WIDTH] * ext_ref[buf, pad:pad + tb, lanes]
        for j in range(halo):
            c = c + taps[j:j + 1] * ext_ref[buf, pad - halo + j:pad - halo + j + tb, lanes]
        return c

    for ch in range(D_FF // FF_CHUNK):
        project(ch * FF_CHUNK, 0)
        project(D_FF + ch * FF_CHUNK, 1)
        for lanes in (slice(0, LANES), slice(LANES, FF_CHUNK)):
            half = conv(ch * FF_CHUNK, 0, lanes, 0.5)
            g = conv(D_FF + ch * FF_CHUNK, 1, lanes)
            act = (half + half * jnp.tanh(half)) * g
            act_ref[:, ch * FF_CHUNK + lanes.start:ch * FF_CHUNK + lanes.stop] = act.astype(BF16)
    out_ref[...] = x + jnp.dot(act_ref[...], wdown_ref[...], preferred_element_type=F32)


def _conv_ffn(x2, gain, w_up, conv_w, conv_b, w_down, seq):
    t = x2.shape[0]
    tb = TOKEN_BLOCK
    return pl.pallas_call(
        functools.partial(_conv_ffn_kernel, per_seq=seq // tb),
        grid=(t // tb,),
        in_specs=[pl.BlockSpec((tb, D_MODEL), lambda i: (i, 0)),
                  _const_spec((1, D_MODEL)),
                  _const_spec((D_MODEL, 2 * D_FF)),
                  _const_spec((CONV_WIDTH, 2 * D_FF)),
                  _const_spec((1, 2 * D_FF)),
                  _const_spec((D_FF, D_MODEL))],
        out_specs=pl.BlockSpec((tb, D_MODEL), lambda i: (i, 0)),
        out_shape=jax.ShapeDtypeStruct((t, D_MODEL), F32),
        scratch_shapes=[pltpu.VMEM((SUBLANES, 2 * D_FF), F32),
                        pltpu.VMEM((2, tb + SUBLANES, FF_CHUNK), F32),
                        pltpu.VMEM((tb, D_FF), BF16)],
        compiler_params=pltpu.CompilerParams(dimension_semantics=("arbitrary",),
                                             vmem_limit_bytes=VMEM_LIMIT),
        name="conv_ffn",
    )(x2, gain, w_up, conv_w, conv_b, w_down)


def _tables():
    lane = np.arange(A_SLAB)
    bd64 = jnp.asarray(lane[:, None] // HEAD_DIM == lane[None, :] // HEAD_DIM, BF16)
    bd128 = jnp.asarray(lane[:, None] // M_HEAD_DIM == lane[None, :] // M_HEAD_DIM, BF16)
    freq_col = jnp.exp(jnp.arange(ROPE_HALF, dtype=F32) * (-2.0 * math.log(ROPE_THETA) / ROPE_DIM))
    freq_col = freq_col.reshape(ROPE_HALF, 1)
    in_head = np.arange(LANES) % HEAD_DIM
    j = np.arange(LANES)[:, None]
    rot = in_head[None, :] < ROPE_DIM
    cos_sel = rot & (j == in_head[None, :] % ROPE_HALF)
    sin_sel = rot & (j == ROPE_HALF + in_head[None, :] % ROPE_HALF)
    lo = in_head[None, :] < ROPE_HALF
    spread = np.concatenate([cos_sel.astype(np.float32), -(sin_sel & lo).astype(np.float32),
                             (sin_sel & ~lo).astype(np.float32)], axis=1)
    one_row = (in_head >= ROPE_DIM).astype(np.float32).reshape(1, LANES)
    return bd64, bd128, freq_col, jnp.asarray(spread, BF16), jnp.asarray(one_row)


def _layer(x2, mem, pos_rows, batch, seq, p):
    bd64, bd128, freq_col, spread, one_row = _tables()
    n_groups = len(A_GROUPS)
    gains = jnp.concatenate([p["a_q_norm"], p["a_k_norm"], p["b_q_norm"][None], p["b_k_norm"][None],
                             p["m_q_norm"].reshape(M_HEAD_DIM // HEAD_DIM, HEAD_DIM),
                             jnp.ones((1, HEAD_DIM), F32)])
    row_bq, row_bk, row_mq = 2 * n_groups, 2 * n_groups + 1, 2 * n_groups + 2
    row_one = row_mq + M_HEAD_DIM // HEAD_DIM
    qk_scale = HEAD_DIM ** -0.5 * LOG2E
    src, scale = [], []
    for g in range(n_groups):
        src += [g] * A_HEADS + [n_groups + g] * A_HEADS + [row_one] * A_HEADS
        scale += [qk_scale] * A_HEADS + [1.0] * (2 * A_HEADS)
    src += [row_bq] * B_Q_HEADS + [row_bk] * B_KV_HEADS + [row_one] * B_KV_HEADS
    scale += [qk_scale] * B_Q_HEADS + [1.0] * (2 * B_KV_HEADS)
    src += [row_mq + i for i in range(M_HEAD_DIM // HEAD_DIM)] * M_HEADS
    scale += [M_HEAD_DIM ** -0.5 * LOG2E] * (M_Q_COLS // HEAD_DIM)
    colgain = (gains[np.asarray(src)] * np.asarray(scale, np.float32)[:, None]).reshape(1, IN_COLS)

    attn_gain = p["attn_norm"].reshape(1, D_MODEL)
    mk, mv, w_in = _mem_kv(mem, p["mem_norm"].reshape(1, D_MODEL), p["w_mem_kv"],
                           p["m_k_norm"].reshape(1, M_HEAD_DIM), p["w_in"])
    later_weights = ("w_gate", "w_o_a", "w_o_b", "w_o_m", "w_out", "w_up", "w_down")
    scales = [0.5 if name in ("w_gate", "w_out") else 1.0 for name in later_weights]
    outs = _in_proj(x2, pos_rows, attn_gain, w_in, colgain, freq_col, spread, one_row,
                    bd64, bd128, [p[name] for name in later_weights], scales)
    (qa0, ka0, va0, qa1, ka1, va1, qa2, ka2, va2, qb, kb, vb, mq) = outs[:13]
    wb = dict(zip(later_weights, outs[13:]))

    def per_batch(t):
        return t.reshape(batch, t.shape[0] // batch, t.shape[-1])

    def flat(t):
        return t.reshape(t.shape[0] * t.shape[1], t.shape[2])

    jobs = [dict(q=per_batch(q), k=per_batch(k), v=per_batch(v), prev_off=0, with_lse=True,
                 along="rows" if dil == 1 else "slabs")
            for (_, dil), (q, k, v) in zip(A_GROUPS, ((qa0, ka0, va0), (qa1, ka1, va1), (qa2, ka2, va2)))]
    jobs.append(dict(q=per_batch(qb), k=per_batch(kb), v=per_batch(vb), prev_off=BLOCK - (B_WINDOW - 1),
                     sinks=p["b_sinks"], with_lse=False, along="rows"))
    *a_outs, (ob,) = _band_attn(jobs, ATTN_STEPS)
    oa = [flat(o) for o, _ in a_outs]
    la = [flat(lse) for _, lse in a_outs]
    ob = flat(ob)

    x2 = _mix(x2, attn_gain, oa, la, ob, mq, mk, mv, wb["w_gate"],
              p["b_gate"].reshape(1, N_BRANCH * D_MODEL), wb["w_o_a"], wb["w_o_b"], wb["w_o_m"],
              wb["w_out"], seq)
    return _conv_ffn(x2, p["ffn_norm"].reshape(1, D_MODEL), wb["w_up"], p["conv_w"],
                     p["conv_b"].reshape(1, 2 * D_FF), wb["w_down"], seq)


def kernel(x, mem, positions, attn_norm, w_in, a_q_norm, a_k_norm, b_q_norm, b_k_norm, b_sinks,
           mem_norm, w_mem_kv, m_q_norm, m_k_norm, w_o_a, w_o_b, w_o_m, w_gate, b_gate, w_out,
           ffn_norm, w_up, conv_w, conv_b, w_down):
    batch, seq, _ = x.shape
    params = dict(attn_norm=attn_norm, w_in=w_in, a_q_norm=a_q_norm, a_k_norm=a_k_norm,
                  b_q_norm=b_q_norm, b_k_norm=b_k_norm, b_sinks=b_sinks, mem_norm=mem_norm,
                  w_mem_kv=w_mem_kv, m_q_norm=m_q_norm, m_k_norm=m_k_norm, w_o_a=w_o_a, w_o_b=w_o_b,
                  w_o_m=w_o_m, w_gate=w_gate, b_gate=b_gate, w_out=w_out, ffn_norm=ffn_norm,
                  w_up=w_up, conv_w=conv_w, conv_b=conv_b, w_down=w_down)
    pos_rows = positions.astype(F32).reshape(batch * seq // TOKEN_BLOCK, 1, TOKEN_BLOCK)
    x2 = x.reshape(batch * seq, D_MODEL)
    for layer in range(attn_norm.shape[0]):
        x2 = _layer(x2, mem, pos_rows, batch, seq, {k: v[layer] for k, v in params.items()})
    return x2.reshape(batch, seq, D_MODEL)
```

```python
import functools
import math

import numpy as np

import jax
import jax.numpy as jnp
from jax import lax
from jax.experimental import pallas as pl
from jax.experimental.pallas import tpu as pltpu

D_MODEL = 1024
HEAD_DIM = 64
A_GROUPS = ((128, 1), (512, 4), (2048, 16))
A_HEADS = 4
A_SLAB = A_HEADS * HEAD_DIM
A_QKV_COLS = len(A_GROUPS) * 3 * A_SLAB
B_Q_HEADS = 8
B_KV_HEADS = 2
B_GROUP = B_Q_HEADS // B_KV_HEADS
B_WINDOW = 128
B_Q_COLS = B_Q_HEADS * HEAD_DIM
B_KV_COLS = B_KV_HEADS * HEAD_DIM
M_HEADS = 4
M_HEAD_DIM = 128
M_Q_COLS = M_HEADS * M_HEAD_DIM
B_Q_OFF = A_QKV_COLS
B_K_OFF = B_Q_OFF + B_Q_COLS
B_V_OFF = B_K_OFF + B_KV_COLS
M_Q_OFF = B_V_OFF + B_KV_COLS
IN_COLS = M_Q_OFF + M_Q_COLS
N_NORMED_TILES = 2 * len(A_GROUPS) + B_Q_COLS // A_SLAB + 1 + M_Q_COLS // A_SLAB
N_BRANCH = 3
D_FF = 2816
CONV_WIDTH = 3
ROPE_THETA = 500000.0
ROPE_DIM = HEAD_DIM // 4
ROPE_HALF = ROPE_DIM // 2
BLOCK = 128
EPS = 1e-6
NEG = -1e30
LOG2E = math.log2(math.e)

LANES = 128
SUBLANES = 8
FAST_STRIDE = 4
BF16_ROWS = 16
HEADS_PER_TILE = LANES // HEAD_DIM
TOKEN_BLOCK = 512
ROW_CHUNK = 32
ATTN_STEPS = 4
FF_CHUNK = 256
VMEM_LIMIT = 56 * 1024 * 1024

F32 = jnp.float32
BF16 = jnp.bfloat16

assert all(window // dil == BLOCK for window, dil in A_GROUPS) and B_WINDOW <= BLOCK
assert B_KV_COLS == LANES and B_KV_HEADS == HEADS_PER_TILE and B_GROUP % HEADS_PER_TILE == 0
assert M_HEAD_DIM == LANES and D_FF % FF_CHUNK == 0 and CONV_WIDTH - 1 <= SUBLANES


def _const_spec(shape):
    return pl.BlockSpec(shape, lambda *_: (0,) * len(shape), pipeline_mode=pl.Buffered(1))


def _rms_rows(x, gain):
    ms = jnp.mean(x * x, axis=-1, keepdims=True)
    return x * lax.rsqrt(ms + EPS) * gain


def _mem_kv_kernel(mem_ref, gain_ref, w_ref, kgain_ref, win_ref, mk_ref, mv_ref, win_out_ref):
    win_out_ref[...] = win_ref[...].astype(BF16)
    hm = _rms_rows(mem_ref[...], gain_ref[...]).astype(BF16)
    kv = jnp.dot(hm, w_ref[...].astype(BF16), preferred_element_type=F32)
    ks = []
    for h in range(M_HEADS):
        kh = kv[:, h * M_HEAD_DIM:(h + 1) * M_HEAD_DIM]
        ks.append(_rms_rows(kh, kgain_ref[...]))
    mk_ref[...] = jnp.concatenate(ks, axis=1).astype(BF16)
    mv_ref[...] = kv[:, M_Q_COLS:].astype(BF16)


def _mem_kv(mem, mem_gain, w_kv, k_gain, w_in):
    b, m, _ = mem.shape
    slab = pl.BlockSpec((_cast_rows(w_in.shape[0], b), w_in.shape[1]), lambda i: (i, 0))
    assert w_in.shape[0] == b * slab.block_shape[0]
    return pl.pallas_call(
        _mem_kv_kernel,
        grid=(b,),
        in_specs=[pl.BlockSpec((None, m, D_MODEL), lambda i: (i, 0, 0)),
                  _const_spec((1, D_MODEL)),
                  _const_spec((D_MODEL, 2 * M_Q_COLS)),
                  _const_spec((1, M_HEAD_DIM)),
                  slab],
        out_specs=[pl.BlockSpec((None, m, M_Q_COLS), lambda i: (i, 0, 0)),
                   pl.BlockSpec((None, m, M_Q_COLS), lambda i: (i, 0, 0)),
                   slab],
        out_shape=[jax.ShapeDtypeStruct((b, m, M_Q_COLS), BF16)] * 2
                  + [jax.ShapeDtypeStruct(w_in.shape, BF16)],
        compiler_params=pltpu.CompilerParams(dimension_semantics=("arbitrary",),
                                             vmem_limit_bytes=VMEM_LIMIT),
        name="mem_kv",
    )(mem, mem_gain, w_kv, k_gain, w_in)


def _in_proj_kernel(x_ref, pos_ref, gain_ref, w_ref, colgain_ref, freq_ref, spread_ref, one_ref,
                    bd64_ref, bd128_ref, *refs, cast_scales):
    n_cast = len(cast_scales)
    cast_in, refs = refs[:n_cast], refs[n_cast:]
    (qa0, ka0, va0, qa1, ka1, va1, qa2, ka2, va2, qb_ref, kb_ref, vb_ref, mq_ref) = refs[:13]
    cast_out, refs = refs[13:13 + n_cast], refs[13 + n_cast:]
    proj_ref, ss_ref, h_ref, cos_ref, sin_lo_ref, sin_hi_ref = refs[:6]
    stage_refs = list(refs[6:])
    tb = x_ref.shape[0]

    for src, dst, scale in zip(cast_in, cast_out, cast_scales):
        w = src[...]
        dst[...] = (w if scale == 1.0 else w * scale).astype(BF16)
    chunks = [slice(r, r + ROW_CHUNK) for r in range(0, tb, ROW_CHUNK)]

    @pl.when(pl.program_id(0) == 0)
    def _():
        proj_ref[...] = jnp.zeros(proj_ref.shape, F32)
        ss_ref[...] = jnp.zeros(ss_ref.shape, F32)

    for rows in chunks:
        h_ref[rows, :] = _rms_rows(x_ref[rows, :], gain_ref[...]).astype(BF16)

    ang = freq_ref[...] * pos_ref[...]
    trig = jnp.concatenate([jnp.cos(ang), jnp.sin(ang),
                            jnp.zeros((LANES - ROPE_DIM, tb), F32)], axis=0).T
    tables = jnp.zeros((tb, 3 * LANES), F32)
    for _ in range(3):
        part = trig.astype(BF16)
        tables = tables + jnp.dot(part, spread_ref[...], preferred_element_type=F32)
        trig = trig - part.astype(F32)
    cos_ref[...] = tables[:, 0:LANES] + one_ref[...]
    sin_lo_ref[...] = tables[:, LANES:2 * LANES]
    sin_hi_ref[...] = tables[:, 2 * LANES:3 * LANES]

    ss_slots = iter(range(ss_ref.shape[0]))
    pending = []

    def flush_pending():
        while pending:
            new, bd, slot = pending.pop()
            ss_ref[slot, :, 0:new.shape[1]] = jnp.dot((new * new).astype(BF16), bd,
                                                      preferred_element_type=F32)

    def proj(c0, width, bd=None):
        tile, off = divmod(c0, A_SLAB)
        assert off + width <= A_SLAB and off % LANES == 0
        t = proj_ref[tile, :, off:off + width]
        ss = None
        if bd is not None:
            slot = next(ss_slots)
            ss = ss_ref[slot, :, 0:width]
        new = jnp.dot(h_ref[...], w_ref[:, c0:c0 + width], preferred_element_type=F32)
        proj_ref[tile, :, off:off + width] = new
        flush_pending()
        if bd is not None:
            pending.append((new, bd, slot))
        return t, ss

    def rope(y, rows):
        parts = []
        for c in range(y.shape[1] // LANES):
            yc = y[:, c * LANES:(c + 1) * LANES]
            parts.append(yc * cos_ref[rows, :]
                         + pltpu.roll(yc, LANES - ROPE_HALF, 1) * sin_lo_ref[rows, :]
                         + pltpu.roll(yc, ROPE_HALF, 1) * sin_hi_ref[rows, :])
        return parts[0] if len(parts) == 1 else jnp.concatenate(parts, axis=1)

    def finish(t_ss, emit, c0, *, dim=None, rotary=False):
        t, ss = t_ss
        if ss is not None:
            gain = colgain_ref[:, c0:c0 + t.shape[1]]
        for rows in chunks:
            y = t[rows]
            if ss is not None:
                y = y * lax.rsqrt(ss[rows] * (1.0 / dim) + EPS) * gain
            if rotary:
                y = rope(y, rows)
            emit(rows, y)

    def to_ref(out_ref, col0=0):
        def emit(rows, y):
            out_ref[rows, col0:col0 + y.shape[1]] = y.astype(BF16)
        return emit

    def store_tile(out_ref, t, dil, c0, **kw):
        if dil == 1:
            finish(t, to_ref(out_ref), c0, **kw)
            return
        stage_ref = stage_refs.pop()

        def emit(rows, y):
            for s in range(A_SLAB // LANES):
                stage_ref[s, rows, :] = y[:, s * LANES:(s + 1) * LANES]
        finish(t, emit, c0, **kw)
        src_ref, step, part = stage_ref, dil, 0
        if dil > FAST_STRIDE:
            src_ref, step, part = stage_refs.pop(), dil // FAST_STRIDE, tb // FAST_STRIDE
            for b in range(FAST_STRIDE):
                for s in range(A_SLAB // LANES):
                    src_ref[s, b * part:(b + 1) * part, :] = stage_ref[s, pl.ds(b, part, stride=FAST_STRIDE), :]
        for r in range(dil):
            a, b = divmod(r, dil // step)
            for s in range(A_SLAB // LANES):
                col = r * A_SLAB + s * LANES
                out_ref[:, col:col + LANES] = src_ref[s, pl.ds(b * part + a, tb // dil, stride=step), :].astype(BF16)

    bd64 = bd64_ref[...]
    qk = dict(dim=HEAD_DIM, rotary=True)
    for g, (q_ref, k_ref, v_ref) in enumerate(((qa0, ka0, va0), (qa1, ka1, va1), (qa2, ka2, va2))):
        c0 = g * 3 * A_SLAB
        dil = A_GROUPS[g][1]
        store_tile(q_ref, proj(c0, A_SLAB, bd64), dil, c0, **qk)
        store_tile(k_ref, proj(c0 + A_SLAB, A_SLAB, bd64), dil, c0 + A_SLAB, **qk)
        store_tile(v_ref, proj(c0 + 2 * A_SLAB, A_SLAB), dil, c0 + 2 * A_SLAB)

    for s in range(B_Q_COLS // A_SLAB):
        c0 = B_Q_OFF + s * A_SLAB
        finish(proj(c0, A_SLAB, bd64), to_ref(qb_ref, s * A_SLAB), c0, **qk)

    def under_query_heads(out_ref):
        def emit(rows, y):
            swapped = pltpu.roll(y, HEAD_DIM, 1)
            first = jnp.where(first_head_lanes, y, swapped).astype(BF16)
            second = jnp.where(first_head_lanes, swapped, y).astype(BF16)
            reps = B_GROUP // HEADS_PER_TILE
            out_ref[rows, :] = jnp.concatenate([first] * reps + [second] * reps, axis=1)
        return emit
    first_head_lanes = lax.broadcasted_iota(jnp.int32, (1, LANES), 1) < HEAD_DIM
    finish(proj(B_K_OFF, B_KV_COLS, bd64_ref[0:B_KV_COLS, 0:B_KV_COLS]), under_query_heads(kb_ref),
           B_K_OFF, **qk)
    finish(proj(B_V_OFF, B_KV_COLS), under_query_heads(vb_ref), B_V_OFF)

    bd128 = bd128_ref[...]
    for s in range(M_Q_COLS // A_SLAB):
        c0 = M_Q_OFF + s * A_SLAB
        finish(proj(c0, A_SLAB, bd128), to_ref(mq_ref, s * A_SLAB), c0, dim=M_HEAD_DIM)
    flush_pending()


def _cast_rows(rows, n_steps):
    for per_step in range(BF16_ROWS, rows + 1, BF16_ROWS):
        if rows % per_step == 0 and rows // per_step <= n_steps:
            return per_step
    raise ValueError(f"no bf16-aligned split of {rows} rows over {n_steps} steps")


def _in_proj(x2, pos_rows, attn_gain, w_in, colgain, freq_col, spread, one_row, bd64, bd128,
             cast_weights, cast_scales):
    t = x2.shape[0]
    tb = TOKEN_BLOCK
    dils = [dil for _, dil in A_GROUPS for _ in range(3)] + [1, 1, 1, 1]
    widths = [A_SLAB] * 9 + [B_Q_COLS, B_Q_COLS, B_Q_COLS, M_Q_COLS]
    n_staged = sum((d > 1) + (d > FAST_STRIDE) for d in dils)
    n = t // tb
    done = lambda i: (jnp.maximum(i - 1, 0), 0)

    def cast_spec(w):
        per_step = _cast_rows(w.shape[0], n + 1)
        last = w.shape[0] // per_step - 1
        return pl.BlockSpec((per_step, w.shape[1]), lambda i: (jnp.minimum(i, last), 0))

    cast_specs = [cast_spec(w) for w in cast_weights]
    return pl.pallas_call(
        functools.partial(_in_proj_kernel, cast_scales=tuple(cast_scales)),
        grid=(n + 1,),
        in_specs=[pl.BlockSpec((tb, D_MODEL), lambda i: (jnp.minimum(i, n - 1), 0)),
                  pl.BlockSpec((None, 1, tb), lambda i: (jnp.maximum(i - 1, 0), 0, 0)),
                  _const_spec((1, D_MODEL)),
                  _const_spec((D_MODEL, IN_COLS)),
                  _const_spec((1, IN_COLS)),
                  _const_spec((ROPE_HALF, 1)),
                  _const_spec((LANES, 3 * LANES)),
                  _const_spec((1, LANES)),
                  _const_spec((A_SLAB, A_SLAB)),
                  _const_spec((A_SLAB, A_SLAB))] + cast_specs,
        out_specs=[pl.BlockSpec((tb // d, w * d), done) for w, d in zip(widths, dils)] + cast_specs,
        out_shape=[jax.ShapeDtypeStruct((t // d, w * d), BF16) for w, d in zip(widths, dils)]
                  + [jax.ShapeDtypeStruct(w.shape, BF16) for w in cast_weights],
        scratch_shapes=[pltpu.VMEM((IN_COLS // A_SLAB, tb, A_SLAB), F32),
                        pltpu.VMEM((N_NORMED_TILES, tb, A_SLAB), F32)]
                       + [pltpu.VMEM((tb, D_MODEL), BF16)] + [pltpu.VMEM((tb, LANES), F32)] * 3
                       + [pltpu.VMEM((A_SLAB // LANES, tb, LANES), F32)] * n_staged,
        compiler_params=pltpu.CompilerParams(dimension_semantics=("arbitrary",),
                                             vmem_limit_bytes=VMEM_LIMIT),
        name="in_proj",
    )(x2, pos_rows, attn_gain, w_in, colgain, freq_col, spread, one_row, bd64, bd128, *cast_weights)


def _band_attn_kernel(*refs, kinds):
    refs = list(refs)
    ins = [[refs.pop(0) for _ in range(5 + has_sink)] for _, _, _, has_sink, _, _ in kinds]
    outs = [[refs.pop(0) for _ in range(1 + with_lse)] for _, _, _, _, with_lse, _ in kinds]
    for (rows, slabs, prev_off, has_sink, with_lse, starts), in_refs, out_refs in zip(kinds, ins, outs):
        _attend(*in_refs[:5], in_refs[5] if has_sink else None, out_refs[0],
                out_refs[1] if with_lse else None,
                first_step=True if starts else pl.program_id(1) == 0,
                rows=rows, slabs=slabs, prev_off=prev_off)


def _attend(q_ref, kc_ref, kp_ref, vc_ref, vp_ref, sink_ref, o_ref, lse_ref, *, first_step, rows, slabs,
            prev_off):
    has_sink = sink_ref is not None
    with_lse = lse_ref is not None
    row = lax.broadcasted_iota(jnp.int32, (BLOCK, 2 * BLOCK), 0)
    col = lax.broadcasted_iota(jnp.int32, (BLOCK, 2 * BLOCK), 1)
    bias = jnp.where(col < BLOCK,
                     jnp.where(col >= row + prev_off, 0.0, NEG),
                     jnp.where(col - BLOCK <= row, 0.0, NEG)).astype(F32)
    bias_first = jnp.where(jnp.logical_and(first_step, col < BLOCK), NEG, bias)
    head_of_lane = lax.broadcasted_iota(jnp.int32, (1, A_SLAB), 1) // HEAD_DIM
    head_mask = [(head_of_lane == hh).astype(BF16) for hh in range(A_HEADS)]
    first_head_lanes = lax.broadcasted_iota(jnp.int32, (1, LANES), 1) < HEAD_DIM
    pack_row = lax.broadcasted_iota(jnp.int32, (BF16_ROWS, A_SLAB), 0)
    nblk = rows // BLOCK

    def drop_first_row(t):
        top = jnp.where(pack_row == 0, jnp.zeros((BF16_ROWS, A_SLAB), BF16), t[0:BF16_ROWS])
        return jnp.concatenate([top, t[BF16_ROWS:]], axis=0)

    for w in range(slabs):
        cs = slice(w * A_SLAB, (w + 1) * A_SLAB)
        biases = {False: [bias] * A_HEADS, True: [bias_first] * A_HEADS}
        if has_sink:
            biases = {first: [jnp.where(col == 0, sink_ref[w * A_HEADS + hh] * LOG2E, b[hh])
                              for hh in range(A_HEADS)] for first, b in biases.items()}
        for jb in range(nblk):
            q = q_ref[jb * BLOCK:(jb + 1) * BLOCK, cs]
            if jb == 0:
                k2 = jnp.concatenate([kp_ref[:, cs], kc_ref[0:BLOCK, cs]], axis=0)
                v2 = jnp.concatenate([vp_ref[:, cs], vc_ref[0:BLOCK, cs]], axis=0)
            else:
                k2 = kc_ref[(jb - 1) * BLOCK:(jb + 1) * BLOCK, cs]
                v2 = vc_ref[(jb - 1) * BLOCK:(jb + 1) * BLOCK, cs]
            if has_sink:
                k2 = drop_first_row(k2)
                v2 = drop_first_row(v2)
            qs = jnp.concatenate([q * head_mask[hh] for hh in range(A_HEADS)], axis=0)
            s = lax.dot_general(qs, k2, (((1,), (1,)), ((), ())), preferred_element_type=F32)
            ps, ms, ls = [], [], []
            for hh in range(A_HEADS):
                sh = s[hh * BLOCK:(hh + 1) * BLOCK] + biases[jb == 0][hh]
                m = jnp.max(sh, axis=-1, keepdims=True)
                p = jnp.exp2(sh - m)
                ms.append(m)
                ls.append(jnp.sum(p, axis=-1, keepdims=True))
                ps.append(p.astype(BF16))
            ost = jnp.dot(jnp.concatenate(ps, axis=0), v2, preferred_element_type=F32)
            o_tiles, lse_tiles = [], []
            for t0 in range(0, A_HEADS, HEADS_PER_TILE):
                lanes = slice(t0 * HEAD_DIM, (t0 + HEADS_PER_TILE) * HEAD_DIM)
                l_tile = jnp.where(first_head_lanes, ls[t0], ls[t0 + 1])
                o_tiles.append(jnp.where(first_head_lanes, ost[t0 * BLOCK:(t0 + 1) * BLOCK, lanes],
                                         ost[(t0 + 1) * BLOCK:(t0 + 2) * BLOCK, lanes]) * (1.0 / l_tile))
                if with_lse:
                    lse_tiles.append(jnp.where(first_head_lanes, ms[t0], ms[t0 + 1]) + jnp.log2(l_tile))
            o_ref[jb * BLOCK:(jb + 1) * BLOCK, cs] = jnp.concatenate(o_tiles, axis=1).astype(o_ref.dtype)
            if with_lse:
                lse_ref[jb * BLOCK:(jb + 1) * BLOCK, cs] = jnp.concatenate(lse_tiles, axis=1)


def _band_attn(jobs, n_steps):
    kinds, in_specs, args, out_specs, out_shape = [], [], [], [], []
    for job in jobs:
        q = job["q"]
        b, length, width = q.shape
        sinks, with_lse = job.get("sinks"), job["with_lse"]
        assert sinks is None or job["prev_off"] >= 1
        if job["along"] == "rows":
            rows, slabs = length // n_steps, width // A_SLAB
            per_step = rows // BLOCK
            cur = pl.BlockSpec((None, rows, width), lambda i, j: (i, j, 0))
            prev = pl.BlockSpec((None, BLOCK, width),
                                lambda i, j, per_step=per_step: (i, jnp.maximum(j * per_step - 1, 0), 0))
        else:
            rows, slabs = length, width // A_SLAB // n_steps
            cur = pl.BlockSpec((None, rows, slabs * A_SLAB), lambda i, j: (i, 0, j))
            prev = pl.BlockSpec((None, BLOCK, slabs * A_SLAB), lambda i, j: (i, 0, j))
        kinds.append((rows, slabs, job["prev_off"], sinks is not None, with_lse, job["along"] == "slabs"))
        in_specs += [cur, cur, prev, cur, prev]
        args += [q, job["k"], job["k"], job["v"], job["v"]]
        if sinks is not None:
            in_specs.append(pl.BlockSpec(memory_space=pltpu.SMEM))
            args.append(sinks)
        out_specs.append(cur)
        out_shape.append(jax.ShapeDtypeStruct(q.shape, BF16))
        if with_lse:
            out_specs.append(cur)
            out_shape.append(jax.ShapeDtypeStruct(q.shape, F32))
    outs = list(pl.pallas_call(
        functools.partial(_band_attn_kernel, kinds=tuple(kinds)),
        grid=(jobs[0]["q"].shape[0], n_steps),
        in_specs=in_specs,
        out_specs=out_specs,
        out_shape=out_shape,
        compiler_params=pltpu.CompilerParams(dimension_semantics=("arbitrary",) * 2,
                                             vmem_limit_bytes=VMEM_LIMIT),
        name="band_attn",
    )(*args))
    return [[outs.pop(0) for _ in range(1 + job["with_lse"])] for job in jobs]


def _mix_kernel(x_ref, gain_ref, oa0, la0, oa1, la1, oa2, la2, ob_ref, mq_ref, mk_ref, mv_ref,
                wg_ref, bg_ref, woa_ref, wob_ref, wom_ref, wout_ref, out_ref, *stage_refs):
    x = x_ref[...]
    h = _rms_rows(x, gain_ref[...]).astype(BF16)
    stage_refs = list(stage_refs)

    def token_major(blk_ref, dil):
        if dil == 1:
            return lambda rows: blk_ref[rows, :].astype(F32)
        stage_ref = stage_refs.pop()
        rows = blk_ref.shape[0]
        if dil <= FAST_STRIDE:
            for r in range(dil):
                for s in range(A_SLAB // LANES):
                    c0 = r * A_SLAB + s * LANES
                    stage_ref[s, pl.ds(r, rows, stride=dil), :] = blk_ref[:, c0:c0 + LANES].astype(F32)
        else:
            half_ref = stage_refs.pop()
            part = rows * dil // FAST_STRIDE
            for r in range(dil):
                a, b = divmod(r, FAST_STRIDE)
                for s in range(A_SLAB // LANES):
                    c0 = r * A_SLAB + s * LANES
                    half_ref[s, pl.ds(b * part + a, rows, stride=dil // FAST_STRIDE), :] = (
                        blk_ref[:, c0:c0 + LANES].astype(F32))
            for b in range(FAST_STRIDE):
                for s in range(A_SLAB // LANES):
                    stage_ref[s, pl.ds(b, part, stride=FAST_STRIDE), :] = half_ref[s, b * part:(b + 1) * part, :]
        return lambda rows: jnp.concatenate([stage_ref[s, rows, :] for s in range(A_SLAB // LANES)], axis=1)

    head_cols = [slice(hh * M_HEAD_DIM, (hh + 1) * M_HEAD_DIM) for hh in range(M_HEADS)]
    scores = [lax.dot_general(mq_ref[:, cs], mk_ref[:, cs], (((1,), (1,)), ((), ())),
                              preferred_element_type=F32) for cs in head_cols]
    gate_cols = [slice(i * D_MODEL, (i + 1) * D_MODEL) for i in range(N_BRANCH)]
    def gate(i, cols=slice(0, D_MODEL)):
        cs = slice(gate_cols[i].start + cols.start, gate_cols[i].start + cols.stop)
        return 1.0 + jnp.tanh(jnp.dot(h, wg_ref[:, cs], preferred_element_type=F32) + 0.5 * bg_ref[:, cs])

    gates = [gate(0)]
    branch_b_parts = []
    dils = [dil for _, dil in A_GROUPS]
    lse = [token_major(r, d) for r, d in zip((la0, la1, la2), dils)]
    outs = [token_major(r, d) for r, d in zip((oa0, oa1, oa2), dils)]
    n_parts = D_MODEL // A_SLAB
    branch_a, gate_1 = [], []
    for part in range(n_parts):
        rows = slice(part * (x.shape[0] // n_parts), (part + 1) * (x.shape[0] // n_parts))
        l0, l1, l2 = (read(rows) for read in lse)
        o0, o1, o2 = (read(rows) for read in outs)
        mx = jnp.maximum(jnp.maximum(l0, l1), l2)
        e0, e1, e2 = jnp.exp2(l0 - mx), jnp.exp2(l1 - mx), jnp.exp2(l2 - mx)
        o_a = (e0 * o0 + e1 * o1 + e2 * o2) / (e0 + e1 + e2)
        gate_1.append(gate(1, slice(part * A_SLAB, (part + 1) * A_SLAB)))
        branch_b_parts.append(jnp.dot(ob_ref[rows, :], wob_ref[...], preferred_element_type=F32))
        branch_a.append(jnp.dot(o_a.astype(BF16), woa_ref[...], preferred_element_type=F32))
    branch_a = jnp.concatenate(branch_a, axis=0)
    branch_b = jnp.concatenate(branch_b_parts, axis=0)
    gates.append(jnp.concatenate(gate_1, axis=1))
    o_m, gate_2 = [], []
    for hh, (s, cs) in enumerate(zip(scores, head_cols)):
        p = jnp.exp2(s - jnp.max(s, axis=-1, keepdims=True))
        l = jnp.sum(p, axis=-1, keepdims=True)
        gate_2.append(gate(2, slice(hh * (D_MODEL // M_HEADS), (hh + 1) * (D_MODEL // M_HEADS))))
        o_m.append(jnp.dot(p.astype(BF16), mv_ref[:, cs], preferred_element_type=F32) / l)
    o_m = jnp.concatenate(o_m, axis=1)
    branch_m = jnp.dot(o_m.astype(BF16), wom_ref[...], preferred_element_type=F32)
    gates.append(jnp.concatenate(gate_2, axis=1))

    merged = jnp.zeros(x.shape, F32)
    for g, branch in zip(gates, (branch_a, branch_b, branch_m)):
        merged = merged + g * branch
    out_ref[...] = x + jnp.dot(merged.astype(BF16), wout_ref[...], preferred_element_type=F32)


def _mix(x2, attn_gain, oa, la, ob, mq, mk, mv, w_gate, b_gate, w_o_a, w_o_b, w_o_m, w_out, seq):
    t = x2.shape[0]
    tb = TOKEN_BLOCK
    mem_len = mk.shape[1]
    per_seq = seq // tb
    tok = lambda w, d=1: pl.BlockSpec((tb // d, w * d), lambda i: (i, 0))
    mem = pl.BlockSpec((None, mem_len, M_Q_COLS), lambda i: (i // per_seq, 0, 0))
    dils = [dil for _, dil in A_GROUPS]
    return pl.pallas_call(
        _mix_kernel,
        grid=(t // tb,),
        in_specs=[tok(D_MODEL), _const_spec((1, D_MODEL))]
                 + [tok(A_SLAB, d) for d in dils for _ in range(2)]
                 + [tok(B_Q_COLS), tok(M_Q_COLS), mem, mem,
                  _const_spec((D_MODEL, N_BRANCH * D_MODEL)), _const_spec((1, N_BRANCH * D_MODEL)),
                  _const_spec((A_SLAB, D_MODEL)), _const_spec((B_Q_COLS, D_MODEL)),
                  _const_spec((M_Q_COLS, D_MODEL)), _const_spec((D_MODEL, D_MODEL))],
        out_specs=tok(D_MODEL),
        out_shape=jax.ShapeDtypeStruct((t, D_MODEL), F32),
        scratch_shapes=[pltpu.VMEM((A_SLAB // LANES, tb, LANES), F32)]
                       * (2 * sum((d > 1) + (d > FAST_STRIDE) for d in dils)),
        compiler_params=pltpu.CompilerParams(dimension_semantics=("arbitrary",),
                                             vmem_limit_bytes=VMEM_LIMIT),
        name="mix",
    )(x2, attn_gain, oa[0], la[0], oa[1], la[1], oa[2], la[2], ob, mq, mk, mv,
      w_gate, b_gate, w_o_a, w_o_b, w_o_m, w_out)


def _conv_ffn_kernel(x_ref, gain_ref, wup_ref, cw_ref, cb_ref, wdown_ref, out_ref,
                     carry_ref, ext_ref, act_ref, *, per_seq):
    tb = x_ref.shape[0]
    halo = CONV_WIDTH - 1
    pad = SUBLANES

    @pl.when(pl.program_id(0) % per_seq == 0)
    def _():
        carry_ref[...] = jnp.zeros(carry_ref.shape, F32)

    x = x_ref[...]
    h = _rms_rows(x, gain_ref[...]).astype(BF16)

    def conv(c0, scale=1.0):
        cs = slice(c0, c0 + FF_CHUNK)
        u = jnp.dot(h, wup_ref[:, cs], preferred_element_type=F32)
        ext_ref[0:pad, :] = carry_ref[:, cs]
        ext_ref[pad:pad + tb, :] = u
        carry_ref[:, cs] = u[tb - pad:tb, :]
        taps = cw_ref[:, cs] * scale
        c = cb_ref[:, cs] * scale + taps[CONV_WIDTH - 1:CONV_WIDTH] * u
        for j in range(halo):
            c = c + taps[j:j + 1] * ext_ref[pad - halo + j:pad - halo + j + tb, :]
        return c

    for ch in range(D_FF // FF_CHUNK):
        half = conv(ch * FF_CHUNK, 0.5)
        g = conv(D_FF + ch * FF_CHUNK)
        act = (half + half * jnp.tanh(half)) * g
        act_ref[:, ch * FF_CHUNK:(ch + 1) * FF_CHUNK] = act.astype(BF16)
    out_ref[...] = x + jnp.dot(act_ref[...], wdown_ref[...], preferred_element_type=F32)


def _conv_ffn(x2, gain, w_up, conv_w, conv_b, w_down, seq):
    t = x2.shape[0]
    tb = TOKEN_BLOCK
    return pl.pallas_call(
        functools.partial(_conv_ffn_kernel, per_seq=seq // tb),
        grid=(t // tb,),
        in_specs=[pl.BlockSpec((tb, D_MODEL), lambda i: (i, 0)),
                  _const_spec((1, D_MODEL)),
                  _const_spec((D_MODEL, 2 * D_FF)),
                  _const_spec((CONV_WIDTH, 2 * D_FF)),
                  _const_spec((1, 2 * D_FF)),
                  _const_spec((D_FF, D_MODEL))],
        out_specs=pl.BlockSpec((tb, D_MODEL), lambda i: (i, 0)),
        out_shape=jax.ShapeDtypeStruct((t, D_MODEL), F32),
        scratch_shapes=[pltpu.VMEM((SUBLANES, 2 * D_FF), F32),
                        pltpu.VMEM((tb + SUBLANES, FF_CHUNK), F32),
                        pltpu.VMEM((tb, D_FF), BF16)],
        compiler_params=pltpu.CompilerParams(dimension_semantics=("arbitrary",),
                                             vmem_limit_bytes=VMEM_LIMIT),
        name="conv_ffn",
    )(x2, gain, w_up, conv_w, conv_b, w_down)


def _tables():
    lane = np.arange(A_SLAB)
    bd64 = jnp.asarray(lane[:, None] // HEAD_DIM == lane[None, :] // HEAD_DIM, BF16)
    bd128 = jnp.asarray(lane[:, None] // M_HEAD_DIM == lane[None, :] // M_HEAD_DIM, BF16)
    freq_col = jnp.exp(jnp.arange(ROPE_HALF, dtype=F32) * (-2.0 * math.log(ROPE_THETA) / ROPE_DIM))
    freq_col = freq_col.reshape(ROPE_HALF, 1)
    in_head = np.arange(LANES) % HEAD_DIM
    j = np.arange(LANES)[:, None]
    rot = in_head[None, :] < ROPE_DIM
    cos_sel = rot & (j == in_head[None, :] % ROPE_HALF)
    sin_sel = rot & (j == ROPE_HALF + in_head[None, :] % ROPE_HALF)
    lo = in_head[None, :] < ROPE_HALF
    spread = np.concatenate([cos_sel.astype(np.float32), -(sin_sel & lo).astype(np.float32),
                             (sin_sel & ~lo).astype(np.float32)], axis=1)
    one_row = (in_head >= ROPE_DIM).astype(np.float32).reshape(1, LANES)
    return bd64, bd128, freq_col, jnp.asarray(spread, BF16), jnp.asarray(one_row)


def _layer(x2, mem, pos_rows, batch, seq, p):
    bd64, bd128, freq_col, spread, one_row = _tables()
    n_groups = len(A_GROUPS)
    gains = jnp.concatenate([p["a_q_norm"], p["a_k_norm"], p["b_q_norm"][None], p["b_k_norm"][None],
                             p["m_q_norm"].reshape(M_HEAD_DIM // HEAD_DIM, HEAD_DIM),
                             jnp.ones((1, HEAD_DIM), F32)])
    row_bq, row_bk, row_mq = 2 * n_groups, 2 * n_groups + 1, 2 * n_groups + 2
    row_one = row_mq + M_HEAD_DIM // HEAD_DIM
    qk_scale = HEAD_DIM ** -0.5 * LOG2E
    src, scale = [], []
    for g in range(n_groups):
        src += [g] * A_HEADS + [n_groups + g] * A_HEADS + [row_one] * A_HEADS
        scale += [qk_scale] * A_HEADS + [1.0] * (2 * A_HEADS)
    src += [row_bq] * B_Q_HEADS + [row_bk] * B_KV_HEADS + [row_one] * B_KV_HEADS
    scale += [qk_scale] * B_Q_HEADS + [1.0] * (2 * B_KV_HEADS)
    src += [row_mq + i for i in range(M_HEAD_DIM // HEAD_DIM)] * M_HEADS
    scale += [M_HEAD_DIM ** -0.5 * LOG2E] * (M_Q_COLS // HEAD_DIM)
    colgain = (gains[np.asarray(src)] * np.asarray(scale, np.float32)[:, None]).reshape(1, IN_COLS)

    attn_gain = p["attn_norm"].reshape(1, D_MODEL)
    mk, mv, w_in = _mem_kv(mem, p["mem_norm"].reshape(1, D_MODEL), p["w_mem_kv"],
                           p["m_k_norm"].reshape(1, M_HEAD_DIM), p["w_in"])
    later_weights = ("w_gate", "w_o_a", "w_o_b", "w_o_m", "w_out", "w_up", "w_down")
    scales = [0.5 if name in ("w_gate", "w_out") else 1.0 for name in later_weights]
    outs = _in_proj(x2, pos_rows, attn_gain, w_in, colgain, freq_col, spread, one_row,
                    bd64, bd128, [p[name] for name in later_weights], scales)
    (qa0, ka0, va0, qa1, ka1, va1, qa2, ka2, va2, qb, kb, vb, mq) = outs[:13]
    wb = dict(zip(later_weights, outs[13:]))

    def per_batch(t):
        return t.reshape(batch, t.shape[0] // batch, t.shape[-1])

    def flat(t):
        return t.reshape(t.shape[0] * t.shape[1], t.shape[2])

    jobs = [dict(q=per_batch(q), k=per_batch(k), v=per_batch(v), prev_off=0, with_lse=True,
                 along="rows" if dil == 1 else "slabs")
            for (_, dil), (q, k, v) in zip(A_GROUPS, ((qa0, ka0, va0), (qa1, ka1, va1), (qa2, ka2, va2)))]
    jobs.append(dict(q=per_batch(qb), k=per_batch(kb), v=per_batch(vb), prev_off=BLOCK - (B_WINDOW - 1),
                     sinks=p["b_sinks"], with_lse=False, along="rows"))
    *a_outs, (ob,) = _band_attn(jobs, ATTN_STEPS)
    oa = [flat(o) for o, _ in a_outs]
    la = [flat(lse) for _, lse in a_outs]
    ob = flat(ob)

    x2 = _mix(x2, attn_gain, oa, la, ob, mq, mk, mv, wb["w_gate"],
              p["b_gate"].reshape(1, N_BRANCH * D_MODEL), wb["w_o_a"], wb["w_o_b"], wb["w_o_m"],
              wb["w_out"], seq)
    return _conv_ffn(x2, p["ffn_norm"].reshape(1, D_MODEL), wb["w_up"], p["conv_w"],
                     p["conv_b"].reshape(1, 2 * D_FF), wb["w_down"], seq)


def kernel(x, mem, positions, attn_norm, w_in, a_q_norm, a_k_norm, b_q_norm, b_k_norm, b_sinks,
           mem_norm, w_mem_kv, m_q_norm, m_k_norm, w_o_a, w_o_b, w_o_m, w_gate, b_gate, w_out,
           ffn_norm, w_up, conv_w, conv_b, w_down):
    batch, seq, _ = x.shape
    params = dict(attn_norm=attn_norm, w_in=w_in, a_q_norm=a_q_norm, a_k_norm=a_k_norm,
                  b_q_norm=b_q_norm, b_k_norm=b_k_norm, b_sinks=b_sinks, mem_norm=mem_norm,
                  w_mem_kv=w_mem_kv, m_q_norm=m_q_norm, m_k_norm=m_k_norm, w_o_a=w_o_a, w_o_b=w_o_b,
                  w_o_m=w_o_m, w_gate=w_gate, b_gate=b_gate, w_out=w_out, ffn_norm=ffn_norm,
                  w_up=w_up, conv_w=conv_w, conv_b=conv_b, w_down=w_down)
    pos_rows = positions.astype(F32).reshape(batch * seq // TOKEN_BLOCK, 1, TOKEN_BLOCK)
    x2 = x.reshape(batch * seq, D_MODEL)
    for layer in range(attn_norm.shape[0]):
        x2 = _layer(x2, mem, pos_rows, batch, seq, {k: v[layer] for k, v in params.items()})
    return x2.reshape(batch, seq, D_MODEL)
```
